```python
import math
import numpy as np
import jax
import jax.numpy as jnp
from jax import lax

D_MODEL = 2048
BATCH = 8
SEQ = 2048
DEPTH = 1

CHUNK = 64
Q_BLOCK = 128
HA = 8
DA_QK = 64
DA_V = 2 * DA_QK
QK_A = HA * 2 * DA_QK
WIDTH_A = HA * DA_V
HB = 8
DB = 128
WIDTH_B = HB * DB
LEFT_CHUNKS = 8
BAND = (LEFT_CHUNKS + 1) * CHUNK
REL_CLIP = 256
T5_BUCKETS = 32
T5_MAX_DIST = 128
IN_SPLITS = [QK_A, 2 * QK_A, 2 * QK_A + WIDTH_A,
             2 * QK_A + WIDTH_A + WIDTH_B,
             2 * QK_A + WIDTH_A + 2 * WIDTH_B,
             2 * QK_A + WIDTH_A + 3 * WIDTH_B]
IN_COLS = 2 * QK_A + WIDTH_A + 3 * WIDTH_B + 2 * D_MODEL
N_EXPERTS = 64
TOP_K = 8
N_GROUPS = 8
TOPK_GROUPS = 4
D_EXPERT = 512
D_SHARED = 512
ROUTED_SCALE = 2.5
MOE_BLOCK = 128
N_MOD = 6
EPS = 1e-6
HEAD_EPS = 1e-5
NEG = -1e30

kernel_name = "hybrid_chunk_causal_diffattn_bandattn_moe"


def rmsnorm(x, g):
    xf = x.astype(jnp.float32)
    y = xf * lax.rsqrt(jnp.mean(xf * xf, axis=-1, keepdims=True) + EPS)
    return (y * g.astype(jnp.float32)).astype(x.dtype)


def head_rmsnorm(x):
    xf = x.astype(jnp.float32)
    return (xf * lax.rsqrt(jnp.mean(xf * xf, axis=-1, keepdims=True) + HEAD_EPS)).astype(x.dtype)


def t5_bucket(rel):
    nb = T5_BUCKETS // 2
    ret = (rel > 0).astype(np.int32) * nb
    n = np.abs(rel)
    max_exact = nb // 2
    large = max_exact + (np.log(np.maximum(n, 1) / max_exact) / math.log(T5_MAX_DIST / max_exact)
                         * (nb - max_exact)).astype(np.int32)
    large = np.minimum(large, nb - 1)
    return (ret + np.where(n < max_exact, n, large)).astype(np.int32)


def diff_attention(q, k, v, lam, lam_init, t5_table):
    b_, s_ = q.shape[0], q.shape[1]
    scale = DA_QK ** -0.5
    outs = []
    for start in range(0, s_, Q_BLOCK):
        end = start + Q_BLOCK
        qpos = np.arange(start, end)
        kpos = np.arange(end)
        allowed = (kpos[None, :] // CHUNK) <= (qpos[:, None] // CHUNK)
        bias = jnp.transpose(t5_table[t5_bucket(kpos[None, :] - qpos[:, None])], (2, 0, 1))
        s = jnp.einsum('bqhmd,bkhmd->bhmqk', q[:, start:end], k[:, :end]).astype(jnp.float32) * scale
        s = jnp.where(allowed, s + bias.astype(jnp.float32)[None, :, None], NEG)
        p = jax.nn.softmax(s, axis=-1)
        a = p[:, :, 0] - lam * p[:, :, 1]
        outs.append(jnp.einsum('bhqk,bkhe->bqhe', a.astype(v.dtype), v[:, :end]))
    o = jnp.concatenate(outs, axis=1)
    o = head_rmsnorm(o) * (1.0 - lam_init)
    return o.reshape(b_, s_, WIDTH_A)


def chunk_band_attention(q, k, v, rel_table):
    b_, s_ = q.shape[0], q.shape[1]
    nc = s_ // CHUNK
    qc = q.reshape(b_, nc, CHUNK, HB, DB)
    pad = ((0, 0), (LEFT_CHUNKS * CHUNK, 0), (0, 0), (0, 0))
    kp = jnp.pad(k, pad).reshape(b_, nc + LEFT_CHUNKS, CHUNK, HB, DB)
    vp = jnp.pad(v, pad).reshape(b_, nc + LEFT_CHUNKS, CHUNK, HB, DB)
    idx = np.arange(nc)[:, None] + np.arange(LEFT_CHUNKS + 1)[None, :]
    kb = kp[:, idx].reshape(b_, nc, BAND, HB, DB)
    vb = vp[:, idx].reshape(b_, nc, BAND, HB, DB)
    s = jnp.einsum('bnqhd,bnkhd->bhnqk', qc, kb).astype(jnp.float32) * (DB ** -0.5)
    qq = np.arange(CHUNK)
    kk = np.arange(BAND) - LEFT_CHUNKS * CHUNK
    rel = np.clip(kk[None, :] - qq[:, None], -REL_CLIP, REL_CLIP) + REL_CLIP
    bias = jnp.transpose(rel_table[rel], (2, 0, 1)).astype(jnp.float32)
    valid = (np.arange(nc)[:, None] * CHUNK + kk[None, :]) >= 0
    s = jnp.where(valid[None, None, :, None, :], s + bias[None, :, None], NEG)
    p = jax.nn.softmax(s, axis=-1)
    o = jnp.einsum('bhnqk,bnkhd->bnqhd', p.astype(v.dtype), vb)
    return o.reshape(b_, s_, WIDTH_B)


def moe_ffn(u, w_router, router_bias, wg, wu, wd, sg, su, sd):
    t_ = u.shape[0]
    scores = jax.nn.sigmoid(jnp.dot(u, w_router).astype(jnp.float32))
    biased = scores + router_bias.astype(jnp.float32)
    grp = biased.reshape(t_, N_GROUPS, N_EXPERTS // N_GROUPS)
    grp_score = lax.top_k(grp, 2)[0].sum(-1)
    _, gidx = lax.top_k(grp_score, TOPK_GROUPS)
    gmask = jax.nn.one_hot(gidx, N_GROUPS, dtype=jnp.float32).sum(1) > 0
    emask = jnp.repeat(gmask, N_EXPERTS // N_GROUPS, axis=1)
    _, eidx = lax.top_k(jnp.where(emask, biased, -jnp.inf), TOP_K)
    w = jnp.take_along_axis(scores, eidx, axis=1)
    w = w / jnp.sum(w, axis=-1, keepdims=True) * ROUTED_SCALE
    a_ = t_ * TOP_K
    nb = -(-(a_ + N_EXPERTS * (MOE_BLOCK - 1)) // MOE_BLOCK)
    p_ = nb * MOE_BLOCK
    flat_e = eidx.reshape(a_)
    flat_tok = jnp.repeat(jnp.arange(t_, dtype=jnp.int32), TOP_K)
    flat_w = w.reshape(a_).astype(u.dtype)
    order = jnp.argsort(flat_e)
    se = flat_e[order]
    counts = jnp.bincount(flat_e, length=N_EXPERTS)
    starts = jnp.cumsum(counts) - counts
    pcounts = (counts + MOE_BLOCK - 1) // MOE_BLOCK * MOE_BLOCK
    pends = jnp.cumsum(pcounts)
    pstarts = pends - pcounts
    dest = pstarts[se] + jnp.arange(a_, dtype=jnp.int32) - starts[se]
    slot_tok = jnp.zeros((p_,), jnp.int32).at[dest].set(flat_tok[order])
    slot_w = jnp.zeros((p_,), u.dtype).at[dest].set(flat_w[order])
    block_e = jnp.minimum(jnp.searchsorted(pends, jnp.arange(nb) * MOE_BLOCK, side='right'),
                          N_EXPERTS - 1)

    def expert_block(args):
        tok, e = args
        xb = u[tok]
        hb = jax.nn.silu(xb @ wg[e]) * (xb @ wu[e])
        return hb @ wd[e]

    out = lax.map(expert_block, (slot_tok.reshape(nb, MOE_BLOCK), block_e))
    routed = jax.ops.segment_sum(out.reshape(p_, -1) * slot_w[:, None], slot_tok, num_segments=t_)
    shared = (jax.nn.silu(u @ sg) * (u @ su)) @ sd
    return routed + shared


def setup_inputs(seed: int = 0) -> dict:
    key = jax.random.key(seed)
    ks = jax.random.split(key, 24)
    f32 = jnp.float32
    D = D_MODEL

    def nrm(k, shape, scale):
        return jax.random.normal(k, shape, f32) * scale

    return {
        "x": nrm(ks[0], (BATCH, SEQ, D), 1.0),
        "c": nrm(ks[1], (BATCH, D), 1.0),
        "w_ada": nrm(ks[2], (DEPTH, D, N_MOD * D), 0.5 * D ** -0.5),
        "b_ada": nrm(ks[3], (DEPTH, N_MOD * D), 0.02),
        "g_attn": 1.0 + nrm(ks[4], (DEPTH, D), 0.02),
        "w_in": nrm(ks[5], (DEPTH, D, IN_COLS), D ** -0.5),
        "lambda_qk": nrm(ks[6], (DEPTH, 4, DA_QK), 0.1),
        "t5_bias": nrm(ks[7], (T5_BUCKETS, HA), 0.5),
        "rel_bias_b": nrm(ks[8], (DEPTH, 2 * REL_CLIP + 1, HB), 0.5),
        "w_up_a": nrm(ks[9], (DEPTH, WIDTH_A, D), WIDTH_A ** -0.5),
        "w_up_b": nrm(ks[10], (DEPTH, WIDTH_B, D), WIDTH_B ** -0.5),
        "w_o": nrm(ks[11], (DEPTH, D, D), D ** -0.5),
        "g_moe": 1.0 + nrm(ks[12], (DEPTH, D), 0.02),
        "w_router": nrm(ks[13], (DEPTH, D, N_EXPERTS), D ** -0.5),
        "router_bias": nrm(ks[14], (DEPTH, N_EXPERTS), 0.01),
        "w_exp_gate": nrm(ks[15], (DEPTH, N_EXPERTS, D, D_EXPERT), D ** -0.5),
        "w_exp_up": nrm(ks[16], (DEPTH, N_EXPERTS, D, D_EXPERT), D ** -0.5),
        "w_exp_down": nrm(ks[17], (DEPTH, N_EXPERTS, D_EXPERT, D), D_EXPERT ** -0.5),
        "w_sh_gate": nrm(ks[18], (DEPTH, D, D_SHARED), D ** -0.5),
        "w_sh_up": nrm(ks[19], (DEPTH, D, D_SHARED), D ** -0.5),
        "w_sh_down": nrm(ks[20], (DEPTH, D_SHARED, D), D_SHARED ** -0.5),
        "g_final": 1.0 + nrm(ks[21], (D,), 0.02),
    }


def reference(x, c, w_ada, b_ada, g_attn, w_in, lambda_qk, t5_bias, rel_bias_b, w_up_a, w_up_b, w_o,
              g_moe, w_router, router_bias, w_exp_gate, w_exp_up, w_exp_down,
              w_sh_gate, w_sh_up, w_sh_down, g_final):
    b_, s_, d_ = x.shape
    h = x
    for l in range(DEPTH):
        mod = jnp.dot(jax.nn.silu(c), w_ada[l]) + b_ada[l]
        sh_a, sc_a, gt_a, sh_m, sc_m, gt_m = jnp.split(mod[:, None, :], N_MOD, axis=-1)

        u = rmsnorm(h, g_attn[l]) * (1.0 + sc_a) + sh_a
        proj = u @ w_in[l]
        qa, ka, va, qb, kb, vb, gates = jnp.split(proj, IN_SPLITS, axis=-1)
        lam_init = 0.8 - 0.6 * math.exp(-0.3 * l)
        lq = lambda_qk[l].astype(jnp.float32)
        lam = jnp.exp(jnp.sum(lq[0] * lq[1])) - jnp.exp(jnp.sum(lq[2] * lq[3])) + lam_init
        ya = diff_attention(qa.reshape(b_, s_, HA, 2, DA_QK), ka.reshape(b_, s_, HA, 2, DA_QK),
                            va.reshape(b_, s_, HA, DA_V), lam, lam_init, t5_bias)
        yb = chunk_band_attention(qb.reshape(b_, s_, HB, DB), kb.reshape(b_, s_, HB, DB),
                                  vb.reshape(b_, s_, HB, DB), rel_bias_b[l])
        g_a, g_b = jnp.split(jax.nn.sigmoid(gates), 2, axis=-1)
        merged = g_a * (ya @ w_up_a[l]) + g_b * (yb @ w_up_b[l])
        h = h + gt_a * (merged @ w_o[l])

        v = rmsnorm(h, g_moe[l]) * (1.0 + sc_m) + sh_m
        y = moe_ffn(v.reshape(b_ * s_, d_), w_router[l], router_bias[l], w_exp_gate[l], w_exp_up[l],
                    w_exp_down[l], w_sh_gate[l], w_sh_up[l], w_sh_down[l])
        h = h + gt_m * y.reshape(b_, s_, d_)
    return rmsnorm(h, g_final)
```

```python
import functools
import math
from typing import NamedTuple

import numpy as np
import jax
import jax.numpy as jnp
from jax import lax
from jax.experimental import pallas as pl
from jax.experimental.pallas import tpu as pltpu

F32 = jnp.float32
BF16 = jnp.bfloat16
NEG = -1e30
EPS = 1e-6
HEAD_EPS = 1e-5
LANES = 128
VMEM_LIMIT = 56 * 1024 * 1024


class Cfg(NamedTuple):
    batch: int = 8
    seq: int = 2048
    d_model: int = 2048
    chunk: int = 64
    ha: int = 8
    da_qk: int = 64
    hb: int = 8
    db: int = 128
    left_chunks: int = 8
    rel_clip: int = 256
    t5_buckets: int = 32
    t5_max_dist: int = 128
    n_experts: int = 64
    top_k: int = 8
    n_groups: int = 8
    topk_groups: int = 4
    d_expert: int = 512
    d_shared: int = 512
    routed_scale: float = 2.5
    n_mod: int = 6
    lam_init: float = 0.8 - 0.6 * math.exp(-0.3 * 0)
    attn_tile: int = 256
    moe_rows: int = 256
    proj_tm: int = 1024
    proj_tn: int = 1024
    merge_tm: int = 256
    route_tc: int = 512
    final_tm: int = 128
    ada_tn: int = 1024


def _sigmoid(x):
    return 1.0 / (1.0 + jnp.exp(-x))


def _split_bf16(x):
    hi = x.astype(BF16)
    lo = (x - hi.astype(F32)).astype(BF16)
    return hi, lo


def _dot(a, b):
    return jnp.dot(a, b, preferred_element_type=F32)


def _dot_t(a, b):
    return lax.dot_general(a, b, (((1,), (1,)), ((), ())), preferred_element_type=F32)


def _params(n_axes):
    return pltpu.CompilerParams(dimension_semantics=("arbitrary",) * n_axes,
                                vmem_limit_bytes=VMEM_LIMIT)


def _ada_kernel(c_ref, w_ref, b_ref, o_ref):
    c = c_ref[...]
    s_hi, s_lo = _split_bf16(c * _sigmoid(c))
    w_hi, w_lo = _split_bf16(w_ref[...])
    acc = _dot(s_hi, w_hi) + _dot(s_hi, w_lo) + _dot(s_lo, w_hi)
    o_ref[...] = acc + b_ref[...]


def ada_mod(c_pad, w_ada, b_ada, cfg):
    rows, d = c_pad.shape
    n = w_ada.shape[1]
    tn = min(cfg.ada_tn, n)
    return pl.pallas_call(
        _ada_kernel,
        grid=(n // tn,),
        in_specs=[pl.BlockSpec((rows, d), lambda j: (0, 0)),
                  pl.BlockSpec((d, tn), lambda j: (0, j)),
                  pl.BlockSpec((1, tn), lambda j: (0, j))],
        out_specs=pl.BlockSpec((rows, tn), lambda j: (0, j)),
        out_shape=jax.ShapeDtypeStruct((rows, n), F32),
        compiler_params=_params(1),
        name="ada_mod",
    )(c_pad, w_ada, b_ada.reshape(1, n))


def _inproj_kernel(x_ref, g_ref, sc_ref, sh_ref, w_ref, cs_ref, o_ref, u_ref):
    @pl.when(pl.program_id(1) == 0)
    def _():
        x = x_ref[...]
        y = x * lax.rsqrt(jnp.mean(x * x, axis=-1, keepdims=True) + EPS) * g_ref[...]
        u_ref[...] = (y * (1.0 + sc_ref[0]) + sh_ref[0]).astype(BF16)

    acc = _dot(u_ref[...], w_ref[...])
    o_ref[...] = (acc * cs_ref[...]).astype(BF16)


def in_proj(x2, g, sc, sh, w_bf, colscale, cfg):
    t, d = x2.shape
    n = w_bf.shape[1]
    tm = min(cfg.proj_tm, cfg.seq)
    tn = min(cfg.proj_tn, n)
    per_b = cfg.seq // tm
    return pl.pallas_call(
        _inproj_kernel,
        grid=(t // tm, n // tn),
        in_specs=[pl.BlockSpec((tm, d), lambda i, j: (i, 0)),
                  pl.BlockSpec((1, d), lambda i, j: (0, 0)),
                  pl.BlockSpec((1, 1, d), lambda i, j: (i // per_b, 0, 0)),
                  pl.BlockSpec((1, 1, d), lambda i, j: (i // per_b, 0, 0)),
                  pl.BlockSpec((d, tn), lambda i, j: (0, j)),
                  pl.BlockSpec((1, tn), lambda i, j: (0, j))],
        out_specs=pl.BlockSpec((tm, tn), lambda i, j: (i, j)),
        out_shape=jax.ShapeDtypeStruct((t, n), BF16),
        scratch_shapes=[pltpu.VMEM((tm, d), BF16)],
        compiler_params=_params(2),
        name="in_proj",
    )(x2, g, sc, sh, w_bf, colscale)


def _attn_a_kernel(lam_ref, q_ref, k_ref, v_ref, tb_ref, o_ref,
                   s0_ref, s1_ref, m0_ref, m1_ref, l0_ref, l1_ref, acc_ref, *, tile, half, out_scale):
    qi = pl.program_id(2)
    nk = qi + 1
    q = q_ref[...]
    lane = lax.broadcasted_iota(jnp.int32, q.shape, 1)
    zero = jnp.zeros_like(q)
    q0 = jnp.where(lane < half, q, zero)
    q1 = jnp.where(lane >= half, q, zero)

    m0_ref[...] = jnp.full(m0_ref.shape, -jnp.inf, F32)
    m1_ref[...] = jnp.full(m1_ref.shape, -jnp.inf, F32)
    l0_ref[...] = jnp.zeros(l0_ref.shape, F32)
    l1_ref[...] = jnp.zeros(l1_ref.shape, F32)
    acc_ref[...] = jnp.zeros(acc_ref.shape, F32)

    def fold(x):
        r = x[:, :LANES]
        for c in range(1, tile // LANES):
            r = r + x[:, c * LANES:(c + 1) * LANES]
        return r

    def foldmax(x):
        r = x[:, :LANES]
        for c in range(1, tile // LANES):
            r = jnp.maximum(r, x[:, c * LANES:(c + 1) * LANES])
        return r

    def scores(kb, carry):
        kblk = k_ref[pl.ds(pl.multiple_of(kb * tile, tile), tile), :]
        bias = tb_ref[0, jnp.minimum(qi - kb, 2)]
        s0 = _dot_t(q0, kblk) + bias
        s1 = _dot_t(q1, kblk) + bias
        s0_ref[kb] = s0
        s1_ref[kb] = s1
        m0_ref[...] = jnp.maximum(m0_ref[...], foldmax(s0))
        m1_ref[...] = jnp.maximum(m1_ref[...], foldmax(s1))
        return carry

    lax.fori_loop(0, nk, scores, 0)
    m0 = jnp.max(m0_ref[...], axis=-1, keepdims=True)
    m1 = jnp.max(m1_ref[...], axis=-1, keepdims=True)

    def expo(kb, carry):
        p0 = jnp.exp(s0_ref[kb] - m0)
        p1 = jnp.exp(s1_ref[kb] - m1)
        s0_ref[kb] = p0
        s1_ref[kb] = p1
        l0_ref[...] += fold(p0)
        l1_ref[...] += fold(p1)
        return carry

    lax.fori_loop(0, nk, expo, 0)
    c0 = 1.0 / jnp.sum(l0_ref[...], axis=-1, keepdims=True)
    c1 = lam_ref[0] / jnp.sum(l1_ref[...], axis=-1, keepdims=True)

    def pv(kb, carry):
        a = s0_ref[kb] * c0 - s1_ref[kb] * c1
        vblk = v_ref[pl.ds(pl.multiple_of(kb * tile, tile), tile), :]
        acc_ref[...] += _dot(a.astype(BF16), vblk)
        return carry

    lax.fori_loop(0, nk, pv, 0)
    o = acc_ref[...]
    o = o * lax.rsqrt(jnp.mean(o * o, axis=-1, keepdims=True) + HEAD_EPS) * out_scale
    o_ref[...] = o.astype(BF16)


def attn_a(proj, lam, tb, cfg):
    t = proj.shape[0]
    tile = cfg.attn_tile
    s = cfg.seq
    nq = s // tile
    dv = 2 * cfg.da_qk
    assert dv == LANES
    kcol = cfg.ha
    vcol = 2 * cfg.ha
    kern = functools.partial(_attn_a_kernel, tile=tile, half=cfg.da_qk, out_scale=1.0 - cfg.lam_init)
    return pl.pallas_call(
        kern,
        grid=(cfg.batch, cfg.ha, nq),
        in_specs=[pl.BlockSpec(memory_space=pltpu.SMEM),
                  pl.BlockSpec((tile, LANES), lambda b, h, i: (b * nq + i, h)),
                  pl.BlockSpec((s, LANES), lambda b, h, i: (b, kcol + h)),
                  pl.BlockSpec((s, LANES), lambda b, h, i: (b, vcol + h)),
                  pl.BlockSpec((1, 3, tile, tile), lambda b, h, i: (h, 0, 0, 0))],
        out_specs=pl.BlockSpec((tile, LANES), lambda b, h, i: (b * nq + i, h)),
        out_shape=jax.ShapeDtypeStruct((t, cfg.ha * dv), BF16),
        scratch_shapes=[pltpu.VMEM((nq, tile, tile), F32), pltpu.VMEM((nq, tile, tile), F32),
                        pltpu.VMEM((tile, LANES), F32), pltpu.VMEM((tile, LANES), F32),
                        pltpu.VMEM((tile, LANES), F32), pltpu.VMEM((tile, LANES), F32),
                        pltpu.VMEM((tile, LANES), F32)],
        compiler_params=_params(3),
        name="attn_a",
    )(lam, proj, proj, proj, tb)


def _attn_b_kernel(q_ref, k_ref, v_ref, tb_ref, o_ref, *, tile, nwin):
    qi = pl.program_id(2)
    q = q_ref[...]
    ss, vs = [], []
    for j in range(nwin):
        kb = qi - (nwin - 1) + j
        start = pl.multiple_of(jnp.maximum(kb, 0) * tile, tile)
        s = _dot_t(q, k_ref[pl.ds(start, tile), :]) + tb_ref[0, j]
        ss.append(jnp.where(kb >= 0, s, NEG))
        vs.append(v_ref[pl.ds(start, tile), :])
    m = jnp.max(ss[0], axis=-1, keepdims=True)
    for s in ss[1:]:
        m = jnp.maximum(m, jnp.max(s, axis=-1, keepdims=True))
    ps = [jnp.exp(s - m) for s in ss]
    l = jnp.sum(ps[0], axis=-1, keepdims=True)
    for p in ps[1:]:
        l = l + jnp.sum(p, axis=-1, keepdims=True)
    inv = 1.0 / l
    acc = _dot((ps[0] * inv).astype(BF16), vs[0])
    for p, v in zip(ps[1:], vs[1:]):
        acc = acc + _dot((p * inv).astype(BF16), v)
    o_ref[...] = acc.astype(BF16)


def attn_b(proj, tb, cfg):
    t = proj.shape[0]
    tile = cfg.attn_tile
    s = cfg.seq
    nq = s // tile
    assert cfg.db == LANES
    nwin = (cfg.left_chunks * cfg.chunk) // tile + 1
    base = (2 * cfg.ha * 2 * cfg.da_qk + cfg.ha * 2 * cfg.da_qk) // LANES
    kern = functools.partial(_attn_b_kernel, tile=tile, nwin=nwin)
    return pl.pallas_call(
        kern,
        grid=(cfg.batch, cfg.hb, nq),
        in_specs=[pl.BlockSpec((tile, LANES), lambda b, h, i: (b * nq + i, base + h)),
                  pl.BlockSpec((s, LANES), lambda b, h, i: (b, base + cfg.hb + h)),
                  pl.BlockSpec((s, LANES), lambda b, h, i: (b, base + 2 * cfg.hb + h)),
                  pl.BlockSpec((1, nwin, tile, tile), lambda b, h, i: (h, 0, 0, 0))],
        out_specs=pl.BlockSpec((tile, LANES), lambda b, h, i: (b * nq + i, h)),
        out_shape=jax.ShapeDtypeStruct((t, cfg.hb * cfg.db), BF16),
        compiler_params=_params(3),
        name="attn_b",
    )(proj, proj, proj, tb)


def _merge_kernel(ya_ref, yb_ref, ga_ref, gb_ref, x_ref, gta_ref, scm_ref, shm_ref, gmoe_ref,
                  wua_ref, wub_ref, wo_ref, wrh_ref, wrl_ref, h1_ref, v_ref, lg_ref):
    a = _dot(ya_ref[...], wua_ref[...])
    b = _dot(yb_ref[...], wub_ref[...])
    merged = _sigmoid(ga_ref[...].astype(F32)) * a + _sigmoid(gb_ref[...].astype(F32)) * b
    att = _dot(merged.astype(BF16), wo_ref[...])
    h1 = x_ref[...] + gta_ref[0] * att
    h1_ref[...] = h1
    y = h1 * lax.rsqrt(jnp.mean(h1 * h1, axis=-1, keepdims=True) + EPS) * gmoe_ref[...]
    v = y * (1.0 + scm_ref[0]) + shm_ref[0]
    v_ref[...] = v
    v_hi, v_lo = _split_bf16(v)
    wrh = wrh_ref[...]
    lg_ref[...] = _dot_t(wrh, v_hi) + _dot_t(wrh, v_lo) + _dot_t(wrl_ref[...], v_hi)


def merge(ya, yb, proj, x2, gta, scm, shm, gmoe, wua, wub, wo, wrh, wrl, cfg):
    t, d = x2.shape
    tm = min(cfg.merge_tm, cfg.seq)
    per_b = cfg.seq // tm
    wa = ya.shape[1]
    wb = yb.shape[1]
    gate0 = (proj.shape[1] - 2 * d) // d
    ne = wrh.shape[0]
    const = lambda i: (0, 0)
    row3 = lambda i: (i // per_b, 0, 0)
    return pl.pallas_call(
        _merge_kernel,
        grid=(t // tm,),
        in_specs=[pl.BlockSpec((tm, wa), lambda i: (i, 0)),
                  pl.BlockSpec((tm, wb), lambda i: (i, 0)),
                  pl.BlockSpec((tm, d), lambda i: (i, gate0)),
                  pl.BlockSpec((tm, d), lambda i: (i, gate0 + 1)),
                  pl.BlockSpec((tm, d), lambda i: (i, 0)),
                  pl.BlockSpec((1, 1, d), row3),
                  pl.BlockSpec((1, 1, d), row3),
                  pl.BlockSpec((1, 1, d), row3),
                  pl.BlockSpec((1, d), const),
                  pl.BlockSpec((wa, d), const),
                  pl.BlockSpec((wb, d), const),
                  pl.BlockSpec((d, d), const),
                  pl.BlockSpec((ne, d), const),
                  pl.BlockSpec((ne, d), const)],
        out_specs=[pl.BlockSpec((tm, d), lambda i: (i, 0)),
                   pl.BlockSpec((tm, d), lambda i: (i, 0)),
                   pl.BlockSpec((ne, tm), lambda i: (0, i))],
        out_shape=[jax.ShapeDtypeStruct((t, d), F32),
                   jax.ShapeDtypeStruct((t, d), F32),
                   jax.ShapeDtypeStruct((ne, t), F32)],
        compiler_params=_params(1),
        name="merge",
    )(ya, yb, proj, proj, x2, gta, scm, shm, gmoe, wua, wub, wo, wrh, wrl)


def _route_kernel(lg_ref, rb_ref, eidx_ref, w_ref, *, n_exp, n_groups, topk_groups, top_k, scale):
    gsz = n_exp // n_groups
    tc = lg_ref.shape[1]
    ninf = -jnp.inf
    row = lax.broadcasted_iota(jnp.int32, (gsz, tc), 0).astype(F32)
    sc, bi = [], []
    for g in range(n_groups):
        s = _sigmoid(lg_ref[g * gsz:(g + 1) * gsz, :])
        sc.append(s)
        bi.append(s + rb_ref[g * gsz:(g + 1) * gsz, :])
    gs = []
    for x in bi:
        m1 = jnp.max(x, axis=0, keepdims=True)
        eq = x == m1
        cnt = jnp.sum(jnp.where(eq, 1.0, 0.0), axis=0, keepdims=True)
        second = jnp.max(jnp.where(eq, ninf, x), axis=0, keepdims=True)
        gs.append(m1 + jnp.where(cnt >= 2.0, m1, second))
    gmat = jnp.concatenate(gs, axis=0)
    grow = lax.broadcasted_iota(jnp.int32, gmat.shape, 0).astype(F32)
    gsel = jnp.zeros(gmat.shape, F32)
    for _ in range(topk_groups):
        m = jnp.max(gmat, axis=0, keepdims=True)
        idx = jnp.min(jnp.where(gmat == m, grow, float(n_groups)), axis=0, keepdims=True)
        hit = grow == idx
        gsel = jnp.where(hit, 1.0, gsel)
        gmat = jnp.where(hit, ninf, gmat)
    cur = [jnp.where(gsel[g:g + 1, :] > 0.5, bi[g], ninf) for g in range(n_groups)]
    ids = [row + float(g * gsz) for g in range(n_groups)]
    e_rows, w_rows = [], []
    for _ in range(top_k):
        mg = cur[0]
        for c in cur[1:]:
            mg = jnp.maximum(mg, c)
        m = jnp.max(mg, axis=0, keepdims=True)
        cand = jnp.where(cur[0] == m, ids[0], float(n_exp))
        for c, i in zip(cur[1:], ids[1:]):
            cand = jnp.minimum(cand, jnp.where(c == m, i, float(n_exp)))
        idx = jnp.min(cand, axis=0, keepdims=True)
        wsum = jnp.zeros((gsz, tc), F32)
        for g in range(n_groups):
            hit = ids[g] == idx
            wsum = wsum + jnp.where(hit, sc[g], 0.0)
            cur[g] = jnp.where(hit, ninf, cur[g])
        e_rows.append(idx)
        w_rows.append(jnp.sum(wsum, axis=0, keepdims=True))
    w = jnp.concatenate(w_rows, axis=0)
    w = w / jnp.sum(w, axis=0, keepdims=True) * scale
    eidx_ref[...] = jnp.concatenate(e_rows, axis=0).astype(jnp.int32)
    w_ref[...] = w


def route(lg, rb, cfg):
    ne, t = lg.shape
    tc = min(cfg.route_tc, t)
    kern = functools.partial(_route_kernel, n_exp=cfg.n_experts, n_groups=cfg.n_groups,
                             topk_groups=cfg.topk_groups, top_k=cfg.top_k, scale=cfg.routed_scale)
    return pl.pallas_call(
        kern,
        grid=(t // tc,),
        in_specs=[pl.BlockSpec((ne, tc), lambda i: (0, i)),
                  pl.BlockSpec((ne, tc), lambda i: (0, 0))],
        out_specs=[pl.BlockSpec((cfg.top_k, tc), lambda i: (0, i)),
                   pl.BlockSpec((cfg.top_k, tc), lambda i: (0, i))],
        out_shape=[jax.ShapeDtypeStruct((cfg.top_k, t), jnp.int32),
                   jax.ShapeDtypeStruct((cfg.top_k, t), F32)],
        compiler_params=_params(1),
        name="route",
    )(lg, rb)


def _moe_kernel(be_ref, tokc_ref, tokn_ref, dst_ref, v_hbm, wg_ref, wu_ref, wd_ref, yk_hbm,
                xbuf, ybuf, wgb, wub, wdb, gsem, ssem, *, bm):
    b = pl.program_id(0)
    nb = pl.num_programs(0)
    slot = lax.rem(b, 2)

    def gather_copy(t, r, s):
        return pltpu.make_async_copy(v_hbm.at[pl.ds(t, 1), :], xbuf.at[s, pl.ds(r, 1), :], gsem.at[s])

    def scatter_copy(r, d, s):
        return pltpu.make_async_copy(ybuf.at[s, pl.ds(r, 1), :], yk_hbm.at[pl.ds(d, 1), :], ssem.at[s])

    def wait_gather(s):
        pltpu.make_async_copy(v_hbm.at[pl.ds(0, bm), :], xbuf.at[s], gsem.at[s]).wait()

    def wait_scatter(s):
        pltpu.make_async_copy(ybuf.at[s], yk_hbm.at[pl.ds(0, bm), :], ssem.at[s]).wait()

    @pl.when(b == 0)
    def _():
        for r in range(bm):
            gather_copy(tokc_ref[0, 0, r], r, 0).start()

    @pl.when(b + 1 < nb)
    def _():
        for r in range(bm):
            gather_copy(tokn_ref[0, 0, r], r, 1 - slot).start()

    @pl.when(b >= 2)
    def _():
        wait_scatter(slot)

    e = be_ref[b]
    eprev = be_ref[jnp.maximum(b - 1, 0)]

    @pl.when(jnp.logical_or(b == 0, e != eprev))
    def _():
        wgb[...] = wg_ref[0].astype(BF16)
        wub[...] = wu_ref[0].astype(BF16)
        wdb[...] = wd_ref[0].astype(BF16)

    wait_gather(slot)
    x = xbuf[slot].astype(BF16)
    g = _dot(x, wgb[...])
    u = _dot(x, wub[...])
    h = (g * _sigmoid(g)) * u
    ybuf[slot] = _dot(h.astype(BF16), wdb[...])
    for r in range(bm):
        scatter_copy(r, dst_ref[0, 0, r], slot).start()

    @pl.when(b == nb - 1)
    def _():
        wait_scatter(slot)

        @pl.when(nb >= 2)
        def _():
            wait_scatter(1 - slot)


def moe(v, block_e, slot_tok, slot_dst, wg, wu, wd, n_rows_out, cfg):
    t, d = v.shape
    bm = cfg.moe_rows
    nb = block_e.shape[0]
    de = wg.shape[2]
    tok3 = slot_tok.reshape(nb, 1, bm)
    dst3 = slot_dst.reshape(nb, 1, bm)
    kern = functools.partial(_moe_kernel, bm=bm)
    smem_blk = lambda f: pl.BlockSpec((1, 1, bm), f, memory_space=pltpu.SMEM)
    grid_spec = pltpu.PrefetchScalarGridSpec(
        num_scalar_prefetch=1,
        grid=(nb,),
        in_specs=[smem_blk(lambda b, be: (b, 0, 0)),
                  smem_blk(lambda b, be: (jnp.minimum(b + 1, nb - 1), 0, 0)),
                  smem_blk(lambda b, be: (b, 0, 0)),
                  pl.BlockSpec(memory_space=pl.ANY),
                  pl.BlockSpec((1, d, de), lambda b, be: (be[b], 0, 0)),
                  pl.BlockSpec((1, d, de), lambda b, be: (be[b], 0, 0)),
                  pl.BlockSpec((1, de, d), lambda b, be: (be[b], 0, 0))],
        out_specs=pl.BlockSpec(memory_space=pl.ANY),
        scratch_shapes=[pltpu.VMEM((2, bm, d), F32), pltpu.VMEM((2, bm, d), F32),
                        pltpu.VMEM((d, de), BF16), pltpu.VMEM((d, de), BF16), pltpu.VMEM((de, d), BF16),
                        pltpu.SemaphoreType.DMA((2,)), pltpu.SemaphoreType.DMA((2,))],
    )
    return pl.pallas_call(
        kern,
        grid_spec=grid_spec,
        out_shape=jax.ShapeDtypeStruct((n_rows_out, d), F32),
        compiler_params=_params(1),
        name="moe",
    )(block_e, tok3, tok3, dst3, v, wg, wu, wd)


def _final_kernel(v_ref, h1_ref, yk_ref, w_ref, gtm_ref, gfin_ref, sg_ref, su_ref, sd_ref, o_ref, *, top_k):
    d = v_ref.shape[1]
    vb = v_ref[...].astype(BF16)
    g = _dot(vb, sg_ref[...])
    u = _dot(vb, su_ref[...])
    y = _dot(((g * _sigmoid(g)) * u).astype(BF16), sd_ref[...])
    for k in range(top_k):
        y = y + w_ref[:, k:k + 1] * yk_ref[:, k * d:(k + 1) * d]
    h2 = h1_ref[...] + gtm_ref[0] * y
    o_ref[...] = h2 * lax.rsqrt(jnp.mean(h2 * h2, axis=-1, keepdims=True) + EPS) * gfin_ref[...]


def final(v, h1, yk2, w_tk, gtm, gfin, sg, su, sd, cfg):
    t, d = v.shape
    tm = min(cfg.final_tm, cfg.seq)
    per_b = cfg.seq // tm
    ds = sg.shape[1]
    const = lambda i: (0, 0)
    kern = functools.partial(_final_kernel, top_k=cfg.top_k)
    return pl.pallas_call(
        kern,
        grid=(t // tm,),
        in_specs=[pl.BlockSpec((tm, d), lambda i: (i, 0)),
                  pl.BlockSpec((tm, d), lambda i: (i, 0)),
                  pl.BlockSpec((tm, cfg.top_k * d), lambda i: (i, 0)),
                  pl.BlockSpec((tm, cfg.top_k), lambda i: (i, 0)),
                  pl.BlockSpec((1, 1, d), lambda i: (i // per_b, 0, 0)),
                  pl.BlockSpec((1, d), const),
                  pl.BlockSpec((d, ds), const),
                  pl.BlockSpec((d, ds), const),
                  pl.BlockSpec((ds, d), const)],
        out_specs=pl.BlockSpec((tm, d), lambda i: (i, 0)),
        out_shape=jax.ShapeDtypeStruct((t, d), F32),
        compiler_params=_params(1),
        name="final",
    )(v, h1, yk2, w_tk, gtm, gfin, sg, su, sd)


def _t5_bucket(rel, cfg):
    nb = cfg.t5_buckets // 2
    ret = (rel > 0).astype(np.int32) * nb
    n = np.abs(rel)
    max_exact = nb // 2
    large = max_exact + (np.log(np.maximum(n, 1) / max_exact) / math.log(cfg.t5_max_dist / max_exact)
                         * (nb - max_exact)).astype(np.int32)
    large = np.minimum(large, nb - 1)
    return (ret + np.where(n < max_exact, n, large)).astype(np.int32)


def _bias_tables_a(t5_bias, cfg):
    tile = cfg.attn_tile
    assert tile + 1 >= cfg.t5_max_dist and tile % cfg.chunk == 0
    ql = np.arange(tile)[:, None]
    kl = np.arange(tile)[None, :]
    tabs = []
    for dlt in range(3):
        rel = kl - ql - dlt * tile
        tab = jnp.transpose(t5_bias[_t5_bucket(rel, cfg)], (2, 0, 1)).astype(F32)
        if dlt == 0:
            allowed = (kl // cfg.chunk) <= (ql // cfg.chunk)
            tab = jnp.where(allowed[None], tab, NEG)
        tabs.append(tab)
    return jnp.stack(tabs, axis=1)


def _bias_tables_b(rel_table, cfg):
    tile = cfg.attn_tile
    assert (cfg.left_chunks * cfg.chunk) % tile == 0 and tile % cfg.chunk == 0
    nwin = (cfg.left_chunks * cfg.chunk) // tile + 1
    ql = np.arange(tile)[:, None]
    kl = np.arange(tile)[None, :]
    tabs = []
    for j in range(nwin):
        off = (j - (nwin - 1)) * tile
        rel = np.clip(kl - ql + off, -cfg.rel_clip, cfg.rel_clip) + cfg.rel_clip
        dchunk = ql // cfg.chunk - (kl + off) // cfg.chunk
        band = (dchunk >= 0) & (dchunk <= cfg.left_chunks)
        tab = jnp.transpose(rel_table[rel], (2, 0, 1)).astype(F32)
        tabs.append(jnp.where(band[None], tab, NEG))
    return jnp.stack(tabs, axis=1)


def _dispatch_tables(eidx, cfg):
    t_, k_ = eidx.shape
    e_ = cfg.n_experts
    bm = cfg.moe_rows
    a_ = t_ * k_
    nb = -(-(a_ + e_ * (bm - 1)) // bm)
    p_ = nb * bm
    flat_e = eidx.reshape(a_)
    order = jnp.argsort(flat_e).astype(jnp.int32)
    se = flat_e[order]
    counts = jnp.bincount(flat_e, length=e_).astype(jnp.int32)
    starts = jnp.cumsum(counts) - counts
    pcounts = (counts + bm - 1) // bm * bm
    pends = jnp.cumsum(pcounts)
    pstarts = pends - pcounts
    dest = pstarts[se] + jnp.arange(a_, dtype=jnp.int32) - starts[se]
    slot_tok = jnp.zeros((p_,), jnp.int32).at[dest].set(order // k_)
    dummy = a_ + jnp.arange(p_, dtype=jnp.int32) % (2 * bm)
    slot_dst = dummy.at[dest].set(order)
    block_e = jnp.minimum(jnp.searchsorted(pends, jnp.arange(nb, dtype=jnp.int32) * bm, side='right'),
                          e_ - 1).astype(jnp.int32)
    return block_e, slot_tok, slot_dst, a_ + 2 * bm


def _forward(cfg, x, c, w_ada, b_ada, g_attn, w_in, lambda_qk, t5_bias, rel_bias_b, w_up_a, w_up_b, w_o,
             g_moe, w_router, router_bias, w_exp_gate, w_exp_up, w_exp_down,
             w_sh_gate, w_sh_up, w_sh_down, g_final):
    b_, s_, d_ = x.shape
    t_ = b_ * s_
    l = 0
    qk_a = cfg.ha * 2 * cfg.da_qk
    width_a = cfg.ha * 2 * cfg.da_qk
    width_b = cfg.hb * cfg.db
    in_cols = w_in.shape[2]

    rows = max(16, b_)
    c_pad = jnp.zeros((rows, d_), F32).at[:b_].set(c)
    mod = ada_mod(c_pad, w_ada[l], b_ada[l], cfg)[:b_]
    sh_a, sc_a, gt_a, sh_m, sc_m, gt_m = [m.reshape(b_, 1, d_) for m in jnp.split(mod, cfg.n_mod, axis=-1)]

    colscale = np.ones((1, in_cols), np.float32)
    colscale[0, :qk_a] = cfg.da_qk ** -0.5
    qb0 = 2 * qk_a + width_a
    colscale[0, qb0:qb0 + width_b] = cfg.db ** -0.5
    x2 = x.reshape(t_, d_)
    proj = in_proj(x2, g_attn[l].reshape(1, d_), sc_a, sh_a, w_in[l].astype(BF16), jnp.asarray(colscale), cfg)

    lq = lambda_qk[l].astype(F32)
    lam = jnp.exp(jnp.sum(lq[0] * lq[1])) - jnp.exp(jnp.sum(lq[2] * lq[3])) + cfg.lam_init
    ya = attn_a(proj, lam.reshape(1), _bias_tables_a(t5_bias, cfg), cfg)
    yb = attn_b(proj, _bias_tables_b(rel_bias_b[l], cfg), cfg)

    ne_pad = max(LANES, cfg.n_experts)
    wrt = jnp.zeros((ne_pad, d_), F32).at[:cfg.n_experts].set(w_router[l].T)
    wrh, wrl = _split_bf16(wrt)
    h1, v, lg = merge(ya, yb, proj, x2, gt_a, sc_m, sh_m, g_moe[l].reshape(1, d_),
                      w_up_a[l].astype(BF16), w_up_b[l].astype(BF16), w_o[l].astype(BF16), wrh, wrl, cfg)

    tc = min(cfg.route_tc, t_)
    rb = jnp.zeros((ne_pad, tc), F32).at[:cfg.n_experts].set(
        jnp.broadcast_to(router_bias[l].astype(F32)[:, None], (cfg.n_experts, tc)))
    eidx_kt, w_kt = route(lg, rb, cfg)
    block_e, slot_tok, slot_dst, n_rows_out = _dispatch_tables(eidx_kt.T, cfg)

    yk = moe(v, block_e, slot_tok, slot_dst, w_exp_gate[l], w_exp_up[l], w_exp_down[l], n_rows_out, cfg)
    yk2 = yk.reshape(n_rows_out // cfg.top_k, cfg.top_k * d_)
    out = final(v, h1, yk2, w_kt.T, gt_m, g_final.reshape(1, d_),
                w_sh_gate[l].astype(BF16), w_sh_up[l].astype(BF16), w_sh_down[l].astype(BF16), cfg)
    return out.reshape(b_, s_, d_)


def kernel(x, c, w_ada, b_ada, g_attn, w_in, lambda_qk, t5_bias, rel_bias_b, w_up_a, w_up_b, w_o, g_moe,
           w_router, router_bias, w_exp_gate, w_exp_up, w_exp_down, w_sh_gate, w_sh_up, w_sh_down, g_final):
    return _forward(Cfg(), x, c, w_ada, b_ada, g_attn, w_in, lambda_qk, t5_bias, rel_bias_b, w_up_a, w_up_b,
                    w_o, g_moe, w_router, router_bias, w_exp_gate, w_exp_up, w_exp_down,
                    w_sh_gate, w_sh_up, w_sh_down, g_final)
```

```python
import functools
import math
from typing import NamedTuple

import numpy as np
import jax
import jax.numpy as jnp
from jax import lax
from jax.experimental import pallas as pl
from jax.experimental.pallas import tpu as pltpu

F32 = jnp.float32
BF16 = jnp.bfloat16
NEG = -1e30
EPS = 1e-6
HEAD_EPS = 1e-5
LANES = 128
VMEM_LIMIT = 56 * 1024 * 1024


class Cfg(NamedTuple):
    batch: int = 8
    seq: int = 2048
    d_model: int = 2048
    chunk: int = 64
    ha: int = 8
    da_qk: int = 64
    hb: int = 8
    db: int = 128
    left_chunks: int = 8
    rel_clip: int = 256
    t5_buckets: int = 32
    t5_max_dist: int = 128
    n_experts: int = 64
    top_k: int = 8
    n_groups: int = 8
    topk_groups: int = 4
    d_expert: int = 512
    d_shared: int = 512
    routed_scale: float = 2.5
    n_mod: int = 6
    lam_init: float = 0.8 - 0.6 * math.exp(-0.3 * 0)
    attn_tile: int = 256
    moe_rows: int = 256
    proj_tm: int = 1024
    proj_tn: int = 1024
    merge_tm: int = 256
    route_tc: int = 512
    final_tm: int = 128
    ada_tn: int = 1024


def _sigmoid(x):
    return 1.0 / (1.0 + jnp.exp(-x))


def _split_bf16(x):
    hi = x.astype(BF16)
    lo = (x - hi.astype(F32)).astype(BF16)
    return hi, lo


def _dot(a, b):
    return jnp.dot(a, b, preferred_element_type=F32)


def _dot_t(a, b):
    return lax.dot_general(a, b, (((1,), (1,)), ((), ())), preferred_element_type=F32)


def _params(n_axes):
    return pltpu.CompilerParams(dimension_semantics=("arbitrary",) * n_axes,
                                vmem_limit_bytes=VMEM_LIMIT)


def _ada_kernel(c_ref, w_ref, b_ref, o_ref):
    c = c_ref[...]
    s_hi, s_lo = _split_bf16(c * _sigmoid(c))
    w_hi, w_lo = _split_bf16(w_ref[...])
    acc = _dot(s_hi, w_hi) + _dot(s_hi, w_lo) + _dot(s_lo, w_hi)
    o_ref[...] = acc + b_ref[...]


def ada_mod(c_pad, w_ada, b_ada, cfg):
    rows, d = c_pad.shape
    n = w_ada.shape[1]
    tn = min(cfg.ada_tn, n)
    return pl.pallas_call(
        _ada_kernel,
        grid=(n // tn,),
        in_specs=[pl.BlockSpec((rows, d), lambda j: (0, 0)),
                  pl.BlockSpec((d, tn), lambda j: (0, j)),
                  pl.BlockSpec((1, tn), lambda j: (0, j))],
        out_specs=pl.BlockSpec((rows, tn), lambda j: (0, j)),
        out_shape=jax.ShapeDtypeStruct((rows, n), F32),
        compiler_params=_params(1),
        name="ada_mod",
    )(c_pad, w_ada, b_ada.reshape(1, n))


def _inproj_kernel(x_ref, g_ref, sc_ref, sh_ref, w_ref, cs_ref, o_ref, u_ref):
    @pl.when(pl.program_id(1) == 0)
    def _():
        x = x_ref[...]
        y = x * lax.rsqrt(jnp.mean(x * x, axis=-1, keepdims=True) + EPS) * g_ref[...]
        u_ref[...] = (y * (1.0 + sc_ref[0]) + sh_ref[0]).astype(BF16)

    acc = _dot(u_ref[...], w_ref[...])
    o_ref[...] = (acc * cs_ref[...]).astype(BF16)


def in_proj(x2, g, sc, sh, w_bf, colscale, cfg):
    t, d = x2.shape
    n = w_bf.shape[1]
    tm = min(cfg.proj_tm, cfg.seq)
    tn = min(cfg.proj_tn, n)
    per_b = cfg.seq // tm
    return pl.pallas_call(
        _inproj_kernel,
        grid=(t // tm, n // tn),
        in_specs=[pl.BlockSpec((tm, d), lambda i, j: (i, 0)),
                  pl.BlockSpec((1, d), lambda i, j: (0, 0)),
                  pl.BlockSpec((1, 1, d), lambda i, j: (i // per_b, 0, 0)),
                  pl.BlockSpec((1, 1, d), lambda i, j: (i // per_b, 0, 0)),
                  pl.BlockSpec((d, tn), lambda i, j: (0, j)),
                  pl.BlockSpec((1, tn), lambda i, j: (0, j))],
        out_specs=pl.BlockSpec((tm, tn), lambda i, j: (i, j)),
        out_shape=jax.ShapeDtypeStruct((t, n), BF16),
        scratch_shapes=[pltpu.VMEM((tm, d), BF16)],
        compiler_params=_params(2),
        name="in_proj",
    )(x2, g, sc, sh, w_bf, colscale)


def _attn_a_kernel(lam_ref, q_ref, k_ref, v_ref, tb_ref, o_ref,
                   s0_ref, s1_ref, m0_ref, m1_ref, l0_ref, l1_ref, acc_ref, *, tile, half, out_scale):
    qi = pl.program_id(2)
    nk = qi + 1
    q = q_ref[...]
    lane = lax.broadcasted_iota(jnp.int32, q.shape, 1)
    zero = jnp.zeros_like(q)
    q0 = jnp.where(lane < half, q, zero)
    q1 = jnp.where(lane >= half, q, zero)

    m0_ref[...] = jnp.full(m0_ref.shape, -jnp.inf, F32)
    m1_ref[...] = jnp.full(m1_ref.shape, -jnp.inf, F32)
    l0_ref[...] = jnp.zeros(l0_ref.shape, F32)
    l1_ref[...] = jnp.zeros(l1_ref.shape, F32)
    acc_ref[...] = jnp.zeros(acc_ref.shape, F32)

    def fold(x):
        r = x[:, :LANES]
        for c in range(1, tile // LANES):
            r = r + x[:, c * LANES:(c + 1) * LANES]
        return r

    def foldmax(x):
        r = x[:, :LANES]
        for c in range(1, tile // LANES):
            r = jnp.maximum(r, x[:, c * LANES:(c + 1) * LANES])
        return r

    def scores(kb, carry):
        kblk = k_ref[pl.ds(pl.multiple_of(kb * tile, tile), tile), :]
        bias = tb_ref[0, jnp.minimum(qi - kb, 2)]
        s0 = _dot_t(q0, kblk) + bias
        s1 = _dot_t(q1, kblk) + bias
        s0_ref[kb] = s0
        s1_ref[kb] = s1
        m0_ref[...] = jnp.maximum(m0_ref[...], foldmax(s0))
        m1_ref[...] = jnp.maximum(m1_ref[...], foldmax(s1))
        return carry

    lax.fori_loop(0, nk, scores, 0)
    m0 = jnp.max(m0_ref[...], axis=-1, keepdims=True)
    m1 = jnp.max(m1_ref[...], axis=-1, keepdims=True)

    def expo(kb, carry):
        p0 = jnp.exp(s0_ref[kb] - m0)
        p1 = jnp.exp(s1_ref[kb] - m1)
        s0_ref[kb] = p0
        s1_ref[kb] = p1
        l0_ref[...] += fold(p0)
        l1_ref[...] += fold(p1)
        return carry

    lax.fori_loop(0, nk, expo, 0)
    c0 = 1.0 / jnp.sum(l0_ref[...], axis=-1, keepdims=True)
    c1 = lam_ref[0] / jnp.sum(l1_ref[...], axis=-1, keepdims=True)

    def pv(kb, carry):
        a = s0_ref[kb] * c0 - s1_ref[kb] * c1
        vblk = v_ref[pl.ds(pl.multiple_of(kb * tile, tile), tile), :]
        acc_ref[...] += _dot(a.astype(BF16), vblk)
        return carry

    lax.fori_loop(0, nk, pv, 0)
    o = acc_ref[...]
    o = o * lax.rsqrt(jnp.mean(o * o, axis=-1, keepdims=True) + HEAD_EPS) * out_scale
    o_ref[...] = o.astype(BF16)


def attn_a(proj, lam, tb, cfg):
    t = proj.shape[0]
    tile = cfg.attn_tile
    s = cfg.seq
    nq = s // tile
    dv = 2 * cfg.da_qk
    assert dv == LANES
    kcol = cfg.ha
    vcol = 2 * cfg.ha
    kern = functools.partial(_attn_a_kernel, tile=tile, half=cfg.da_qk, out_scale=1.0 - cfg.lam_init)
    return pl.pallas_call(
        kern,
        grid=(cfg.batch, cfg.ha, nq),
        in_specs=[pl.BlockSpec(memory_space=pltpu.SMEM),
                  pl.BlockSpec((tile, LANES), lambda b, h, i: (b * nq + i, h)),
                  pl.BlockSpec((s, LANES), lambda b, h, i: (b, kcol + h)),
                  pl.BlockSpec((s, LANES), lambda b, h, i: (b, vcol + h)),
                  pl.BlockSpec((1, 3, tile, tile), lambda b, h, i: (h, 0, 0, 0))],
        out_specs=pl.BlockSpec((tile, LANES), lambda b, h, i: (b * nq + i, h)),
        out_shape=jax.ShapeDtypeStruct((t, cfg.ha * dv), BF16),
        scratch_shapes=[pltpu.VMEM((nq, tile, tile), F32), pltpu.VMEM((nq, tile, tile), F32),
                        pltpu.VMEM((tile, LANES), F32), pltpu.VMEM((tile, LANES), F32),
                        pltpu.VMEM((tile, LANES), F32), pltpu.VMEM((tile, LANES), F32),
                        pltpu.VMEM((tile, LANES), F32)],
        compiler_params=_params(3),
        name="attn_a",
    )(lam, proj, proj, proj, tb)


def _attn_b_kernel(q_ref, k_ref, v_ref, tb_ref, o_ref, *, tile, nwin):
    qi = pl.program_id(2)
    q = q_ref[...]
    ss, vs = [], []
    for j in range(nwin):
        kb = qi - (nwin - 1) + j
        start = pl.multiple_of(jnp.maximum(kb, 0) * tile, tile)
        s = _dot_t(q, k_ref[pl.ds(start, tile), :]) + tb_ref[0, j]
        ss.append(jnp.where(kb >= 0, s, NEG))
        vs.append(v_ref[pl.ds(start, tile), :])
    m = jnp.max(ss[0], axis=-1, keepdims=True)
    for s in ss[1:]:
        m = jnp.maximum(m, jnp.max(s, axis=-1, keepdims=True))
    ps = [jnp.exp(s - m) for s in ss]
    l = jnp.sum(ps[0], axis=-1, keepdims=True)
    for p in ps[1:]:
        l = l + jnp.sum(p, axis=-1, keepdims=True)
    inv = 1.0 / l
    acc = _dot((ps[0] * inv).astype(BF16), vs[0])
    for p, v in zip(ps[1:], vs[1:]):
        acc = acc + _dot((p * inv).astype(BF16), v)
    o_ref[...] = acc.astype(BF16)


def attn_b(proj, tb, cfg):
    t = proj.shape[0]
    tile = cfg.attn_tile
    s = cfg.seq
    nq = s // tile
    assert cfg.db == LANES
    nwin = (cfg.left_chunks * cfg.chunk) // tile + 1
    base = (2 * cfg.ha * 2 * cfg.da_qk + cfg.ha * 2 * cfg.da_qk) // LANES
    kern = functools.partial(_attn_b_kernel, tile=tile, nwin=nwin)
    return pl.pallas_call(
        kern,
        grid=(cfg.batch, cfg.hb, nq),
        in_specs=[pl.BlockSpec((tile, LANES), lambda b, h, i: (b * nq + i, base + h)),
                  pl.BlockSpec((s, LANES), lambda b, h, i: (b, base + cfg.hb + h)),
                  pl.BlockSpec((s, LANES), lambda b, h, i: (b, base + 2 * cfg.hb + h)),
                  pl.BlockSpec((1, nwin, tile, tile), lambda b, h, i: (h, 0, 0, 0))],
        out_specs=pl.BlockSpec((tile, LANES), lambda b, h, i: (b * nq + i, h)),
        out_shape=jax.ShapeDtypeStruct((t, cfg.hb * cfg.db), BF16),
        compiler_params=_params(3),
        name="attn_b",
    )(proj, proj, proj, tb)


def _merge_kernel(ya_ref, yb_ref, ga_ref, gb_ref, x_ref, gta_ref, scm_ref, shm_ref, gmoe_ref,
                  wua_ref, wub_ref, wo_ref, wrh_ref, wrl_ref, h1_ref, v_ref, lg_ref):
    a = _dot(ya_ref[...], wua_ref[...])
    b = _dot(yb_ref[...], wub_ref[...])
    merged = _sigmoid(ga_ref[...].astype(F32)) * a + _sigmoid(gb_ref[...].astype(F32)) * b
    att = _dot(merged.astype(BF16), wo_ref[...])
    h1 = x_ref[...] + gta_ref[0] * att
    h1_ref[...] = h1
    y = h1 * lax.rsqrt(jnp.mean(h1 * h1, axis=-1, keepdims=True) + EPS) * gmoe_ref[...]
    v = y * (1.0 + scm_ref[0]) + shm_ref[0]
    v_ref[...] = v
    v_hi, v_lo = _split_bf16(v)
    wrh = wrh_ref[...]
    lg_ref[...] = _dot_t(wrh, v_hi) + _dot_t(wrh, v_lo) + _dot_t(wrl_ref[...], v_hi)


def merge(ya, yb, proj, x2, gta, scm, shm, gmoe, wua, wub, wo, wrh, wrl, cfg):
    t, d = x2.shape
    tm = min(cfg.merge_tm, cfg.seq)
    per_b = cfg.seq // tm
    wa = ya.shape[1]
    wb = yb.shape[1]
    gate0 = (proj.shape[1] - 2 * d) // d
    ne = wrh.shape[0]
    const = lambda i: (0, 0)
    row3 = lambda i: (i // per_b, 0, 0)
    return pl.pallas_call(
        _merge_kernel,
        grid=(t // tm,),
        in_specs=[pl.BlockSpec((tm, wa), lambda i: (i, 0)),
                  pl.BlockSpec((tm, wb), lambda i: (i, 0)),
                  pl.BlockSpec((tm, d), lambda i: (i, gate0)),
                  pl.BlockSpec((tm, d), lambda i: (i, gate0 + 1)),
                  pl.BlockSpec((tm, d), lambda i: (i, 0)),
                  pl.BlockSpec((1, 1, d), row3),
                  pl.BlockSpec((1, 1, d), row3),
                  pl.BlockSpec((1, 1, d), row3),
                  pl.BlockSpec((1, d), const),
                  pl.BlockSpec((wa, d), const),
                  pl.BlockSpec((wb, d), const),
                  pl.BlockSpec((d, d), const),
                  pl.BlockSpec((ne, d), const),
                  pl.BlockSpec((ne, d), const)],
        out_specs=[pl.BlockSpec((tm, d), lambda i: (i, 0)),
                   pl.BlockSpec((tm, d), lambda i: (i, 0)),
                   pl.BlockSpec((ne, tm), lambda i: (0, i))],
        out_shape=[jax.ShapeDtypeStruct((t, d), F32),
                   jax.ShapeDtypeStruct((t, d), F32),
                   jax.ShapeDtypeStruct((ne, t), F32)],
        compiler_params=_params(1),
        name="merge",
    )(ya, yb, proj, proj, x2, gta, scm, shm, gmoe, wua, wub, wo, wrh, wrl)


def _route_kernel(lg_ref, rb_ref, eidx_ref, w_ref, *, n_exp, n_groups, topk_groups, top_k, scale):
    gsz = n_exp // n_groups
    tc = lg_ref.shape[1]
    ninf = -jnp.inf
    row = lax.broadcasted_iota(jnp.int32, (gsz, tc), 0).astype(F32)
    sc, bi = [], []
    for g in range(n_groups):
        s = _sigmoid(lg_ref[g * gsz:(g + 1) * gsz, :])
        sc.append(s)
        bi.append(s + rb_ref[g * gsz:(g + 1) * gsz, :])
    gs = []
    for x in bi:
        m1 = jnp.max(x, axis=0, keepdims=True)
        eq = x == m1
        cnt = jnp.sum(jnp.where(eq, 1.0, 0.0), axis=0, keepdims=True)
        second = jnp.max(jnp.where(eq, ninf, x), axis=0, keepdims=True)
        gs.append(m1 + jnp.where(cnt >= 2.0, m1, second))
    gmat = jnp.concatenate(gs, axis=0)
    grow = lax.broadcasted_iota(jnp.int32, gmat.shape, 0).astype(F32)
    gsel = jnp.zeros(gmat.shape, F32)
    for _ in range(topk_groups):
        m = jnp.max(gmat, axis=0, keepdims=True)
        idx = jnp.min(jnp.where(gmat == m, grow, float(n_groups)), axis=0, keepdims=True)
        hit = grow == idx
        gsel = jnp.where(hit, 1.0, gsel)
        gmat = jnp.where(hit, ninf, gmat)
    cur = [jnp.where(gsel[g:g + 1, :] > 0.5, bi[g], ninf) for g in range(n_groups)]
    ids = [row + float(g * gsz) for g in range(n_groups)]
    e_rows, w_rows = [], []
    for _ in range(top_k):
        mg = cur[0]
        for c in cur[1:]:
            mg = jnp.maximum(mg, c)
        m = jnp.max(mg, axis=0, keepdims=True)
        cand = jnp.where(cur[0] == m, ids[0], float(n_exp))
        for c, i in zip(cur[1:], ids[1:]):
            cand = jnp.minimum(cand, jnp.where(c == m, i, float(n_exp)))
        idx = jnp.min(cand, axis=0, keepdims=True)
        wsum = jnp.zeros((gsz, tc), F32)
        for g in range(n_groups):
            hit = ids[g] == idx
            wsum = wsum + jnp.where(hit, sc[g], 0.0)
            cur[g] = jnp.where(hit, ninf, cur[g])
        e_rows.append(idx)
        w_rows.append(jnp.sum(wsum, axis=0, keepdims=True))
    w = jnp.concatenate(w_rows, axis=0)
    w = w / jnp.sum(w, axis=0, keepdims=True) * scale
    eidx_ref[...] = jnp.concatenate(e_rows, axis=0).astype(jnp.int32)
    w_ref[...] = w


def route(lg, rb, cfg):
    ne, t = lg.shape
    tc = min(cfg.route_tc, t)
    kern = functools.partial(_route_kernel, n_exp=cfg.n_experts, n_groups=cfg.n_groups,
                             topk_groups=cfg.topk_groups, top_k=cfg.top_k, scale=cfg.routed_scale)
    return pl.pallas_call(
        kern,
        grid=(t // tc,),
        in_specs=[pl.BlockSpec((ne, tc), lambda i: (0, i)),
                  pl.BlockSpec((ne, tc), lambda i: (0, 0))],
        out_specs=[pl.BlockSpec((cfg.top_k, tc), lambda i: (0, i)),
                   pl.BlockSpec((cfg.top_k, tc), lambda i: (0, i))],
        out_shape=[jax.ShapeDtypeStruct((cfg.top_k, t), jnp.int32),
                   jax.ShapeDtypeStruct((cfg.top_k, t), F32)],
        compiler_params=_params(1),
        name="route",
    )(lg, rb)


def _moe_kernel(be_ref, tokc_ref, tokn_ref, dst_ref, v_hbm, wg_ref, wu_ref, wd_ref, yk_hbm,
                xbuf, ybuf, wgb, wub, wdb, gsem, ssem, *, bm):
    b = pl.program_id(0)
    nb = pl.num_programs(0)
    slot = lax.rem(b, 2)

    def gather_copy(t, r, s):
        return pltpu.make_async_copy(v_hbm.at[pl.ds(t, 1), :], xbuf.at[s, pl.ds(r, 1), :], gsem.at[s])

    def scatter_copy(r, d, s):
        return pltpu.make_async_copy(ybuf.at[s, pl.ds(r, 1), :], yk_hbm.at[pl.ds(d, 1), :], ssem.at[s])

    def wait_gather(s):
        pltpu.make_async_copy(v_hbm.at[pl.ds(0, bm), :], xbuf.at[s], gsem.at[s]).wait()

    def wait_scatter(s):
        pltpu.make_async_copy(ybuf.at[s], yk_hbm.at[pl.ds(0, bm), :], ssem.at[s]).wait()

    @pl.when(b == 0)
    def _():
        for r in range(bm):
            gather_copy(tokc_ref[0, 0, r], r, 0).start()

    @pl.when(b + 1 < nb)
    def _():
        for r in range(bm):
            gather_copy(tokn_ref[0, 0, r], r, 1 - slot).start()

    @pl.when(b >= 2)
    def _():
        wait_scatter(slot)

    e = be_ref[b]
    eprev = be_ref[jnp.maximum(b - 1, 0)]

    @pl.when(jnp.logical_or(b == 0, e != eprev))
    def _():
        wgb[...] = wg_ref[0].astype(BF16)
        wub[...] = wu_ref[0].astype(BF16)
        wdb[...] = wd_ref[0].astype(BF16)

    wait_gather(slot)
    x = xbuf[slot].astype(BF16)
    g = _dot(x, wgb[...])
    u = _dot(x, wub[...])
    h = (g * _sigmoid(g)) * u
    ybuf[slot] = _dot(h.astype(BF16), wdb[...])
    for r in range(bm):
        scatter_copy(r, dst_ref[0, 0, r], slot).start()

    @pl.when(b == nb - 1)
    def _():
        wait_scatter(slot)

        @pl.when(nb >= 2)
        def _():
            wait_scatter(1 - slot)


def moe(v, block_e, slot_tok, slot_dst, wg, wu, wd, n_rows_out, cfg):
    t, d = v.shape
    bm = cfg.moe_rows
    nb = block_e.shape[0]
    de = wg.shape[2]
    tok3 = slot_tok.reshape(nb, 1, bm)
    dst3 = slot_dst.reshape(nb, 1, bm)
    kern = functools.partial(_moe_kernel, bm=bm)
    smem_blk = lambda f: pl.BlockSpec((1, 1, bm), f, memory_space=pltpu.SMEM)
    grid_spec = pltpu.PrefetchScalarGridSpec(
        num_scalar_prefetch=1,
        grid=(nb,),
        in_specs=[smem_blk(lambda b, be: (b, 0, 0)),
                  smem_blk(lambda b, be: (jnp.minimum(b + 1, nb - 1), 0, 0)),
                  smem_blk(lambda b, be: (b, 0, 0)),
                  pl.BlockSpec(memory_space=pl.ANY),
                  pl.BlockSpec((1, d, de), lambda b, be: (be[b], 0, 0)),
                  pl.BlockSpec((1, d, de), lambda b, be: (be[b], 0, 0)),
                  pl.BlockSpec((1, de, d), lambda b, be: (be[b], 0, 0))],
        out_specs=pl.BlockSpec(memory_space=pl.ANY),
        scratch_shapes=[pltpu.VMEM((2, bm, d), F32), pltpu.VMEM((2, bm, d), F32),
                        pltpu.VMEM((d, de), BF16), pltpu.VMEM((d, de), BF16), pltpu.VMEM((de, d), BF16),
                        pltpu.SemaphoreType.DMA((2,)), pltpu.SemaphoreType.DMA((2,))],
    )
    return pl.pallas_call(
        kern,
        grid_spec=grid_spec,
        out_shape=jax.ShapeDtypeStruct((n_rows_out, d), F32),
        compiler_params=_params(1),
        name="moe",
    )(block_e, tok3, tok3, dst3, v, wg, wu, wd)


def _final_kernel(v_ref, h1_ref, w_ref, gtm_ref, gfin_ref, sg_ref, su_ref, sd_ref, *rest, top_k):
    yk_refs, o_ref = rest[:top_k], rest[top_k]
    vb = v_ref[...].astype(BF16)
    g = _dot(vb, sg_ref[...])
    u = _dot(vb, su_ref[...])
    y = _dot(((g * _sigmoid(g)) * u).astype(BF16), sd_ref[...])
    for k in range(top_k):
        y = y + w_ref[:, k:k + 1] * yk_refs[k][...]
    h2 = h1_ref[...] + gtm_ref[0] * y
    o_ref[...] = h2 * lax.rsqrt(jnp.mean(h2 * h2, axis=-1, keepdims=True) + EPS) * gfin_ref[...]


def final(v, h1, yk, w_tk, gtm, gfin, sg, su, sd, cfg):
    t, d = v.shape
    tm = min(cfg.final_tm, cfg.seq)
    per_b = cfg.seq // tm
    nt = t // tm
    ds = sg.shape[1]
    const = lambda i: (0, 0)
    kern = functools.partial(_final_kernel, top_k=cfg.top_k)
    plane = lambda k: pl.BlockSpec((tm, d), lambda i: (k * nt + i, 0))
    return pl.pallas_call(
        kern,
        grid=(nt,),
        in_specs=[pl.BlockSpec((tm, d), lambda i: (i, 0)),
                  pl.BlockSpec((tm, d), lambda i: (i, 0)),
                  pl.BlockSpec((tm, cfg.top_k), lambda i: (i, 0)),
                  pl.BlockSpec((1, 1, d), lambda i: (i // per_b, 0, 0)),
                  pl.BlockSpec((1, d), const),
                  pl.BlockSpec((d, ds), const),
                  pl.BlockSpec((d, ds), const),
                  pl.BlockSpec((ds, d), const)] + [plane(k) for k in range(cfg.top_k)],
        out_specs=pl.BlockSpec((tm, d), lambda i: (i, 0)),
        out_shape=jax.ShapeDtypeStruct((t, d), F32),
        compiler_params=_params(1),
        name="final",
    )(v, h1, w_tk, gtm, gfin, sg, su, sd, *([yk] * cfg.top_k))


def _t5_bucket(rel, cfg):
    nb = cfg.t5_buckets // 2
    ret = (rel > 0).astype(np.int32) * nb
    n = np.abs(rel)
    max_exact = nb // 2
    large = max_exact + (np.log(np.maximum(n, 1) / max_exact) / math.log(cfg.t5_max_dist / max_exact)
                         * (nb - max_exact)).astype(np.int32)
    large = np.minimum(large, nb - 1)
    return (ret + np.where(n < max_exact, n, large)).astype(np.int32)


def _toeplitz(vec, n):
    lead = vec.shape[:-1]
    flat = jnp.tile(vec, (1,) * len(lead) + (n,))[..., :n * (2 * n - 1)]
    return flat.reshape(lead + (n, 2 * n - 1))[..., :n]


def _rel_offsets(n):
    m = np.arange(2 * n)
    return np.where(m < n, m, m - 2 * n)


def _bias_tables_a(t5_bias, cfg):
    tile = cfg.attn_tile
    assert tile + 1 >= cfg.t5_max_dist and tile % cfg.chunk == 0
    ql = np.arange(tile)[:, None]
    kl = np.arange(tile)[None, :]
    allowed = (kl // cfg.chunk) <= (ql // cfg.chunk)
    r = _rel_offsets(tile)
    t5 = t5_bias.astype(F32)
    diag = _toeplitz(t5[_t5_bucket(r, cfg)].T, tile)
    prev = _toeplitz(t5[_t5_bucket(r - tile, cfg)].T, tile)
    far = jnp.broadcast_to(t5[_t5_bucket(np.array(-2 * tile), cfg)][:, None, None], diag.shape)
    return jnp.stack([jnp.where(allowed[None], diag, NEG), prev, far], axis=1)


def _bias_tables_b(rel_table, cfg):
    tile = cfg.attn_tile
    assert (cfg.left_chunks * cfg.chunk) % tile == 0 and tile % cfg.chunk == 0
    nwin = (cfg.left_chunks * cfg.chunk) // tile + 1
    ql = np.arange(tile)[:, None]
    kl = np.arange(tile)[None, :]
    r = _rel_offsets(tile)
    rt = rel_table.astype(F32)
    tabs = []
    for j in range(nwin):
        off = (j - (nwin - 1)) * tile
        idx = np.clip(r + off, -cfg.rel_clip, cfg.rel_clip) + cfg.rel_clip
        dchunk = ql // cfg.chunk - (kl + off) // cfg.chunk
        band = (dchunk >= 0) & (dchunk <= cfg.left_chunks)
        tabs.append(jnp.where(band[None], _toeplitz(rt[idx].T, tile), NEG))
    return jnp.stack(tabs, axis=1)


def _dispatch_tables(eidx, cfg):
    t_, k_ = eidx.shape
    e_ = cfg.n_experts
    bm = cfg.moe_rows
    a_ = t_ * k_
    nb = -(-(a_ + e_ * (bm - 1)) // bm)
    p_ = nb * bm
    flat_e = eidx.reshape(a_)
    order = jnp.argsort(flat_e).astype(jnp.int32)
    se = flat_e[order]
    counts = jnp.bincount(flat_e, length=e_).astype(jnp.int32)
    starts = jnp.cumsum(counts) - counts
    pcounts = (counts + bm - 1) // bm * bm
    pends = jnp.cumsum(pcounts)
    pstarts = pends - pcounts
    dest = pstarts[se] + jnp.arange(a_, dtype=jnp.int32) - starts[se]
    slot_tok = jnp.zeros((p_,), jnp.int32).at[dest].set(order // k_)
    dummy = a_ + jnp.arange(p_, dtype=jnp.int32) % (2 * bm)
    slot_dst = dummy.at[dest].set((order % k_) * t_ + order // k_)
    block_e = jnp.minimum(jnp.searchsorted(pends, jnp.arange(nb, dtype=jnp.int32) * bm, side='right'),
                          e_ - 1).astype(jnp.int32)
    return block_e, slot_tok, slot_dst, a_ + 2 * bm


def _forward(cfg, x, c, w_ada, b_ada, g_attn, w_in, lambda_qk, t5_bias, rel_bias_b, w_up_a, w_up_b, w_o,
             g_moe, w_router, router_bias, w_exp_gate, w_exp_up, w_exp_down,
             w_sh_gate, w_sh_up, w_sh_down, g_final):
    b_, s_, d_ = x.shape
    t_ = b_ * s_
    l = 0
    qk_a = cfg.ha * 2 * cfg.da_qk
    width_a = cfg.ha * 2 * cfg.da_qk
    width_b = cfg.hb * cfg.db
    in_cols = w_in.shape[2]

    rows = max(16, b_)
    c_pad = jnp.zeros((rows, d_), F32).at[:b_].set(c)
    mod = ada_mod(c_pad, w_ada[l], b_ada[l], cfg)[:b_]
    sh_a, sc_a, gt_a, sh_m, sc_m, gt_m = [m.reshape(b_, 1, d_) for m in jnp.split(mod, cfg.n_mod, axis=-1)]

    colscale = np.ones((1, in_cols), np.float32)
    colscale[0, :qk_a] = cfg.da_qk ** -0.5
    qb0 = 2 * qk_a + width_a
    colscale[0, qb0:qb0 + width_b] = cfg.db ** -0.5
    x2 = x.reshape(t_, d_)
    proj = in_proj(x2, g_attn[l].reshape(1, d_), sc_a, sh_a, w_in[l].astype(BF16), jnp.asarray(colscale), cfg)

    lq = lambda_qk[l].astype(F32)
    lam = jnp.exp(jnp.sum(lq[0] * lq[1])) - jnp.exp(jnp.sum(lq[2] * lq[3])) + cfg.lam_init
    ya = attn_a(proj, lam.reshape(1), _bias_tables_a(t5_bias, cfg), cfg)
    yb = attn_b(proj, _bias_tables_b(rel_bias_b[l], cfg), cfg)

    ne_pad = max(LANES, cfg.n_experts)
    wrt = jnp.zeros((ne_pad, d_), F32).at[:cfg.n_experts].set(w_router[l].T)
    wrh, wrl = _split_bf16(wrt)
    h1, v, lg = merge(ya, yb, proj, x2, gt_a, sc_m, sh_m, g_moe[l].reshape(1, d_),
                      w_up_a[l].astype(BF16), w_up_b[l].astype(BF16), w_o[l].astype(BF16), wrh, wrl, cfg)

    tc = min(cfg.route_tc, t_)
    rb = jnp.zeros((ne_pad, tc), F32).at[:cfg.n_experts].set(
        jnp.broadcast_to(router_bias[l].astype(F32)[:, None], (cfg.n_experts, tc)))
    eidx_kt, w_kt = route(lg, rb, cfg)
    block_e, slot_tok, slot_dst, n_rows_out = _dispatch_tables(eidx_kt.T, cfg)

    yk = moe(v, block_e, slot_tok, slot_dst, w_exp_gate[l], w_exp_up[l], w_exp_down[l], n_rows_out, cfg)
    out = final(v, h1, yk, w_kt.T, gt_m, g_final.reshape(1, d_),
                w_sh_gate[l].astype(BF16), w_sh_up[l].astype(BF16), w_sh_down[l].astype(BF16), cfg)
    return out.reshape(b_, s_, d_)


def kernel(x, c, w_ada, b_ada, g_attn, w_in, lambda_qk, t5_bias, rel_bias_b, w_up_a, w_up_b, w_o, g_moe,
           w_router, router_bias, w_exp_gate, w_exp_up, w_exp_down, w_sh_gate, w_sh_up, w_sh_down, g_final):
    return _forward(Cfg(), x, c, w_ada, b_ada, g_attn, w_in, lambda_qk, t5_bias, rel_bias_b, w_up_a, w_up_b,
                    w_o, g_moe, w_router, router_bias, w_exp_gate, w_exp_up, w_exp_down,
                    w_sh_gate, w_sh_up, w_sh_down, g_final)
```

```python
import functools
import math
from typing import NamedTuple

import numpy as np
import jax
import jax.numpy as jnp
from jax import lax
from jax.experimental import pallas as pl
from jax.experimental.pallas import tpu as pltpu

F32 = jnp.float32
BF16 = jnp.bfloat16
NEG = -1e30
EPS = 1e-6
HEAD_EPS = 1e-5
LANES = 128
VMEM_LIMIT = 56 * 1024 * 1024


class Cfg(NamedTuple):
    batch: int = 8
    seq: int = 2048
    d_model: int = 2048
    chunk: int = 64
    ha: int = 8
    da_qk: int = 64
    hb: int = 8
    db: int = 128
    left_chunks: int = 8
    rel_clip: int = 256
    t5_buckets: int = 32
    t5_max_dist: int = 128
    n_experts: int = 64
    top_k: int = 8
    n_groups: int = 8
    topk_groups: int = 4
    d_expert: int = 512
    d_shared: int = 512
    routed_scale: float = 2.5
    n_mod: int = 6
    lam_init: float = 0.8 - 0.6 * math.exp(-0.3 * 0)
    attn_tile: int = 256
    moe_rows: int = 256
    proj_tm: int = 1024
    proj_tn: int = 1024
    merge_tm: int = 256
    route_tc: int = 512
    final_tm: int = 128
    ada_tn: int = 1024


def _sigmoid(x):
    return 1.0 / (1.0 + jnp.exp(-x))


def _split_bf16(x):
    hi = x.astype(BF16)
    lo = (x - hi.astype(F32)).astype(BF16)
    return hi, lo


def _dot(a, b):
    return jnp.dot(a, b, preferred_element_type=F32)


def _dot_t(a, b):
    return lax.dot_general(a, b, (((1,), (1,)), ((), ())), preferred_element_type=F32)


def _params(n_axes):
    return pltpu.CompilerParams(dimension_semantics=("arbitrary",) * n_axes,
                                vmem_limit_bytes=VMEM_LIMIT)


def _ada_kernel(c_ref, w_ref, b_ref, o_ref):
    c = c_ref[...]
    s_hi, s_lo = _split_bf16(c * _sigmoid(c))
    w_hi, w_lo = _split_bf16(w_ref[...])
    acc = _dot(s_hi, w_hi) + _dot(s_hi, w_lo) + _dot(s_lo, w_hi)
    o_ref[...] = acc + b_ref[...]


def ada_mod(c_pad, w_ada, b_ada, cfg):
    rows, d = c_pad.shape
    n = w_ada.shape[1]
    tn = min(cfg.ada_tn, n)
    return pl.pallas_call(
        _ada_kernel,
        grid=(n // tn,),
        in_specs=[pl.BlockSpec((rows, d), lambda j: (0, 0)),
                  pl.BlockSpec((d, tn), lambda j: (0, j)),
                  pl.BlockSpec((1, tn), lambda j: (0, j))],
        out_specs=pl.BlockSpec((rows, tn), lambda j: (0, j)),
        out_shape=jax.ShapeDtypeStruct((rows, n), F32),
        compiler_params=_params(1),
        name="ada_mod",
    )(c_pad, w_ada, b_ada.reshape(1, n))


def _inproj_kernel(x_ref, g_ref, sc_ref, sh_ref, w_ref, cs_ref, o_ref, u_ref):
    @pl.when(pl.program_id(1) == 0)
    def _():
        x = x_ref[...]
        y = x * lax.rsqrt(jnp.mean(x * x, axis=-1, keepdims=True) + EPS) * g_ref[...]
        u_ref[...] = (y * (1.0 + sc_ref[0]) + sh_ref[0]).astype(BF16)

    acc = _dot(u_ref[...], w_ref[...])
    o_ref[...] = (acc * cs_ref[...]).astype(BF16)


def in_proj(x2, g, sc, sh, w_bf, colscale, cfg):
    t, d = x2.shape
    n = w_bf.shape[1]
    tm = min(cfg.proj_tm, cfg.seq)
    tn = min(cfg.proj_tn, n)
    per_b = cfg.seq // tm
    return pl.pallas_call(
        _inproj_kernel,
        grid=(t // tm, n // tn),
        in_specs=[pl.BlockSpec((tm, d), lambda i, j: (i, 0)),
                  pl.BlockSpec((1, d), lambda i, j: (0, 0)),
                  pl.BlockSpec((1, 1, d), lambda i, j: (i // per_b, 0, 0)),
                  pl.BlockSpec((1, 1, d), lambda i, j: (i // per_b, 0, 0)),
                  pl.BlockSpec((d, tn), lambda i, j: (0, j)),
                  pl.BlockSpec((1, tn), lambda i, j: (0, j))],
        out_specs=pl.BlockSpec((tm, tn), lambda i, j: (i, j)),
        out_shape=jax.ShapeDtypeStruct((t, n), BF16),
        scratch_shapes=[pltpu.VMEM((tm, d), BF16)],
        compiler_params=_params(2),
        name="in_proj",
    )(x2, g, sc, sh, w_bf, colscale)


def _attn_a_kernel(lam_ref, q_ref, k_ref, v_ref, tb_ref, o_ref,
                   s0_ref, s1_ref, m0_ref, m1_ref, l0_ref, l1_ref, acc_ref, *, tile, half, out_scale):
    qi = pl.program_id(2)
    nk = qi + 1
    q = q_ref[...]
    lane = lax.broadcasted_iota(jnp.int32, q.shape, 1)
    zero = jnp.zeros_like(q)
    q0 = jnp.where(lane < half, q, zero)
    q1 = jnp.where(lane >= half, q, zero)

    m0_ref[...] = jnp.full(m0_ref.shape, -jnp.inf, F32)
    m1_ref[...] = jnp.full(m1_ref.shape, -jnp.inf, F32)
    l0_ref[...] = jnp.zeros(l0_ref.shape, F32)
    l1_ref[...] = jnp.zeros(l1_ref.shape, F32)
    acc_ref[...] = jnp.zeros(acc_ref.shape, F32)

    def fold(x):
        r = x[:, :LANES]
        for c in range(1, tile // LANES):
            r = r + x[:, c * LANES:(c + 1) * LANES]
        return r

    def foldmax(x):
        r = x[:, :LANES]
        for c in range(1, tile // LANES):
            r = jnp.maximum(r, x[:, c * LANES:(c + 1) * LANES])
        return r

    def scores(kb, carry):
        kblk = k_ref[pl.ds(pl.multiple_of(kb * tile, tile), tile), :]
        bias = tb_ref[0, jnp.minimum(qi - kb, 2)]
        s0 = _dot_t(q0, kblk) + bias
        s1 = _dot_t(q1, kblk) + bias
        s0_ref[kb] = s0
        s1_ref[kb] = s1
        m0_ref[...] = jnp.maximum(m0_ref[...], foldmax(s0))
        m1_ref[...] = jnp.maximum(m1_ref[...], foldmax(s1))
        return carry

    lax.fori_loop(0, nk, scores, 0)
    m0 = jnp.max(m0_ref[...], axis=-1, keepdims=True)
    m1 = jnp.max(m1_ref[...], axis=-1, keepdims=True)

    def expo(kb, carry):
        p0 = jnp.exp(s0_ref[kb] - m0)
        p1 = jnp.exp(s1_ref[kb] - m1)
        s0_ref[kb] = p0
        s1_ref[kb] = p1
        l0_ref[...] += fold(p0)
        l1_ref[...] += fold(p1)
        return carry

    lax.fori_loop(0, nk, expo, 0)
    c0 = 1.0 / jnp.sum(l0_ref[...], axis=-1, keepdims=True)
    c1 = lam_ref[0] / jnp.sum(l1_ref[...], axis=-1, keepdims=True)

    def pv(kb, carry):
        a = s0_ref[kb] * c0 - s1_ref[kb] * c1
        vblk = v_ref[pl.ds(pl.multiple_of(kb * tile, tile), tile), :]
        acc_ref[...] += _dot(a.astype(BF16), vblk)
        return carry

    lax.fori_loop(0, nk, pv, 0)
    o = acc_ref[...]
    o = o * lax.rsqrt(jnp.mean(o * o, axis=-1, keepdims=True) + HEAD_EPS) * out_scale
    o_ref[...] = o.astype(BF16)


def attn_a(proj, lam, tb, cfg):
    t = proj.shape[0]
    tile = cfg.attn_tile
    s = cfg.seq
    nq = s // tile
    dv = 2 * cfg.da_qk
    assert dv == LANES
    kcol = cfg.ha
    vcol = 2 * cfg.ha
    kern = functools.partial(_attn_a_kernel, tile=tile, half=cfg.da_qk, out_scale=1.0 - cfg.lam_init)
    return pl.pallas_call(
        kern,
        grid=(cfg.batch, cfg.ha, nq),
        in_specs=[pl.BlockSpec(memory_space=pltpu.SMEM),
                  pl.BlockSpec((tile, LANES), lambda b, h, i: (b * nq + i, h)),
                  pl.BlockSpec((s, LANES), lambda b, h, i: (b, kcol + h)),
                  pl.BlockSpec((s, LANES), lambda b, h, i: (b, vcol + h)),
                  pl.BlockSpec((1, 3, tile, tile), lambda b, h, i: (h, 0, 0, 0))],
        out_specs=pl.BlockSpec((tile, LANES), lambda b, h, i: (b * nq + i, h)),
        out_shape=jax.ShapeDtypeStruct((t, cfg.ha * dv), BF16),
        scratch_shapes=[pltpu.VMEM((nq, tile, tile), F32), pltpu.VMEM((nq, tile, tile), F32),
                        pltpu.VMEM((tile, LANES), F32), pltpu.VMEM((tile, LANES), F32),
                        pltpu.VMEM((tile, LANES), F32), pltpu.VMEM((tile, LANES), F32),
                        pltpu.VMEM((tile, LANES), F32)],
        compiler_params=_params(3),
        name="attn_a",
    )(lam, proj, proj, proj, tb)


def _attn_b_kernel(q_ref, k_ref, v_ref, tb_ref, o_ref, *, tile, nwin):
    qi = pl.program_id(2)
    q = q_ref[...]
    ss, vs = [], []
    for j in range(nwin):
        kb = qi - (nwin - 1) + j
        start = pl.multiple_of(jnp.maximum(kb, 0) * tile, tile)
        s = _dot_t(q, k_ref[pl.ds(start, tile), :]) + tb_ref[0, j]
        ss.append(jnp.where(kb >= 0, s, NEG))
        vs.append(v_ref[pl.ds(start, tile), :])
    m = jnp.max(ss[0], axis=-1, keepdims=True)
    for s in ss[1:]:
        m = jnp.maximum(m, jnp.max(s, axis=-1, keepdims=True))
    ps = [jnp.exp(s - m) for s in ss]
    l = jnp.sum(ps[0], axis=-1, keepdims=True)
    for p in ps[1:]:
        l = l + jnp.sum(p, axis=-1, keepdims=True)
    inv = 1.0 / l
    acc = _dot((ps[0] * inv).astype(BF16), vs[0])
    for p, v in zip(ps[1:], vs[1:]):
        acc = acc + _dot((p * inv).astype(BF16), v)
    o_ref[...] = acc.astype(BF16)


def attn_b(proj, tb, cfg):
    t = proj.shape[0]
    tile = cfg.attn_tile
    s = cfg.seq
    nq = s // tile
    assert cfg.db == LANES
    nwin = (cfg.left_chunks * cfg.chunk) // tile + 1
    base = (2 * cfg.ha * 2 * cfg.da_qk + cfg.ha * 2 * cfg.da_qk) // LANES
    kern = functools.partial(_attn_b_kernel, tile=tile, nwin=nwin)
    return pl.pallas_call(
        kern,
        grid=(cfg.batch, cfg.hb, nq),
        in_specs=[pl.BlockSpec((tile, LANES), lambda b, h, i: (b * nq + i, base + h)),
                  pl.BlockSpec((s, LANES), lambda b, h, i: (b, base + cfg.hb + h)),
                  pl.BlockSpec((s, LANES), lambda b, h, i: (b, base + 2 * cfg.hb + h)),
                  pl.BlockSpec((1, nwin, tile, tile), lambda b, h, i: (h, 0, 0, 0))],
        out_specs=pl.BlockSpec((tile, LANES), lambda b, h, i: (b * nq + i, h)),
        out_shape=jax.ShapeDtypeStruct((t, cfg.hb * cfg.db), BF16),
        compiler_params=_params(3),
        name="attn_b",
    )(proj, proj, proj, tb)


def _merge_kernel(ya_ref, yb_ref, ga_ref, gb_ref, x_ref, gta_ref, scm_ref, shm_ref, gmoe_ref,
                  wua_ref, wub_ref, wo_ref, wrh_ref, wrl_ref, h1_ref, v_ref, lg_ref):
    a = _dot(ya_ref[...], wua_ref[...])
    b = _dot(yb_ref[...], wub_ref[...])
    merged = _sigmoid(ga_ref[...].astype(F32)) * a + _sigmoid(gb_ref[...].astype(F32)) * b
    att = _dot(merged.astype(BF16), wo_ref[...])
    h1 = x_ref[...] + gta_ref[0] * att
    h1_ref[...] = h1
    y = h1 * lax.rsqrt(jnp.mean(h1 * h1, axis=-1, keepdims=True) + EPS) * gmoe_ref[...]
    v = y * (1.0 + scm_ref[0]) + shm_ref[0]
    v_ref[...] = v
    v_hi, v_lo = _split_bf16(v)
    wrh = wrh_ref[...]
    lg_ref[...] = _dot_t(wrh, v_hi) + _dot_t(wrh, v_lo) + _dot_t(wrl_ref[...], v_hi)


def merge(ya, yb, proj, x2, gta, scm, shm, gmoe, wua, wub, wo, wrh, wrl, cfg):
    t, d = x2.shape
    tm = min(cfg.merge_tm, cfg.seq)
    per_b = cfg.seq // tm
    wa = ya.shape[1]
    wb = yb.shape[1]
    gate0 = (proj.shape[1] - 2 * d) // d
    ne = wrh.shape[0]
    const = lambda i: (0, 0)
    row3 = lambda i: (i // per_b, 0, 0)
    return pl.pallas_call(
        _merge_kernel,
        grid=(t // tm,),
        in_specs=[pl.BlockSpec((tm, wa), lambda i: (i, 0)),
                  pl.BlockSpec((tm, wb), lambda i: (i, 0)),
                  pl.BlockSpec((tm, d), lambda i: (i, gate0)),
                  pl.BlockSpec((tm, d), lambda i: (i, gate0 + 1)),
                  pl.BlockSpec((tm, d), lambda i: (i, 0)),
                  pl.BlockSpec((1, 1, d), row3),
                  pl.BlockSpec((1, 1, d), row3),
                  pl.BlockSpec((1, 1, d), row3),
                  pl.BlockSpec((1, d), const),
                  pl.BlockSpec((wa, d), const),
                  pl.BlockSpec((wb, d), const),
                  pl.BlockSpec((d, d), const),
                  pl.BlockSpec((ne, d), const),
                  pl.BlockSpec((ne, d), const)],
        out_specs=[pl.BlockSpec((tm, d), lambda i: (i, 0)),
                   pl.BlockSpec((tm, d), lambda i: (i, 0)),
                   pl.BlockSpec((ne, tm), lambda i: (0, i))],
        out_shape=[jax.ShapeDtypeStruct((t, d), F32),
                   jax.ShapeDtypeStruct((t, d), F32),
                   jax.ShapeDtypeStruct((ne, t), F32)],
        compiler_params=_params(1),
        name="merge",
    )(ya, yb, proj, proj, x2, gta, scm, shm, gmoe, wua, wub, wo, wrh, wrl)


def _route_kernel(lg_ref, rb_ref, tri_ref, w_ref, csel_ref, code_ref, cend_ref, carry_ref,
                  *, n_exp, n_groups, topk_groups, top_k, scale):
    gsz = n_exp // n_groups
    tc = lg_ref.shape[1]
    ninf = -jnp.inf
    step = pl.program_id(0)

    @pl.when(step == 0)
    def _():
        carry_ref[...] = jnp.zeros(carry_ref.shape, F32)

    row = lax.broadcasted_iota(jnp.int32, (gsz, tc), 0).astype(F32)
    sc, bi = [], []
    for g in range(n_groups):
        s = _sigmoid(lg_ref[g * gsz:(g + 1) * gsz, :])
        sc.append(s)
        bi.append(s + rb_ref[g * gsz:(g + 1) * gsz, :])
    gs = []
    for x in bi:
        m1 = jnp.max(x, axis=0, keepdims=True)
        eq = x == m1
        cnt = jnp.sum(jnp.where(eq, 1.0, 0.0), axis=0, keepdims=True)
        second = jnp.max(jnp.where(eq, ninf, x), axis=0, keepdims=True)
        gs.append(m1 + jnp.where(cnt >= 2.0, m1, second))
    gmat = jnp.concatenate(gs, axis=0)
    grow = lax.broadcasted_iota(jnp.int32, gmat.shape, 0).astype(F32)
    gsel = jnp.zeros(gmat.shape, F32)
    for _ in range(topk_groups):
        m = jnp.max(gmat, axis=0, keepdims=True)
        idx = jnp.min(jnp.where(gmat == m, grow, float(n_groups)), axis=0, keepdims=True)
        hit = grow == idx
        gsel = jnp.where(hit, 1.0, gsel)
        gmat = jnp.where(hit, ninf, gmat)
    cur = [jnp.where(gsel[g:g + 1, :] > 0.5, bi[g], ninf) for g in range(n_groups)]
    ids = [row + float(g * gsz) for g in range(n_groups)]
    w_rows = []
    chosen = [jnp.zeros((gsz, tc), F32) for _ in range(n_groups)]
    kplus = [jnp.zeros((gsz, tc), F32) for _ in range(n_groups)]
    for k in range(top_k):
        mg = cur[0]
        for c in cur[1:]:
            mg = jnp.maximum(mg, c)
        m = jnp.max(mg, axis=0, keepdims=True)
        cand = jnp.where(cur[0] == m, ids[0], float(n_exp))
        for c, i in zip(cur[1:], ids[1:]):
            cand = jnp.minimum(cand, jnp.where(c == m, i, float(n_exp)))
        idx = jnp.min(cand, axis=0, keepdims=True)
        wsum = jnp.zeros((gsz, tc), F32)
        for g in range(n_groups):
            hit = ids[g] == idx
            wsum = wsum + jnp.where(hit, sc[g], 0.0)
            cur[g] = jnp.where(hit, ninf, cur[g])
            chosen[g] = jnp.where(hit, 1.0, chosen[g])
            kplus[g] = jnp.where(hit, float(k + 1), kplus[g])
        w_rows.append(jnp.sum(wsum, axis=0, keepdims=True))
    w = jnp.concatenate(w_rows, axis=0)
    w_ref[...] = w / jnp.sum(w, axis=0, keepdims=True) * scale

    onehot = jnp.concatenate(chosen, axis=0)
    incl = _dot(onehot.astype(BF16), tri_ref[...])
    carry = carry_ref[...]
    cglob = carry[:, 0:1] + incl
    tok = (lax.broadcasted_iota(jnp.int32, (n_exp, tc), 1) + step * tc).astype(F32)
    picked = onehot > 0.5
    csel_ref[0:n_exp, :] = jnp.where(picked, cglob, 0.0)
    code_ref[0:n_exp, :] = jnp.where(picked, tok * float(top_k) + jnp.concatenate(kplus, axis=0) - 1.0, 0.0)
    new_carry = carry + jnp.sum(onehot, axis=1, keepdims=True)
    carry_ref[...] = new_carry
    cend_ref[0] = new_carry


def route(lg, rb, cfg):
    ne, t = lg.shape
    nx = cfg.n_experts
    tc = min(cfg.route_tc, t)
    nsteps = t // tc
    tri = jnp.asarray(np.triu(np.ones((tc, tc), np.float32)), BF16)
    kern = functools.partial(_route_kernel, n_exp=cfg.n_experts, n_groups=cfg.n_groups,
                             topk_groups=cfg.topk_groups, top_k=cfg.top_k, scale=cfg.routed_scale)
    return pl.pallas_call(
        kern,
        grid=(nsteps,),
        in_specs=[pl.BlockSpec((ne, tc), lambda i: (0, i)),
                  pl.BlockSpec((ne, tc), lambda i: (0, 0)),
                  pl.BlockSpec((tc, tc), lambda i: (0, 0))],
        out_specs=[pl.BlockSpec((cfg.top_k, tc), lambda i: (0, i)),
                   pl.BlockSpec((nx, tc), lambda i: (0, i)),
                   pl.BlockSpec((nx, tc), lambda i: (0, i)),
                   pl.BlockSpec((1, nx, LANES), lambda i: (i, 0, 0))],
        out_shape=[jax.ShapeDtypeStruct((cfg.top_k, t), F32),
                   jax.ShapeDtypeStruct((nx, t), F32),
                   jax.ShapeDtypeStruct((nx, t), F32),
                   jax.ShapeDtypeStruct((nsteps, nx, LANES), F32)],
        scratch_shapes=[pltpu.VMEM((cfg.n_experts, LANES), F32)],
        compiler_params=_params(1),
        name="route",
    )(lg, rb, tri)


def _slots_kernel(be_ref, r0_ref, cnt_ref, csel_ref, code_ref, cend_ref, tok_ref, dst_ref,
                  *, bm, tc, top_k, n_tok, n_assign):
    b = pl.program_id(0)
    e = be_ref[b]
    r0 = r0_ref[b]
    cnt = cnt_ref[b]
    sub = lax.rem(e, 8)
    nsteps = cend_ref.shape[0]
    pick3 = lax.broadcasted_iota(jnp.int32, cend_ref.shape, 1) == sub
    cend = jnp.sum(jnp.where(pick3, cend_ref[...], 0.0), axis=1)
    first = (r0 + 1).astype(F32)
    last = jnp.minimum(r0 + bm, cnt).astype(F32)
    jlo = (jnp.sum(jnp.where(cend < first, 1.0, 0.0)) * (1.0 / LANES)).astype(jnp.int32)
    jhi = (jnp.sum(jnp.where(cend < last, 1.0, 0.0)) * (1.0 / LANES)).astype(jnp.int32)
    jhi = jnp.minimum(jhi, nsteps - 1)
    jhi = jnp.where(last < first, jlo - 1, jhi)

    want = (lax.broadcasted_iota(jnp.int32, (bm, tc), 0) + (r0 + 1)).astype(F32)
    pick2 = lax.broadcasted_iota(jnp.int32, (8, tc), 0) == sub
    zrows = jnp.zeros((13, tc), F32)

    def body(j, acc):
        off = pl.multiple_of(j * tc, tc)
        crow = jnp.sum(jnp.where(pick2, csel_ref[:, pl.ds(off, tc)], 0.0), axis=0, keepdims=True)
        drow = jnp.sum(jnp.where(pick2, code_ref[:, pl.ds(off, tc)], 0.0), axis=0, keepdims=True)
        di = drow.astype(jnp.int32)
        digits = jnp.concatenate([(di >> 14).astype(F32), ((di >> 7) & 127).astype(F32),
                                  (di & 127).astype(F32), zrows], axis=0).astype(BF16)
        eq = jnp.where(want == crow, 1.0, 0.0).astype(BF16)
        return acc + _dot_t(digits, eq)

    acc = lax.fori_loop(jlo, jhi + 1, body, jnp.zeros((16, bm), F32))
    code = (acc[0:1, :] * 16384.0 + acc[1:2, :] * 128.0 + acc[2:3, :]).astype(jnp.int32)
    lane = lax.broadcasted_iota(jnp.int32, (1, bm), 1)
    valid = (lane + r0) < cnt
    shift = top_k.bit_length() - 1
    tok = code >> shift
    kk = code & (top_k - 1)
    dummy = n_assign + lax.rem(b, 2) * bm + lane
    tok_ref[0] = jnp.where(valid, tok, 0)
    dst_ref[0] = jnp.where(valid, kk * n_tok + tok, dummy)


def slots(block_e, r0, cntb, csel, code, cend, cfg, n_tok):
    nb = block_e.shape[0]
    bm = cfg.moe_rows
    ne, t = csel.shape
    tc = min(cfg.route_tc, t)
    nsteps = cend.shape[0]
    assert cfg.top_k & (cfg.top_k - 1) == 0 and n_tok * cfg.top_k < (1 << 21)
    kern = functools.partial(_slots_kernel, bm=bm, tc=tc, top_k=cfg.top_k, n_tok=n_tok, n_assign=n_tok * cfg.top_k)
    grid_spec = pltpu.PrefetchScalarGridSpec(
        num_scalar_prefetch=3,
        grid=(nb,),
        in_specs=[pl.BlockSpec((8, t), lambda b, be, r0, cn: (be[b] // 8, 0)),
                  pl.BlockSpec((8, t), lambda b, be, r0, cn: (be[b] // 8, 0)),
                  pl.BlockSpec((nsteps, 8, LANES), lambda b, be, r0, cn: (0, be[b] // 8, 0))],
        out_specs=[pl.BlockSpec((1, 1, bm), lambda b, be, r0, cn: (b, 0, 0)),
                   pl.BlockSpec((1, 1, bm), lambda b, be, r0, cn: (b, 0, 0))],
    )
    return pl.pallas_call(
        kern,
        grid_spec=grid_spec,
        out_shape=[jax.ShapeDtypeStruct((nb, 1, bm), jnp.int32), jax.ShapeDtypeStruct((nb, 1, bm), jnp.int32)],
        compiler_params=_params(1),
        name="slots",
    )(block_e, r0, cntb, csel, code, cend)


def _moe_kernel(be_ref, tokc_ref, tokn_ref, dstp_ref, dstc_ref, v_hbm, wg_ref, wu_ref, wd_ref, yk_hbm,
                xbuf, ybuf, wgb, wub, wdb, gsem, ssem, *, bm, n_assign):
    b = pl.program_id(0)
    nb = pl.num_programs(0)
    slot = lax.rem(b, 2)
    other = 1 - slot

    def gather_copy(t, r, s):
        return pltpu.make_async_copy(v_hbm.at[pl.ds(t, 1), :], xbuf.at[s, pl.ds(r, 1), :], gsem)

    def scatter_copy(r, d, s):
        return pltpu.make_async_copy(ybuf.at[s, pl.ds(r, 1), :], yk_hbm.at[pl.ds(d, 1), :], ssem)

    def gather_batch(s):
        return pltpu.make_async_copy(v_hbm.at[pl.ds(0, bm), :], xbuf.at[s], gsem)

    def scatter_batch(s, row0):
        return pltpu.make_async_copy(ybuf.at[s], yk_hbm.at[pl.ds(row0, bm), :], ssem)

    @pl.when(b == 0)
    def _():
        ybuf[1] = jnp.zeros(ybuf.shape[1:], F32)
        scatter_batch(1, n_assign).start()
        for r in range(bm):
            gather_copy(tokc_ref[0, 0, r], r, 0).start()

    e = be_ref[b]
    eprev = be_ref[jnp.maximum(b - 1, 0)]

    @pl.when(jnp.logical_or(b == 0, e != eprev))
    def _():
        wgb[...] = wg_ref[0].astype(BF16)
        wub[...] = wu_ref[0].astype(BF16)
        wdb[...] = wd_ref[0].astype(BF16)

    def block_step(s):
        o = 1 - s
        gather_batch(s).wait()
        scatter_batch(s, 0).wait()
        for r in range(bm):
            gather_copy(tokn_ref[0, 0, r], r, o).start()
        for r in range(bm):
            scatter_copy(r, dstp_ref[0, 0, r], o).start()
        x = xbuf[s].astype(BF16)
        g = _dot(x, wgb[...])
        u = _dot(x, wub[...])
        h = (g * _sigmoid(g)) * u
        ybuf[s] = _dot(h.astype(BF16), wdb[...])

    for s in range(2):
        pl.when(slot == s)(functools.partial(block_step, s))

    @pl.when(b == nb - 1)
    def _():
        gather_batch(other).wait()
        scatter_batch(other, 0).wait()
        for r in range(bm):
            scatter_copy(r, dstc_ref[0, 0, r], slot).start()
        scatter_batch(slot, 0).wait()


def moe(v, block_e, slot_tok, slot_dst, wg, wu, wd, cfg):
    t, d = v.shape
    bm = cfg.moe_rows
    nb = block_e.shape[0]
    de = wg.shape[2]
    n_assign = t * cfg.top_k
    kern = functools.partial(_moe_kernel, bm=bm, n_assign=n_assign)
    smem_blk = lambda f: pl.BlockSpec((1, 1, bm), f, memory_space=pltpu.SMEM)
    grid_spec = pltpu.PrefetchScalarGridSpec(
        num_scalar_prefetch=1,
        grid=(nb,),
        in_specs=[smem_blk(lambda b, be: (b, 0, 0)),
                  smem_blk(lambda b, be: (jnp.minimum(b + 1, nb - 1), 0, 0)),
                  smem_blk(lambda b, be: (jnp.maximum(b - 1, 0), 0, 0)),
                  smem_blk(lambda b, be: (b, 0, 0)),
                  pl.BlockSpec(memory_space=pl.ANY),
                  pl.BlockSpec((1, d, de), lambda b, be: (be[b], 0, 0)),
                  pl.BlockSpec((1, d, de), lambda b, be: (be[b], 0, 0)),
                  pl.BlockSpec((1, de, d), lambda b, be: (be[b], 0, 0))],
        out_specs=pl.BlockSpec(memory_space=pl.ANY),
        scratch_shapes=[pltpu.VMEM((2, bm, d), F32), pltpu.VMEM((2, bm, d), F32),
                        pltpu.VMEM((d, de), BF16), pltpu.VMEM((d, de), BF16), pltpu.VMEM((de, d), BF16),
                        pltpu.SemaphoreType.DMA(()), pltpu.SemaphoreType.DMA(())],
    )
    return pl.pallas_call(
        kern,
        grid_spec=grid_spec,
        out_shape=jax.ShapeDtypeStruct((n_assign + 2 * bm, d), F32),
        compiler_params=_params(1),
        name="moe",
    )(block_e, slot_tok, slot_tok, slot_dst, slot_dst, v, wg, wu, wd)


def _final_kernel(v_ref, h1_ref, w_ref, gtm_ref, gfin_ref, sg_ref, su_ref, sd_ref, *rest, top_k):
    yk_refs, o_ref = rest[:top_k], rest[top_k]
    vb = v_ref[...].astype(BF16)
    g = _dot(vb, sg_ref[...])
    u = _dot(vb, su_ref[...])
    y = _dot(((g * _sigmoid(g)) * u).astype(BF16), sd_ref[...])
    for k in range(top_k):
        y = y + w_ref[:, k:k + 1] * yk_refs[k][...]
    h2 = h1_ref[...] + gtm_ref[0] * y
    o_ref[...] = h2 * lax.rsqrt(jnp.mean(h2 * h2, axis=-1, keepdims=True) + EPS) * gfin_ref[...]


def final(v, h1, yk, w_tk, gtm, gfin, sg, su, sd, cfg):
    t, d = v.shape
    tm = min(cfg.final_tm, cfg.seq)
    per_b = cfg.seq // tm
    nt = t // tm
    ds = sg.shape[1]
    const = lambda i: (0, 0)
    kern = functools.partial(_final_kernel, top_k=cfg.top_k)
    plane = lambda k: pl.BlockSpec((tm, d), lambda i: (k * nt + i, 0))
    return pl.pallas_call(
        kern,
        grid=(nt,),
        in_specs=[pl.BlockSpec((tm, d), lambda i: (i, 0)),
                  pl.BlockSpec((tm, d), lambda i: (i, 0)),
                  pl.BlockSpec((tm, cfg.top_k), lambda i: (i, 0)),
                  pl.BlockSpec((1, 1, d), lambda i: (i // per_b, 0, 0)),
                  pl.BlockSpec((1, d), const),
                  pl.BlockSpec((d, ds), const),
                  pl.BlockSpec((d, ds), const),
                  pl.BlockSpec((ds, d), const)] + [plane(k) for k in range(cfg.top_k)],
        out_specs=pl.BlockSpec((tm, d), lambda i: (i, 0)),
        out_shape=jax.ShapeDtypeStruct((t, d), F32),
        compiler_params=_params(1),
        name="final",
    )(v, h1, w_tk, gtm, gfin, sg, su, sd, *([yk] * cfg.top_k))


def _t5_bucket(rel, cfg):
    nb = cfg.t5_buckets // 2
    ret = (rel > 0).astype(np.int32) * nb
    n = np.abs(rel)
    max_exact = nb // 2
    large = max_exact + (np.log(np.maximum(n, 1) / max_exact) / math.log(cfg.t5_max_dist / max_exact)
                         * (nb - max_exact)).astype(np.int32)
    large = np.minimum(large, nb - 1)
    return (ret + np.where(n < max_exact, n, large)).astype(np.int32)


def _toeplitz(vec, n):
    lead = vec.shape[:-1]
    flat = jnp.tile(vec, (1,) * len(lead) + (n,))[..., :n * (2 * n - 1)]
    return flat.reshape(lead + (n, 2 * n - 1))[..., :n]


def _rel_offsets(n):
    m = np.arange(2 * n)
    return np.where(m < n, m, m - 2 * n)


def _bias_tables_a(t5_bias, cfg):
    tile = cfg.attn_tile
    assert tile + 1 >= cfg.t5_max_dist and tile % cfg.chunk == 0
    ql = np.arange(tile)[:, None]
    kl = np.arange(tile)[None, :]
    allowed = (kl // cfg.chunk) <= (ql // cfg.chunk)
    r = _rel_offsets(tile)
    t5 = t5_bias.astype(F32)
    diag = _toeplitz(t5[_t5_bucket(r, cfg)].T, tile)
    prev = _toeplitz(t5[_t5_bucket(r - tile, cfg)].T, tile)
    far = jnp.broadcast_to(t5[_t5_bucket(np.array(-2 * tile), cfg)][:, None, None], diag.shape)
    return jnp.stack([jnp.where(allowed[None], diag, NEG), prev, far], axis=1)


def _bias_tables_b(rel_table, cfg):
    tile = cfg.attn_tile
    assert (cfg.left_chunks * cfg.chunk) % tile == 0 and tile % cfg.chunk == 0
    nwin = (cfg.left_chunks * cfg.chunk) // tile + 1
    ql = np.arange(tile)[:, None]
    kl = np.arange(tile)[None, :]
    r = _rel_offsets(tile)
    rt = rel_table.astype(F32)
    tabs = []
    for j in range(nwin):
        off = (j - (nwin - 1)) * tile
        idx = np.clip(r + off, -cfg.rel_clip, cfg.rel_clip) + cfg.rel_clip
        dchunk = ql // cfg.chunk - (kl + off) // cfg.chunk
        band = (dchunk >= 0) & (dchunk <= cfg.left_chunks)
        tabs.append(jnp.where(band[None], _toeplitz(rt[idx].T, tile), NEG))
    return jnp.stack(tabs, axis=1)


def _block_tables(counts, n_assign, cfg):
    e_ = cfg.n_experts
    bm = cfg.moe_rows
    nb = -(-(n_assign + e_ * (bm - 1)) // bm)
    pcounts = (counts + bm - 1) // bm * bm
    pends = jnp.cumsum(pcounts)
    pstarts = pends - pcounts
    start_b = jnp.arange(nb, dtype=jnp.int32) * bm
    block_e = jnp.minimum(jnp.sum((pends[None, :] <= start_b[:, None]).astype(jnp.int32), axis=1), e_ - 1)
    mine = jnp.arange(e_, dtype=jnp.int32)[None, :] == block_e[:, None]
    pstart_b = jnp.sum(jnp.where(mine, pstarts[None, :], 0), axis=1)
    count_b = jnp.sum(jnp.where(mine, counts[None, :], 0), axis=1)
    return block_e.astype(jnp.int32), (start_b - pstart_b).astype(jnp.int32), count_b.astype(jnp.int32)


def _forward(cfg, x, c, w_ada, b_ada, g_attn, w_in, lambda_qk, t5_bias, rel_bias_b, w_up_a, w_up_b, w_o,
             g_moe, w_router, router_bias, w_exp_gate, w_exp_up, w_exp_down,
             w_sh_gate, w_sh_up, w_sh_down, g_final):
    b_, s_, d_ = x.shape
    t_ = b_ * s_
    l = 0
    qk_a = cfg.ha * 2 * cfg.da_qk
    width_a = cfg.ha * 2 * cfg.da_qk
    width_b = cfg.hb * cfg.db
    in_cols = w_in.shape[2]

    rows = max(16, b_)
    c_pad = jnp.zeros((rows, d_), F32).at[:b_].set(c)
    mod = ada_mod(c_pad, w_ada[l], b_ada[l], cfg)[:b_]
    sh_a, sc_a, gt_a, sh_m, sc_m, gt_m = [m.reshape(b_, 1, d_) for m in jnp.split(mod, cfg.n_mod, axis=-1)]

    colscale = np.ones((1, in_cols), np.float32)
    colscale[0, :qk_a] = cfg.da_qk ** -0.5
    qb0 = 2 * qk_a + width_a
    colscale[0, qb0:qb0 + width_b] = cfg.db ** -0.5
    x2 = x.reshape(t_, d_)
    proj = in_proj(x2, g_attn[l].reshape(1, d_), sc_a, sh_a, w_in[l].astype(BF16), jnp.asarray(colscale), cfg)

    lq = lambda_qk[l].astype(F32)
    lam = jnp.exp(jnp.sum(lq[0] * lq[1])) - jnp.exp(jnp.sum(lq[2] * lq[3])) + cfg.lam_init
    ya = attn_a(proj, lam.reshape(1), _bias_tables_a(t5_bias, cfg), cfg)
    yb = attn_b(proj, _bias_tables_b(rel_bias_b[l], cfg), cfg)

    ne_pad = max(LANES, cfg.n_experts)
    wrt = jnp.zeros((ne_pad, d_), F32).at[:cfg.n_experts].set(w_router[l].T)
    wrh, wrl = _split_bf16(wrt)
    h1, v, lg = merge(ya, yb, proj, x2, gt_a, sc_m, sh_m, g_moe[l].reshape(1, d_),
                      w_up_a[l].astype(BF16), w_up_b[l].astype(BF16), w_o[l].astype(BF16), wrh, wrl, cfg)

    tc = min(cfg.route_tc, t_)
    rb = jnp.zeros((ne_pad, tc), F32).at[:cfg.n_experts].set(
        jnp.broadcast_to(router_bias[l].astype(F32)[:, None], (cfg.n_experts, tc)))
    w_kt, csel, code, cend = route(lg, rb, cfg)
    counts = cend[-1, :, 0].astype(jnp.int32)
    block_e, r0, count_b = _block_tables(counts, t_ * cfg.top_k, cfg)
    slot_tok, slot_dst = slots(block_e, r0, count_b, csel, code, cend, cfg, t_)

    yk = moe(v, block_e, slot_tok, slot_dst, w_exp_gate[l], w_exp_up[l], w_exp_down[l], cfg)
    out = final(v, h1, yk, w_kt.T, gt_m, g_final.reshape(1, d_),
                w_sh_gate[l].astype(BF16), w_sh_up[l].astype(BF16), w_sh_down[l].astype(BF16), cfg)
    return out.reshape(b_, s_, d_)


def kernel(x, c, w_ada, b_ada, g_attn, w_in, lambda_qk, t5_bias, rel_bias_b, w_up_a, w_up_b, w_o, g_moe,
           w_router, router_bias, w_exp_gate, w_exp_up, w_exp_down, w_sh_gate, w_sh_up, w_sh_down, g_final):
    return _forward(Cfg(), x, c, w_ada, b_ada, g_attn, w_in, lambda_qk, t5_bias, rel_bias_b, w_up_a, w_up_b,
                    w_o, g_moe, w_router, router_bias, w_exp_gate, w_exp_up, w_exp_down,
                    w_sh_gate, w_sh_up, w_sh_down, g_final)
```

```python
import functools
import math
from typing import NamedTuple

import numpy as np
import jax
import jax.numpy as jnp
from jax import lax
from jax.experimental import pallas as pl
from jax.experimental.pallas import tpu as pltpu

F32 = jnp.float32
BF16 = jnp.bfloat16
NEG = -1e30
EPS = 1e-6
HEAD_EPS = 1e-5
LANES = 128
VMEM_LIMIT = 56 * 1024 * 1024


class Cfg(NamedTuple):
    batch: int = 8
    seq: int = 2048
    d_model: int = 2048
    chunk: int = 64
    ha: int = 8
    da_qk: int = 64
    hb: int = 8
    db: int = 128
    left_chunks: int = 8
    rel_clip: int = 256
    t5_buckets: int = 32
    t5_max_dist: int = 128
    n_experts: int = 64
    top_k: int = 8
    n_groups: int = 8
    topk_groups: int = 4
    d_expert: int = 512
    d_shared: int = 512
    routed_scale: float = 2.5
    n_mod: int = 6
    lam_init: float = 0.8 - 0.6 * math.exp(-0.3 * 0)
    attn_tile: int = 256
    moe_rows: int = 256
    proj_tm: int = 1024
    proj_tn: int = 1024
    merge_tm: int = 256
    route_tc: int = 512
    final_tm: int = 128
    ada_tn: int = 1024


def _sigmoid(x):
    return 1.0 / (1.0 + jnp.exp(-x))


def _split_bf16(x):
    hi = x.astype(BF16)
    lo = (x - hi.astype(F32)).astype(BF16)
    return hi, lo


def _dot(a, b):
    return jnp.dot(a, b, preferred_element_type=F32)


def _dot_t(a, b):
    return lax.dot_general(a, b, (((1,), (1,)), ((), ())), preferred_element_type=F32)


def _params(n_axes):
    return pltpu.CompilerParams(dimension_semantics=("arbitrary",) * n_axes,
                                vmem_limit_bytes=VMEM_LIMIT)


def _ada_kernel(c_ref, w_ref, b_ref, o_ref):
    c = c_ref[...]
    s_hi, s_lo = _split_bf16(c * _sigmoid(c))
    w_hi, w_lo = _split_bf16(w_ref[...])
    acc = _dot(s_hi, w_hi) + _dot(s_hi, w_lo) + _dot(s_lo, w_hi)
    o_ref[...] = acc + b_ref[...]


def ada_mod(c_pad, w_ada, b_ada, cfg):
    rows, d = c_pad.shape
    n = w_ada.shape[1]
    tn = min(cfg.ada_tn, n)
    return pl.pallas_call(
        _ada_kernel,
        grid=(n // tn,),
        in_specs=[pl.BlockSpec((rows, d), lambda j: (0, 0)),
                  pl.BlockSpec((d, tn), lambda j: (0, j)),
                  pl.BlockSpec((1, tn), lambda j: (0, j))],
        out_specs=pl.BlockSpec((rows, tn), lambda j: (0, j)),
        out_shape=jax.ShapeDtypeStruct((rows, n), F32),
        compiler_params=_params(1),
        name="ada_mod",
    )(c_pad, w_ada, b_ada.reshape(1, n))


def _inproj_kernel(x_ref, g_ref, sc_ref, sh_ref, w_ref, cs_ref, o_ref, u_ref):
    @pl.when(pl.program_id(1) == 0)
    def _():
        x = x_ref[...]
        y = x * lax.rsqrt(jnp.mean(x * x, axis=-1, keepdims=True) + EPS) * g_ref[...]
        u_ref[...] = (y * (1.0 + sc_ref[0]) + sh_ref[0]).astype(BF16)

    acc = _dot(u_ref[...], w_ref[...])
    o_ref[...] = (acc * cs_ref[...]).astype(BF16)


def in_proj(x2, g, sc, sh, w_bf, colscale, cfg):
    t, d = x2.shape
    n = w_bf.shape[1]
    tm = min(cfg.proj_tm, cfg.seq)
    tn = min(cfg.proj_tn, n)
    per_b = cfg.seq // tm
    return pl.pallas_call(
        _inproj_kernel,
        grid=(t // tm, n // tn),
        in_specs=[pl.BlockSpec((tm, d), lambda i, j: (i, 0)),
                  pl.BlockSpec((1, d), lambda i, j: (0, 0)),
                  pl.BlockSpec((1, 1, d), lambda i, j: (i // per_b, 0, 0)),
                  pl.BlockSpec((1, 1, d), lambda i, j: (i // per_b, 0, 0)),
                  pl.BlockSpec((d, tn), lambda i, j: (0, j)),
                  pl.BlockSpec((1, tn), lambda i, j: (0, j))],
        out_specs=pl.BlockSpec((tm, tn), lambda i, j: (i, j)),
        out_shape=jax.ShapeDtypeStruct((t, n), BF16),
        scratch_shapes=[pltpu.VMEM((tm, d), BF16)],
        compiler_params=_params(2),
        name="in_proj",
    )(x2, g, sc, sh, w_bf, colscale)


def _attn_a_kernel(lam_ref, q_ref, k_ref, v_ref, tb_ref, o_ref,
                   s0_ref, s1_ref, m0_ref, m1_ref, l0_ref, l1_ref, acc_ref, *, tile, half, out_scale):
    qi = pl.program_id(2)
    nk = qi + 1
    q = q_ref[...]
    lane = lax.broadcasted_iota(jnp.int32, q.shape, 1)
    zero = jnp.zeros_like(q)
    q0 = jnp.where(lane < half, q, zero)
    q1 = jnp.where(lane >= half, q, zero)

    m0_ref[...] = jnp.full(m0_ref.shape, -jnp.inf, F32)
    m1_ref[...] = jnp.full(m1_ref.shape, -jnp.inf, F32)
    l0_ref[...] = jnp.zeros(l0_ref.shape, F32)
    l1_ref[...] = jnp.zeros(l1_ref.shape, F32)
    acc_ref[...] = jnp.zeros(acc_ref.shape, F32)

    def fold(x):
        r = x[:, :LANES]
        for c in range(1, tile // LANES):
            r = r + x[:, c * LANES:(c + 1) * LANES]
        return r

    def foldmax(x):
        r = x[:, :LANES]
        for c in range(1, tile // LANES):
            r = jnp.maximum(r, x[:, c * LANES:(c + 1) * LANES])
        return r

    def scores(kb, carry):
        kblk = k_ref[pl.ds(pl.multiple_of(kb * tile, tile), tile), :]
        bias = tb_ref[0, jnp.minimum(qi - kb, 2)]
        s0 = _dot_t(q0, kblk) + bias
        s1 = _dot_t(q1, kblk) + bias
        s0_ref[kb] = s0
        s1_ref[kb] = s1
        m0_ref[...] = jnp.maximum(m0_ref[...], foldmax(s0))
        m1_ref[...] = jnp.maximum(m1_ref[...], foldmax(s1))
        return carry

    lax.fori_loop(0, nk, scores, 0)
    m0 = jnp.max(m0_ref[...], axis=-1, keepdims=True)
    m1 = jnp.max(m1_ref[...], axis=-1, keepdims=True)

    def expo(kb, carry):
        p0 = jnp.exp(s0_ref[kb] - m0)
        p1 = jnp.exp(s1_ref[kb] - m1)
        s0_ref[kb] = p0
        s1_ref[kb] = p1
        l0_ref[...] += fold(p0)
        l1_ref[...] += fold(p1)
        return carry

    lax.fori_loop(0, nk, expo, 0)
    c0 = 1.0 / jnp.sum(l0_ref[...], axis=-1, keepdims=True)
    c1 = lam_ref[0] / jnp.sum(l1_ref[...], axis=-1, keepdims=True)

    def pv(kb, carry):
        a = s0_ref[kb] * c0 - s1_ref[kb] * c1
        vblk = v_ref[pl.ds(pl.multiple_of(kb * tile, tile), tile), :]
        acc_ref[...] += _dot(a.astype(BF16), vblk)
        return carry

    lax.fori_loop(0, nk, pv, 0)
    o = acc_ref[...]
    o = o * lax.rsqrt(jnp.mean(o * o, axis=-1, keepdims=True) + HEAD_EPS) * out_scale
    o_ref[...] = o.astype(BF16)


def attn_a(proj, lam, tb, cfg):
    t = proj.shape[0]
    tile = cfg.attn_tile
    s = cfg.seq
    nq = s // tile
    dv = 2 * cfg.da_qk
    assert dv == LANES
    kcol = cfg.ha
    vcol = 2 * cfg.ha
    kern = functools.partial(_attn_a_kernel, tile=tile, half=cfg.da_qk, out_scale=1.0 - cfg.lam_init)
    return pl.pallas_call(
        kern,
        grid=(cfg.batch, cfg.ha, nq),
        in_specs=[pl.BlockSpec(memory_space=pltpu.SMEM),
                  pl.BlockSpec((tile, LANES), lambda b, h, i: (b * nq + i, h)),
                  pl.BlockSpec((s, LANES), lambda b, h, i: (b, kcol + h)),
                  pl.BlockSpec((s, LANES), lambda b, h, i: (b, vcol + h)),
                  pl.BlockSpec((1, 3, tile, tile), lambda b, h, i: (h, 0, 0, 0))],
        out_specs=pl.BlockSpec((tile, LANES), lambda b, h, i: (b * nq + i, h)),
        out_shape=jax.ShapeDtypeStruct((t, cfg.ha * dv), BF16),
        scratch_shapes=[pltpu.VMEM((nq, tile, tile), F32), pltpu.VMEM((nq, tile, tile), F32),
                        pltpu.VMEM((tile, LANES), F32), pltpu.VMEM((tile, LANES), F32),
                        pltpu.VMEM((tile, LANES), F32), pltpu.VMEM((tile, LANES), F32),
                        pltpu.VMEM((tile, LANES), F32)],
        compiler_params=_params(3),
        name="attn_a",
    )(lam, proj, proj, proj, tb)


def _attn_b_kernel(q_ref, k_ref, v_ref, tb_ref, o_ref, *, tile, nwin):
    qi = pl.program_id(2)
    q = q_ref[...]
    ss, vs = [], []
    for j in range(nwin):
        kb = qi - (nwin - 1) + j
        start = pl.multiple_of(jnp.maximum(kb, 0) * tile, tile)
        s = _dot_t(q, k_ref[pl.ds(start, tile), :]) + tb_ref[0, j]
        ss.append(jnp.where(kb >= 0, s, NEG))
        vs.append(v_ref[pl.ds(start, tile), :])
    m = jnp.max(ss[0], axis=-1, keepdims=True)
    for s in ss[1:]:
        m = jnp.maximum(m, jnp.max(s, axis=-1, keepdims=True))
    ps = [jnp.exp(s - m) for s in ss]
    l = jnp.sum(ps[0], axis=-1, keepdims=True)
    for p in ps[1:]:
        l = l + jnp.sum(p, axis=-1, keepdims=True)
    inv = 1.0 / l
    acc = _dot((ps[0] * inv).astype(BF16), vs[0])
    for p, v in zip(ps[1:], vs[1:]):
        acc = acc + _dot((p * inv).astype(BF16), v)
    o_ref[...] = acc.astype(BF16)


def attn_b(proj, tb, cfg):
    t = proj.shape[0]
    tile = cfg.attn_tile
    s = cfg.seq
    nq = s // tile
    assert cfg.db == LANES
    nwin = (cfg.left_chunks * cfg.chunk) // tile + 1
    base = (2 * cfg.ha * 2 * cfg.da_qk + cfg.ha * 2 * cfg.da_qk) // LANES
    kern = functools.partial(_attn_b_kernel, tile=tile, nwin=nwin)
    return pl.pallas_call(
        kern,
        grid=(cfg.batch, cfg.hb, nq),
        in_specs=[pl.BlockSpec((tile, LANES), lambda b, h, i: (b * nq + i, base + h)),
                  pl.BlockSpec((s, LANES), lambda b, h, i: (b, base + cfg.hb + h)),
                  pl.BlockSpec((s, LANES), lambda b, h, i: (b, base + 2 * cfg.hb + h)),
                  pl.BlockSpec((1, nwin, tile, tile), lambda b, h, i: (h, 0, 0, 0))],
        out_specs=pl.BlockSpec((tile, LANES), lambda b, h, i: (b * nq + i, h)),
        out_shape=jax.ShapeDtypeStruct((t, cfg.hb * cfg.db), BF16),
        compiler_params=_params(3),
        name="attn_b",
    )(proj, proj, proj, tb)


def _merge_kernel(ya_ref, yb_ref, ga_ref, gb_ref, x_ref, gta_ref, scm_ref, shm_ref, gmoe_ref,
                  wua_ref, wub_ref, wo_ref, wrh_ref, wrl_ref, h1_ref, v_ref, lg_ref):
    a = _dot(ya_ref[...], wua_ref[...])
    b = _dot(yb_ref[...], wub_ref[...])
    merged = _sigmoid(ga_ref[...].astype(F32)) * a + _sigmoid(gb_ref[...].astype(F32)) * b
    att = _dot(merged.astype(BF16), wo_ref[...])
    h1 = x_ref[...] + gta_ref[0] * att
    h1_ref[...] = h1
    y = h1 * lax.rsqrt(jnp.mean(h1 * h1, axis=-1, keepdims=True) + EPS) * gmoe_ref[...]
    v = y * (1.0 + scm_ref[0]) + shm_ref[0]
    v_ref[...] = v
    v_hi, v_lo = _split_bf16(v)
    wrh = wrh_ref[...]
    lg_ref[...] = _dot_t(wrh, v_hi) + _dot_t(wrh, v_lo) + _dot_t(wrl_ref[...], v_hi)


def merge(ya, yb, proj, x2, gta, scm, shm, gmoe, wua, wub, wo, wrh, wrl, cfg):
    t, d = x2.shape
    tm = min(cfg.merge_tm, cfg.seq)
    per_b = cfg.seq // tm
    wa = ya.shape[1]
    wb = yb.shape[1]
    gate0 = (proj.shape[1] - 2 * d) // d
    ne = wrh.shape[0]
    const = lambda i: (0, 0)
    row3 = lambda i: (i // per_b, 0, 0)
    return pl.pallas_call(
        _merge_kernel,
        grid=(t // tm,),
        in_specs=[pl.BlockSpec((tm, wa), lambda i: (i, 0)),
                  pl.BlockSpec((tm, wb), lambda i: (i, 0)),
                  pl.BlockSpec((tm, d), lambda i: (i, gate0)),
                  pl.BlockSpec((tm, d), lambda i: (i, gate0 + 1)),
                  pl.BlockSpec((tm, d), lambda i: (i, 0)),
                  pl.BlockSpec((1, 1, d), row3),
                  pl.BlockSpec((1, 1, d), row3),
                  pl.BlockSpec((1, 1, d), row3),
                  pl.BlockSpec((1, d), const),
                  pl.BlockSpec((wa, d), const),
                  pl.BlockSpec((wb, d), const),
                  pl.BlockSpec((d, d), const),
                  pl.BlockSpec((ne, d), const),
                  pl.BlockSpec((ne, d), const)],
        out_specs=[pl.BlockSpec((tm, d), lambda i: (i, 0)),
                   pl.BlockSpec((tm, d), lambda i: (i, 0)),
                   pl.BlockSpec((ne, tm), lambda i: (0, i))],
        out_shape=[jax.ShapeDtypeStruct((t, d), F32),
                   jax.ShapeDtypeStruct((t, d), F32),
                   jax.ShapeDtypeStruct((ne, t), F32)],
        compiler_params=_params(1),
        name="merge",
    )(ya, yb, proj, proj, x2, gta, scm, shm, gmoe, wua, wub, wo, wrh, wrl)


def _route_kernel(lg_ref, rb_ref, tri_ref, w_ref, csel_ref, code_ref, cend_ref, carry_ref,
                  *, n_exp, n_groups, topk_groups, top_k, scale):
    gsz = n_exp // n_groups
    tc = lg_ref.shape[1]
    ninf = -jnp.inf
    step = pl.program_id(0)

    @pl.when(step == 0)
    def _():
        carry_ref[...] = jnp.zeros(carry_ref.shape, F32)

    row = lax.broadcasted_iota(jnp.int32, (gsz, tc), 0).astype(F32)
    sc, bi = [], []
    for g in range(n_groups):
        s = _sigmoid(lg_ref[g * gsz:(g + 1) * gsz, :])
        sc.append(s)
        bi.append(s + rb_ref[g * gsz:(g + 1) * gsz, :])
    gs = []
    for x in bi:
        m1 = jnp.max(x, axis=0, keepdims=True)
        eq = x == m1
        cnt = jnp.sum(jnp.where(eq, 1.0, 0.0), axis=0, keepdims=True)
        second = jnp.max(jnp.where(eq, ninf, x), axis=0, keepdims=True)
        gs.append(m1 + jnp.where(cnt >= 2.0, m1, second))
    gmat = jnp.concatenate(gs, axis=0)
    grow = lax.broadcasted_iota(jnp.int32, gmat.shape, 0).astype(F32)
    gsel = jnp.zeros(gmat.shape, F32)
    for _ in range(topk_groups):
        m = jnp.max(gmat, axis=0, keepdims=True)
        idx = jnp.min(jnp.where(gmat == m, grow, float(n_groups)), axis=0, keepdims=True)
        hit = grow == idx
        gsel = jnp.where(hit, 1.0, gsel)
        gmat = jnp.where(hit, ninf, gmat)
    cur = [jnp.where(gsel[g:g + 1, :] > 0.5, bi[g], ninf) for g in range(n_groups)]
    ids = [row + float(g * gsz) for g in range(n_groups)]
    w_rows = []
    chosen = [jnp.zeros((gsz, tc), F32) for _ in range(n_groups)]
    kplus = [jnp.zeros((gsz, tc), F32) for _ in range(n_groups)]
    for k in range(top_k):
        mg = cur[0]
        for c in cur[1:]:
            mg = jnp.maximum(mg, c)
        m = jnp.max(mg, axis=0, keepdims=True)
        cand = jnp.where(cur[0] == m, ids[0], float(n_exp))
        for c, i in zip(cur[1:], ids[1:]):
            cand = jnp.minimum(cand, jnp.where(c == m, i, float(n_exp)))
        idx = jnp.min(cand, axis=0, keepdims=True)
        wsum = jnp.zeros((gsz, tc), F32)
        for g in range(n_groups):
            hit = ids[g] == idx
            wsum = wsum + jnp.where(hit, sc[g], 0.0)
            cur[g] = jnp.where(hit, ninf, cur[g])
            chosen[g] = jnp.where(hit, 1.0, chosen[g])
            kplus[g] = jnp.where(hit, float(k + 1), kplus[g])
        w_rows.append(jnp.sum(wsum, axis=0, keepdims=True))
    w = jnp.concatenate(w_rows, axis=0)
    w_ref[...] = w / jnp.sum(w, axis=0, keepdims=True) * scale

    onehot = jnp.concatenate(chosen, axis=0)
    incl = _dot(onehot.astype(BF16), tri_ref[...])
    carry = carry_ref[...]
    cglob = carry[:, 0:1] + incl
    tok = (lax.broadcasted_iota(jnp.int32, (n_exp, tc), 1) + step * tc).astype(F32)
    picked = onehot > 0.5
    csel_ref[0:n_exp, :] = jnp.where(picked, cglob, 0.0)
    code_ref[0:n_exp, :] = jnp.where(picked, tok * float(top_k) + jnp.concatenate(kplus, axis=0) - 1.0, 0.0)
    new_carry = carry + jnp.sum(onehot, axis=1, keepdims=True)
    carry_ref[...] = new_carry
    cend_ref[0] = new_carry


def route(lg, rb, cfg):
    ne, t = lg.shape
    nx = cfg.n_experts
    tc = min(cfg.route_tc, t)
    nsteps = t // tc
    tri = jnp.asarray(np.triu(np.ones((tc, tc), np.float32)), BF16)
    kern = functools.partial(_route_kernel, n_exp=cfg.n_experts, n_groups=cfg.n_groups,
                             topk_groups=cfg.topk_groups, top_k=cfg.top_k, scale=cfg.routed_scale)
    return pl.pallas_call(
        kern,
        grid=(nsteps,),
        in_specs=[pl.BlockSpec((ne, tc), lambda i: (0, i)),
                  pl.BlockSpec((ne, tc), lambda i: (0, 0)),
                  pl.BlockSpec((tc, tc), lambda i: (0, 0))],
        out_specs=[pl.BlockSpec((cfg.top_k, tc), lambda i: (0, i)),
                   pl.BlockSpec((nx, tc), lambda i: (0, i)),
                   pl.BlockSpec((nx, tc), lambda i: (0, i)),
                   pl.BlockSpec((1, nx, LANES), lambda i: (i, 0, 0))],
        out_shape=[jax.ShapeDtypeStruct((cfg.top_k, t), F32),
                   jax.ShapeDtypeStruct((nx, t), F32),
                   jax.ShapeDtypeStruct((nx, t), F32),
                   jax.ShapeDtypeStruct((nsteps, nx, LANES), F32)],
        scratch_shapes=[pltpu.VMEM((cfg.n_experts, LANES), F32)],
        compiler_params=_params(1),
        name="route",
    )(lg, rb, tri)


def _slots_kernel(be_ref, r0_ref, cnt_ref, csel_ref, code_ref, cend_ref, tok_ref, dst_ref,
                  *, bm, tc, top_k, n_tok, n_assign):
    b = pl.program_id(0)
    e = be_ref[b]
    r0 = r0_ref[b]
    cnt = cnt_ref[b]
    sub = lax.rem(e, 8)
    nsteps = cend_ref.shape[0]
    pick3 = lax.broadcasted_iota(jnp.int32, cend_ref.shape, 1) == sub
    cend = jnp.sum(jnp.where(pick3, cend_ref[...], 0.0), axis=1)
    first = (r0 + 1).astype(F32)
    last = jnp.minimum(r0 + bm, cnt).astype(F32)
    jlo = (jnp.sum(jnp.where(cend < first, 1.0, 0.0)) * (1.0 / LANES)).astype(jnp.int32)
    jhi = (jnp.sum(jnp.where(cend < last, 1.0, 0.0)) * (1.0 / LANES)).astype(jnp.int32)
    jhi = jnp.minimum(jhi, nsteps - 1)
    jhi = jnp.where(last < first, jlo - 1, jhi)

    want = (lax.broadcasted_iota(jnp.int32, (bm, tc), 0) + (r0 + 1)).astype(F32)
    pick2 = lax.broadcasted_iota(jnp.int32, (8, tc), 0) == sub
    zrows = jnp.zeros((13, tc), F32)

    def body(j, acc):
        off = pl.multiple_of(j * tc, tc)
        crow = jnp.sum(jnp.where(pick2, csel_ref[:, pl.ds(off, tc)], 0.0), axis=0, keepdims=True)
        drow = jnp.sum(jnp.where(pick2, code_ref[:, pl.ds(off, tc)], 0.0), axis=0, keepdims=True)
        di = drow.astype(jnp.int32)
        digits = jnp.concatenate([(di >> 14).astype(F32), ((di >> 7) & 127).astype(F32),
                                  (di & 127).astype(F32), zrows], axis=0).astype(BF16)
        eq = jnp.where(want == crow, 1.0, 0.0).astype(BF16)
        return acc + _dot_t(digits, eq)

    acc = lax.fori_loop(jlo, jhi + 1, body, jnp.zeros((16, bm), F32))
    code = (acc[0:1, :] * 16384.0 + acc[1:2, :] * 128.0 + acc[2:3, :]).astype(jnp.int32)
    lane = lax.broadcasted_iota(jnp.int32, (1, bm), 1)
    valid = (lane + r0) < cnt
    shift = top_k.bit_length() - 1
    tok = code >> shift
    kk = code & (top_k - 1)
    dummy = n_assign + lax.rem(b, 2) * bm + lane
    tok_ref[0] = jnp.where(valid, tok, 0)
    dst_ref[0] = jnp.where(valid, kk * n_tok + tok, dummy)


def slots(block_e, r0, cntb, csel, code, cend, cfg, n_tok):
    nb = block_e.shape[0]
    bm = cfg.moe_rows
    ne, t = csel.shape
    tc = min(cfg.route_tc, t)
    nsteps = cend.shape[0]
    assert cfg.top_k & (cfg.top_k - 1) == 0 and n_tok * cfg.top_k < (1 << 21)
    kern = functools.partial(_slots_kernel, bm=bm, tc=tc, top_k=cfg.top_k, n_tok=n_tok, n_assign=n_tok * cfg.top_k)
    grid_spec = pltpu.PrefetchScalarGridSpec(
        num_scalar_prefetch=3,
        grid=(nb,),
        in_specs=[pl.BlockSpec((8, t), lambda b, be, r0, cn: (be[b] // 8, 0)),
                  pl.BlockSpec((8, t), lambda b, be, r0, cn: (be[b] // 8, 0)),
                  pl.BlockSpec((nsteps, 8, LANES), lambda b, be, r0, cn: (0, be[b] // 8, 0))],
        out_specs=[pl.BlockSpec((1, 1, bm), lambda b, be, r0, cn: (b, 0, 0)),
                   pl.BlockSpec((1, 1, bm), lambda b, be, r0, cn: (b, 0, 0))],
    )
    return pl.pallas_call(
        kern,
        grid_spec=grid_spec,
        out_shape=[jax.ShapeDtypeStruct((nb, 1, bm), jnp.int32), jax.ShapeDtypeStruct((nb, 1, bm), jnp.int32)],
        compiler_params=_params(1),
        name="slots",
    )(block_e, r0, cntb, csel, code, cend)


def _moe_kernel(be_ref, tokc_ref, tokn_ref, dstp_ref, dstc_ref, v_hbm, wg_ref, wu_ref, wd_ref, yk_hbm,
                xbuf, ybuf, wgb, wub, wdb, gsem, ssem, *, bm, n_assign):
    b = pl.program_id(0)
    nb = pl.num_programs(0)
    slot = lax.rem(b, 2)
    other = 1 - slot

    nslab = xbuf.shape[2]

    def gather_copy(t, r, s):
        return pltpu.make_async_copy(v_hbm.at[t], xbuf.at[s, r // 8, :, r % 8, :], gsem)

    def scatter_copy(r, d, s):
        return pltpu.make_async_copy(ybuf.at[s, pl.ds(r, 1), :], yk_hbm.at[pl.ds(d, 1), :], ssem)

    def wait_gather(s):
        for r in range(bm):
            gather_copy(0, r, s).wait()

    def scatter_batch(s, row0):
        return pltpu.make_async_copy(ybuf.at[s], yk_hbm.at[pl.ds(row0, bm), :], ssem)

    @pl.when(b == 0)
    def _():
        ybuf[1] = jnp.zeros(ybuf.shape[1:], F32)
        scatter_batch(1, n_assign).start()
        for r in range(bm):
            gather_copy(tokc_ref[0, 0, r], r, 0).start(priority=r % 2)

    e = be_ref[b]
    eprev = be_ref[jnp.maximum(b - 1, 0)]

    @pl.when(jnp.logical_or(b == 0, e != eprev))
    def _():
        wgb[...] = wg_ref[0].astype(BF16)
        wub[...] = wu_ref[0].astype(BF16)
        wdb[...] = wd_ref[0].astype(BF16)

    def block_step(s):
        o = 1 - s
        wait_gather(s)
        scatter_batch(s, 0).wait()
        for r in range(bm):
            gather_copy(tokn_ref[0, 0, r], r, o).start(priority=r % 2)
        for r in range(bm):
            scatter_copy(r, dstp_ref[0, 0, r], o).start(priority=r % 2)
        x = jnp.concatenate([xbuf[s, :, j].reshape(bm, LANES) for j in range(nslab)], axis=1).astype(BF16)
        g = _dot(x, wgb[...])
        u = _dot(x, wub[...])
        h = (g * _sigmoid(g)) * u
        ybuf[s] = _dot(h.astype(BF16), wdb[...])

    for s in range(2):
        pl.when(slot == s)(functools.partial(block_step, s))

    @pl.when(b == nb - 1)
    def _():
        for s in range(2):
            pl.when(other == s)(functools.partial(wait_gather, s))
        scatter_batch(other, 0).wait()
        for r in range(bm):
            scatter_copy(r, dstc_ref[0, 0, r], slot).start(priority=r % 2)
        scatter_batch(slot, 0).wait()


def moe(v, block_e, slot_tok, slot_dst, wg, wu, wd, cfg):
    t, d = v.shape
    bm = cfg.moe_rows
    v3 = v.reshape(t, d // LANES, LANES)
    nb = block_e.shape[0]
    de = wg.shape[2]
    n_assign = t * cfg.top_k
    kern = functools.partial(_moe_kernel, bm=bm, n_assign=n_assign)
    smem_blk = lambda f: pl.BlockSpec((1, 1, bm), f, memory_space=pltpu.SMEM)
    grid_spec = pltpu.PrefetchScalarGridSpec(
        num_scalar_prefetch=1,
        grid=(nb,),
        in_specs=[smem_blk(lambda b, be: (b, 0, 0)),
                  smem_blk(lambda b, be: (jnp.minimum(b + 1, nb - 1), 0, 0)),
                  smem_blk(lambda b, be: (jnp.maximum(b - 1, 0), 0, 0)),
                  smem_blk(lambda b, be: (b, 0, 0)),
                  pl.BlockSpec(memory_space=pl.ANY),
                  pl.BlockSpec((1, d, de), lambda b, be: (be[b], 0, 0)),
                  pl.BlockSpec((1, d, de), lambda b, be: (be[b], 0, 0)),
                  pl.BlockSpec((1, de, d), lambda b, be: (be[b], 0, 0))],
        out_specs=pl.BlockSpec(memory_space=pl.ANY),
        scratch_shapes=[pltpu.VMEM((2, bm // 8, d // LANES, 8, LANES), F32), pltpu.VMEM((2, bm, d), F32),
                        pltpu.VMEM((d, de), BF16), pltpu.VMEM((d, de), BF16), pltpu.VMEM((de, d), BF16),
                        pltpu.SemaphoreType.DMA(()), pltpu.SemaphoreType.DMA(())],
    )
    return pl.pallas_call(
        kern,
        grid_spec=grid_spec,
        out_shape=jax.ShapeDtypeStruct((n_assign + 2 * bm, d), F32),
        compiler_params=_params(1),
        name="moe",
    )(block_e, slot_tok, slot_tok, slot_dst, slot_dst, v3, wg, wu, wd)


def _final_kernel(v_ref, h1_ref, w_ref, gtm_ref, gfin_ref, sg_ref, su_ref, sd_ref, *rest, top_k):
    yk_refs, o_ref = rest[:top_k], rest[top_k]
    vb = v_ref[...].astype(BF16)
    g = _dot(vb, sg_ref[...])
    u = _dot(vb, su_ref[...])
    y = _dot(((g * _sigmoid(g)) * u).astype(BF16), sd_ref[...])
    for k in range(top_k):
        y = y + w_ref[:, k:k + 1] * yk_refs[k][...]
    h2 = h1_ref[...] + gtm_ref[0] * y
    o_ref[...] = h2 * lax.rsqrt(jnp.mean(h2 * h2, axis=-1, keepdims=True) + EPS) * gfin_ref[...]


def final(v, h1, yk, w_tk, gtm, gfin, sg, su, sd, cfg):
    t, d = v.shape
    tm = min(cfg.final_tm, cfg.seq)
    per_b = cfg.seq // tm
    nt = t // tm
    ds = sg.shape[1]
    const = lambda i: (0, 0)
    kern = functools.partial(_final_kernel, top_k=cfg.top_k)
    plane = lambda k: pl.BlockSpec((tm, d), lambda i: (k * nt + i, 0))
    return pl.pallas_call(
        kern,
        grid=(nt,),
        in_specs=[pl.BlockSpec((tm, d), lambda i: (i, 0)),
                  pl.BlockSpec((tm, d), lambda i: (i, 0)),
                  pl.BlockSpec((tm, cfg.top_k), lambda i: (i, 0)),
                  pl.BlockSpec((1, 1, d), lambda i: (i // per_b, 0, 0)),
                  pl.BlockSpec((1, d), const),
                  pl.BlockSpec((d, ds), const),
                  pl.BlockSpec((d, ds), const),
                  pl.BlockSpec((ds, d), const)] + [plane(k) for k in range(cfg.top_k)],
        out_specs=pl.BlockSpec((tm, d), lambda i: (i, 0)),
        out_shape=jax.ShapeDtypeStruct((t, d), F32),
        compiler_params=_params(1),
        name="final",
    )(v, h1, w_tk, gtm, gfin, sg, su, sd, *([yk] * cfg.top_k))


def _t5_bucket(rel, cfg):
    nb = cfg.t5_buckets // 2
    ret = (rel > 0).astype(np.int32) * nb
    n = np.abs(rel)
    max_exact = nb // 2
    large = max_exact + (np.log(np.maximum(n, 1) / max_exact) / math.log(cfg.t5_max_dist / max_exact)
                         * (nb - max_exact)).astype(np.int32)
    large = np.minimum(large, nb - 1)
    return (ret + np.where(n < max_exact, n, large)).astype(np.int32)


def _toeplitz(vec, n):
    lead = vec.shape[:-1]
    flat = jnp.tile(vec, (1,) * len(lead) + (n,))[..., :n * (2 * n - 1)]
    return flat.reshape(lead + (n, 2 * n - 1))[..., :n]


def _rel_offsets(n):
    m = np.arange(2 * n)
    return np.where(m < n, m, m - 2 * n)


def _bias_tables_a(t5_bias, cfg):
    tile = cfg.attn_tile
    assert tile + 1 >= cfg.t5_max_dist and tile % cfg.chunk == 0
    ql = np.arange(tile)[:, None]
    kl = np.arange(tile)[None, :]
    allowed = (kl // cfg.chunk) <= (ql // cfg.chunk)
    r = _rel_offsets(tile)
    t5 = t5_bias.astype(F32)
    diag = _toeplitz(t5[_t5_bucket(r, cfg)].T, tile)
    prev = _toeplitz(t5[_t5_bucket(r - tile, cfg)].T, tile)
    far = jnp.broadcast_to(t5[_t5_bucket(np.array(-2 * tile), cfg)][:, None, None], diag.shape)
    return jnp.stack([jnp.where(allowed[None], diag, NEG), prev, far], axis=1)


def _bias_tables_b(rel_table, cfg):
    tile = cfg.attn_tile
    assert (cfg.left_chunks * cfg.chunk) % tile == 0 and tile % cfg.chunk == 0
    nwin = (cfg.left_chunks * cfg.chunk) // tile + 1
    ql = np.arange(tile)[:, None]
    kl = np.arange(tile)[None, :]
    r = _rel_offsets(tile)
    rt = rel_table.astype(F32)
    tabs = []
    for j in range(nwin):
        off = (j - (nwin - 1)) * tile
        idx = np.clip(r + off, -cfg.rel_clip, cfg.rel_clip) + cfg.rel_clip
        dchunk = ql // cfg.chunk - (kl + off) // cfg.chunk
        band = (dchunk >= 0) & (dchunk <= cfg.left_chunks)
        tabs.append(jnp.where(band[None], _toeplitz(rt[idx].T, tile), NEG))
    return jnp.stack(tabs, axis=1)


def _block_tables(counts, n_assign, cfg):
    e_ = cfg.n_experts
    bm = cfg.moe_rows
    nb = -(-(n_assign + e_ * (bm - 1)) // bm)
    pcounts = (counts + bm - 1) // bm * bm
    pends = jnp.cumsum(pcounts)
    pstarts = pends - pcounts
    start_b = jnp.arange(nb, dtype=jnp.int32) * bm
    block_e = jnp.minimum(jnp.sum((pends[None, :] <= start_b[:, None]).astype(jnp.int32), axis=1), e_ - 1)
    mine = jnp.arange(e_, dtype=jnp.int32)[None, :] == block_e[:, None]
    pstart_b = jnp.sum(jnp.where(mine, pstarts[None, :], 0), axis=1)
    count_b = jnp.sum(jnp.where(mine, counts[None, :], 0), axis=1)
    return block_e.astype(jnp.int32), (start_b - pstart_b).astype(jnp.int32), count_b.astype(jnp.int32)


def _forward(cfg, x, c, w_ada, b_ada, g_attn, w_in, lambda_qk, t5_bias, rel_bias_b, w_up_a, w_up_b, w_o,
             g_moe, w_router, router_bias, w_exp_gate, w_exp_up, w_exp_down,
             w_sh_gate, w_sh_up, w_sh_down, g_final):
    b_, s_, d_ = x.shape
    t_ = b_ * s_
    l = 0
    qk_a = cfg.ha * 2 * cfg.da_qk
    width_a = cfg.ha * 2 * cfg.da_qk
    width_b = cfg.hb * cfg.db
    in_cols = w_in.shape[2]

    rows = max(16, b_)
    c_pad = jnp.zeros((rows, d_), F32).at[:b_].set(c)
    mod = ada_mod(c_pad, w_ada[l], b_ada[l], cfg)[:b_]
    sh_a, sc_a, gt_a, sh_m, sc_m, gt_m = [m.reshape(b_, 1, d_) for m in jnp.split(mod, cfg.n_mod, axis=-1)]

    colscale = np.ones((1, in_cols), np.float32)
    colscale[0, :qk_a] = cfg.da_qk ** -0.5
    qb0 = 2 * qk_a + width_a
    colscale[0, qb0:qb0 + width_b] = cfg.db ** -0.5
    x2 = x.reshape(t_, d_)
    proj = in_proj(x2, g_attn[l].reshape(1, d_), sc_a, sh_a, w_in[l].astype(BF16), jnp.asarray(colscale), cfg)

    lq = lambda_qk[l].astype(F32)
    lam = jnp.exp(jnp.sum(lq[0] * lq[1])) - jnp.exp(jnp.sum(lq[2] * lq[3])) + cfg.lam_init
    ya = attn_a(proj, lam.reshape(1), _bias_tables_a(t5_bias, cfg), cfg)
    yb = attn_b(proj, _bias_tables_b(rel_bias_b[l], cfg), cfg)

    ne_pad = max(LANES, cfg.n_experts)
    wrt = jnp.zeros((ne_pad, d_), F32).at[:cfg.n_experts].set(w_router[l].T)
    wrh, wrl = _split_bf16(wrt)
    h1, v, lg = merge(ya, yb, proj, x2, gt_a, sc_m, sh_m, g_moe[l].reshape(1, d_),
                      w_up_a[l].astype(BF16), w_up_b[l].astype(BF16), w_o[l].astype(BF16), wrh, wrl, cfg)

    tc = min(cfg.route_tc, t_)
    rb = jnp.zeros((ne_pad, tc), F32).at[:cfg.n_experts].set(
        jnp.broadcast_to(router_bias[l].astype(F32)[:, None], (cfg.n_experts, tc)))
    w_kt, csel, code, cend = route(lg, rb, cfg)
    counts = cend[-1, :, 0].astype(jnp.int32)
    block_e, r0, count_b = _block_tables(counts, t_ * cfg.top_k, cfg)
    slot_tok, slot_dst = slots(block_e, r0, count_b, csel, code, cend, cfg, t_)

    yk = moe(v, block_e, slot_tok, slot_dst, w_exp_gate[l], w_exp_up[l], w_exp_down[l], cfg)
    out = final(v, h1, yk, w_kt.T, gt_m, g_final.reshape(1, d_),
                w_sh_gate[l].astype(BF16), w_sh_up[l].astype(BF16), w_sh_down[l].astype(BF16), cfg)
    return out.reshape(b_, s_, d_)


def kernel(x, c, w_ada, b_ada, g_attn, w_in, lambda_qk, t5_bias, rel_bias_b, w_up_a, w_up_b, w_o, g_moe,
           w_router, router_bias, w_exp_gate, w_exp_up, w_exp_down, w_sh_gate, w_sh_up, w_sh_down, g_final):
    return _forward(Cfg(), x, c, w_ada, b_ada, g_attn, w_in, lambda_qk, t5_bias, rel_bias_b, w_up_a, w_up_b,
                    w_o, g_moe, w_router, router_bias, w_exp_gate, w_exp_up, w_exp_down,
                    w_sh_gate, w_sh_up, w_sh_down, g_final)
```

```python
import functools
import math
from typing import NamedTuple

import numpy as np
import jax
import jax.numpy as jnp
from jax import lax
from jax.experimental import pallas as pl
from jax.experimental.pallas import tpu as pltpu

F32 = jnp.float32
BF16 = jnp.bfloat16
NEG = -1e30
EPS = 1e-6
HEAD_EPS = 1e-5
LANES = 128
VMEM_LIMIT = 56 * 1024 * 1024


class Cfg(NamedTuple):
    batch: int = 8
    seq: int = 2048
    d_model: int = 2048
    chunk: int = 64
    ha: int = 8
    da_qk: int = 64
    hb: int = 8
    db: int = 128
    left_chunks: int = 8
    rel_clip: int = 256
    t5_buckets: int = 32
    t5_max_dist: int = 128
    n_experts: int = 64
    top_k: int = 8
    n_groups: int = 8
    topk_groups: int = 4
    d_expert: int = 512
    d_shared: int = 512
    routed_scale: float = 2.5
    n_mod: int = 6
    lam_init: float = 0.8 - 0.6 * math.exp(-0.3 * 0)
    attn_tile: int = 256
    moe_rows: int = 256
    proj_tm: int = 1024
    proj_tn: int = 1024
    merge_tm: int = 256
    route_tc: int = 512
    final_tm: int = 128
    ada_tn: int = 1024


def _sigmoid(x):
    return 1.0 / (1.0 + jnp.exp(-x))


def _split_bf16(x):
    hi = x.astype(BF16)
    lo = (x - hi.astype(F32)).astype(BF16)
    return hi, lo


def _dot(a, b):
    return jnp.dot(a, b, preferred_element_type=F32)


def _dot_t(a, b):
    return lax.dot_general(a, b, (((1,), (1,)), ((), ())), preferred_element_type=F32)


def _params(n_axes):
    return pltpu.CompilerParams(dimension_semantics=("arbitrary",) * n_axes,
                                vmem_limit_bytes=VMEM_LIMIT)


def _ada_kernel(c_ref, w_ref, b_ref, o_ref):
    c = c_ref[...]
    s_hi, s_lo = _split_bf16(c * _sigmoid(c))
    w_hi, w_lo = _split_bf16(w_ref[...])
    acc = _dot(s_hi, w_hi) + _dot(s_hi, w_lo) + _dot(s_lo, w_hi)
    o_ref[...] = acc + b_ref[...]


def ada_mod(c_pad, w_ada, b_ada, cfg):
    rows, d = c_pad.shape
    n = w_ada.shape[1]
    tn = min(cfg.ada_tn, n)
    return pl.pallas_call(
        _ada_kernel,
        grid=(n // tn,),
        in_specs=[pl.BlockSpec((rows, d), lambda j: (0, 0)),
                  pl.BlockSpec((d, tn), lambda j: (0, j)),
                  pl.BlockSpec((1, tn), lambda j: (0, j))],
        out_specs=pl.BlockSpec((rows, tn), lambda j: (0, j)),
        out_shape=jax.ShapeDtypeStruct((rows, n), F32),
        compiler_params=_params(1),
        name="ada_mod",
    )(c_pad, w_ada, b_ada.reshape(1, n))


def _inproj_kernel(x_ref, g_ref, sc_ref, sh_ref, w_ref, cs_ref, o_ref, u_ref):
    @pl.when(pl.program_id(1) == 0)
    def _():
        x = x_ref[...]
        y = x * lax.rsqrt(jnp.mean(x * x, axis=-1, keepdims=True) + EPS) * g_ref[...]
        u_ref[...] = (y * (1.0 + sc_ref[0]) + sh_ref[0]).astype(BF16)

    acc = _dot(u_ref[...], w_ref[...])
    o_ref[...] = (acc * cs_ref[...]).astype(BF16)


def in_proj(x2, g, sc, sh, w_bf, colscale, cfg):
    t, d = x2.shape
    n = w_bf.shape[1]
    tm = min(cfg.proj_tm, cfg.seq)
    tn = min(cfg.proj_tn, n)
    per_b = cfg.seq // tm
    return pl.pallas_call(
        _inproj_kernel,
        grid=(t // tm, n // tn),
        in_specs=[pl.BlockSpec((tm, d), lambda i, j: (i, 0)),
                  pl.BlockSpec((1, d), lambda i, j: (0, 0)),
                  pl.BlockSpec((1, 1, d), lambda i, j: (i // per_b, 0, 0)),
                  pl.BlockSpec((1, 1, d), lambda i, j: (i // per_b, 0, 0)),
                  pl.BlockSpec((d, tn), lambda i, j: (0, j)),
                  pl.BlockSpec((1, tn), lambda i, j: (0, j))],
        out_specs=pl.BlockSpec((tm, tn), lambda i, j: (i, j)),
        out_shape=jax.ShapeDtypeStruct((t, n), BF16),
        scratch_shapes=[pltpu.VMEM((tm, d), BF16)],
        compiler_params=_params(2),
        name="in_proj",
    )(x2, g, sc, sh, w_bf, colscale)


def _attn_a_kernel(lam_ref, q_ref, k_ref, v_ref, tb_ref, o_ref,
                   s0_ref, s1_ref, m0_ref, m1_ref, l0_ref, l1_ref, acc_ref, *, tile, half, out_scale):
    qi = pl.program_id(2)
    nk = qi + 1
    q = q_ref[...]
    lane = lax.broadcasted_iota(jnp.int32, q.shape, 1)
    zero = jnp.zeros_like(q)
    q0 = jnp.where(lane < half, q, zero)
    q1 = jnp.where(lane >= half, q, zero)

    m0_ref[...] = jnp.full(m0_ref.shape, -jnp.inf, F32)
    m1_ref[...] = jnp.full(m1_ref.shape, -jnp.inf, F32)
    l0_ref[...] = jnp.zeros(l0_ref.shape, F32)
    l1_ref[...] = jnp.zeros(l1_ref.shape, F32)
    acc_ref[...] = jnp.zeros(acc_ref.shape, F32)

    def fold(x):
        r = x[:, :LANES]
        for c in range(1, tile // LANES):
            r = r + x[:, c * LANES:(c + 1) * LANES]
        return r

    def foldmax(x):
        r = x[:, :LANES]
        for c in range(1, tile // LANES):
            r = jnp.maximum(r, x[:, c * LANES:(c + 1) * LANES])
        return r

    def scores(kb, carry):
        kblk = k_ref[pl.ds(pl.multiple_of(kb * tile, tile), tile), :]
        bias = tb_ref[0, jnp.minimum(qi - kb, 2)]
        s0 = _dot_t(q0, kblk) + bias
        s1 = _dot_t(q1, kblk) + bias
        s0_ref[kb] = s0
        s1_ref[kb] = s1
        m0_ref[...] = jnp.maximum(m0_ref[...], foldmax(s0))
        m1_ref[...] = jnp.maximum(m1_ref[...], foldmax(s1))
        return carry

    lax.fori_loop(0, nk, scores, 0)
    m0 = jnp.max(m0_ref[...], axis=-1, keepdims=True)
    m1 = jnp.max(m1_ref[...], axis=-1, keepdims=True)

    def expo(kb, carry):
        p0 = jnp.exp(s0_ref[kb] - m0)
        p1 = jnp.exp(s1_ref[kb] - m1)
        s0_ref[kb] = p0
        s1_ref[kb] = p1
        l0_ref[...] += fold(p0)
        l1_ref[...] += fold(p1)
        return carry

    lax.fori_loop(0, nk, expo, 0)
    c0 = 1.0 / jnp.sum(l0_ref[...], axis=-1, keepdims=True)
    c1 = lam_ref[0] / jnp.sum(l1_ref[...], axis=-1, keepdims=True)

    def pv(kb, carry):
        a = s0_ref[kb] * c0 - s1_ref[kb] * c1
        vblk = v_ref[pl.ds(pl.multiple_of(kb * tile, tile), tile), :]
        acc_ref[...] += _dot(a.astype(BF16), vblk)
        return carry

    lax.fori_loop(0, nk, pv, 0)
    o = acc_ref[...]
    o = o * lax.rsqrt(jnp.mean(o * o, axis=-1, keepdims=True) + HEAD_EPS) * out_scale
    o_ref[...] = o.astype(BF16)


def attn_a(proj, lam, tb, cfg):
    t = proj.shape[0]
    tile = cfg.attn_tile
    s = cfg.seq
    nq = s // tile
    dv = 2 * cfg.da_qk
    assert dv == LANES
    kcol = cfg.ha
    vcol = 2 * cfg.ha
    kern = functools.partial(_attn_a_kernel, tile=tile, half=cfg.da_qk, out_scale=1.0 - cfg.lam_init)
    return pl.pallas_call(
        kern,
        grid=(cfg.batch, cfg.ha, nq),
        in_specs=[pl.BlockSpec(memory_space=pltpu.SMEM),
                  pl.BlockSpec((tile, LANES), lambda b, h, i: (b * nq + i, h)),
                  pl.BlockSpec((s, LANES), lambda b, h, i: (b, kcol + h)),
                  pl.BlockSpec((s, LANES), lambda b, h, i: (b, vcol + h)),
                  pl.BlockSpec((1, 3, tile, tile), lambda b, h, i: (h, 0, 0, 0))],
        out_specs=pl.BlockSpec((tile, LANES), lambda b, h, i: (b * nq + i, h)),
        out_shape=jax.ShapeDtypeStruct((t, cfg.ha * dv), BF16),
        scratch_shapes=[pltpu.VMEM((nq, tile, tile), F32), pltpu.VMEM((nq, tile, tile), F32),
                        pltpu.VMEM((tile, LANES), F32), pltpu.VMEM((tile, LANES), F32),
                        pltpu.VMEM((tile, LANES), F32), pltpu.VMEM((tile, LANES), F32),
                        pltpu.VMEM((tile, LANES), F32)],
        compiler_params=_params(3),
        name="attn_a",
    )(lam, proj, proj, proj, tb)


def _attn_b_kernel(q_ref, k_ref, v_ref, tb_ref, o_ref, *, tile, nwin):
    qi = pl.program_id(2)
    q = q_ref[...]
    ss, vs = [], []
    for j in range(nwin):
        kb = qi - (nwin - 1) + j
        start = pl.multiple_of(jnp.maximum(kb, 0) * tile, tile)
        s = _dot_t(q, k_ref[pl.ds(start, tile), :]) + tb_ref[0, j]
        ss.append(jnp.where(kb >= 0, s, NEG))
        vs.append(v_ref[pl.ds(start, tile), :])
    m = jnp.max(ss[0], axis=-1, keepdims=True)
    for s in ss[1:]:
        m = jnp.maximum(m, jnp.max(s, axis=-1, keepdims=True))
    ps = [jnp.exp(s - m) for s in ss]
    l = jnp.sum(ps[0], axis=-1, keepdims=True)
    for p in ps[1:]:
        l = l + jnp.sum(p, axis=-1, keepdims=True)
    inv = 1.0 / l
    acc = _dot((ps[0] * inv).astype(BF16), vs[0])
    for p, v in zip(ps[1:], vs[1:]):
        acc = acc + _dot((p * inv).astype(BF16), v)
    o_ref[...] = acc.astype(BF16)


def attn_b(proj, tb, cfg):
    t = proj.shape[0]
    tile = cfg.attn_tile
    s = cfg.seq
    nq = s // tile
    assert cfg.db == LANES
    nwin = (cfg.left_chunks * cfg.chunk) // tile + 1
    base = (2 * cfg.ha * 2 * cfg.da_qk + cfg.ha * 2 * cfg.da_qk) // LANES
    kern = functools.partial(_attn_b_kernel, tile=tile, nwin=nwin)
    return pl.pallas_call(
        kern,
        grid=(cfg.batch, cfg.hb, nq),
        in_specs=[pl.BlockSpec((tile, LANES), lambda b, h, i: (b * nq + i, base + h)),
                  pl.BlockSpec((s, LANES), lambda b, h, i: (b, base + cfg.hb + h)),
                  pl.BlockSpec((s, LANES), lambda b, h, i: (b, base + 2 * cfg.hb + h)),
                  pl.BlockSpec((1, nwin, tile, tile), lambda b, h, i: (h, 0, 0, 0))],
        out_specs=pl.BlockSpec((tile, LANES), lambda b, h, i: (b * nq + i, h)),
        out_shape=jax.ShapeDtypeStruct((t, cfg.hb * cfg.db), BF16),
        compiler_params=_params(3),
        name="attn_b",
    )(proj, proj, proj, tb)


def _merge_kernel(ya_ref, yb_ref, ga_ref, gb_ref, x_ref, gta_ref, scm_ref, shm_ref, gmoe_ref,
                  wua_ref, wub_ref, wo_ref, wrh_ref, wrl_ref, h1_ref, v_ref, lg_ref):
    a = _dot(ya_ref[...], wua_ref[...])
    b = _dot(yb_ref[...], wub_ref[...])
    merged = _sigmoid(ga_ref[...].astype(F32)) * a + _sigmoid(gb_ref[...].astype(F32)) * b
    att = _dot(merged.astype(BF16), wo_ref[...])
    h1 = x_ref[...] + gta_ref[0] * att
    h1_ref[...] = h1
    y = h1 * lax.rsqrt(jnp.mean(h1 * h1, axis=-1, keepdims=True) + EPS) * gmoe_ref[...]
    v = y * (1.0 + scm_ref[0]) + shm_ref[0]
    v_ref[...] = v
    v_hi, v_lo = _split_bf16(v)
    wrh = wrh_ref[...]
    lg_ref[...] = _dot_t(wrh, v_hi) + _dot_t(wrh, v_lo) + _dot_t(wrl_ref[...], v_hi)


def merge(ya, yb, proj, x2, gta, scm, shm, gmoe, wua, wub, wo, wrh, wrl, cfg):
    t, d = x2.shape
    tm = min(cfg.merge_tm, cfg.seq)
    per_b = cfg.seq // tm
    wa = ya.shape[1]
    wb = yb.shape[1]
    gate0 = (proj.shape[1] - 2 * d) // d
    ne = wrh.shape[0]
    const = lambda i: (0, 0)
    row3 = lambda i: (i // per_b, 0, 0)
    return pl.pallas_call(
        _merge_kernel,
        grid=(t // tm,),
        in_specs=[pl.BlockSpec((tm, wa), lambda i: (i, 0)),
                  pl.BlockSpec((tm, wb), lambda i: (i, 0)),
                  pl.BlockSpec((tm, d), lambda i: (i, gate0)),
                  pl.BlockSpec((tm, d), lambda i: (i, gate0 + 1)),
                  pl.BlockSpec((tm, d), lambda i: (i, 0)),
                  pl.BlockSpec((1, 1, d), row3),
                  pl.BlockSpec((1, 1, d), row3),
                  pl.BlockSpec((1, 1, d), row3),
                  pl.BlockSpec((1, d), const),
                  pl.BlockSpec((wa, d), const),
                  pl.BlockSpec((wb, d), const),
                  pl.BlockSpec((d, d), const),
                  pl.BlockSpec((ne, d), const),
                  pl.BlockSpec((ne, d), const)],
        out_specs=[pl.BlockSpec((tm, d), lambda i: (i, 0)),
                   pl.BlockSpec((tm, d), lambda i: (i, 0)),
                   pl.BlockSpec((ne, tm), lambda i: (0, i))],
        out_shape=[jax.ShapeDtypeStruct((t, d), F32),
                   jax.ShapeDtypeStruct((t, d), F32),
                   jax.ShapeDtypeStruct((ne, t), F32)],
        compiler_params=_params(1),
        name="merge",
    )(ya, yb, proj, proj, x2, gta, scm, shm, gmoe, wua, wub, wo, wrh, wrl)


def _route_kernel(lg_ref, rb_ref, tri_ref, w_ref, csel_ref, code_ref, cend_ref, carry_ref,
                  *, n_exp, n_groups, topk_groups, top_k, scale):
    gsz = n_exp // n_groups
    tc = lg_ref.shape[1]
    ninf = -jnp.inf
    step = pl.program_id(0)

    @pl.when(step == 0)
    def _():
        carry_ref[...] = jnp.zeros(carry_ref.shape, F32)

    row = lax.broadcasted_iota(jnp.int32, (gsz, tc), 0).astype(F32)
    sc, bi = [], []
    for g in range(n_groups):
        s = _sigmoid(lg_ref[g * gsz:(g + 1) * gsz, :])
        sc.append(s)
        bi.append(s + rb_ref[g * gsz:(g + 1) * gsz, :])
    gs = []
    for x in bi:
        m1 = jnp.max(x, axis=0, keepdims=True)
        eq = x == m1
        cnt = jnp.sum(jnp.where(eq, 1.0, 0.0), axis=0, keepdims=True)
        second = jnp.max(jnp.where(eq, ninf, x), axis=0, keepdims=True)
        gs.append(m1 + jnp.where(cnt >= 2.0, m1, second))
    gmat = jnp.concatenate(gs, axis=0)
    grow = lax.broadcasted_iota(jnp.int32, gmat.shape, 0).astype(F32)
    gsel = jnp.zeros(gmat.shape, F32)
    for _ in range(topk_groups):
        m = jnp.max(gmat, axis=0, keepdims=True)
        idx = jnp.min(jnp.where(gmat == m, grow, float(n_groups)), axis=0, keepdims=True)
        hit = grow == idx
        gsel = jnp.where(hit, 1.0, gsel)
        gmat = jnp.where(hit, ninf, gmat)
    cur = [jnp.where(gsel[g:g + 1, :] > 0.5, bi[g], ninf) for g in range(n_groups)]
    ids = [row + float(g * gsz) for g in range(n_groups)]
    w_rows = []
    chosen = [jnp.zeros((gsz, tc), F32) for _ in range(n_groups)]
    kplus = [jnp.zeros((gsz, tc), F32) for _ in range(n_groups)]
    for k in range(top_k):
        mg = cur[0]
        for c in cur[1:]:
            mg = jnp.maximum(mg, c)
        m = jnp.max(mg, axis=0, keepdims=True)
        cand = jnp.where(cur[0] == m, ids[0], float(n_exp))
        for c, i in zip(cur[1:], ids[1:]):
            cand = jnp.minimum(cand, jnp.where(c == m, i, float(n_exp)))
        idx = jnp.min(cand, axis=0, keepdims=True)
        wsum = jnp.zeros((gsz, tc), F32)
        for g in range(n_groups):
            hit = ids[g] == idx
            wsum = wsum + jnp.where(hit, sc[g], 0.0)
            cur[g] = jnp.where(hit, ninf, cur[g])
            chosen[g] = jnp.where(hit, 1.0, chosen[g])
            kplus[g] = jnp.where(hit, float(k + 1), kplus[g])
        w_rows.append(jnp.sum(wsum, axis=0, keepdims=True))
    w = jnp.concatenate(w_rows, axis=0)
    w_ref[...] = w / jnp.sum(w, axis=0, keepdims=True) * scale

    onehot = jnp.concatenate(chosen, axis=0)
    incl = _dot(onehot.astype(BF16), tri_ref[...])
    carry = carry_ref[...]
    cglob = carry[:, 0:1] + incl
    tok = (lax.broadcasted_iota(jnp.int32, (n_exp, tc), 1) + step * tc).astype(F32)
    picked = onehot > 0.5
    csel_ref[0:n_exp, :] = jnp.where(picked, cglob, 0.0)
    code_ref[0:n_exp, :] = jnp.where(picked, tok * float(top_k) + jnp.concatenate(kplus, axis=0) - 1.0, 0.0)
    new_carry = carry + jnp.sum(onehot, axis=1, keepdims=True)
    carry_ref[...] = new_carry
    cend_ref[0] = new_carry


def route(lg, rb, cfg):
    ne, t = lg.shape
    nx = cfg.n_experts
    tc = min(cfg.route_tc, t)
    nsteps = t // tc
    tri = jnp.asarray(np.triu(np.ones((tc, tc), np.float32)), BF16)
    kern = functools.partial(_route_kernel, n_exp=cfg.n_experts, n_groups=cfg.n_groups,
                             topk_groups=cfg.topk_groups, top_k=cfg.top_k, scale=cfg.routed_scale)
    return pl.pallas_call(
        kern,
        grid=(nsteps,),
        in_specs=[pl.BlockSpec((ne, tc), lambda i: (0, i)),
                  pl.BlockSpec((ne, tc), lambda i: (0, 0)),
                  pl.BlockSpec((tc, tc), lambda i: (0, 0))],
        out_specs=[pl.BlockSpec((cfg.top_k, tc), lambda i: (0, i)),
                   pl.BlockSpec((nx, tc), lambda i: (0, i)),
                   pl.BlockSpec((nx, tc), lambda i: (0, i)),
                   pl.BlockSpec((1, nx, LANES), lambda i: (i, 0, 0))],
        out_shape=[jax.ShapeDtypeStruct((cfg.top_k, t), F32),
                   jax.ShapeDtypeStruct((nx, t), F32),
                   jax.ShapeDtypeStruct((nx, t), F32),
                   jax.ShapeDtypeStruct((nsteps, nx, LANES), F32)],
        scratch_shapes=[pltpu.VMEM((cfg.n_experts, LANES), F32)],
        compiler_params=_params(1),
        name="route",
    )(lg, rb, tri)


def _slots_kernel(be_ref, r0_ref, cnt_ref, csel_ref, code_ref, cend_ref, tok_ref, dst_ref,
                  *, bm, tc, top_k, n_tok, n_assign):
    b = pl.program_id(0)
    e = be_ref[b]
    r0 = r0_ref[b]
    cnt = cnt_ref[b]
    sub = lax.rem(e, 8)
    nsteps = cend_ref.shape[0]
    pick3 = lax.broadcasted_iota(jnp.int32, cend_ref.shape, 1) == sub
    cend = jnp.sum(jnp.where(pick3, cend_ref[...], 0.0), axis=1)
    first = (r0 + 1).astype(F32)
    last = jnp.minimum(r0 + bm, cnt).astype(F32)
    jlo = (jnp.sum(jnp.where(cend < first, 1.0, 0.0)) * (1.0 / LANES)).astype(jnp.int32)
    jhi = (jnp.sum(jnp.where(cend < last, 1.0, 0.0)) * (1.0 / LANES)).astype(jnp.int32)
    jhi = jnp.minimum(jhi, nsteps - 1)
    jhi = jnp.where(last < first, jlo - 1, jhi)

    want = (lax.broadcasted_iota(jnp.int32, (bm, tc), 0) + (r0 + 1)).astype(F32)
    pick2 = lax.broadcasted_iota(jnp.int32, (8, tc), 0) == sub
    zrows = jnp.zeros((13, tc), F32)

    def body(j, acc):
        off = pl.multiple_of(j * tc, tc)
        crow = jnp.sum(jnp.where(pick2, csel_ref[:, pl.ds(off, tc)], 0.0), axis=0, keepdims=True)
        drow = jnp.sum(jnp.where(pick2, code_ref[:, pl.ds(off, tc)], 0.0), axis=0, keepdims=True)
        di = drow.astype(jnp.int32)
        digits = jnp.concatenate([(di >> 14).astype(F32), ((di >> 7) & 127).astype(F32),
                                  (di & 127).astype(F32), zrows], axis=0).astype(BF16)
        eq = jnp.where(want == crow, 1.0, 0.0).astype(BF16)
        return acc + _dot_t(digits, eq)

    acc = lax.fori_loop(jlo, jhi + 1, body, jnp.zeros((16, bm), F32))
    code = (acc[0:1, :] * 16384.0 + acc[1:2, :] * 128.0 + acc[2:3, :]).astype(jnp.int32)
    lane = lax.broadcasted_iota(jnp.int32, (1, bm), 1)
    valid = (lane + r0) < cnt
    shift = top_k.bit_length() - 1
    tok = code >> shift
    kk = code & (top_k - 1)
    dummy = n_assign + lax.rem(b, 2) * bm + lane
    tok_ref[0] = jnp.where(valid, tok, 0)
    dst_ref[0] = jnp.where(valid, kk * n_tok + tok, dummy)


def slots(block_e, r0, cntb, csel, code, cend, cfg, n_tok):
    nb = block_e.shape[0]
    bm = cfg.moe_rows
    ne, t = csel.shape
    tc = min(cfg.route_tc, t)
    nsteps = cend.shape[0]
    assert cfg.top_k & (cfg.top_k - 1) == 0 and n_tok * cfg.top_k < (1 << 21)
    kern = functools.partial(_slots_kernel, bm=bm, tc=tc, top_k=cfg.top_k, n_tok=n_tok, n_assign=n_tok * cfg.top_k)
    grid_spec = pltpu.PrefetchScalarGridSpec(
        num_scalar_prefetch=3,
        grid=(nb,),
        in_specs=[pl.BlockSpec((8, t), lambda b, be, r0, cn: (be[b] // 8, 0)),
                  pl.BlockSpec((8, t), lambda b, be, r0, cn: (be[b] // 8, 0)),
                  pl.BlockSpec((nsteps, 8, LANES), lambda b, be, r0, cn: (0, be[b] // 8, 0))],
        out_specs=[pl.BlockSpec((1, 1, bm), lambda b, be, r0, cn: (b, 0, 0)),
                   pl.BlockSpec((1, 1, bm), lambda b, be, r0, cn: (b, 0, 0))],
    )
    return pl.pallas_call(
        kern,
        grid_spec=grid_spec,
        out_shape=[jax.ShapeDtypeStruct((nb, 1, bm), jnp.int32), jax.ShapeDtypeStruct((nb, 1, bm), jnp.int32)],
        compiler_params=_params(1),
        name="slots",
    )(block_e, r0, cntb, csel, code, cend)


MOE_RING = 3


def _moe_kernel(be_ref, tok0_ref, tok1_ref, tokn_ref, dstp_ref, dstc_ref, v_hbm, wg_ref, wu_ref, wd_ref, yk_hbm,
                xbuf, ybuf, wgb, wub, wdb, gsem, ssem, *, bm, n_assign):
    b = pl.program_id(0)
    nb = pl.num_programs(0)
    slot = lax.rem(b, MOE_RING)
    nslab = xbuf.shape[2]

    def gather_copy(t, r, s):
        return pltpu.make_async_copy(v_hbm.at[t], xbuf.at[s, r // 8, :, r % 8, :], gsem.at[s])

    def scatter_copy(r, d, s):
        return pltpu.make_async_copy(ybuf.at[s, pl.ds(r, 1), :], yk_hbm.at[pl.ds(d, 1), :], ssem.at[s])

    def wait_gather(s):
        for r in range(bm):
            gather_copy(0, r, s).wait()

    def scatter_batch(s, row0):
        return pltpu.make_async_copy(ybuf.at[s], yk_hbm.at[pl.ds(row0, bm), :], ssem.at[s])

    @pl.when(b == 0)
    def _():
        ybuf[2] = jnp.zeros(ybuf.shape[1:], F32)
        ybuf[0] = jnp.zeros(ybuf.shape[1:], F32)
        ybuf[1] = jnp.zeros(ybuf.shape[1:], F32)
        scatter_batch(0, n_assign).start()
        scatter_batch(1, n_assign + bm).start()
        for r in range(bm):
            gather_copy(tok0_ref[0, 0, r], r, 0).start(priority=r % 2)
        for r in range(bm):
            gather_copy(tok1_ref[0, 0, r], r, 1).start(priority=r % 2)

    e = be_ref[b]
    eprev = be_ref[jnp.maximum(b - 1, 0)]

    @pl.when(jnp.logical_or(b == 0, e != eprev))
    def _():
        wgb[...] = wg_ref[0].astype(BF16)
        wub[...] = wu_ref[0].astype(BF16)
        wdb[...] = wd_ref[0].astype(BF16)

    def block_step(s):
        p = (s + 2) % MOE_RING
        wait_gather(s)
        scatter_batch(s, 0).wait()
        for r in range(bm):
            scatter_copy(r, dstp_ref[0, 0, r], p).start(priority=r % 2)
        for r in range(bm):
            gather_copy(tokn_ref[0, 0, r], r, p).start(priority=r % 2)
        x = jnp.concatenate([xbuf[s, :, j].reshape(bm, LANES) for j in range(nslab)], axis=1).astype(BF16)
        g = _dot(x, wgb[...])
        u = _dot(x, wub[...])
        h = (g * _sigmoid(g)) * u
        ybuf[s] = _dot(h.astype(BF16), wdb[...])

    for s in range(MOE_RING):
        pl.when(slot == s)(functools.partial(block_step, s))

    @pl.when(b == nb - 1)
    def _():
        def drain(s):
            p, q = (s + 2) % MOE_RING, (s + 1) % MOE_RING
            scatter_batch(q, 0).wait()
            for r in range(bm):
                scatter_copy(r, dstc_ref[0, 0, r], s).start(priority=r % 2)
            wait_gather(p)
            wait_gather(q)
            scatter_batch(p, 0).wait()
            scatter_batch(s, 0).wait()

        for s in range(MOE_RING):
            pl.when(slot == s)(functools.partial(drain, s))


def moe(v, block_e, slot_tok, slot_dst, wg, wu, wd, cfg):
    t, d = v.shape
    bm = cfg.moe_rows
    v3 = v.reshape(t, d // LANES, LANES)
    nb = block_e.shape[0]
    assert nb >= 2
    de = wg.shape[2]
    n_assign = t * cfg.top_k
    first = (n_assign + 2 * bm + jnp.arange(bm, dtype=jnp.int32)).reshape(1, 1, bm)
    dst_ext = jnp.concatenate([first, slot_dst], axis=0)
    kern = functools.partial(_moe_kernel, bm=bm, n_assign=n_assign)
    smem_blk = lambda f: pl.BlockSpec((1, 1, bm), f, memory_space=pltpu.SMEM)
    grid_spec = pltpu.PrefetchScalarGridSpec(
        num_scalar_prefetch=1,
        grid=(nb,),
        in_specs=[smem_blk(lambda b, be: (0, 0, 0)),
                  smem_blk(lambda b, be: (1, 0, 0)),
                  smem_blk(lambda b, be: (jnp.minimum(b + 2, nb - 1), 0, 0)),
                  smem_blk(lambda b, be: (b, 0, 0)),
                  smem_blk(lambda b, be: (b + 1, 0, 0)),
                  pl.BlockSpec(memory_space=pl.ANY),
                  pl.BlockSpec((1, d, de), lambda b, be: (be[b], 0, 0)),
                  pl.BlockSpec((1, d, de), lambda b, be: (be[b], 0, 0)),
                  pl.BlockSpec((1, de, d), lambda b, be: (be[b], 0, 0))],
        out_specs=pl.BlockSpec(memory_space=pl.ANY),
        scratch_shapes=[pltpu.VMEM((MOE_RING, bm // 8, d // LANES, 8, LANES), F32),
                        pltpu.VMEM((MOE_RING, bm, d), F32),
                        pltpu.VMEM((d, de), BF16), pltpu.VMEM((d, de), BF16), pltpu.VMEM((de, d), BF16),
                        pltpu.SemaphoreType.DMA((MOE_RING,)), pltpu.SemaphoreType.DMA((MOE_RING,))],
    )
    return pl.pallas_call(
        kern,
        grid_spec=grid_spec,
        out_shape=jax.ShapeDtypeStruct((n_assign + 3 * bm, d), F32),
        compiler_params=_params(1),
        name="moe",
    )(block_e, slot_tok, slot_tok, slot_tok, dst_ext, dst_ext, v3, wg, wu, wd)


def _final_kernel(v_ref, h1_ref, w_ref, gtm_ref, gfin_ref, sg_ref, su_ref, sd_ref, *rest, top_k):
    yk_refs, o_ref = rest[:top_k], rest[top_k]
    vb = v_ref[...].astype(BF16)
    g = _dot(vb, sg_ref[...])
    u = _dot(vb, su_ref[...])
    y = _dot(((g * _sigmoid(g)) * u).astype(BF16), sd_ref[...])
    for k in range(top_k):
        y = y + w_ref[:, k:k + 1] * yk_refs[k][...]
    h2 = h1_ref[...] + gtm_ref[0] * y
    o_ref[...] = h2 * lax.rsqrt(jnp.mean(h2 * h2, axis=-1, keepdims=True) + EPS) * gfin_ref[...]


def final(v, h1, yk, w_tk, gtm, gfin, sg, su, sd, cfg):
    t, d = v.shape
    tm = min(cfg.final_tm, cfg.seq)
    per_b = cfg.seq // tm
    nt = t // tm
    ds = sg.shape[1]
    const = lambda i: (0, 0)
    kern = functools.partial(_final_kernel, top_k=cfg.top_k)
    plane = lambda k: pl.BlockSpec((tm, d), lambda i: (k * nt + i, 0))
    return pl.pallas_call(
        kern,
        grid=(nt,),
        in_specs=[pl.BlockSpec((tm, d), lambda i: (i, 0)),
                  pl.BlockSpec((tm, d), lambda i: (i, 0)),
                  pl.BlockSpec((tm, cfg.top_k), lambda i: (i, 0)),
                  pl.BlockSpec((1, 1, d), lambda i: (i // per_b, 0, 0)),
                  pl.BlockSpec((1, d), const),
                  pl.BlockSpec((d, ds), const),
                  pl.BlockSpec((d, ds), const),
                  pl.BlockSpec((ds, d), const)] + [plane(k) for k in range(cfg.top_k)],
        out_specs=pl.BlockSpec((tm, d), lambda i: (i, 0)),
        out_shape=jax.ShapeDtypeStruct((t, d), F32),
        compiler_params=_params(1),
        name="final",
    )(v, h1, w_tk, gtm, gfin, sg, su, sd, *([yk] * cfg.top_k))


def _t5_bucket(rel, cfg):
    nb = cfg.t5_buckets // 2
    ret = (rel > 0).astype(np.int32) * nb
    n = np.abs(rel)
    max_exact = nb // 2
    large = max_exact + (np.log(np.maximum(n, 1) / max_exact) / math.log(cfg.t5_max_dist / max_exact)
                         * (nb - max_exact)).astype(np.int32)
    large = np.minimum(large, nb - 1)
    return (ret + np.where(n < max_exact, n, large)).astype(np.int32)


def _toeplitz(vec, n):
    lead = vec.shape[:-1]
    flat = jnp.tile(vec, (1,) * len(lead) + (n,))[..., :n * (2 * n - 1)]
    return flat.reshape(lead + (n, 2 * n - 1))[..., :n]


def _rel_offsets(n):
    m = np.arange(2 * n)
    return np.where(m < n, m, m - 2 * n)


def _bias_tables_a(t5_bias, cfg):
    tile = cfg.attn_tile
    assert tile + 1 >= cfg.t5_max_dist and tile % cfg.chunk == 0
    ql = np.arange(tile)[:, None]
    kl = np.arange(tile)[None, :]
    allowed = (kl // cfg.chunk) <= (ql // cfg.chunk)
    r = _rel_offsets(tile)
    t5 = t5_bias.astype(F32)
    diag = _toeplitz(t5[_t5_bucket(r, cfg)].T, tile)
    prev = _toeplitz(t5[_t5_bucket(r - tile, cfg)].T, tile)
    far = jnp.broadcast_to(t5[_t5_bucket(np.array(-2 * tile), cfg)][:, None, None], diag.shape)
    return jnp.stack([jnp.where(allowed[None], diag, NEG), prev, far], axis=1)


def _bias_tables_b(rel_table, cfg):
    tile = cfg.attn_tile
    assert (cfg.left_chunks * cfg.chunk) % tile == 0 and tile % cfg.chunk == 0
    nwin = (cfg.left_chunks * cfg.chunk) // tile + 1
    ql = np.arange(tile)[:, None]
    kl = np.arange(tile)[None, :]
    r = _rel_offsets(tile)
    rt = rel_table.astype(F32)
    tabs = []
    for j in range(nwin):
        off = (j - (nwin - 1)) * tile
        idx = np.clip(r + off, -cfg.rel_clip, cfg.rel_clip) + cfg.rel_clip
        dchunk = ql // cfg.chunk - (kl + off) // cfg.chunk
        band = (dchunk >= 0) & (dchunk <= cfg.left_chunks)
        tabs.append(jnp.where(band[None], _toeplitz(rt[idx].T, tile), NEG))
    return jnp.stack(tabs, axis=1)


def _block_tables(counts, n_assign, cfg):
    e_ = cfg.n_experts
    bm = cfg.moe_rows
    nb = -(-(n_assign + e_ * (bm - 1)) // bm)
    pcounts = (counts + bm - 1) // bm * bm
    pends = jnp.cumsum(pcounts)
    pstarts = pends - pcounts
    start_b = jnp.arange(nb, dtype=jnp.int32) * bm
    block_e = jnp.minimum(jnp.sum((pends[None, :] <= start_b[:, None]).astype(jnp.int32), axis=1), e_ - 1)
    mine = jnp.arange(e_, dtype=jnp.int32)[None, :] == block_e[:, None]
    pstart_b = jnp.sum(jnp.where(mine, pstarts[None, :], 0), axis=1)
    count_b = jnp.sum(jnp.where(mine, counts[None, :], 0), axis=1)
    return block_e.astype(jnp.int32), (start_b - pstart_b).astype(jnp.int32), count_b.astype(jnp.int32)


def _forward(cfg, x, c, w_ada, b_ada, g_attn, w_in, lambda_qk, t5_bias, rel_bias_b, w_up_a, w_up_b, w_o,
             g_moe, w_router, router_bias, w_exp_gate, w_exp_up, w_exp_down,
             w_sh_gate, w_sh_up, w_sh_down, g_final):
    b_, s_, d_ = x.shape
    t_ = b_ * s_
    l = 0
    qk_a = cfg.ha * 2 * cfg.da_qk
    width_a = cfg.ha * 2 * cfg.da_qk
    width_b = cfg.hb * cfg.db
    in_cols = w_in.shape[2]

    rows = max(16, b_)
    c_pad = jnp.zeros((rows, d_), F32).at[:b_].set(c)
    mod = ada_mod(c_pad, w_ada[l], b_ada[l], cfg)[:b_]
    sh_a, sc_a, gt_a, sh_m, sc_m, gt_m = [m.reshape(b_, 1, d_) for m in jnp.split(mod, cfg.n_mod, axis=-1)]

    colscale = np.ones((1, in_cols), np.float32)
    colscale[0, :qk_a] = cfg.da_qk ** -0.5
    qb0 = 2 * qk_a + width_a
    colscale[0, qb0:qb0 + width_b] = cfg.db ** -0.5
    x2 = x.reshape(t_, d_)
    proj = in_proj(x2, g_attn[l].reshape(1, d_), sc_a, sh_a, w_in[l].astype(BF16), jnp.asarray(colscale), cfg)

    lq = lambda_qk[l].astype(F32)
    lam = jnp.exp(jnp.sum(lq[0] * lq[1])) - jnp.exp(jnp.sum(lq[2] * lq[3])) + cfg.lam_init
    ya = attn_a(proj, lam.reshape(1), _bias_tables_a(t5_bias, cfg), cfg)
    yb = attn_b(proj, _bias_tables_b(rel_bias_b[l], cfg), cfg)

    ne_pad = max(LANES, cfg.n_experts)
    wrt = jnp.zeros((ne_pad, d_), F32).at[:cfg.n_experts].set(w_router[l].T)
    wrh, wrl = _split_bf16(wrt)
    h1, v, lg = merge(ya, yb, proj, x2, gt_a, sc_m, sh_m, g_moe[l].reshape(1, d_),
                      w_up_a[l].astype(BF16), w_up_b[l].astype(BF16), w_o[l].astype(BF16), wrh, wrl, cfg)

    tc = min(cfg.route_tc, t_)
    rb = jnp.zeros((ne_pad, tc), F32).at[:cfg.n_experts].set(
        jnp.broadcast_to(router_bias[l].astype(F32)[:, None], (cfg.n_experts, tc)))
    w_kt, csel, code, cend = route(lg, rb, cfg)
    counts = cend[-1, :, 0].astype(jnp.int32)
    block_e, r0, count_b = _block_tables(counts, t_ * cfg.top_k, cfg)
    slot_tok, slot_dst = slots(block_e, r0, count_b, csel, code, cend, cfg, t_)

    yk = moe(v, block_e, slot_tok, slot_dst, w_exp_gate[l], w_exp_up[l], w_exp_down[l], cfg)
    out = final(v, h1, yk, w_kt.T, gt_m, g_final.reshape(1, d_),
                w_sh_gate[l].astype(BF16), w_sh_up[l].astype(BF16), w_sh_down[l].astype(BF16), cfg)
    return out.reshape(b_, s_, d_)


def kernel(x, c, w_ada, b_ada, g_attn, w_in, lambda_qk, t5_bias, rel_bias_b, w_up_a, w_up_b, w_o, g_moe,
           w_router, router_bias, w_exp_gate, w_exp_up, w_exp_down, w_sh_gate, w_sh_up, w_sh_down, g_final):
    return _forward(Cfg(), x, c, w_ada, b_ada, g_attn, w_in, lambda_qk, t5_bias, rel_bias_b, w_up_a, w_up_b,
                    w_o, g_moe, w_router, router_bias, w_exp_gate, w_exp_up, w_exp_down,
                    w_sh_gate, w_sh_up, w_sh_down, g_final)
```

```python
import functools
import math
from typing import NamedTuple

import numpy as np
import jax
import jax.numpy as jnp
from jax import lax
from jax.experimental import pallas as pl
from jax.experimental.pallas import tpu as pltpu

F32 = jnp.float32
BF16 = jnp.bfloat16
NEG = -1e30
EPS = 1e-6
HEAD_EPS = 1e-5
LANES = 128
VMEM_LIMIT = 56 * 1024 * 1024


class Cfg(NamedTuple):
    batch: int = 8
    seq: int = 2048
    d_model: int = 2048
    chunk: int = 64
    ha: int = 8
    da_qk: int = 64
    hb: int = 8
    db: int = 128
    left_chunks: int = 8
    rel_clip: int = 256
    t5_buckets: int = 32
    t5_max_dist: int = 128
    n_experts: int = 64
    top_k: int = 8
    n_groups: int = 8
    topk_groups: int = 4
    d_expert: int = 512
    d_shared: int = 512
    routed_scale: float = 2.5
    n_mod: int = 6
    lam_init: float = 0.8 - 0.6 * math.exp(-0.3 * 0)
    attn_tile: int = 256
    moe_rows: int = 256
    proj_tm: int = 1024
    proj_tn: int = 1024
    merge_tm: int = 256
    route_tc: int = 512
    final_tm: int = 128
    ada_tn: int = 1024


def _sigmoid(x):
    return 1.0 / (1.0 + jnp.exp(-x))


def _split_bf16(x):
    hi = x.astype(BF16)
    lo = (x - hi.astype(F32)).astype(BF16)
    return hi, lo


def _dot(a, b):
    return jnp.dot(a, b, preferred_element_type=F32)


def _dot_t(a, b):
    return lax.dot_general(a, b, (((1,), (1,)), ((), ())), preferred_element_type=F32)


def _params(n_axes):
    return pltpu.CompilerParams(dimension_semantics=("arbitrary",) * n_axes,
                                vmem_limit_bytes=VMEM_LIMIT)


def _ada_kernel(c_ref, w_ref, b_ref, o_ref):
    c = c_ref[...]
    s_hi, s_lo = _split_bf16(c * _sigmoid(c))
    w_hi, w_lo = _split_bf16(w_ref[...])
    acc = _dot(s_hi, w_hi) + _dot(s_hi, w_lo) + _dot(s_lo, w_hi)
    o_ref[...] = acc + b_ref[...]


def ada_mod(c_pad, w_ada, b_ada, cfg):
    rows, d = c_pad.shape
    n = w_ada.shape[1]
    tn = min(cfg.ada_tn, n)
    return pl.pallas_call(
        _ada_kernel,
        grid=(n // tn,),
        in_specs=[pl.BlockSpec((rows, d), lambda j: (0, 0)),
                  pl.BlockSpec((d, tn), lambda j: (0, j)),
                  pl.BlockSpec((1, tn), lambda j: (0, j))],
        out_specs=pl.BlockSpec((rows, tn), lambda j: (0, j)),
        out_shape=jax.ShapeDtypeStruct((rows, n), F32),
        compiler_params=_params(1),
        name="ada_mod",
    )(c_pad, w_ada, b_ada.reshape(1, n))


def _inproj_kernel(x_ref, g_ref, sc_ref, sh_ref, w_ref, cs_ref, o_ref, u_ref):
    @pl.when(pl.program_id(1) == 0)
    def _():
        x = x_ref[...]
        y = x * lax.rsqrt(jnp.mean(x * x, axis=-1, keepdims=True) + EPS) * g_ref[...]
        u_ref[...] = (y * (1.0 + sc_ref[0]) + sh_ref[0]).astype(BF16)

    acc = _dot(u_ref[...], w_ref[...])
    o_ref[...] = (acc * cs_ref[...]).astype(BF16)


def in_proj(x2, g, sc, sh, w_bf, colscale, cfg):
    t, d = x2.shape
    n = w_bf.shape[1]
    tm = min(cfg.proj_tm, cfg.seq)
    tn = min(cfg.proj_tn, n)
    per_b = cfg.seq // tm
    return pl.pallas_call(
        _inproj_kernel,
        grid=(t // tm, n // tn),
        in_specs=[pl.BlockSpec((tm, d), lambda i, j: (i, 0)),
                  pl.BlockSpec((1, d), lambda i, j: (0, 0)),
                  pl.BlockSpec((1, 1, d), lambda i, j: (i // per_b, 0, 0)),
                  pl.BlockSpec((1, 1, d), lambda i, j: (i // per_b, 0, 0)),
                  pl.BlockSpec((d, tn), lambda i, j: (0, j)),
                  pl.BlockSpec((1, tn), lambda i, j: (0, j))],
        out_specs=pl.BlockSpec((tm, tn), lambda i, j: (i, j)),
        out_shape=jax.ShapeDtypeStruct((t, n), BF16),
        scratch_shapes=[pltpu.VMEM((tm, d), BF16)],
        compiler_params=_params(2),
        name="in_proj",
    )(x2, g, sc, sh, w_bf, colscale)


def _attn_a_kernel(lam_ref, q_ref, k_ref, v_ref, tb_ref, o_ref,
                   s0_ref, s1_ref, m0_ref, m1_ref, l0_ref, l1_ref, acc_ref, *, tile, half, out_scale):
    qi = pl.program_id(2)
    nk = qi + 1
    q = q_ref[...]
    lane = lax.broadcasted_iota(jnp.int32, q.shape, 1)
    zero = jnp.zeros_like(q)
    q0 = jnp.where(lane < half, q, zero)
    q1 = jnp.where(lane >= half, q, zero)

    m0_ref[...] = jnp.full(m0_ref.shape, -jnp.inf, F32)
    m1_ref[...] = jnp.full(m1_ref.shape, -jnp.inf, F32)
    l0_ref[...] = jnp.zeros(l0_ref.shape, F32)
    l1_ref[...] = jnp.zeros(l1_ref.shape, F32)
    acc_ref[...] = jnp.zeros(acc_ref.shape, F32)

    def fold(x):
        r = x[:, :LANES]
        for c in range(1, tile // LANES):
            r = r + x[:, c * LANES:(c + 1) * LANES]
        return r

    def foldmax(x):
        r = x[:, :LANES]
        for c in range(1, tile // LANES):
            r = jnp.maximum(r, x[:, c * LANES:(c + 1) * LANES])
        return r

    def scores(kb, carry):
        kblk = k_ref[pl.ds(pl.multiple_of(kb * tile, tile), tile), :]
        bias = tb_ref[0, jnp.minimum(qi - kb, 2)]
        s0 = _dot_t(q0, kblk) + bias
        s1 = _dot_t(q1, kblk) + bias
        s0_ref[kb] = s0
        s1_ref[kb] = s1
        m0_ref[...] = jnp.maximum(m0_ref[...], foldmax(s0))
        m1_ref[...] = jnp.maximum(m1_ref[...], foldmax(s1))
        return carry

    lax.fori_loop(0, nk, scores, 0)
    m0 = jnp.max(m0_ref[...], axis=-1, keepdims=True)
    m1 = jnp.max(m1_ref[...], axis=-1, keepdims=True)

    def expo(kb, carry):
        p0 = jnp.exp(s0_ref[kb] - m0)
        p1 = jnp.exp(s1_ref[kb] - m1)
        s0_ref[kb] = p0
        s1_ref[kb] = p1
        l0_ref[...] += fold(p0)
        l1_ref[...] += fold(p1)
        return carry

    lax.fori_loop(0, nk, expo, 0)
    c0 = 1.0 / jnp.sum(l0_ref[...], axis=-1, keepdims=True)
    c1 = lam_ref[0] / jnp.sum(l1_ref[...], axis=-1, keepdims=True)

    def pv(kb, carry):
        a = s0_ref[kb] * c0 - s1_ref[kb] * c1
        vblk = v_ref[pl.ds(pl.multiple_of(kb * tile, tile), tile), :]
        acc_ref[...] += _dot(a.astype(BF16), vblk)
        return carry

    lax.fori_loop(0, nk, pv, 0)
    o = acc_ref[...]
    o = o * lax.rsqrt(jnp.mean(o * o, axis=-1, keepdims=True) + HEAD_EPS) * out_scale
    o_ref[...] = o.astype(BF16)


def attn_a(proj, lam, tb, cfg):
    t = proj.shape[0]
    tile = cfg.attn_tile
    s = cfg.seq
    nq = s // tile
    dv = 2 * cfg.da_qk
    assert dv == LANES
    kcol = cfg.ha
    vcol = 2 * cfg.ha
    kern = functools.partial(_attn_a_kernel, tile=tile, half=cfg.da_qk, out_scale=1.0 - cfg.lam_init)
    return pl.pallas_call(
        kern,
        grid=(cfg.batch, cfg.ha, nq),
        in_specs=[pl.BlockSpec(memory_space=pltpu.SMEM),
                  pl.BlockSpec((tile, LANES), lambda b, h, i: (b * nq + i, h)),
                  pl.BlockSpec((s, LANES), lambda b, h, i: (b, kcol + h)),
                  pl.BlockSpec((s, LANES), lambda b, h, i: (b, vcol + h)),
                  pl.BlockSpec((1, 3, tile, tile), lambda b, h, i: (h, 0, 0, 0))],
        out_specs=pl.BlockSpec((tile, LANES), lambda b, h, i: (b * nq + i, h)),
        out_shape=jax.ShapeDtypeStruct((t, cfg.ha * dv), BF16),
        scratch_shapes=[pltpu.VMEM((nq, tile, tile), F32), pltpu.VMEM((nq, tile, tile), F32),
                        pltpu.VMEM((tile, LANES), F32), pltpu.VMEM((tile, LANES), F32),
                        pltpu.VMEM((tile, LANES), F32), pltpu.VMEM((tile, LANES), F32),
                        pltpu.VMEM((tile, LANES), F32)],
        compiler_params=_params(3),
        name="attn_a",
    )(lam, proj, proj, proj, tb)


def _attn_b_kernel(q_ref, k_ref, v_ref, tb_ref, o_ref, *, tile, nwin):
    qi = pl.program_id(2)
    q = q_ref[...]
    ss, vs = [], []
    for j in range(nwin):
        kb = qi - (nwin - 1) + j
        start = pl.multiple_of(jnp.maximum(kb, 0) * tile, tile)
        s = _dot_t(q, k_ref[pl.ds(start, tile), :]) + tb_ref[0, j]
        ss.append(jnp.where(kb >= 0, s, NEG))
        vs.append(v_ref[pl.ds(start, tile), :])
    m = jnp.max(ss[0], axis=-1, keepdims=True)
    for s in ss[1:]:
        m = jnp.maximum(m, jnp.max(s, axis=-1, keepdims=True))
    ps = [jnp.exp(s - m) for s in ss]
    l = jnp.sum(ps[0], axis=-1, keepdims=True)
    for p in ps[1:]:
        l = l + jnp.sum(p, axis=-1, keepdims=True)
    inv = 1.0 / l
    acc = _dot((ps[0] * inv).astype(BF16), vs[0])
    for p, v in zip(ps[1:], vs[1:]):
        acc = acc + _dot((p * inv).astype(BF16), v)
    o_ref[...] = acc.astype(BF16)


def attn_b(proj, tb, cfg):
    t = proj.shape[0]
    tile = cfg.attn_tile
    s = cfg.seq
    nq = s // tile
    assert cfg.db == LANES
    nwin = (cfg.left_chunks * cfg.chunk) // tile + 1
    base = (2 * cfg.ha * 2 * cfg.da_qk + cfg.ha * 2 * cfg.da_qk) // LANES
    kern = functools.partial(_attn_b_kernel, tile=tile, nwin=nwin)
    return pl.pallas_call(
        kern,
        grid=(cfg.batch, cfg.hb, nq),
        in_specs=[pl.BlockSpec((tile, LANES), lambda b, h, i: (b * nq + i, base + h)),
                  pl.BlockSpec((s, LANES), lambda b, h, i: (b, base + cfg.hb + h)),
                  pl.BlockSpec((s, LANES), lambda b, h, i: (b, base + 2 * cfg.hb + h)),
                  pl.BlockSpec((1, nwin, tile, tile), lambda b, h, i: (h, 0, 0, 0))],
        out_specs=pl.BlockSpec((tile, LANES), lambda b, h, i: (b * nq + i, h)),
        out_shape=jax.ShapeDtypeStruct((t, cfg.hb * cfg.db), BF16),
        compiler_params=_params(3),
        name="attn_b",
    )(proj, proj, proj, tb)


def _merge_kernel(ya_ref, yb_ref, ga_ref, gb_ref, x_ref, gta_ref, scm_ref, shm_ref, gmoe_ref,
                  wua_ref, wub_ref, wo_ref, wrh_ref, wrl_ref, h1_ref, v_ref, lg_ref):
    a = _dot(ya_ref[...], wua_ref[...])
    b = _dot(yb_ref[...], wub_ref[...])
    merged = _sigmoid(ga_ref[...].astype(F32)) * a + _sigmoid(gb_ref[...].astype(F32)) * b
    att = _dot(merged.astype(BF16), wo_ref[...])
    h1 = x_ref[...] + gta_ref[0] * att
    h1_ref[...] = h1
    y = h1 * lax.rsqrt(jnp.mean(h1 * h1, axis=-1, keepdims=True) + EPS) * gmoe_ref[...]
    v = y * (1.0 + scm_ref[0]) + shm_ref[0]
    v_ref[...] = v
    v_hi, v_lo = _split_bf16(v)
    wrh = wrh_ref[...]
    lg_ref[...] = _dot_t(wrh, v_hi) + _dot_t(wrh, v_lo) + _dot_t(wrl_ref[...], v_hi)


def merge(ya, yb, proj, x2, gta, scm, shm, gmoe, wua, wub, wo, wrh, wrl, cfg):
    t, d = x2.shape
    tm = min(cfg.merge_tm, cfg.seq)
    per_b = cfg.seq // tm
    wa = ya.shape[1]
    wb = yb.shape[1]
    gate0 = (proj.shape[1] - 2 * d) // d
    ne = wrh.shape[0]
    const = lambda i: (0, 0)
    row3 = lambda i: (i // per_b, 0, 0)
    return pl.pallas_call(
        _merge_kernel,
        grid=(t // tm,),
        in_specs=[pl.BlockSpec((tm, wa), lambda i: (i, 0)),
                  pl.BlockSpec((tm, wb), lambda i: (i, 0)),
                  pl.BlockSpec((tm, d), lambda i: (i, gate0)),
                  pl.BlockSpec((tm, d), lambda i: (i, gate0 + 1)),
                  pl.BlockSpec((tm, d), lambda i: (i, 0)),
                  pl.BlockSpec((1, 1, d), row3),
                  pl.BlockSpec((1, 1, d), row3),
                  pl.BlockSpec((1, 1, d), row3),
                  pl.BlockSpec((1, d), const),
                  pl.BlockSpec((wa, d), const),
                  pl.BlockSpec((wb, d), const),
                  pl.BlockSpec((d, d), const),
                  pl.BlockSpec((ne, d), const),
                  pl.BlockSpec((ne, d), const)],
        out_specs=[pl.BlockSpec((tm, d), lambda i: (i, 0)),
                   pl.BlockSpec((tm, d), lambda i: (i, 0)),
                   pl.BlockSpec((ne, tm), lambda i: (0, i))],
        out_shape=[jax.ShapeDtypeStruct((t, d), F32),
                   jax.ShapeDtypeStruct((t, d), F32),
                   jax.ShapeDtypeStruct((ne, t), F32)],
        compiler_params=_params(1),
        name="merge",
    )(ya, yb, proj, proj, x2, gta, scm, shm, gmoe, wua, wub, wo, wrh, wrl)


def _route_kernel(lg_ref, rb_ref, tri_ref, w_ref, csel_ref, code_ref, cend_ref, carry_ref,
                  *, n_exp, n_groups, topk_groups, top_k, scale):
    gsz = n_exp // n_groups
    tc = lg_ref.shape[1]
    ninf = -jnp.inf
    step = pl.program_id(0)

    @pl.when(step == 0)
    def _():
        carry_ref[...] = jnp.zeros(carry_ref.shape, F32)

    row = lax.broadcasted_iota(jnp.int32, (gsz, tc), 0).astype(F32)
    sc, bi = [], []
    for g in range(n_groups):
        s = _sigmoid(lg_ref[g * gsz:(g + 1) * gsz, :])
        sc.append(s)
        bi.append(s + rb_ref[g * gsz:(g + 1) * gsz, :])
    gs = []
    for x in bi:
        m1 = jnp.max(x, axis=0, keepdims=True)
        eq = x == m1
        cnt = jnp.sum(jnp.where(eq, 1.0, 0.0), axis=0, keepdims=True)
        second = jnp.max(jnp.where(eq, ninf, x), axis=0, keepdims=True)
        gs.append(m1 + jnp.where(cnt >= 2.0, m1, second))
    gmat = jnp.concatenate(gs, axis=0)
    grow = lax.broadcasted_iota(jnp.int32, gmat.shape, 0).astype(F32)
    gsel = jnp.zeros(gmat.shape, F32)
    for _ in range(topk_groups):
        m = jnp.max(gmat, axis=0, keepdims=True)
        idx = jnp.min(jnp.where(gmat == m, grow, float(n_groups)), axis=0, keepdims=True)
        hit = grow == idx
        gsel = jnp.where(hit, 1.0, gsel)
        gmat = jnp.where(hit, ninf, gmat)
    cur = [jnp.where(gsel[g:g + 1, :] > 0.5, bi[g], ninf) for g in range(n_groups)]
    ids = [row + float(g * gsz) for g in range(n_groups)]
    w_rows = []
    chosen = [jnp.zeros((gsz, tc), F32) for _ in range(n_groups)]
    kplus = [jnp.zeros((gsz, tc), F32) for _ in range(n_groups)]
    for k in range(top_k):
        mg = cur[0]
        for c in cur[1:]:
            mg = jnp.maximum(mg, c)
        m = jnp.max(mg, axis=0, keepdims=True)
        cand = jnp.where(cur[0] == m, ids[0], float(n_exp))
        for c, i in zip(cur[1:], ids[1:]):
            cand = jnp.minimum(cand, jnp.where(c == m, i, float(n_exp)))
        idx = jnp.min(cand, axis=0, keepdims=True)
        wsum = jnp.zeros((gsz, tc), F32)
        for g in range(n_groups):
            hit = ids[g] == idx
            wsum = wsum + jnp.where(hit, sc[g], 0.0)
            cur[g] = jnp.where(hit, ninf, cur[g])
            chosen[g] = jnp.where(hit, 1.0, chosen[g])
            kplus[g] = jnp.where(hit, float(k + 1), kplus[g])
        w_rows.append(jnp.sum(wsum, axis=0, keepdims=True))
    w = jnp.concatenate(w_rows, axis=0)
    w_ref[...] = w / jnp.sum(w, axis=0, keepdims=True) * scale

    onehot = jnp.concatenate(chosen, axis=0)
    incl = _dot(onehot.astype(BF16), tri_ref[...])
    carry = carry_ref[...]
    cglob = carry[:, 0:1] + incl
    tok = (lax.broadcasted_iota(jnp.int32, (n_exp, tc), 1) + step * tc).astype(F32)
    picked = onehot > 0.5
    csel_ref[0:n_exp, :] = jnp.where(picked, cglob, 0.0)
    code_ref[0:n_exp, :] = jnp.where(picked, tok * float(top_k) + jnp.concatenate(kplus, axis=0) - 1.0, 0.0)
    new_carry = carry + jnp.sum(onehot, axis=1, keepdims=True)
    carry_ref[...] = new_carry
    cend_ref[0] = new_carry


def route(lg, rb, cfg):
    ne, t = lg.shape
    nx = cfg.n_experts
    tc = min(cfg.route_tc, t)
    nsteps = t // tc
    tri = jnp.asarray(np.triu(np.ones((tc, tc), np.float32)), BF16)
    kern = functools.partial(_route_kernel, n_exp=cfg.n_experts, n_groups=cfg.n_groups,
                             topk_groups=cfg.topk_groups, top_k=cfg.top_k, scale=cfg.routed_scale)
    return pl.pallas_call(
        kern,
        grid=(nsteps,),
        in_specs=[pl.BlockSpec((ne, tc), lambda i: (0, i)),
                  pl.BlockSpec((ne, tc), lambda i: (0, 0)),
                  pl.BlockSpec((tc, tc), lambda i: (0, 0))],
        out_specs=[pl.BlockSpec((cfg.top_k, tc), lambda i: (0, i)),
                   pl.BlockSpec((nx, tc), lambda i: (0, i)),
                   pl.BlockSpec((nx, tc), lambda i: (0, i)),
                   pl.BlockSpec((1, nx, LANES), lambda i: (i, 0, 0))],
        out_shape=[jax.ShapeDtypeStruct((cfg.top_k, t), F32),
                   jax.ShapeDtypeStruct((nx, t), F32),
                   jax.ShapeDtypeStruct((nx, t), F32),
                   jax.ShapeDtypeStruct((nsteps, nx, LANES), F32)],
        scratch_shapes=[pltpu.VMEM((cfg.n_experts, LANES), F32)],
        compiler_params=_params(1),
        name="route",
    )(lg, rb, tri)


def _slots_kernel(be_ref, r0_ref, cnt_ref, csel_ref, code_ref, cend_ref, tok_ref, dst_ref,
                  *, bm, tc, top_k, n_tok, n_assign):
    b = pl.program_id(0)
    e = be_ref[b]
    r0 = r0_ref[b]
    cnt = cnt_ref[b]
    sub = lax.rem(e, 8)
    nsteps = cend_ref.shape[0]
    pick3 = lax.broadcasted_iota(jnp.int32, cend_ref.shape, 1) == sub
    cend = jnp.sum(jnp.where(pick3, cend_ref[...], 0.0), axis=1)
    first = (r0 + 1).astype(F32)
    last = jnp.minimum(r0 + bm, cnt).astype(F32)
    jlo = (jnp.sum(jnp.where(cend < first, 1.0, 0.0)) * (1.0 / LANES)).astype(jnp.int32)
    jhi = (jnp.sum(jnp.where(cend < last, 1.0, 0.0)) * (1.0 / LANES)).astype(jnp.int32)
    jhi = jnp.minimum(jhi, nsteps - 1)
    jhi = jnp.where(last < first, jlo - 1, jhi)

    want = (lax.broadcasted_iota(jnp.int32, (bm, tc), 0) + (r0 + 1)).astype(F32)
    pick2 = lax.broadcasted_iota(jnp.int32, (8, tc), 0) == sub
    zrows = jnp.zeros((13, tc), F32)

    def body(j, acc):
        off = pl.multiple_of(j * tc, tc)
        crow = jnp.sum(jnp.where(pick2, csel_ref[:, pl.ds(off, tc)], 0.0), axis=0, keepdims=True)
        drow = jnp.sum(jnp.where(pick2, code_ref[:, pl.ds(off, tc)], 0.0), axis=0, keepdims=True)
        di = drow.astype(jnp.int32)
        digits = jnp.concatenate([(di >> 14).astype(F32), ((di >> 7) & 127).astype(F32),
                                  (di & 127).astype(F32), zrows], axis=0).astype(BF16)
        eq = jnp.where(want == crow, 1.0, 0.0).astype(BF16)
        return acc + _dot_t(digits, eq)

    acc = lax.fori_loop(jlo, jhi + 1, body, jnp.zeros((16, bm), F32))
    code = (acc[0:1, :] * 16384.0 + acc[1:2, :] * 128.0 + acc[2:3, :]).astype(jnp.int32)
    lane = lax.broadcasted_iota(jnp.int32, (1, bm), 1)
    valid = (lane + r0) < cnt
    shift = top_k.bit_length() - 1
    tok = code >> shift
    kk = code & (top_k - 1)
    dummy = n_assign + lax.rem(b, 2) * bm + lane
    tok_ref[0] = jnp.where(valid, tok, 0)
    dst_ref[0] = jnp.where(valid, kk * n_tok + tok, dummy)


def slots(block_e, r0, cntb, csel, code, cend, cfg, n_tok):
    nb = block_e.shape[0]
    bm = cfg.moe_rows
    ne, t = csel.shape
    tc = min(cfg.route_tc, t)
    nsteps = cend.shape[0]
    assert cfg.top_k & (cfg.top_k - 1) == 0 and n_tok * cfg.top_k < (1 << 21)
    kern = functools.partial(_slots_kernel, bm=bm, tc=tc, top_k=cfg.top_k, n_tok=n_tok, n_assign=n_tok * cfg.top_k)
    grid_spec = pltpu.PrefetchScalarGridSpec(
        num_scalar_prefetch=3,
        grid=(nb,),
        in_specs=[pl.BlockSpec((8, t), lambda b, be, r0, cn: (be[b] // 8, 0)),
                  pl.BlockSpec((8, t), lambda b, be, r0, cn: (be[b] // 8, 0)),
                  pl.BlockSpec((nsteps, 8, LANES), lambda b, be, r0, cn: (0, be[b] // 8, 0))],
        out_specs=[pl.BlockSpec((1, 1, bm), lambda b, be, r0, cn: (b, 0, 0)),
                   pl.BlockSpec((1, 1, bm), lambda b, be, r0, cn: (b, 0, 0))],
    )
    return pl.pallas_call(
        kern,
        grid_spec=grid_spec,
        out_shape=[jax.ShapeDtypeStruct((nb, 1, bm), jnp.int32), jax.ShapeDtypeStruct((nb, 1, bm), jnp.int32)],
        compiler_params=_params(1),
        name="slots",
    )(block_e, r0, cntb, csel, code, cend)


MOE_RING = 3


def _moe_kernel(be_ref, tok0_ref, tok1_ref, tokn_ref, dstp_ref, dstc_ref, v_hbm, wg_ref, wu_ref, wd_ref, yk_hbm,
                xbuf, ybuf, wgb, wub, wdb, gsem, ssem, *, bm, n_assign):
    b = pl.program_id(0)
    nb = pl.num_programs(0)
    slot = lax.rem(b, MOE_RING)
    nslab = xbuf.shape[2]

    def gather_copy(t, r, s):
        return pltpu.make_async_copy(v_hbm.at[t], xbuf.at[s, r // 8, :, r % 8, :], gsem.at[s])

    def scatter_copy(r, d, s):
        return pltpu.make_async_copy(ybuf.at[s, r // 8, :, r % 8, :], yk_hbm.at[d], ssem.at[s])

    def wait_gather(s):
        for r in range(bm):
            gather_copy(0, r, s).wait()

    def wait_scatter(s):
        for r in range(bm):
            scatter_copy(r, 0, s).wait()

    @pl.when(b == 0)
    def _():
        ybuf[...] = jnp.zeros(ybuf.shape, F32)
        for s in range(2):
            for r in range(bm):
                scatter_copy(r, n_assign + s * bm + r, s).start(priority=r % 2)
        for r in range(bm):
            gather_copy(tok0_ref[0, 0, r], r, 0).start(priority=r % 2)
        for r in range(bm):
            gather_copy(tok1_ref[0, 0, r], r, 1).start(priority=r % 2)

    e = be_ref[b]
    eprev = be_ref[jnp.maximum(b - 1, 0)]

    @pl.when(jnp.logical_or(b == 0, e != eprev))
    def _():
        wgb[...] = wg_ref[0].astype(BF16)
        wub[...] = wu_ref[0].astype(BF16)
        wdb[...] = wd_ref[0].astype(BF16)

    def block_step(s):
        p = (s + 2) % MOE_RING
        wait_gather(s)
        wait_scatter(s)
        for r in range(bm):
            scatter_copy(r, dstp_ref[0, 0, r], p).start(priority=r % 2)
        for r in range(bm):
            gather_copy(tokn_ref[0, 0, r], r, p).start(priority=r % 2)
        x = jnp.concatenate([xbuf[s, :, j].reshape(bm, LANES) for j in range(nslab)], axis=1).astype(BF16)
        g = _dot(x, wgb[...])
        u = _dot(x, wub[...])
        h = (g * _sigmoid(g)) * u
        y = _dot(h.astype(BF16), wdb[...])
        for j in range(nslab):
            ybuf[s, :, j] = y[:, j * LANES:(j + 1) * LANES].reshape(bm // 8, 8, LANES)

    for s in range(MOE_RING):
        pl.when(slot == s)(functools.partial(block_step, s))

    @pl.when(b == nb - 1)
    def _():
        def drain(s):
            p, q = (s + 2) % MOE_RING, (s + 1) % MOE_RING
            wait_scatter(q)
            for r in range(bm):
                scatter_copy(r, dstc_ref[0, 0, r], s).start(priority=r % 2)
            wait_gather(p)
            wait_gather(q)
            wait_scatter(p)
            wait_scatter(s)

        for s in range(MOE_RING):
            pl.when(slot == s)(functools.partial(drain, s))


def moe(v, block_e, slot_tok, slot_dst, wg, wu, wd, cfg):
    t, d = v.shape
    bm = cfg.moe_rows
    v3 = v.reshape(t, d // LANES, LANES)
    nb = block_e.shape[0]
    assert nb >= 2
    de = wg.shape[2]
    n_assign = t * cfg.top_k
    first = (n_assign + 2 * bm + jnp.arange(bm, dtype=jnp.int32)).reshape(1, 1, bm)
    dst_ext = jnp.concatenate([first, slot_dst], axis=0)
    kern = functools.partial(_moe_kernel, bm=bm, n_assign=n_assign)
    smem_blk = lambda f: pl.BlockSpec((1, 1, bm), f, memory_space=pltpu.SMEM)
    grid_spec = pltpu.PrefetchScalarGridSpec(
        num_scalar_prefetch=1,
        grid=(nb,),
        in_specs=[smem_blk(lambda b, be: (0, 0, 0)),
                  smem_blk(lambda b, be: (1, 0, 0)),
                  smem_blk(lambda b, be: (jnp.minimum(b + 2, nb - 1), 0, 0)),
                  smem_blk(lambda b, be: (b, 0, 0)),
                  smem_blk(lambda b, be: (b + 1, 0, 0)),
                  pl.BlockSpec(memory_space=pl.ANY),
                  pl.BlockSpec((1, d, de), lambda b, be: (be[b], 0, 0)),
                  pl.BlockSpec((1, d, de), lambda b, be: (be[b], 0, 0)),
                  pl.BlockSpec((1, de, d), lambda b, be: (be[b], 0, 0))],
        out_specs=pl.BlockSpec(memory_space=pl.ANY),
        scratch_shapes=[pltpu.VMEM((MOE_RING, bm // 8, d // LANES, 8, LANES), F32),
                        pltpu.VMEM((MOE_RING, bm // 8, d // LANES, 8, LANES), F32),
                        pltpu.VMEM((d, de), BF16), pltpu.VMEM((d, de), BF16), pltpu.VMEM((de, d), BF16),
                        pltpu.SemaphoreType.DMA((MOE_RING,)), pltpu.SemaphoreType.DMA((MOE_RING,))],
    )
    return pl.pallas_call(
        kern,
        grid_spec=grid_spec,
        out_shape=jax.ShapeDtypeStruct((n_assign + 3 * bm, d // LANES, LANES), F32),
        compiler_params=_params(1),
        name="moe",
    )(block_e, slot_tok, slot_tok, slot_tok, dst_ext, dst_ext, v3, wg, wu, wd)


def _final_kernel(w_ref, v_ref, h1_ref, gtm_ref, gfin_ref, sg_ref, su_ref, sd_ref, *rest, top_k):
    yk_refs, o_ref, comb_ref = rest[:top_k], rest[top_k], rest[top_k + 1]
    tm = v_ref.shape[0]
    nslab = yk_refs[0].shape[1]

    def combine(t, carry):
        acc = w_ref[t, 0] * yk_refs[0][t]
        for k in range(1, top_k):
            acc = acc + w_ref[t, k] * yk_refs[k][t]
        comb_ref[pl.ds(pl.multiple_of(t * nslab, nslab), nslab), :] = acc
        return carry

    lax.fori_loop(0, tm, combine, 0, unroll=8)
    routed = jnp.concatenate([comb_ref[pl.ds(j, tm, stride=nslab), :] for j in range(nslab)], axis=1)

    vb = v_ref[...].astype(BF16)
    g = _dot(vb, sg_ref[...])
    u = _dot(vb, su_ref[...])
    y = routed + _dot(((g * _sigmoid(g)) * u).astype(BF16), sd_ref[...])
    h2 = h1_ref[...] + gtm_ref[0] * y
    o_ref[...] = h2 * lax.rsqrt(jnp.mean(h2 * h2, axis=-1, keepdims=True) + EPS) * gfin_ref[...]


def final(v, h1, yk3, w_tk, gtm, gfin, sg, su, sd, cfg):
    t, d = v.shape
    tm = min(cfg.final_tm, cfg.seq)
    per_b = cfg.seq // tm
    nt = t // tm
    ds = sg.shape[1]
    nslab = d // LANES
    const = lambda i: (0, 0)
    kern = functools.partial(_final_kernel, top_k=cfg.top_k)
    plane = lambda k: pl.BlockSpec((tm, nslab, LANES), lambda i: (k * nt + i, 0, 0))
    return pl.pallas_call(
        kern,
        grid=(nt,),
        in_specs=[pl.BlockSpec((tm, cfg.top_k), lambda i: (i, 0), memory_space=pltpu.SMEM),
                  pl.BlockSpec((tm, d), lambda i: (i, 0)),
                  pl.BlockSpec((tm, d), lambda i: (i, 0)),
                  pl.BlockSpec((1, 1, d), lambda i: (i // per_b, 0, 0)),
                  pl.BlockSpec((1, d), const),
                  pl.BlockSpec((d, ds), const),
                  pl.BlockSpec((d, ds), const),
                  pl.BlockSpec((ds, d), const)] + [plane(k) for k in range(cfg.top_k)],
        out_specs=pl.BlockSpec((tm, d), lambda i: (i, 0)),
        out_shape=jax.ShapeDtypeStruct((t, d), F32),
        scratch_shapes=[pltpu.VMEM((tm * nslab, LANES), F32)],
        compiler_params=_params(1),
        name="final",
    )(w_tk, v, h1, gtm, gfin, sg, su, sd, *([yk3] * cfg.top_k))


def _t5_bucket(rel, cfg):
    nb = cfg.t5_buckets // 2
    ret = (rel > 0).astype(np.int32) * nb
    n = np.abs(rel)
    max_exact = nb // 2
    large = max_exact + (np.log(np.maximum(n, 1) / max_exact) / math.log(cfg.t5_max_dist / max_exact)
                         * (nb - max_exact)).astype(np.int32)
    large = np.minimum(large, nb - 1)
    return (ret + np.where(n < max_exact, n, large)).astype(np.int32)


def _toeplitz(vec, n):
    lead = vec.shape[:-1]
    flat = jnp.tile(vec, (1,) * len(lead) + (n,))[..., :n * (2 * n - 1)]
    return flat.reshape(lead + (n, 2 * n - 1))[..., :n]


def _rel_offsets(n):
    m = np.arange(2 * n)
    return np.where(m < n, m, m - 2 * n)


def _bias_tables_a(t5_bias, cfg):
    tile = cfg.attn_tile
    assert tile + 1 >= cfg.t5_max_dist and tile % cfg.chunk == 0
    ql = np.arange(tile)[:, None]
    kl = np.arange(tile)[None, :]
    allowed = (kl // cfg.chunk) <= (ql // cfg.chunk)
    r = _rel_offsets(tile)
    t5 = t5_bias.astype(F32)
    diag = _toeplitz(t5[_t5_bucket(r, cfg)].T, tile)
    prev = _toeplitz(t5[_t5_bucket(r - tile, cfg)].T, tile)
    far = jnp.broadcast_to(t5[_t5_bucket(np.array(-2 * tile), cfg)][:, None, None], diag.shape)
    return jnp.stack([jnp.where(allowed[None], diag, NEG), prev, far], axis=1)


def _bias_tables_b(rel_table, cfg):
    tile = cfg.attn_tile
    assert (cfg.left_chunks * cfg.chunk) % tile == 0 and tile % cfg.chunk == 0
    nwin = (cfg.left_chunks * cfg.chunk) // tile + 1
    ql = np.arange(tile)[:, None]
    kl = np.arange(tile)[None, :]
    r = _rel_offsets(tile)
    rt = rel_table.astype(F32)
    tabs = []
    for j in range(nwin):
        off = (j - (nwin - 1)) * tile
        idx = np.clip(r + off, -cfg.rel_clip, cfg.rel_clip) + cfg.rel_clip
        dchunk = ql // cfg.chunk - (kl + off) // cfg.chunk
        band = (dchunk >= 0) & (dchunk <= cfg.left_chunks)
        tabs.append(jnp.where(band[None], _toeplitz(rt[idx].T, tile), NEG))
    return jnp.stack(tabs, axis=1)


def _block_tables(counts, n_assign, cfg):
    e_ = cfg.n_experts
    bm = cfg.moe_rows
    nb = -(-(n_assign + e_ * (bm - 1)) // bm)
    pcounts = (counts + bm - 1) // bm * bm
    pends = jnp.cumsum(pcounts)
    pstarts = pends - pcounts
    start_b = jnp.arange(nb, dtype=jnp.int32) * bm
    block_e = jnp.minimum(jnp.sum((pends[None, :] <= start_b[:, None]).astype(jnp.int32), axis=1), e_ - 1)
    mine = jnp.arange(e_, dtype=jnp.int32)[None, :] == block_e[:, None]
    pstart_b = jnp.sum(jnp.where(mine, pstarts[None, :], 0), axis=1)
    count_b = jnp.sum(jnp.where(mine, counts[None, :], 0), axis=1)
    return block_e.astype(jnp.int32), (start_b - pstart_b).astype(jnp.int32), count_b.astype(jnp.int32)


def _forward(cfg, x, c, w_ada, b_ada, g_attn, w_in, lambda_qk, t5_bias, rel_bias_b, w_up_a, w_up_b, w_o,
             g_moe, w_router, router_bias, w_exp_gate, w_exp_up, w_exp_down,
             w_sh_gate, w_sh_up, w_sh_down, g_final):
    b_, s_, d_ = x.shape
    t_ = b_ * s_
    l = 0
    qk_a = cfg.ha * 2 * cfg.da_qk
    width_a = cfg.ha * 2 * cfg.da_qk
    width_b = cfg.hb * cfg.db
    in_cols = w_in.shape[2]

    rows = max(16, b_)
    c_pad = jnp.zeros((rows, d_), F32).at[:b_].set(c)
    mod = ada_mod(c_pad, w_ada[l], b_ada[l], cfg)[:b_]
    sh_a, sc_a, gt_a, sh_m, sc_m, gt_m = [m.reshape(b_, 1, d_) for m in jnp.split(mod, cfg.n_mod, axis=-1)]

    colscale = np.ones((1, in_cols), np.float32)
    colscale[0, :qk_a] = cfg.da_qk ** -0.5
    qb0 = 2 * qk_a + width_a
    colscale[0, qb0:qb0 + width_b] = cfg.db ** -0.5
    x2 = x.reshape(t_, d_)
    proj = in_proj(x2, g_attn[l].reshape(1, d_), sc_a, sh_a, w_in[l].astype(BF16), jnp.asarray(colscale), cfg)

    lq = lambda_qk[l].astype(F32)
    lam = jnp.exp(jnp.sum(lq[0] * lq[1])) - jnp.exp(jnp.sum(lq[2] * lq[3])) + cfg.lam_init
    ya = attn_a(proj, lam.reshape(1), _bias_tables_a(t5_bias, cfg), cfg)
    yb = attn_b(proj, _bias_tables_b(rel_bias_b[l], cfg), cfg)

    ne_pad = max(LANES, cfg.n_experts)
    wrt = jnp.zeros((ne_pad, d_), F32).at[:cfg.n_experts].set(w_router[l].T)
    wrh, wrl = _split_bf16(wrt)
    h1, v, lg = merge(ya, yb, proj, x2, gt_a, sc_m, sh_m, g_moe[l].reshape(1, d_),
                      w_up_a[l].astype(BF16), w_up_b[l].astype(BF16), w_o[l].astype(BF16), wrh, wrl, cfg)

    tc = min(cfg.route_tc, t_)
    rb = jnp.zeros((ne_pad, tc), F32).at[:cfg.n_experts].set(
        jnp.broadcast_to(router_bias[l].astype(F32)[:, None], (cfg.n_experts, tc)))
    w_kt, csel, code, cend = route(lg, rb, cfg)
    counts = cend[-1, :, 0].astype(jnp.int32)
    block_e, r0, count_b = _block_tables(counts, t_ * cfg.top_k, cfg)
    slot_tok, slot_dst = slots(block_e, r0, count_b, csel, code, cend, cfg, t_)

    yk = moe(v, block_e, slot_tok, slot_dst, w_exp_gate[l], w_exp_up[l], w_exp_down[l], cfg)
    out = final(v, h1, yk, w_kt.T, gt_m, g_final.reshape(1, d_),
                w_sh_gate[l].astype(BF16), w_sh_up[l].astype(BF16), w_sh_down[l].astype(BF16), cfg)
    return out.reshape(b_, s_, d_)


def kernel(x, c, w_ada, b_ada, g_attn, w_in, lambda_qk, t5_bias, rel_bias_b, w_up_a, w_up_b, w_o, g_moe,
           w_router, router_bias, w_exp_gate, w_exp_up, w_exp_down, w_sh_gate, w_sh_up, w_sh_down, g_final):
    return _forward(Cfg(), x, c, w_ada, b_ada, g_attn, w_in, lambda_qk, t5_bias, rel_bias_b, w_up_a, w_up_b,
                    w_o, g_moe, w_router, router_bias, w_exp_gate, w_exp_up, w_exp_down,
                    w_sh_gate, w_sh_up, w_sh_down, g_final)
```

```python
import functools
import math
from typing import NamedTuple

import numpy as np
import jax
import jax.numpy as jnp
from jax import lax
from jax.experimental import pallas as pl
from jax.experimental.pallas import tpu as pltpu

F32 = jnp.float32
BF16 = jnp.bfloat16
NEG = -1e30
EPS = 1e-6
HEAD_EPS = 1e-5
LANES = 128
VMEM_LIMIT = 56 * 1024 * 1024


class Cfg(NamedTuple):
    batch: int = 8
    seq: int = 2048
    d_model: int = 2048
    chunk: int = 64
    ha: int = 8
    da_qk: int = 64
    hb: int = 8
    db: int = 128
    left_chunks: int = 8
    rel_clip: int = 256
    t5_buckets: int = 32
    t5_max_dist: int = 128
    n_experts: int = 64
    top_k: int = 8
    n_groups: int = 8
    topk_groups: int = 4
    d_expert: int = 512
    d_shared: int = 512
    routed_scale: float = 2.5
    n_mod: int = 6
    lam_init: float = 0.8 - 0.6 * math.exp(-0.3 * 0)
    attn_tile: int = 256
    moe_rows: int = 256
    proj_tm: int = 1024
    proj_tn: int = 1024
    merge_tm: int = 256
    route_tc: int = 512
    final_tm: int = 128
    ada_tn: int = 1024


def _sigmoid(x):
    return 1.0 / (1.0 + jnp.exp(-x))


def _split_bf16(x):
    hi = x.astype(BF16)
    lo = (x - hi.astype(F32)).astype(BF16)
    return hi, lo


def _dot(a, b):
    return jnp.dot(a, b, preferred_element_type=F32)


def _dot_t(a, b):
    return lax.dot_general(a, b, (((1,), (1,)), ((), ())), preferred_element_type=F32)


U32 = jnp.uint32
HI_MASK = 0xFFFF0000


def _pack_pairs(x):
    n = x.shape[1] // 2
    bits = pltpu.bitcast(x.astype(BF16).astype(F32), U32)
    return (bits[:, :n] >> 16) | (bits[:, n:] & U32(HI_MASK))


def _unpack_pairs(p):
    return pltpu.bitcast(p << 16, F32), pltpu.bitcast(p & U32(HI_MASK), F32)


def _load_slabs(ref, m, ns):
    return jnp.concatenate([ref[pl.ds(j, m, stride=ns), :] for j in range(ns)], axis=1)


def _store_slabs(ref, val, ns):
    m = val.shape[0]
    for j in range(ns):
        ref[pl.ds(j, m, stride=ns), :] = val[:, j * LANES:(j + 1) * LANES]


def _params(n_axes):
    return pltpu.CompilerParams(dimension_semantics=("arbitrary",) * n_axes,
                                vmem_limit_bytes=VMEM_LIMIT)


def _ada_kernel(c_ref, w_ref, b_ref, o_ref):
    c = c_ref[...]
    s_hi, s_lo = _split_bf16(c * _sigmoid(c))
    w_hi, w_lo = _split_bf16(w_ref[...])
    acc = _dot(s_hi, w_hi) + _dot(s_hi, w_lo) + _dot(s_lo, w_hi)
    o_ref[...] = acc + b_ref[...]


def ada_mod(c_pad, w_ada, b_ada, cfg):
    rows, d = c_pad.shape
    n = w_ada.shape[1]
    tn = min(cfg.ada_tn, n)
    return pl.pallas_call(
        _ada_kernel,
        grid=(n // tn,),
        in_specs=[pl.BlockSpec((rows, d), lambda j: (0, 0)),
                  pl.BlockSpec((d, tn), lambda j: (0, j)),
                  pl.BlockSpec((1, tn), lambda j: (0, j))],
        out_specs=pl.BlockSpec((rows, tn), lambda j: (0, j)),
        out_shape=jax.ShapeDtypeStruct((rows, n), F32),
        compiler_params=_params(1),
        name="ada_mod",
    )(c_pad, w_ada, b_ada.reshape(1, n))


def _inproj_kernel(x_ref, g_ref, sc_ref, sh_ref, w_ref, cs_ref, o_ref, u_ref):
    @pl.when(pl.program_id(1) == 0)
    def _():
        x = x_ref[...]
        y = x * lax.rsqrt(jnp.mean(x * x, axis=-1, keepdims=True) + EPS) * g_ref[...]
        u_ref[...] = (y * (1.0 + sc_ref[0]) + sh_ref[0]).astype(BF16)

    acc = _dot(u_ref[...], w_ref[...])
    o_ref[...] = (acc * cs_ref[...]).astype(BF16)


def in_proj(x2, g, sc, sh, w_bf, colscale, cfg):
    t, d = x2.shape
    n = w_bf.shape[1]
    tm = min(cfg.proj_tm, cfg.seq)
    tn = min(cfg.proj_tn, n)
    per_b = cfg.seq // tm
    return pl.pallas_call(
        _inproj_kernel,
        grid=(t // tm, n // tn),
        in_specs=[pl.BlockSpec((tm, d), lambda i, j: (i, 0)),
                  pl.BlockSpec((1, d), lambda i, j: (0, 0)),
                  pl.BlockSpec((1, 1, d), lambda i, j: (i // per_b, 0, 0)),
                  pl.BlockSpec((1, 1, d), lambda i, j: (i // per_b, 0, 0)),
                  pl.BlockSpec((d, tn), lambda i, j: (0, j)),
                  pl.BlockSpec((1, tn), lambda i, j: (0, j))],
        out_specs=pl.BlockSpec((tm, tn), lambda i, j: (i, j)),
        out_shape=jax.ShapeDtypeStruct((t, n), BF16),
        scratch_shapes=[pltpu.VMEM((tm, d), BF16)],
        compiler_params=_params(2),
        name="in_proj",
    )(x2, g, sc, sh, w_bf, colscale)


def _attn_a_kernel(lam_ref, q_ref, k_ref, v_ref, tb_ref, o_ref,
                   s0_ref, s1_ref, m0_ref, m1_ref, l0_ref, l1_ref, acc_ref, *, tile, half, out_scale):
    qi = pl.program_id(2)
    nk = qi + 1
    q = q_ref[...]
    lane = lax.broadcasted_iota(jnp.int32, q.shape, 1)
    zero = jnp.zeros_like(q)
    q0 = jnp.where(lane < half, q, zero)
    q1 = jnp.where(lane >= half, q, zero)

    m0_ref[...] = jnp.full(m0_ref.shape, -jnp.inf, F32)
    m1_ref[...] = jnp.full(m1_ref.shape, -jnp.inf, F32)
    l0_ref[...] = jnp.zeros(l0_ref.shape, F32)
    l1_ref[...] = jnp.zeros(l1_ref.shape, F32)
    acc_ref[...] = jnp.zeros(acc_ref.shape, F32)

    def fold(x):
        r = x[:, :LANES]
        for c in range(1, tile // LANES):
            r = r + x[:, c * LANES:(c + 1) * LANES]
        return r

    def foldmax(x):
        r = x[:, :LANES]
        for c in range(1, tile // LANES):
            r = jnp.maximum(r, x[:, c * LANES:(c + 1) * LANES])
        return r

    def scores(kb, carry):
        kblk = k_ref[pl.ds(pl.multiple_of(kb * tile, tile), tile), :]
        bias = tb_ref[0, jnp.minimum(qi - kb, 2)]
        s0 = _dot_t(q0, kblk) + bias
        s1 = _dot_t(q1, kblk) + bias
        s0_ref[kb] = s0
        s1_ref[kb] = s1
        m0_ref[...] = jnp.maximum(m0_ref[...], foldmax(s0))
        m1_ref[...] = jnp.maximum(m1_ref[...], foldmax(s1))
        return carry

    lax.fori_loop(0, nk, scores, 0)
    m0 = jnp.max(m0_ref[...], axis=-1, keepdims=True)
    m1 = jnp.max(m1_ref[...], axis=-1, keepdims=True)

    def expo(kb, carry):
        p0 = jnp.exp(s0_ref[kb] - m0)
        p1 = jnp.exp(s1_ref[kb] - m1)
        s0_ref[kb] = p0
        s1_ref[kb] = p1
        l0_ref[...] += fold(p0)
        l1_ref[...] += fold(p1)
        return carry

    lax.fori_loop(0, nk, expo, 0)
    c0 = 1.0 / jnp.sum(l0_ref[...], axis=-1, keepdims=True)
    c1 = lam_ref[0] / jnp.sum(l1_ref[...], axis=-1, keepdims=True)

    def pv(kb, carry):
        a = s0_ref[kb] * c0 - s1_ref[kb] * c1
        vblk = v_ref[pl.ds(pl.multiple_of(kb * tile, tile), tile), :]
        acc_ref[...] += _dot(a.astype(BF16), vblk)
        return carry

    lax.fori_loop(0, nk, pv, 0)
    o = acc_ref[...]
    o = o * lax.rsqrt(jnp.mean(o * o, axis=-1, keepdims=True) + HEAD_EPS) * out_scale
    o_ref[...] = o.astype(BF16)


def attn_a(proj, lam, tb, cfg):
    t = proj.shape[0]
    tile = cfg.attn_tile
    s = cfg.seq
    nq = s // tile
    dv = 2 * cfg.da_qk
    assert dv == LANES
    kcol = cfg.ha
    vcol = 2 * cfg.ha
    kern = functools.partial(_attn_a_kernel, tile=tile, half=cfg.da_qk, out_scale=1.0 - cfg.lam_init)
    return pl.pallas_call(
        kern,
        grid=(cfg.batch, cfg.ha, nq),
        in_specs=[pl.BlockSpec(memory_space=pltpu.SMEM),
                  pl.BlockSpec((tile, LANES), lambda b, h, i: (b * nq + i, h)),
                  pl.BlockSpec((s, LANES), lambda b, h, i: (b, kcol + h)),
                  pl.BlockSpec((s, LANES), lambda b, h, i: (b, vcol + h)),
                  pl.BlockSpec((1, 3, tile, tile), lambda b, h, i: (h, 0, 0, 0))],
        out_specs=pl.BlockSpec((tile, LANES), lambda b, h, i: (b * nq + i, h)),
        out_shape=jax.ShapeDtypeStruct((t, cfg.ha * dv), BF16),
        scratch_shapes=[pltpu.VMEM((nq, tile, tile), F32), pltpu.VMEM((nq, tile, tile), F32),
                        pltpu.VMEM((tile, LANES), F32), pltpu.VMEM((tile, LANES), F32),
                        pltpu.VMEM((tile, LANES), F32), pltpu.VMEM((tile, LANES), F32),
                        pltpu.VMEM((tile, LANES), F32)],
        compiler_params=_params(3),
        name="attn_a",
    )(lam, proj, proj, proj, tb)


def _attn_b_kernel(q_ref, k_ref, v_ref, tb_ref, o_ref, *, tile, nwin):
    qi = pl.program_id(2)
    q = q_ref[...]
    ss, vs = [], []
    for j in range(nwin):
        kb = qi - (nwin - 1) + j
        start = pl.multiple_of(jnp.maximum(kb, 0) * tile, tile)
        s = _dot_t(q, k_ref[pl.ds(start, tile), :]) + tb_ref[0, j]
        ss.append(jnp.where(kb >= 0, s, NEG))
        vs.append(v_ref[pl.ds(start, tile), :])
    m = jnp.max(ss[0], axis=-1, keepdims=True)
    for s in ss[1:]:
        m = jnp.maximum(m, jnp.max(s, axis=-1, keepdims=True))
    ps = [jnp.exp(s - m) for s in ss]
    l = jnp.sum(ps[0], axis=-1, keepdims=True)
    for p in ps[1:]:
        l = l + jnp.sum(p, axis=-1, keepdims=True)
    inv = 1.0 / l
    acc = _dot((ps[0] * inv).astype(BF16), vs[0])
    for p, v in zip(ps[1:], vs[1:]):
        acc = acc + _dot((p * inv).astype(BF16), v)
    o_ref[...] = acc.astype(BF16)


def attn_b(proj, tb, cfg):
    t = proj.shape[0]
    tile = cfg.attn_tile
    s = cfg.seq
    nq = s // tile
    assert cfg.db == LANES
    nwin = (cfg.left_chunks * cfg.chunk) // tile + 1
    base = (2 * cfg.ha * 2 * cfg.da_qk + cfg.ha * 2 * cfg.da_qk) // LANES
    kern = functools.partial(_attn_b_kernel, tile=tile, nwin=nwin)
    return pl.pallas_call(
        kern,
        grid=(cfg.batch, cfg.hb, nq),
        in_specs=[pl.BlockSpec((tile, LANES), lambda b, h, i: (b * nq + i, base + h)),
                  pl.BlockSpec((s, LANES), lambda b, h, i: (b, base + cfg.hb + h)),
                  pl.BlockSpec((s, LANES), lambda b, h, i: (b, base + 2 * cfg.hb + h)),
                  pl.BlockSpec((1, nwin, tile, tile), lambda b, h, i: (h, 0, 0, 0))],
        out_specs=pl.BlockSpec((tile, LANES), lambda b, h, i: (b * nq + i, h)),
        out_shape=jax.ShapeDtypeStruct((t, cfg.hb * cfg.db), BF16),
        compiler_params=_params(3),
        name="attn_b",
    )(proj, proj, proj, tb)


def _merge_kernel(ya_ref, yb_ref, ga_ref, gb_ref, x_ref, gta_ref, scm_ref, shm_ref, gmoe_ref,
                  wua_ref, wub_ref, wo_ref, wrh_ref, wrl_ref, h1_ref, v_ref, lg_ref):
    a = _dot(ya_ref[...], wua_ref[...])
    b = _dot(yb_ref[...], wub_ref[...])
    merged = _sigmoid(ga_ref[...].astype(F32)) * a + _sigmoid(gb_ref[...].astype(F32)) * b
    att = _dot(merged.astype(BF16), wo_ref[...])
    h1 = x_ref[...] + gta_ref[0] * att
    h1_ref[...] = h1
    y = h1 * lax.rsqrt(jnp.mean(h1 * h1, axis=-1, keepdims=True) + EPS) * gmoe_ref[...]
    v = y * (1.0 + scm_ref[0]) + shm_ref[0]
    _store_slabs(v_ref, _pack_pairs(v), v_ref.shape[0] // v.shape[0])
    v_hi, v_lo = _split_bf16(v)
    wrh = wrh_ref[...]
    lg_ref[...] = _dot_t(wrh, v_hi) + _dot_t(wrh, v_lo) + _dot_t(wrl_ref[...], v_hi)


def merge(ya, yb, proj, x2, gta, scm, shm, gmoe, wua, wub, wo, wrh, wrl, cfg):
    t, d = x2.shape
    tm = min(cfg.merge_tm, cfg.seq)
    per_b = cfg.seq // tm
    wa = ya.shape[1]
    wb = yb.shape[1]
    gate0 = (proj.shape[1] - 2 * d) // d
    ne = wrh.shape[0]
    ns = d // (2 * LANES)
    const = lambda i: (0, 0)
    row3 = lambda i: (i // per_b, 0, 0)
    return pl.pallas_call(
        _merge_kernel,
        grid=(t // tm,),
        in_specs=[pl.BlockSpec((tm, wa), lambda i: (i, 0)),
                  pl.BlockSpec((tm, wb), lambda i: (i, 0)),
                  pl.BlockSpec((tm, d), lambda i: (i, gate0)),
                  pl.BlockSpec((tm, d), lambda i: (i, gate0 + 1)),
                  pl.BlockSpec((tm, d), lambda i: (i, 0)),
                  pl.BlockSpec((1, 1, d), row3),
                  pl.BlockSpec((1, 1, d), row3),
                  pl.BlockSpec((1, 1, d), row3),
                  pl.BlockSpec((1, d), const),
                  pl.BlockSpec((wa, d), const),
                  pl.BlockSpec((wb, d), const),
                  pl.BlockSpec((d, d), const),
                  pl.BlockSpec((ne, d), const),
                  pl.BlockSpec((ne, d), const)],
        out_specs=[pl.BlockSpec((tm, d), lambda i: (i, 0)),
                   pl.BlockSpec((tm * ns, LANES), lambda i: (i, 0)),
                   pl.BlockSpec((ne, tm), lambda i: (0, i))],
        out_shape=[jax.ShapeDtypeStruct((t, d), F32),
                   jax.ShapeDtypeStruct((t * ns, LANES), U32),
                   jax.ShapeDtypeStruct((ne, t), F32)],
        compiler_params=_params(1),
        name="merge",
    )(ya, yb, proj, proj, x2, gta, scm, shm, gmoe, wua, wub, wo, wrh, wrl)


def _route_kernel(lg_ref, rb_ref, tri_ref, w_ref, csel_ref, code_ref, cend_ref, carry_ref,
                  *, n_exp, n_groups, topk_groups, top_k, scale):
    gsz = n_exp // n_groups
    tc = lg_ref.shape[1]
    ninf = -jnp.inf
    step = pl.program_id(0)

    @pl.when(step == 0)
    def _():
        carry_ref[...] = jnp.zeros(carry_ref.shape, F32)

    row = lax.broadcasted_iota(jnp.int32, (gsz, tc), 0).astype(F32)
    sc, bi = [], []
    for g in range(n_groups):
        s = _sigmoid(lg_ref[g * gsz:(g + 1) * gsz, :])
        sc.append(s)
        bi.append(s + rb_ref[g * gsz:(g + 1) * gsz, :])
    gs = []
    for x in bi:
        m1 = jnp.max(x, axis=0, keepdims=True)
        eq = x == m1
        cnt = jnp.sum(jnp.where(eq, 1.0, 0.0), axis=0, keepdims=True)
        second = jnp.max(jnp.where(eq, ninf, x), axis=0, keepdims=True)
        gs.append(m1 + jnp.where(cnt >= 2.0, m1, second))
    gmat = jnp.concatenate(gs, axis=0)
    grow = lax.broadcasted_iota(jnp.int32, gmat.shape, 0).astype(F32)
    gsel = jnp.zeros(gmat.shape, F32)
    for _ in range(topk_groups):
        m = jnp.max(gmat, axis=0, keepdims=True)
        idx = jnp.min(jnp.where(gmat == m, grow, float(n_groups)), axis=0, keepdims=True)
        hit = grow == idx
        gsel = jnp.where(hit, 1.0, gsel)
        gmat = jnp.where(hit, ninf, gmat)
    cur = [jnp.where(gsel[g:g + 1, :] > 0.5, bi[g], ninf) for g in range(n_groups)]
    ids = [row + float(g * gsz) for g in range(n_groups)]
    w_rows = []
    chosen = [jnp.zeros((gsz, tc), F32) for _ in range(n_groups)]
    kplus = [jnp.zeros((gsz, tc), F32) for _ in range(n_groups)]
    for k in range(top_k):
        mg = cur[0]
        for c in cur[1:]:
            mg = jnp.maximum(mg, c)
        m = jnp.max(mg, axis=0, keepdims=True)
        cand = jnp.where(cur[0] == m, ids[0], float(n_exp))
        for c, i in zip(cur[1:], ids[1:]):
            cand = jnp.minimum(cand, jnp.where(c == m, i, float(n_exp)))
        idx = jnp.min(cand, axis=0, keepdims=True)
        wsum = jnp.zeros((gsz, tc), F32)
        for g in range(n_groups):
            hit = ids[g] == idx
            wsum = wsum + jnp.where(hit, sc[g], 0.0)
            cur[g] = jnp.where(hit, ninf, cur[g])
            chosen[g] = jnp.where(hit, 1.0, chosen[g])
            kplus[g] = jnp.where(hit, float(k + 1), kplus[g])
        w_rows.append(jnp.sum(wsum, axis=0, keepdims=True))
    w = jnp.concatenate(w_rows, axis=0)
    w_ref[...] = w / jnp.sum(w, axis=0, keepdims=True) * scale

    onehot = jnp.concatenate(chosen, axis=0)
    incl = _dot(onehot.astype(BF16), tri_ref[...])
    carry = carry_ref[...]
    cglob = carry[:, 0:1] + incl
    tok = (lax.broadcasted_iota(jnp.int32, (n_exp, tc), 1) + step * tc).astype(F32)
    picked = onehot > 0.5
    csel_ref[0:n_exp, :] = jnp.where(picked, cglob, 0.0)
    code_ref[0:n_exp, :] = jnp.where(picked, tok * float(top_k) + jnp.concatenate(kplus, axis=0) - 1.0, 0.0)
    new_carry = carry + jnp.sum(onehot, axis=1, keepdims=True)
    carry_ref[...] = new_carry
    cend_ref[0] = new_carry


def route(lg, rb, cfg):
    ne, t = lg.shape
    nx = cfg.n_experts
    tc = min(cfg.route_tc, t)
    nsteps = t // tc
    tri = jnp.asarray(np.triu(np.ones((tc, tc), np.float32)), BF16)
    kern = functools.partial(_route_kernel, n_exp=cfg.n_experts, n_groups=cfg.n_groups,
                             topk_groups=cfg.topk_groups, top_k=cfg.top_k, scale=cfg.routed_scale)
    return pl.pallas_call(
        kern,
        grid=(nsteps,),
        in_specs=[pl.BlockSpec((ne, tc), lambda i: (0, i)),
                  pl.BlockSpec((ne, tc), lambda i: (0, 0)),
                  pl.BlockSpec((tc, tc), lambda i: (0, 0))],
        out_specs=[pl.BlockSpec((cfg.top_k, tc), lambda i: (0, i)),
                   pl.BlockSpec((nx, tc), lambda i: (0, i)),
                   pl.BlockSpec((nx, tc), lambda i: (0, i)),
                   pl.BlockSpec((1, nx, LANES), lambda i: (i, 0, 0))],
        out_shape=[jax.ShapeDtypeStruct((cfg.top_k, t), F32),
                   jax.ShapeDtypeStruct((nx, t), F32),
                   jax.ShapeDtypeStruct((nx, t), F32),
                   jax.ShapeDtypeStruct((nsteps, nx, LANES), F32)],
        scratch_shapes=[pltpu.VMEM((cfg.n_experts, LANES), F32)],
        compiler_params=_params(1),
        name="route",
    )(lg, rb, tri)


def _slots_kernel(be_ref, r0_ref, cnt_ref, csel_ref, code_ref, cend_ref, tok_ref, dst_ref,
                  *, bm, tc, top_k, n_tok, n_assign):
    b = pl.program_id(0)
    e = be_ref[b]
    r0 = r0_ref[b]
    cnt = cnt_ref[b]
    sub = lax.rem(e, 8)
    nsteps = cend_ref.shape[0]
    pick3 = lax.broadcasted_iota(jnp.int32, cend_ref.shape, 1) == sub
    cend = jnp.sum(jnp.where(pick3, cend_ref[...], 0.0), axis=1)
    first = (r0 + 1).astype(F32)
    last = jnp.minimum(r0 + bm, cnt).astype(F32)
    jlo = (jnp.sum(jnp.where(cend < first, 1.0, 0.0)) * (1.0 / LANES)).astype(jnp.int32)
    jhi = (jnp.sum(jnp.where(cend < last, 1.0, 0.0)) * (1.0 / LANES)).astype(jnp.int32)
    jhi = jnp.minimum(jhi, nsteps - 1)
    jhi = jnp.where(last < first, jlo - 1, jhi)

    want = (lax.broadcasted_iota(jnp.int32, (bm, tc), 0) + (r0 + 1)).astype(F32)
    pick2 = lax.broadcasted_iota(jnp.int32, (8, tc), 0) == sub
    zrows = jnp.zeros((13, tc), F32)

    def body(j, acc):
        off = pl.multiple_of(j * tc, tc)
        crow = jnp.sum(jnp.where(pick2, csel_ref[:, pl.ds(off, tc)], 0.0), axis=0, keepdims=True)
        drow = jnp.sum(jnp.where(pick2, code_ref[:, pl.ds(off, tc)], 0.0), axis=0, keepdims=True)
        di = drow.astype(jnp.int32)
        digits = jnp.concatenate([(di >> 14).astype(F32), ((di >> 7) & 127).astype(F32),
                                  (di & 127).astype(F32), zrows], axis=0).astype(BF16)
        eq = jnp.where(want == crow, 1.0, 0.0).astype(BF16)
        return acc + _dot_t(digits, eq)

    acc = lax.fori_loop(jlo, jhi + 1, body, jnp.zeros((16, bm), F32))
    code = (acc[0:1, :] * 16384.0 + acc[1:2, :] * 128.0 + acc[2:3, :]).astype(jnp.int32)
    lane = lax.broadcasted_iota(jnp.int32, (1, bm), 1)
    valid = (lane + r0) < cnt
    shift = top_k.bit_length() - 1
    tok = code >> shift
    kk = code & (top_k - 1)
    dummy = n_assign + lax.rem(b, 2) * bm + lane
    tok_ref[0] = jnp.where(valid, tok, 0)
    dst_ref[0] = jnp.where(valid, kk * n_tok + tok, dummy)


def slots(block_e, r0, cntb, csel, code, cend, cfg, n_tok):
    nb = block_e.shape[0]
    bm = cfg.moe_rows
    ne, t = csel.shape
    tc = min(cfg.route_tc, t)
    nsteps = cend.shape[0]
    assert cfg.top_k & (cfg.top_k - 1) == 0 and n_tok * cfg.top_k < (1 << 21)
    kern = functools.partial(_slots_kernel, bm=bm, tc=tc, top_k=cfg.top_k, n_tok=n_tok, n_assign=n_tok * cfg.top_k)
    grid_spec = pltpu.PrefetchScalarGridSpec(
        num_scalar_prefetch=3,
        grid=(nb,),
        in_specs=[pl.BlockSpec((8, t), lambda b, be, r0, cn: (be[b] // 8, 0)),
                  pl.BlockSpec((8, t), lambda b, be, r0, cn: (be[b] // 8, 0)),
                  pl.BlockSpec((nsteps, 8, LANES), lambda b, be, r0, cn: (0, be[b] // 8, 0))],
        out_specs=[pl.BlockSpec((1, 1, bm), lambda b, be, r0, cn: (b, 0, 0)),
                   pl.BlockSpec((1, 1, bm), lambda b, be, r0, cn: (b, 0, 0))],
    )
    return pl.pallas_call(
        kern,
        grid_spec=grid_spec,
        out_shape=[jax.ShapeDtypeStruct((nb, 1, bm), jnp.int32), jax.ShapeDtypeStruct((nb, 1, bm), jnp.int32)],
        compiler_params=_params(1),
        name="slots",
    )(block_e, r0, cntb, csel, code, cend)


MOE_RING = 3


def _moe_kernel(be_ref, tok0_ref, tok1_ref, tokn_ref, dstp_ref, dstc_ref, v_hbm, wg_ref, wu_ref, wd_ref, yk_hbm,
                xbuf, ybuf, wgb, wub, wdb, gsem, ssem, *, bm, n_assign):
    b = pl.program_id(0)
    nb = pl.num_programs(0)
    slot = lax.rem(b, MOE_RING)
    ns = v_hbm.shape[1]

    def gather_copy(t, r, s):
        return pltpu.make_async_copy(v_hbm.at[t], xbuf.at[s, pl.ds(r * ns, ns), :], gsem.at[s])

    def scatter_copy(r, d, s):
        return pltpu.make_async_copy(ybuf.at[s, pl.ds(r * ns, ns), :], yk_hbm.at[d], ssem.at[s])

    def wait_gather(s):
        for r in range(bm):
            gather_copy(0, r, s).wait()

    def wait_scatter(s):
        for r in range(bm):
            scatter_copy(r, 0, s).wait()

    @pl.when(b == 0)
    def _():
        ybuf[...] = jnp.zeros(ybuf.shape, U32)
        for s in range(2):
            for r in range(bm):
                scatter_copy(r, n_assign + s * bm + r, s).start(priority=r % 2)
        for r in range(bm):
            gather_copy(tok0_ref[0, 0, r], r, 0).start(priority=r % 2)
        for r in range(bm):
            gather_copy(tok1_ref[0, 0, r], r, 1).start(priority=r % 2)

    e = be_ref[b]
    eprev = be_ref[jnp.maximum(b - 1, 0)]

    @pl.when(jnp.logical_or(b == 0, e != eprev))
    def _():
        wgb[...] = wg_ref[0].astype(BF16)
        wub[...] = wu_ref[0].astype(BF16)
        wdb[...] = wd_ref[0].astype(BF16)

    def block_step(s):
        p = (s + 2) % MOE_RING
        wait_gather(s)
        wait_scatter(s)
        for r in range(bm):
            scatter_copy(r, dstp_ref[0, 0, r], p).start(priority=r % 2)
        for r in range(bm):
            gather_copy(tokn_ref[0, 0, r], r, p).start(priority=r % 2)
        lo, hi = _unpack_pairs(_load_slabs(xbuf.at[s], bm, ns))
        x = jnp.concatenate([lo, hi], axis=1).astype(BF16)
        g = _dot(x, wgb[...])
        u = _dot(x, wub[...])
        h = (g * _sigmoid(g)) * u
        _store_slabs(ybuf.at[s], _pack_pairs(_dot(h.astype(BF16), wdb[...])), ns)

    for s in range(MOE_RING):
        pl.when(slot == s)(functools.partial(block_step, s))

    @pl.when(b == nb - 1)
    def _():
        def drain(s):
            p, q = (s + 2) % MOE_RING, (s + 1) % MOE_RING
            wait_scatter(q)
            for r in range(bm):
                scatter_copy(r, dstc_ref[0, 0, r], s).start(priority=r % 2)
            wait_gather(p)
            wait_gather(q)
            wait_scatter(p)
            wait_scatter(s)

        for s in range(MOE_RING):
            pl.when(slot == s)(functools.partial(drain, s))


def moe(vp, block_e, slot_tok, slot_dst, wg, wu, wd, cfg):
    d = wg.shape[1]
    ns = d // (2 * LANES)
    t = vp.shape[0] // ns
    bm = cfg.moe_rows
    v3 = vp.reshape(t, ns, LANES)
    nb = block_e.shape[0]
    assert nb >= 2
    de = wg.shape[2]
    n_assign = t * cfg.top_k
    first = (n_assign + 2 * bm + jnp.arange(bm, dtype=jnp.int32)).reshape(1, 1, bm)
    dst_ext = jnp.concatenate([first, slot_dst], axis=0)
    kern = functools.partial(_moe_kernel, bm=bm, n_assign=n_assign)
    smem_blk = lambda f: pl.BlockSpec((1, 1, bm), f, memory_space=pltpu.SMEM)
    grid_spec = pltpu.PrefetchScalarGridSpec(
        num_scalar_prefetch=1,
        grid=(nb,),
        in_specs=[smem_blk(lambda b, be: (0, 0, 0)),
                  smem_blk(lambda b, be: (1, 0, 0)),
                  smem_blk(lambda b, be: (jnp.minimum(b + 2, nb - 1), 0, 0)),
                  smem_blk(lambda b, be: (b, 0, 0)),
                  smem_blk(lambda b, be: (b + 1, 0, 0)),
                  pl.BlockSpec(memory_space=pl.ANY),
                  pl.BlockSpec((1, d, de), lambda b, be: (be[b], 0, 0)),
                  pl.BlockSpec((1, d, de), lambda b, be: (be[b], 0, 0)),
                  pl.BlockSpec((1, de, d), lambda b, be: (be[b], 0, 0))],
        out_specs=pl.BlockSpec(memory_space=pl.ANY),
        scratch_shapes=[pltpu.VMEM((MOE_RING, bm * ns, LANES), U32),
                        pltpu.VMEM((MOE_RING, bm * ns, LANES), U32),
                        pltpu.VMEM((d, de), BF16), pltpu.VMEM((d, de), BF16), pltpu.VMEM((de, d), BF16),
                        pltpu.SemaphoreType.DMA((MOE_RING,)), pltpu.SemaphoreType.DMA((MOE_RING,))],
    )
    return pl.pallas_call(
        kern,
        grid_spec=grid_spec,
        out_shape=jax.ShapeDtypeStruct((n_assign + 3 * bm, ns, LANES), U32),
        compiler_params=_params(1),
        name="moe",
    )(block_e, slot_tok, slot_tok, slot_tok, dst_ext, dst_ext, v3, wg, wu, wd)


def _final_kernel(w_ref, v_ref, h1_ref, gtm_ref, gfin_ref, sg_ref, su_ref, sd_ref, *rest, top_k):
    yk_refs, o_ref, lo_ref, hi_ref = rest[:top_k], rest[top_k], rest[top_k + 1], rest[top_k + 2]
    tm = h1_ref.shape[0]
    ns = yk_refs[0].shape[1]

    def combine(t, carry):
        lo, hi = _unpack_pairs(yk_refs[0][t])
        acc_lo, acc_hi = w_ref[t, 0] * lo, w_ref[t, 0] * hi
        for k in range(1, top_k):
            lo, hi = _unpack_pairs(yk_refs[k][t])
            acc_lo, acc_hi = acc_lo + w_ref[t, k] * lo, acc_hi + w_ref[t, k] * hi
        rows = pl.ds(pl.multiple_of(t * ns, ns), ns)
        lo_ref[rows, :] = acc_lo
        hi_ref[rows, :] = acc_hi
        return carry

    lax.fori_loop(0, tm, combine, 0, unroll=8)
    routed = jnp.concatenate([_load_slabs(lo_ref, tm, ns), _load_slabs(hi_ref, tm, ns)], axis=1)

    vb = jnp.concatenate(_unpack_pairs(_load_slabs(v_ref, tm, ns)), axis=1).astype(BF16)
    g = _dot(vb, sg_ref[...])
    u = _dot(vb, su_ref[...])
    y = routed + _dot(((g * _sigmoid(g)) * u).astype(BF16), sd_ref[...])
    h2 = h1_ref[...] + gtm_ref[0] * y
    o_ref[...] = h2 * lax.rsqrt(jnp.mean(h2 * h2, axis=-1, keepdims=True) + EPS) * gfin_ref[...]


def final(vp, h1, yk3, w_tk, gtm, gfin, sg, su, sd, cfg):
    t, d = h1.shape
    tm = min(cfg.final_tm, cfg.seq)
    per_b = cfg.seq // tm
    nt = t // tm
    ds = sg.shape[1]
    nslab = d // (2 * LANES)
    const = lambda i: (0, 0)
    kern = functools.partial(_final_kernel, top_k=cfg.top_k)
    plane = lambda k: pl.BlockSpec((tm, nslab, LANES), lambda i: (k * nt + i, 0, 0))
    return pl.pallas_call(
        kern,
        grid=(nt,),
        in_specs=[pl.BlockSpec((tm, cfg.top_k), lambda i: (i, 0), memory_space=pltpu.SMEM),
                  pl.BlockSpec((tm * nslab, LANES), lambda i: (i, 0)),
                  pl.BlockSpec((tm, d), lambda i: (i, 0)),
                  pl.BlockSpec((1, 1, d), lambda i: (i // per_b, 0, 0)),
                  pl.BlockSpec((1, d), const),
                  pl.BlockSpec((d, ds), const),
                  pl.BlockSpec((d, ds), const),
                  pl.BlockSpec((ds, d), const)] + [plane(k) for k in range(cfg.top_k)],
        out_specs=pl.BlockSpec((tm, d), lambda i: (i, 0)),
        out_shape=jax.ShapeDtypeStruct((t, d), F32),
        scratch_shapes=[pltpu.VMEM((tm * nslab, LANES), F32), pltpu.VMEM((tm * nslab, LANES), F32)],
        compiler_params=_params(1),
        name="final",
    )(w_tk, vp, h1, gtm, gfin, sg, su, sd, *([yk3] * cfg.top_k))


def _t5_bucket(rel, cfg):
    nb = cfg.t5_buckets // 2
    ret = (rel > 0).astype(np.int32) * nb
    n = np.abs(rel)
    max_exact = nb // 2
    large = max_exact + (np.log(np.maximum(n, 1) / max_exact) / math.log(cfg.t5_max_dist / max_exact)
                         * (nb - max_exact)).astype(np.int32)
    large = np.minimum(large, nb - 1)
    return (ret + np.where(n < max_exact, n, large)).astype(np.int32)


def _toeplitz(vec, n):
    lead = vec.shape[:-1]
    flat = jnp.tile(vec, (1,) * len(lead) + (n,))[..., :n * (2 * n - 1)]
    return flat.reshape(lead + (n, 2 * n - 1))[..., :n]


def _rel_offsets(n):
    m = np.arange(2 * n)
    return np.where(m < n, m, m - 2 * n)


def _bias_tables_a(t5_bias, cfg):
    tile = cfg.attn_tile
    assert tile + 1 >= cfg.t5_max_dist and tile % cfg.chunk == 0
    ql = np.arange(tile)[:, None]
    kl = np.arange(tile)[None, :]
    allowed = (kl // cfg.chunk) <= (ql // cfg.chunk)
    r = _rel_offsets(tile)
    t5 = t5_bias.astype(F32)
    diag = _toeplitz(t5[_t5_bucket(r, cfg)].T, tile)
    prev = _toeplitz(t5[_t5_bucket(r - tile, cfg)].T, tile)
    far = jnp.broadcast_to(t5[_t5_bucket(np.array(-2 * tile), cfg)][:, None, None], diag.shape)
    return jnp.stack([jnp.where(allowed[None], diag, NEG), prev, far], axis=1)


def _bias_tables_b(rel_table, cfg):
    tile = cfg.attn_tile
    assert (cfg.left_chunks * cfg.chunk) % tile == 0 and tile % cfg.chunk == 0
    nwin = (cfg.left_chunks * cfg.chunk) // tile + 1
    ql = np.arange(tile)[:, None]
    kl = np.arange(tile)[None, :]
    r = _rel_offsets(tile)
    rt = rel_table.astype(F32)
    tabs = []
    for j in range(nwin):
        off = (j - (nwin - 1)) * tile
        idx = np.clip(r + off, -cfg.rel_clip, cfg.rel_clip) + cfg.rel_clip
        dchunk = ql // cfg.chunk - (kl + off) // cfg.chunk
        band = (dchunk >= 0) & (dchunk <= cfg.left_chunks)
        tabs.append(jnp.where(band[None], _toeplitz(rt[idx].T, tile), NEG))
    return jnp.stack(tabs, axis=1)


def _block_tables(counts, n_assign, cfg):
    e_ = cfg.n_experts
    bm = cfg.moe_rows
    nb = -(-(n_assign + e_ * (bm - 1)) // bm)
    pcounts = (counts + bm - 1) // bm * bm
    pends = jnp.cumsum(pcounts)
    pstarts = pends - pcounts
    start_b = jnp.arange(nb, dtype=jnp.int32) * bm
    block_e = jnp.minimum(jnp.sum((pends[None, :] <= start_b[:, None]).astype(jnp.int32), axis=1), e_ - 1)
    mine = jnp.arange(e_, dtype=jnp.int32)[None, :] == block_e[:, None]
    pstart_b = jnp.sum(jnp.where(mine, pstarts[None, :], 0), axis=1)
    count_b = jnp.sum(jnp.where(mine, counts[None, :], 0), axis=1)
    return block_e.astype(jnp.int32), (start_b - pstart_b).astype(jnp.int32), count_b.astype(jnp.int32)


def _forward(cfg, x, c, w_ada, b_ada, g_attn, w_in, lambda_qk, t5_bias, rel_bias_b, w_up_a, w_up_b, w_o,
             g_moe, w_router, router_bias, w_exp_gate, w_exp_up, w_exp_down,
             w_sh_gate, w_sh_up, w_sh_down, g_final):
    b_, s_, d_ = x.shape
    t_ = b_ * s_
    l = 0
    qk_a = cfg.ha * 2 * cfg.da_qk
    width_a = cfg.ha * 2 * cfg.da_qk
    width_b = cfg.hb * cfg.db
    in_cols = w_in.shape[2]

    rows = max(16, b_)
    c_pad = jnp.zeros((rows, d_), F32).at[:b_].set(c)
    mod = ada_mod(c_pad, w_ada[l], b_ada[l], cfg)[:b_]
    sh_a, sc_a, gt_a, sh_m, sc_m, gt_m = [m.reshape(b_, 1, d_) for m in jnp.split(mod, cfg.n_mod, axis=-1)]

    colscale = np.ones((1, in_cols), np.float32)
    colscale[0, :qk_a] = cfg.da_qk ** -0.5
    qb0 = 2 * qk_a + width_a
    colscale[0, qb0:qb0 + width_b] = cfg.db ** -0.5
    x2 = x.reshape(t_, d_)
    proj = in_proj(x2, g_attn[l].reshape(1, d_), sc_a, sh_a, w_in[l].astype(BF16), jnp.asarray(colscale), cfg)

    lq = lambda_qk[l].astype(F32)
    lam = jnp.exp(jnp.sum(lq[0] * lq[1])) - jnp.exp(jnp.sum(lq[2] * lq[3])) + cfg.lam_init
    ya = attn_a(proj, lam.reshape(1), _bias_tables_a(t5_bias, cfg), cfg)
    yb = attn_b(proj, _bias_tables_b(rel_bias_b[l], cfg), cfg)

    ne_pad = max(LANES, cfg.n_experts)
    wrt = jnp.zeros((ne_pad, d_), F32).at[:cfg.n_experts].set(w_router[l].T)
    wrh, wrl = _split_bf16(wrt)
    h1, v, lg = merge(ya, yb, proj, x2, gt_a, sc_m, sh_m, g_moe[l].reshape(1, d_),
                      w_up_a[l].astype(BF16), w_up_b[l].astype(BF16), w_o[l].astype(BF16), wrh, wrl, cfg)

    tc = min(cfg.route_tc, t_)
    rb = jnp.zeros((ne_pad, tc), F32).at[:cfg.n_experts].set(
        jnp.broadcast_to(router_bias[l].astype(F32)[:, None], (cfg.n_experts, tc)))
    w_kt, csel, code, cend = route(lg, rb, cfg)
    counts = cend[-1, :, 0].astype(jnp.int32)
    block_e, r0, count_b = _block_tables(counts, t_ * cfg.top_k, cfg)
    slot_tok, slot_dst = slots(block_e, r0, count_b, csel, code, cend, cfg, t_)

    yk = moe(v, block_e, slot_tok, slot_dst, w_exp_gate[l], w_exp_up[l], w_exp_down[l], cfg)
    out = final(v, h1, yk, w_kt.T, gt_m, g_final.reshape(1, d_),
                w_sh_gate[l].astype(BF16), w_sh_up[l].astype(BF16), w_sh_down[l].astype(BF16), cfg)
    return out.reshape(b_, s_, d_)


def kernel(x, c, w_ada, b_ada, g_attn, w_in, lambda_qk, t5_bias, rel_bias_b, w_up_a, w_up_b, w_o, g_moe,
           w_router, router_bias, w_exp_gate, w_exp_up, w_exp_down, w_sh_gate, w_sh_up, w_sh_down, g_final):
    return _forward(Cfg(), x, c, w_ada, b_ada, g_attn, w_in, lambda_qk, t5_bias, rel_bias_b, w_up_a, w_up_b,
                    w_o, g_moe, w_router, router_bias, w_exp_gate, w_exp_up, w_exp_down,
                    w_sh_gate, w_sh_up, w_sh_down, g_final)
```

```python
import functools
import math
from typing import NamedTuple

import numpy as np
import jax
import jax.numpy as jnp
from jax import lax
from jax.experimental import pallas as pl
from jax.experimental.pallas import tpu as pltpu

F32 = jnp.float32
BF16 = jnp.bfloat16
NEG = -1e30
EPS = 1e-6
HEAD_EPS = 1e-5
LOG2E = 1.4426950408889634
LANES = 128
VMEM_LIMIT = 56 * 1024 * 1024


class Cfg(NamedTuple):
    batch: int = 8
    seq: int = 2048
    d_model: int = 2048
    chunk: int = 64
    ha: int = 8
    da_qk: int = 64
    hb: int = 8
    db: int = 128
    left_chunks: int = 8
    rel_clip: int = 256
    t5_buckets: int = 32
    t5_max_dist: int = 128
    n_experts: int = 64
    top_k: int = 8
    n_groups: int = 8
    topk_groups: int = 4
    d_expert: int = 512
    d_shared: int = 512
    routed_scale: float = 2.5
    n_mod: int = 6
    lam_init: float = 0.8 - 0.6 * math.exp(-0.3 * 0)
    attn_tile: int = 256
    moe_rows: int = 256
    proj_tm: int = 1024
    proj_tn: int = 1024
    merge_tm: int = 256
    route_tc: int = 512
    final_tm: int = 128
    ada_tn: int = 1024


def _sigmoid(x):
    return 1.0 / (1.0 + jnp.exp(-x))


def _split_bf16(x):
    hi = x.astype(BF16)
    lo = (x - hi.astype(F32)).astype(BF16)
    return hi, lo


def _dot(a, b):
    return jnp.dot(a, b, preferred_element_type=F32)


def _dot_t(a, b):
    return lax.dot_general(a, b, (((1,), (1,)), ((), ())), preferred_element_type=F32)


U32 = jnp.uint32
HI_MASK = 0xFFFF0000


def _pack_pairs(x):
    n = x.shape[1] // 2
    bits = pltpu.bitcast(x.astype(BF16).astype(F32), U32)
    return (bits[:, :n] >> 16) | (bits[:, n:] & U32(HI_MASK))


def _unpack_pairs(p):
    return pltpu.bitcast(p << 16, F32), pltpu.bitcast(p & U32(HI_MASK), F32)


def _load_slabs(ref, m, ns):
    return jnp.concatenate([ref[pl.ds(j, m, stride=ns), :] for j in range(ns)], axis=1)


def _store_slabs(ref, val, ns):
    m = val.shape[0]
    for j in range(ns):
        ref[pl.ds(j, m, stride=ns), :] = val[:, j * LANES:(j + 1) * LANES]


def _params(n_axes):
    return pltpu.CompilerParams(dimension_semantics=("arbitrary",) * n_axes,
                                vmem_limit_bytes=VMEM_LIMIT)


def _ada_kernel(c_ref, w_ref, b_ref, o_ref):
    c = c_ref[...]
    s_hi, s_lo = _split_bf16(c * _sigmoid(c))
    w_hi, w_lo = _split_bf16(w_ref[...])
    acc = _dot(s_hi, w_hi) + _dot(s_hi, w_lo) + _dot(s_lo, w_hi)
    o_ref[...] = acc + b_ref[...]


def ada_mod(c_pad, w_ada, b_ada, cfg):
    rows, d = c_pad.shape
    n = w_ada.shape[1]
    tn = min(cfg.ada_tn, n)
    return pl.pallas_call(
        _ada_kernel,
        grid=(n // tn,),
        in_specs=[pl.BlockSpec((rows, d), lambda j: (0, 0)),
                  pl.BlockSpec((d, tn), lambda j: (0, j)),
                  pl.BlockSpec((1, tn), lambda j: (0, j))],
        out_specs=pl.BlockSpec((rows, tn), lambda j: (0, j)),
        out_shape=jax.ShapeDtypeStruct((rows, n), F32),
        compiler_params=_params(1),
        name="ada_mod",
    )(c_pad, w_ada, b_ada.reshape(1, n))


def _inproj_kernel(x_ref, g_ref, sc_ref, sh_ref, w_ref, cs_ref, o_ref, u_ref):
    @pl.when(pl.program_id(1) == 0)
    def _():
        x = x_ref[...]
        y = x * lax.rsqrt(jnp.mean(x * x, axis=-1, keepdims=True) + EPS) * g_ref[...]
        u_ref[...] = (y * (1.0 + sc_ref[0]) + sh_ref[0]).astype(BF16)

    acc = _dot(u_ref[...], w_ref[...])
    o_ref[...] = (acc * cs_ref[...]).astype(BF16)


def in_proj(x2, g, sc, sh, w_bf, colscale, cfg):
    t, d = x2.shape
    n = w_bf.shape[1]
    tm = min(cfg.proj_tm, cfg.seq)
    tn = min(cfg.proj_tn, n)
    per_b = cfg.seq // tm
    return pl.pallas_call(
        _inproj_kernel,
        grid=(t // tm, n // tn),
        in_specs=[pl.BlockSpec((tm, d), lambda i, j: (i, 0)),
                  pl.BlockSpec((1, d), lambda i, j: (0, 0)),
                  pl.BlockSpec((1, 1, d), lambda i, j: (i // per_b, 0, 0)),
                  pl.BlockSpec((1, 1, d), lambda i, j: (i // per_b, 0, 0)),
                  pl.BlockSpec((d, tn), lambda i, j: (0, j)),
                  pl.BlockSpec((1, tn), lambda i, j: (0, j))],
        out_specs=pl.BlockSpec((tm, tn), lambda i, j: (i, j)),
        out_shape=jax.ShapeDtypeStruct((t, n), BF16),
        scratch_shapes=[pltpu.VMEM((tm, d), BF16)],
        compiler_params=_params(2),
        name="in_proj",
    )(x2, g, sc, sh, w_bf, colscale)


def _attn_a_kernel(lam_ref, qlo_ref, qhi_ref, k_ref, v_ref, tb_ref, olo_ref, ohi_ref,
                   s_ref, m_ref, l_ref, o_ref, *, tile, half, nq, out_scale):
    p = pl.program_id(2)
    hq = nq - 1 - p
    nvis = nq + 1
    ninf = -jnp.inf

    def stacked(q):
        lane = lax.broadcasted_iota(jnp.int32, q.shape, 1)
        zero = jnp.zeros_like(q)
        return jnp.concatenate([jnp.where(lane < half, q, zero), jnp.where(lane >= half, q, zero)], axis=0)

    qq_lo, qq_hi = stacked(qlo_ref[...]), stacked(qhi_ref[...])

    def fold(x, op):
        r = x[:, :LANES]
        for c in range(1, tile // LANES):
            r = op(r, x[:, c * LANES:(c + 1) * LANES])
        return r

    def pick(i, lo, hi):
        return jnp.where(i <= p, lo, hi)

    def key_tile(i):
        return jnp.where(i <= p, i, i - p - 1)

    def key_rows(i):
        return pl.ds(pl.multiple_of(key_tile(i) * tile, tile), tile)

    for i in range(nvis):
        bias = tb_ref[0, jnp.minimum(pick(i, p, hq) - key_tile(i), 2)]
        s = _dot_t(pick(i, qq_lo, qq_hi), k_ref[key_rows(i), :]) + jnp.concatenate([bias, bias], axis=0)
        s_ref[i] = s
        m_ref[i] = fold(s, jnp.maximum)

    def reduce_tiles(ref, op, init):
        lo = jnp.full(ref.shape[1:], init, F32)
        hi = lo
        for i in range(nvis):
            lo = op(lo, jnp.where(i <= p, ref[i], init))
            hi = op(hi, jnp.where(i <= p, init, ref[i]))
        return lo, hi

    m_lo, m_hi = reduce_tiles(m_ref, jnp.maximum, ninf)
    m_lo = jnp.max(m_lo, axis=-1, keepdims=True)
    m_hi = jnp.max(m_hi, axis=-1, keepdims=True)
    for i in range(nvis):
        e = jnp.exp2(s_ref[i] - pick(i, m_lo, m_hi))
        s_ref[i] = e
        l_ref[i] = fold(e, jnp.add)

    l_lo, l_hi = reduce_tiles(l_ref, jnp.add, 0.0)
    lam = lam_ref[0]

    def coefs(l):
        inv = 1.0 / jnp.sum(l, axis=-1, keepdims=True)
        return inv[:tile], lam * inv[tile:]

    (c0_lo, c1_lo), (c0_hi, c1_hi) = coefs(l_lo), coefs(l_hi)
    for i in range(nvis):
        e = s_ref[i]
        a = e[:tile] * pick(i, c0_lo, c0_hi) - e[tile:] * pick(i, c1_lo, c1_hi)
        o_ref[i] = _dot(a.astype(BF16), v_ref[key_rows(i), :])

    o_lo, o_hi = reduce_tiles(o_ref, jnp.add, 0.0)
    for o, out in ((o_lo, olo_ref), (o_hi, ohi_ref)):
        out[...] = (o * lax.rsqrt(jnp.mean(o * o, axis=-1, keepdims=True) + HEAD_EPS) * out_scale).astype(BF16)


def attn_a(proj, lam, tb, cfg):
    tile = cfg.attn_tile
    s = cfg.seq
    nq = s // tile
    assert nq % 2 == 0
    npair = nq // 2
    dv = 2 * cfg.da_qk
    assert dv == LANES
    kcol = cfg.ha
    vcol = 2 * cfg.ha
    kern = functools.partial(_attn_a_kernel, tile=tile, half=cfg.da_qk, nq=nq, out_scale=1.0 - cfg.lam_init)
    half_shape = jax.ShapeDtypeStruct((cfg.batch * npair * tile, cfg.ha * dv), BF16)
    y_lo, y_hi = pl.pallas_call(
        kern,
        grid=(cfg.batch, cfg.ha, npair),
        in_specs=[pl.BlockSpec(memory_space=pltpu.SMEM),
                  pl.BlockSpec((tile, LANES), lambda b, h, p: (b * nq + p, h)),
                  pl.BlockSpec((tile, LANES), lambda b, h, p: (b * nq + nq - 1 - p, h)),
                  pl.BlockSpec((s, LANES), lambda b, h, p: (b, kcol + h)),
                  pl.BlockSpec((s, LANES), lambda b, h, p: (b, vcol + h)),
                  pl.BlockSpec((1, 3, tile, tile), lambda b, h, p: (h, 0, 0, 0))],
        out_specs=[pl.BlockSpec((tile, LANES), lambda b, h, p: (b * npair + p, h)),
                   pl.BlockSpec((tile, LANES), lambda b, h, p: (b * npair + npair - 1 - p, h))],
        out_shape=[half_shape, half_shape],
        scratch_shapes=[pltpu.VMEM((nq + 1, 2 * tile, tile), F32),
                        pltpu.VMEM((nq + 1, 2 * tile, LANES), F32),
                        pltpu.VMEM((nq + 1, 2 * tile, LANES), F32),
                        pltpu.VMEM((nq + 1, tile, LANES), F32)],
        compiler_params=_params(3),
        name="attn_a",
    )(lam, proj, proj, proj, proj, tb)
    w = cfg.ha * dv
    rows = npair * tile
    return jnp.concatenate([y_lo.reshape(cfg.batch, rows, w), y_hi.reshape(cfg.batch, rows, w)],
                           axis=1).reshape(cfg.batch * s, w)


def _attn_b_kernel(q_ref, k_ref, v_ref, tb_ref, o_ref, *, tile, nwin):
    qi = pl.program_id(2)
    q = q_ref[...]
    ss, vs = [], []
    for j in range(nwin):
        kb = qi - (nwin - 1) + j
        start = pl.multiple_of(jnp.maximum(kb, 0) * tile, tile)
        s = _dot_t(q, k_ref[pl.ds(start, tile), :]) + tb_ref[0, j]
        ss.append(jnp.where(kb >= 0, s, NEG))
        vs.append(v_ref[pl.ds(start, tile), :])
    m = jnp.max(ss[0], axis=-1, keepdims=True)
    for s in ss[1:]:
        m = jnp.maximum(m, jnp.max(s, axis=-1, keepdims=True))
    ps = [jnp.exp(s - m) for s in ss]
    l = jnp.sum(ps[0], axis=-1, keepdims=True)
    for p in ps[1:]:
        l = l + jnp.sum(p, axis=-1, keepdims=True)
    inv = 1.0 / l
    acc = _dot((ps[0] * inv).astype(BF16), vs[0])
    for p, v in zip(ps[1:], vs[1:]):
        acc = acc + _dot((p * inv).astype(BF16), v)
    o_ref[...] = acc.astype(BF16)


def attn_b(proj, tb, cfg):
    t = proj.shape[0]
    tile = cfg.attn_tile
    s = cfg.seq
    nq = s // tile
    assert cfg.db == LANES
    nwin = (cfg.left_chunks * cfg.chunk) // tile + 1
    base = (2 * cfg.ha * 2 * cfg.da_qk + cfg.ha * 2 * cfg.da_qk) // LANES
    kern = functools.partial(_attn_b_kernel, tile=tile, nwin=nwin)
    return pl.pallas_call(
        kern,
        grid=(cfg.batch, cfg.hb, nq),
        in_specs=[pl.BlockSpec((tile, LANES), lambda b, h, i: (b * nq + i, base + h)),
                  pl.BlockSpec((s, LANES), lambda b, h, i: (b, base + cfg.hb + h)),
                  pl.BlockSpec((s, LANES), lambda b, h, i: (b, base + 2 * cfg.hb + h)),
                  pl.BlockSpec((1, nwin, tile, tile), lambda b, h, i: (h, 0, 0, 0))],
        out_specs=pl.BlockSpec((tile, LANES), lambda b, h, i: (b * nq + i, h)),
        out_shape=jax.ShapeDtypeStruct((t, cfg.hb * cfg.db), BF16),
        compiler_params=_params(3),
        name="attn_b",
    )(proj, proj, proj, tb)


def _merge_kernel(ya_ref, yb_ref, ga_ref, gb_ref, x_ref, gta_ref, scm_ref, shm_ref, gmoe_ref,
                  wua_ref, wub_ref, wo_ref, wrh_ref, wrl_ref, h1_ref, v_ref, lg_ref):
    a = _dot(ya_ref[...], wua_ref[...])
    b = _dot(yb_ref[...], wub_ref[...])
    merged = _sigmoid(ga_ref[...].astype(F32)) * a + _sigmoid(gb_ref[...].astype(F32)) * b
    att = _dot(merged.astype(BF16), wo_ref[...])
    h1 = x_ref[...] + gta_ref[0] * att
    h1_ref[...] = h1
    y = h1 * lax.rsqrt(jnp.mean(h1 * h1, axis=-1, keepdims=True) + EPS) * gmoe_ref[...]
    v = y * (1.0 + scm_ref[0]) + shm_ref[0]
    _store_slabs(v_ref, _pack_pairs(v), v_ref.shape[0] // v.shape[0])
    v_hi, v_lo = _split_bf16(v)
    wrh = wrh_ref[...]
    lg_ref[...] = _dot_t(wrh, v_hi) + _dot_t(wrh, v_lo) + _dot_t(wrl_ref[...], v_hi)


def merge(ya, yb, proj, x2, gta, scm, shm, gmoe, wua, wub, wo, wrh, wrl, cfg):
    t, d = x2.shape
    tm = min(cfg.merge_tm, cfg.seq)
    per_b = cfg.seq // tm
    wa = ya.shape[1]
    wb = yb.shape[1]
    gate0 = (proj.shape[1] - 2 * d) // d
    ne = wrh.shape[0]
    ns = d // (2 * LANES)
    const = lambda i: (0, 0)
    row3 = lambda i: (i // per_b, 0, 0)
    return pl.pallas_call(
        _merge_kernel,
        grid=(t // tm,),
        in_specs=[pl.BlockSpec((tm, wa), lambda i: (i, 0)),
                  pl.BlockSpec((tm, wb), lambda i: (i, 0)),
                  pl.BlockSpec((tm, d), lambda i: (i, gate0)),
                  pl.BlockSpec((tm, d), lambda i: (i, gate0 + 1)),
                  pl.BlockSpec((tm, d), lambda i: (i, 0)),
                  pl.BlockSpec((1, 1, d), row3),
                  pl.BlockSpec((1, 1, d), row3),
                  pl.BlockSpec((1, 1, d), row3),
                  pl.BlockSpec((1, d), const),
                  pl.BlockSpec((wa, d), const),
                  pl.BlockSpec((wb, d), const),
                  pl.BlockSpec((d, d), const),
                  pl.BlockSpec((ne, d), const),
                  pl.BlockSpec((ne, d), const)],
        out_specs=[pl.BlockSpec((tm, d), lambda i: (i, 0)),
                   pl.BlockSpec((tm * ns, LANES), lambda i: (i, 0)),
                   pl.BlockSpec((ne, tm), lambda i: (0, i))],
        out_shape=[jax.ShapeDtypeStruct((t, d), F32),
                   jax.ShapeDtypeStruct((t * ns, LANES), U32),
                   jax.ShapeDtypeStruct((ne, t), F32)],
        compiler_params=_params(1),
        name="merge",
    )(ya, yb, proj, proj, x2, gta, scm, shm, gmoe, wua, wub, wo, wrh, wrl)


def _route_kernel(lg_ref, rb_ref, tri_ref, w_ref, csel_ref, code_ref, cend_ref, carry_ref,
                  *, n_exp, n_groups, topk_groups, top_k, scale):
    gsz = n_exp // n_groups
    tc = lg_ref.shape[1]
    ninf = -jnp.inf
    step = pl.program_id(0)

    @pl.when(step == 0)
    def _():
        carry_ref[...] = jnp.zeros(carry_ref.shape, F32)

    row = lax.broadcasted_iota(jnp.int32, (gsz, tc), 0).astype(F32)
    sc, bi = [], []
    for g in range(n_groups):
        s = _sigmoid(lg_ref[g * gsz:(g + 1) * gsz, :])
        sc.append(s)
        bi.append(s + rb_ref[g * gsz:(g + 1) * gsz, :])
    gs = []
    for x in bi:
        m1 = jnp.max(x, axis=0, keepdims=True)
        eq = x == m1
        cnt = jnp.sum(jnp.where(eq, 1.0, 0.0), axis=0, keepdims=True)
        second = jnp.max(jnp.where(eq, ninf, x), axis=0, keepdims=True)
        gs.append(m1 + jnp.where(cnt >= 2.0, m1, second))
    gmat = jnp.concatenate(gs, axis=0)
    grow = lax.broadcasted_iota(jnp.int32, gmat.shape, 0).astype(F32)
    gsel = jnp.zeros(gmat.shape, F32)
    for _ in range(topk_groups):
        m = jnp.max(gmat, axis=0, keepdims=True)
        idx = jnp.min(jnp.where(gmat == m, grow, float(n_groups)), axis=0, keepdims=True)
        hit = grow == idx
        gsel = jnp.where(hit, 1.0, gsel)
        gmat = jnp.where(hit, ninf, gmat)
    cur = [jnp.where(gsel[g:g + 1, :] > 0.5, bi[g], ninf) for g in range(n_groups)]
    ids = [row + float(g * gsz) for g in range(n_groups)]
    w_rows = []
    chosen = [jnp.zeros((gsz, tc), F32) for _ in range(n_groups)]
    kplus = [jnp.zeros((gsz, tc), F32) for _ in range(n_groups)]
    for k in range(top_k):
        mg = cur[0]
        for c in cur[1:]:
            mg = jnp.maximum(mg, c)
        m = jnp.max(mg, axis=0, keepdims=True)
        cand = jnp.where(cur[0] == m, ids[0], float(n_exp))
        for c, i in zip(cur[1:], ids[1:]):
            cand = jnp.minimum(cand, jnp.where(c == m, i, float(n_exp)))
        idx = jnp.min(cand, axis=0, keepdims=True)
        wsum = jnp.zeros((gsz, tc), F32)
        for g in range(n_groups):
            hit = ids[g] == idx
            wsum = wsum + jnp.where(hit, sc[g], 0.0)
            cur[g] = jnp.where(hit, ninf, cur[g])
            chosen[g] = jnp.where(hit, 1.0, chosen[g])
            kplus[g] = jnp.where(hit, float(k + 1), kplus[g])
        w_rows.append(jnp.sum(wsum, axis=0, keepdims=True))
    w = jnp.concatenate(w_rows, axis=0)
    w_ref[...] = w / jnp.sum(w, axis=0, keepdims=True) * scale

    onehot = jnp.concatenate(chosen, axis=0)
    incl = _dot(onehot.astype(BF16), tri_ref[...])
    carry = carry_ref[...]
    cglob = carry[:, 0:1] + incl
    tok = (lax.broadcasted_iota(jnp.int32, (n_exp, tc), 1) + step * tc).astype(F32)
    picked = onehot > 0.5
    csel_ref[0:n_exp, :] = jnp.where(picked, cglob, 0.0)
    code_ref[0:n_exp, :] = jnp.where(picked, tok * float(top_k) + jnp.concatenate(kplus, axis=0) - 1.0, 0.0)
    new_carry = carry + jnp.sum(onehot, axis=1, keepdims=True)
    carry_ref[...] = new_carry
    cend_ref[0] = new_carry


def route(lg, rb, cfg):
    ne, t = lg.shape
    nx = cfg.n_experts
    tc = min(cfg.route_tc, t)
    nsteps = t // tc
    tri = jnp.asarray(np.triu(np.ones((tc, tc), np.float32)), BF16)
    kern = functools.partial(_route_kernel, n_exp=cfg.n_experts, n_groups=cfg.n_groups,
                             topk_groups=cfg.topk_groups, top_k=cfg.top_k, scale=cfg.routed_scale)
    return pl.pallas_call(
        kern,
        grid=(nsteps,),
        in_specs=[pl.BlockSpec((ne, tc), lambda i: (0, i)),
                  pl.BlockSpec((ne, tc), lambda i: (0, 0)),
                  pl.BlockSpec((tc, tc), lambda i: (0, 0))],
        out_specs=[pl.BlockSpec((cfg.top_k, tc), lambda i: (0, i)),
                   pl.BlockSpec((nx, tc), lambda i: (0, i)),
                   pl.BlockSpec((nx, tc), lambda i: (0, i)),
                   pl.BlockSpec((1, nx, LANES), lambda i: (i, 0, 0))],
        out_shape=[jax.ShapeDtypeStruct((cfg.top_k, t), F32),
                   jax.ShapeDtypeStruct((nx, t), F32),
                   jax.ShapeDtypeStruct((nx, t), F32),
                   jax.ShapeDtypeStruct((nsteps, nx, LANES), F32)],
        scratch_shapes=[pltpu.VMEM((cfg.n_experts, LANES), F32)],
        compiler_params=_params(1),
        name="route",
    )(lg, rb, tri)


def _slots_kernel(be_ref, r0_ref, cnt_ref, csel_ref, code_ref, cend_ref, tok_ref, dst_ref,
                  *, bm, tc, top_k, n_tok, n_assign):
    b = pl.program_id(0)
    e = be_ref[b]
    r0 = r0_ref[b]
    cnt = cnt_ref[b]
    sub = lax.rem(e, 8)
    nsteps = cend_ref.shape[0]
    pick3 = lax.broadcasted_iota(jnp.int32, cend_ref.shape, 1) == sub
    cend = jnp.sum(jnp.where(pick3, cend_ref[...], 0.0), axis=1)
    first = (r0 + 1).astype(F32)
    last = jnp.minimum(r0 + bm, cnt).astype(F32)
    jlo = (jnp.sum(jnp.where(cend < first, 1.0, 0.0)) * (1.0 / LANES)).astype(jnp.int32)
    jhi = (jnp.sum(jnp.where(cend < last, 1.0, 0.0)) * (1.0 / LANES)).astype(jnp.int32)
    jhi = jnp.minimum(jhi, nsteps - 1)
    jhi = jnp.where(last < first, jlo - 1, jhi)

    want = (lax.broadcasted_iota(jnp.int32, (bm, tc), 0) + (r0 + 1)).astype(F32)
    pick2 = lax.broadcasted_iota(jnp.int32, (8, tc), 0) == sub
    zrows = jnp.zeros((13, tc), F32)

    def body(j, acc):
        off = pl.multiple_of(j * tc, tc)
        crow = jnp.sum(jnp.where(pick2, csel_ref[:, pl.ds(off, tc)], 0.0), axis=0, keepdims=True)
        drow = jnp.sum(jnp.where(pick2, code_ref[:, pl.ds(off, tc)], 0.0), axis=0, keepdims=True)
        di = drow.astype(jnp.int32)
        digits = jnp.concatenate([(di >> 14).astype(F32), ((di >> 7) & 127).astype(F32),
                                  (di & 127).astype(F32), zrows], axis=0).astype(BF16)
        eq = jnp.where(want == crow, 1.0, 0.0).astype(BF16)
        return acc + _dot_t(digits, eq)

    acc = lax.fori_loop(jlo, jhi + 1, body, jnp.zeros((16, bm), F32))
    code = (acc[0:1, :] * 16384.0 + acc[1:2, :] * 128.0 + acc[2:3, :]).astype(jnp.int32)
    lane = lax.broadcasted_iota(jnp.int32, (1, bm), 1)
    valid = (lane + r0) < cnt
    shift = top_k.bit_length() - 1
    tok = code >> shift
    kk = code & (top_k - 1)
    dummy = n_assign + lax.rem(b, 2) * bm + lane
    tok_ref[0] = jnp.where(valid, tok, 0)
    dst_ref[0] = jnp.where(valid, kk * n_tok + tok, dummy)


def slots(block_e, r0, cntb, csel, code, cend, cfg, n_tok):
    nb = block_e.shape[0]
    bm = cfg.moe_rows
    ne, t = csel.shape
    tc = min(cfg.route_tc, t)
    nsteps = cend.shape[0]
    assert cfg.top_k & (cfg.top_k - 1) == 0 and n_tok * cfg.top_k < (1 << 21)
    kern = functools.partial(_slots_kernel, bm=bm, tc=tc, top_k=cfg.top_k, n_tok=n_tok, n_assign=n_tok * cfg.top_k)
    grid_spec = pltpu.PrefetchScalarGridSpec(
        num_scalar_prefetch=3,
        grid=(nb,),
        in_specs=[pl.BlockSpec((8, t), lambda b, be, r0, cn: (be[b] // 8, 0)),
                  pl.BlockSpec((8, t), lambda b, be, r0, cn: (be[b] // 8, 0)),
                  pl.BlockSpec((nsteps, 8, LANES), lambda b, be, r0, cn: (0, be[b] // 8, 0))],
        out_specs=[pl.BlockSpec((1, 1, bm), lambda b, be, r0, cn: (b, 0, 0)),
                   pl.BlockSpec((1, 1, bm), lambda b, be, r0, cn: (b, 0, 0))],
    )
    return pl.pallas_call(
        kern,
        grid_spec=grid_spec,
        out_shape=[jax.ShapeDtypeStruct((nb, 1, bm), jnp.int32), jax.ShapeDtypeStruct((nb, 1, bm), jnp.int32)],
        compiler_params=_params(1),
        name="slots",
    )(block_e, r0, cntb, csel, code, cend)


MOE_RING = 3


def _moe_kernel(be_ref, tok0_ref, tok1_ref, tokn_ref, dstp_ref, dstc_ref, v_hbm, wg_ref, wu_ref, wd_ref, yk_hbm,
                xbuf, ybuf, wgb, wub, wdb, gsem, ssem, *, bm, n_assign):
    b = pl.program_id(0)
    nb = pl.num_programs(0)
    slot = lax.rem(b, MOE_RING)
    ns = v_hbm.shape[1]

    def gather_copy(t, r, s):
        return pltpu.make_async_copy(v_hbm.at[t], xbuf.at[s, pl.ds(r * ns, ns), :], gsem.at[s])

    def scatter_copy(r, d, s):
        return pltpu.make_async_copy(ybuf.at[s, pl.ds(r * ns, ns), :], yk_hbm.at[d], ssem.at[s])

    def wait_gather(s):
        for r in range(bm):
            gather_copy(0, r, s).wait()

    def wait_scatter(s):
        for r in range(bm):
            scatter_copy(r, 0, s).wait()

    @pl.when(b == 0)
    def _():
        ybuf[...] = jnp.zeros(ybuf.shape, U32)
        for s in range(2):
            for r in range(bm):
                scatter_copy(r, n_assign + s * bm + r, s).start(priority=r % 2)
        for r in range(bm):
            gather_copy(tok0_ref[0, 0, r], r, 0).start(priority=r % 2)
        for r in range(bm):
            gather_copy(tok1_ref[0, 0, r], r, 1).start(priority=r % 2)

    e = be_ref[b]
    eprev = be_ref[jnp.maximum(b - 1, 0)]

    @pl.when(jnp.logical_or(b == 0, e != eprev))
    def _():
        wgb[...] = wg_ref[0].astype(BF16)
        wub[...] = wu_ref[0].astype(BF16)
        wdb[...] = wd_ref[0].astype(BF16)

    def block_step(s):
        p = (s + 2) % MOE_RING
        wait_gather(s)
        wait_scatter(s)
        for r in range(bm):
            scatter_copy(r, dstp_ref[0, 0, r], p).start(priority=r % 2)
        for r in range(bm):
            gather_copy(tokn_ref[0, 0, r], r, p).start(priority=r % 2)
        lo, hi = _unpack_pairs(_load_slabs(xbuf.at[s], bm, ns))
        x = jnp.concatenate([lo, hi], axis=1).astype(BF16)
        g = _dot(x, wgb[...])
        u = _dot(x, wub[...])
        h = (g * _sigmoid(g)) * u
        _store_slabs(ybuf.at[s], _pack_pairs(_dot(h.astype(BF16), wdb[...])), ns)

    for s in range(MOE_RING):
        pl.when(slot == s)(functools.partial(block_step, s))

    @pl.when(b == nb - 1)
    def _():
        def drain(s):
            p, q = (s + 2) % MOE_RING, (s + 1) % MOE_RING
            wait_scatter(q)
            for r in range(bm):
                scatter_copy(r, dstc_ref[0, 0, r], s).start(priority=r % 2)
            wait_gather(p)
            wait_gather(q)
            wait_scatter(p)
            wait_scatter(s)

        for s in range(MOE_RING):
            pl.when(slot == s)(functools.partial(drain, s))


def moe(vp, block_e, slot_tok, slot_dst, wg, wu, wd, cfg):
    d = wg.shape[1]
    ns = d // (2 * LANES)
    t = vp.shape[0] // ns
    bm = cfg.moe_rows
    v3 = vp.reshape(t, ns, LANES)
    nb = block_e.shape[0]
    assert nb >= 2
    de = wg.shape[2]
    n_assign = t * cfg.top_k
    first = (n_assign + 2 * bm + jnp.arange(bm, dtype=jnp.int32)).reshape(1, 1, bm)
    dst_ext = jnp.concatenate([first, slot_dst], axis=0)
    kern = functools.partial(_moe_kernel, bm=bm, n_assign=n_assign)
    smem_blk = lambda f: pl.BlockSpec((1, 1, bm), f, memory_space=pltpu.SMEM)
    grid_spec = pltpu.PrefetchScalarGridSpec(
        num_scalar_prefetch=1,
        grid=(nb,),
        in_specs=[smem_blk(lambda b, be: (0, 0, 0)),
                  smem_blk(lambda b, be: (1, 0, 0)),
                  smem_blk(lambda b, be: (jnp.minimum(b + 2, nb - 1), 0, 0)),
                  smem_blk(lambda b, be: (b, 0, 0)),
                  smem_blk(lambda b, be: (b + 1, 0, 0)),
                  pl.BlockSpec(memory_space=pl.ANY),
                  pl.BlockSpec((1, d, de), lambda b, be: (be[b], 0, 0)),
                  pl.BlockSpec((1, d, de), lambda b, be: (be[b], 0, 0)),
                  pl.BlockSpec((1, de, d), lambda b, be: (be[b], 0, 0))],
        out_specs=pl.BlockSpec(memory_space=pl.ANY),
        scratch_shapes=[pltpu.VMEM((MOE_RING, bm * ns, LANES), U32),
                        pltpu.VMEM((MOE_RING, bm * ns, LANES), U32),
                        pltpu.VMEM((d, de), BF16), pltpu.VMEM((d, de), BF16), pltpu.VMEM((de, d), BF16),
                        pltpu.SemaphoreType.DMA((MOE_RING,)), pltpu.SemaphoreType.DMA((MOE_RING,))],
    )
    return pl.pallas_call(
        kern,
        grid_spec=grid_spec,
        out_shape=jax.ShapeDtypeStruct((n_assign + 3 * bm, ns, LANES), U32),
        compiler_params=_params(1),
        name="moe",
    )(block_e, slot_tok, slot_tok, slot_tok, dst_ext, dst_ext, v3, wg, wu, wd)


def _final_kernel(w_ref, v_ref, h1_ref, gtm_ref, gfin_ref, sg_ref, su_ref, sd_ref, *rest, top_k):
    yk_refs, o_ref, lo_ref, hi_ref = rest[:top_k], rest[top_k], rest[top_k + 1], rest[top_k + 2]
    tm = h1_ref.shape[0]
    ns = yk_refs[0].shape[1]

    def combine(t, carry):
        lo, hi = _unpack_pairs(yk_refs[0][t])
        acc_lo, acc_hi = w_ref[t, 0] * lo, w_ref[t, 0] * hi
        for k in range(1, top_k):
            lo, hi = _unpack_pairs(yk_refs[k][t])
            acc_lo, acc_hi = acc_lo + w_ref[t, k] * lo, acc_hi + w_ref[t, k] * hi
        rows = pl.ds(pl.multiple_of(t * ns, ns), ns)
        lo_ref[rows, :] = acc_lo
        hi_ref[rows, :] = acc_hi
        return carry

    lax.fori_loop(0, tm, combine, 0, unroll=8)
    routed = jnp.concatenate([_load_slabs(lo_ref, tm, ns), _load_slabs(hi_ref, tm, ns)], axis=1)

    vb = jnp.concatenate(_unpack_pairs(_load_slabs(v_ref, tm, ns)), axis=1).astype(BF16)
    g = _dot(vb, sg_ref[...])
    u = _dot(vb, su_ref[...])
    y = routed + _dot(((g * _sigmoid(g)) * u).astype(BF16), sd_ref[...])
    h2 = h1_ref[...] + gtm_ref[0] * y
    o_ref[...] = h2 * lax.rsqrt(jnp.mean(h2 * h2, axis=-1, keepdims=True) + EPS) * gfin_ref[...]


def final(vp, h1, yk3, w_tk, gtm, gfin, sg, su, sd, cfg):
    t, d = h1.shape
    tm = min(cfg.final_tm, cfg.seq)
    per_b = cfg.seq // tm
    nt = t // tm
    ds = sg.shape[1]
    nslab = d // (2 * LANES)
    const = lambda i: (0, 0)
    kern = functools.partial(_final_kernel, top_k=cfg.top_k)
    plane = lambda k: pl.BlockSpec((tm, nslab, LANES), lambda i: (k * nt + i, 0, 0))
    return pl.pallas_call(
        kern,
        grid=(nt,),
        in_specs=[pl.BlockSpec((tm, cfg.top_k), lambda i: (i, 0), memory_space=pltpu.SMEM),
                  pl.BlockSpec((tm * nslab, LANES), lambda i: (i, 0)),
                  pl.BlockSpec((tm, d), lambda i: (i, 0)),
                  pl.BlockSpec((1, 1, d), lambda i: (i // per_b, 0, 0)),
                  pl.BlockSpec((1, d), const),
                  pl.BlockSpec((d, ds), const),
                  pl.BlockSpec((d, ds), const),
                  pl.BlockSpec((ds, d), const)] + [plane(k) for k in range(cfg.top_k)],
        out_specs=pl.BlockSpec((tm, d), lambda i: (i, 0)),
        out_shape=jax.ShapeDtypeStruct((t, d), F32),
        scratch_shapes=[pltpu.VMEM((tm * nslab, LANES), F32), pltpu.VMEM((tm * nslab, LANES), F32)],
        compiler_params=_params(1),
        name="final",
    )(w_tk, vp, h1, gtm, gfin, sg, su, sd, *([yk3] * cfg.top_k))


def _t5_bucket(rel, cfg):
    nb = cfg.t5_buckets // 2
    ret = (rel > 0).astype(np.int32) * nb
    n = np.abs(rel)
    max_exact = nb // 2
    large = max_exact + (np.log(np.maximum(n, 1) / max_exact) / math.log(cfg.t5_max_dist / max_exact)
                         * (nb - max_exact)).astype(np.int32)
    large = np.minimum(large, nb - 1)
    return (ret + np.where(n < max_exact, n, large)).astype(np.int32)


def _toeplitz(vec, n):
    lead = vec.shape[:-1]
    flat = jnp.tile(vec, (1,) * len(lead) + (n,))[..., :n * (2 * n - 1)]
    return flat.reshape(lead + (n, 2 * n - 1))[..., :n]


def _rel_offsets(n):
    m = np.arange(2 * n)
    return np.where(m < n, m, m - 2 * n)


def _bias_tables_a(t5_bias, cfg):
    tile = cfg.attn_tile
    assert tile + 1 >= cfg.t5_max_dist and tile % cfg.chunk == 0
    ql = np.arange(tile)[:, None]
    kl = np.arange(tile)[None, :]
    allowed = (kl // cfg.chunk) <= (ql // cfg.chunk)
    r = _rel_offsets(tile)
    t5 = t5_bias.astype(F32)
    diag = _toeplitz(t5[_t5_bucket(r, cfg)].T, tile)
    prev = _toeplitz(t5[_t5_bucket(r - tile, cfg)].T, tile)
    far = jnp.broadcast_to(t5[_t5_bucket(np.array(-2 * tile), cfg)][:, None, None], diag.shape)
    return jnp.stack([jnp.where(allowed[None], diag, NEG), prev, far], axis=1)


def _bias_tables_b(rel_table, cfg):
    tile = cfg.attn_tile
    assert (cfg.left_chunks * cfg.chunk) % tile == 0 and tile % cfg.chunk == 0
    nwin = (cfg.left_chunks * cfg.chunk) // tile + 1
    ql = np.arange(tile)[:, None]
    kl = np.arange(tile)[None, :]
    r = _rel_offsets(tile)
    rt = rel_table.astype(F32)
    tabs = []
    for j in range(nwin):
        off = (j - (nwin - 1)) * tile
        idx = np.clip(r + off, -cfg.rel_clip, cfg.rel_clip) + cfg.rel_clip
        dchunk = ql // cfg.chunk - (kl + off) // cfg.chunk
        band = (dchunk >= 0) & (dchunk <= cfg.left_chunks)
        tabs.append(jnp.where(band[None], _toeplitz(rt[idx].T, tile), NEG))
    return jnp.stack(tabs, axis=1)


def _block_tables(counts, n_assign, cfg):
    e_ = cfg.n_experts
    bm = cfg.moe_rows
    nb = -(-(n_assign + e_ * (bm - 1)) // bm)
    pcounts = (counts + bm - 1) // bm * bm
    pends = jnp.cumsum(pcounts)
    pstarts = pends - pcounts
    start_b = jnp.arange(nb, dtype=jnp.int32) * bm
    block_e = jnp.minimum(jnp.sum((pends[None, :] <= start_b[:, None]).astype(jnp.int32), axis=1), e_ - 1)
    mine = jnp.arange(e_, dtype=jnp.int32)[None, :] == block_e[:, None]
    pstart_b = jnp.sum(jnp.where(mine, pstarts[None, :], 0), axis=1)
    count_b = jnp.sum(jnp.where(mine, counts[None, :], 0), axis=1)
    return block_e.astype(jnp.int32), (start_b - pstart_b).astype(jnp.int32), count_b.astype(jnp.int32)


def _forward(cfg, x, c, w_ada, b_ada, g_attn, w_in, lambda_qk, t5_bias, rel_bias_b, w_up_a, w_up_b, w_o,
             g_moe, w_router, router_bias, w_exp_gate, w_exp_up, w_exp_down,
             w_sh_gate, w_sh_up, w_sh_down, g_final):
    b_, s_, d_ = x.shape
    t_ = b_ * s_
    l = 0
    qk_a = cfg.ha * 2 * cfg.da_qk
    width_a = cfg.ha * 2 * cfg.da_qk
    width_b = cfg.hb * cfg.db
    in_cols = w_in.shape[2]

    rows = max(16, b_)
    c_pad = jnp.zeros((rows, d_), F32).at[:b_].set(c)
    mod = ada_mod(c_pad, w_ada[l], b_ada[l], cfg)[:b_]
    sh_a, sc_a, gt_a, sh_m, sc_m, gt_m = [m.reshape(b_, 1, d_) for m in jnp.split(mod, cfg.n_mod, axis=-1)]

    colscale = np.ones((1, in_cols), np.float32)
    colscale[0, :qk_a] = cfg.da_qk ** -0.5 * LOG2E
    qb0 = 2 * qk_a + width_a
    colscale[0, qb0:qb0 + width_b] = cfg.db ** -0.5
    x2 = x.reshape(t_, d_)
    proj = in_proj(x2, g_attn[l].reshape(1, d_), sc_a, sh_a, w_in[l].astype(BF16), jnp.asarray(colscale), cfg)

    lq = lambda_qk[l].astype(F32)
    lam = jnp.exp(jnp.sum(lq[0] * lq[1])) - jnp.exp(jnp.sum(lq[2] * lq[3])) + cfg.lam_init
    ya = attn_a(proj, lam.reshape(1), _bias_tables_a(t5_bias * LOG2E, cfg), cfg)
    yb = attn_b(proj, _bias_tables_b(rel_bias_b[l], cfg), cfg)

    ne_pad = max(LANES, cfg.n_experts)
    wrt = jnp.zeros((ne_pad, d_), F32).at[:cfg.n_experts].set(w_router[l].T)
    wrh, wrl = _split_bf16(wrt)
    h1, v, lg = merge(ya, yb, proj, x2, gt_a, sc_m, sh_m, g_moe[l].reshape(1, d_),
                      w_up_a[l].astype(BF16), w_up_b[l].astype(BF16), w_o[l].astype(BF16), wrh, wrl, cfg)

    tc = min(cfg.route_tc, t_)
    rb = jnp.zeros((ne_pad, tc), F32).at[:cfg.n_experts].set(
        jnp.broadcast_to(router_bias[l].astype(F32)[:, None], (cfg.n_experts, tc)))
    w_kt, csel, code, cend = route(lg, rb, cfg)
    counts = cend[-1, :, 0].astype(jnp.int32)
    block_e, r0, count_b = _block_tables(counts, t_ * cfg.top_k, cfg)
    slot_tok, slot_dst = slots(block_e, r0, count_b, csel, code, cend, cfg, t_)

    yk = moe(v, block_e, slot_tok, slot_dst, w_exp_gate[l], w_exp_up[l], w_exp_down[l], cfg)
    out = final(v, h1, yk, w_kt.T, gt_m, g_final.reshape(1, d_),
                w_sh_gate[l].astype(BF16), w_sh_up[l].astype(BF16), w_sh_down[l].astype(BF16), cfg)
    return out.reshape(b_, s_, d_)


def kernel(x, c, w_ada, b_ada, g_attn, w_in, lambda_qk, t5_bias, rel_bias_b, w_up_a, w_up_b, w_o, g_moe,
           w_router, router_bias, w_exp_gate, w_exp_up, w_exp_down, w_sh_gate, w_sh_up, w_sh_down, g_final):
    return _forward(Cfg(), x, c, w_ada, b_ada, g_attn, w_in, lambda_qk, t5_bias, rel_bias_b, w_up_a, w_up_b,
                    w_o, g_moe, w_router, router_bias, w_exp_gate, w_exp_up, w_exp_down,
                    w_sh_gate, w_sh_up, w_sh_down, g_final)
```

```python
import functools
import math
from typing import NamedTuple

import numpy as np
import jax
import jax.numpy as jnp
from jax import lax
from jax.experimental import pallas as pl
from jax.experimental.pallas import tpu as pltpu

F32 = jnp.float32
BF16 = jnp.bfloat16
NEG = -1e30
EPS = 1e-6
HEAD_EPS = 1e-5
LOG2E = 1.4426950408889634
LANES = 128
VMEM_LIMIT = 56 * 1024 * 1024


class Cfg(NamedTuple):
    batch: int = 8
    seq: int = 2048
    d_model: int = 2048
    chunk: int = 64
    ha: int = 8
    da_qk: int = 64
    hb: int = 8
    db: int = 128
    left_chunks: int = 8
    rel_clip: int = 256
    t5_buckets: int = 32
    t5_max_dist: int = 128
    n_experts: int = 64
    top_k: int = 8
    n_groups: int = 8
    topk_groups: int = 4
    d_expert: int = 512
    d_shared: int = 512
    routed_scale: float = 2.5
    n_mod: int = 6
    lam_init: float = 0.8 - 0.6 * math.exp(-0.3 * 0)
    attn_tile: int = 256
    moe_rows: int = 256
    proj_tm: int = 1024
    proj_tn: int = 1024
    merge_tm: int = 256
    route_tc: int = 512
    final_tm: int = 128
    ada_tn: int = 1024


def _sigmoid(x):
    return 1.0 / (1.0 + jnp.exp(-x))


def _split_bf16(x):
    hi = x.astype(BF16)
    lo = (x - hi.astype(F32)).astype(BF16)
    return hi, lo


def _dot(a, b):
    return jnp.dot(a, b, preferred_element_type=F32)


def _dot_t(a, b):
    return lax.dot_general(a, b, (((1,), (1,)), ((), ())), preferred_element_type=F32)


U32 = jnp.uint32
HI_MASK = 0xFFFF0000


def _pack_pairs(x):
    n = x.shape[1] // 2
    bits = pltpu.bitcast(x.astype(BF16).astype(F32), U32)
    return (bits[:, :n] >> 16) | (bits[:, n:] & U32(HI_MASK))


def _unpack_pairs(p):
    return pltpu.bitcast(p << 16, F32), pltpu.bitcast(p & U32(HI_MASK), F32)


def _load_slabs(ref, m, ns):
    return jnp.concatenate([ref[pl.ds(j, m, stride=ns), :] for j in range(ns)], axis=1)


def _store_slabs(ref, val, ns):
    m = val.shape[0]
    for j in range(ns):
        ref[pl.ds(j, m, stride=ns), :] = val[:, j * LANES:(j + 1) * LANES]


def _params(n_axes):
    return pltpu.CompilerParams(dimension_semantics=("arbitrary",) * n_axes,
                                vmem_limit_bytes=VMEM_LIMIT)


def _ada_kernel(c_ref, w_ref, b_ref, o_ref):
    c = c_ref[...]
    s_hi, s_lo = _split_bf16(c * _sigmoid(c))
    w_hi, w_lo = _split_bf16(w_ref[...])
    acc = _dot(s_hi, w_hi) + _dot(s_hi, w_lo) + _dot(s_lo, w_hi)
    o_ref[...] = acc + b_ref[...]


def ada_mod(c_pad, w_ada, b_ada, cfg):
    rows, d = c_pad.shape
    n = w_ada.shape[1]
    tn = min(cfg.ada_tn, n)
    return pl.pallas_call(
        _ada_kernel,
        grid=(n // tn,),
        in_specs=[pl.BlockSpec((rows, d), lambda j: (0, 0)),
                  pl.BlockSpec((d, tn), lambda j: (0, j)),
                  pl.BlockSpec((1, tn), lambda j: (0, j))],
        out_specs=pl.BlockSpec((rows, tn), lambda j: (0, j)),
        out_shape=jax.ShapeDtypeStruct((rows, n), F32),
        compiler_params=_params(1),
        name="ada_mod",
    )(c_pad, w_ada, b_ada.reshape(1, n))


def _inproj_kernel(x_ref, g_ref, sc_ref, sh_ref, w_ref, cs_ref, o_ref, u_ref):
    @pl.when(pl.program_id(1) == 0)
    def _():
        x = x_ref[...]
        y = x * lax.rsqrt(jnp.mean(x * x, axis=-1, keepdims=True) + EPS) * g_ref[...]
        u_ref[...] = (y * (1.0 + sc_ref[0]) + sh_ref[0]).astype(BF16)

    acc = _dot(u_ref[...], w_ref[...])
    o_ref[...] = (acc * cs_ref[...]).astype(BF16)


def in_proj(x2, g, sc, sh, w_bf, colscale, cfg):
    t, d = x2.shape
    n = w_bf.shape[1]
    tm = min(cfg.proj_tm, cfg.seq)
    tn = min(cfg.proj_tn, n)
    per_b = cfg.seq // tm
    return pl.pallas_call(
        _inproj_kernel,
        grid=(t // tm, n // tn),
        in_specs=[pl.BlockSpec((tm, d), lambda i, j: (i, 0)),
                  pl.BlockSpec((1, d), lambda i, j: (0, 0)),
                  pl.BlockSpec((1, 1, d), lambda i, j: (i // per_b, 0, 0)),
                  pl.BlockSpec((1, 1, d), lambda i, j: (i // per_b, 0, 0)),
                  pl.BlockSpec((d, tn), lambda i, j: (0, j)),
                  pl.BlockSpec((1, tn), lambda i, j: (0, j))],
        out_specs=pl.BlockSpec((tm, tn), lambda i, j: (i, j)),
        out_shape=jax.ShapeDtypeStruct((t, n), BF16),
        scratch_shapes=[pltpu.VMEM((tm, d), BF16)],
        compiler_params=_params(2),
        name="in_proj",
    )(x2, g, sc, sh, w_bf, colscale)


def _attn_a_kernel(lam_ref, qlo_ref, qhi_ref, k_ref, v_ref, tb_ref, olo_ref, ohi_ref,
                   s_ref, m_ref, l_ref, o_ref, *, tile, half, nq, out_scale):
    p = pl.program_id(2)
    hq = nq - 1 - p
    nvis = nq + 1
    ninf = -jnp.inf

    def stacked(q):
        lane = lax.broadcasted_iota(jnp.int32, q.shape, 1)
        zero = jnp.zeros_like(q)
        return jnp.concatenate([jnp.where(lane < half, q, zero), jnp.where(lane >= half, q, zero)], axis=0)

    qq_lo, qq_hi = stacked(qlo_ref[...]), stacked(qhi_ref[...])

    def fold(x, op):
        r = x[:, :LANES]
        for c in range(1, tile // LANES):
            r = op(r, x[:, c * LANES:(c + 1) * LANES])
        return r

    def pick(i, lo, hi):
        return jnp.where(i <= p, lo, hi)

    def key_tile(i):
        return jnp.where(i <= p, i, i - p - 1)

    def key_rows(i):
        return pl.ds(pl.multiple_of(key_tile(i) * tile, tile), tile)

    for i in range(nvis):
        bias = tb_ref[0, jnp.minimum(pick(i, p, hq) - key_tile(i), 2)]
        s = _dot_t(pick(i, qq_lo, qq_hi), k_ref[key_rows(i), :]) + jnp.concatenate([bias, bias], axis=0)
        s_ref[i] = s
        m_ref[i] = fold(s, jnp.maximum)

    def reduce_tiles(ref, op, init):
        lo = jnp.full(ref.shape[1:], init, F32)
        hi = lo
        for i in range(nvis):
            lo = op(lo, jnp.where(i <= p, ref[i], init))
            hi = op(hi, jnp.where(i <= p, init, ref[i]))
        return lo, hi

    m_lo, m_hi = reduce_tiles(m_ref, jnp.maximum, ninf)
    m_lo = jnp.max(m_lo, axis=-1, keepdims=True)
    m_hi = jnp.max(m_hi, axis=-1, keepdims=True)
    for i in range(nvis):
        e = jnp.exp2(s_ref[i] - pick(i, m_lo, m_hi))
        s_ref[i] = e
        l_ref[i] = fold(e, jnp.add)

    l_lo, l_hi = reduce_tiles(l_ref, jnp.add, 0.0)
    lam = lam_ref[0]

    def coefs(l):
        inv = 1.0 / jnp.sum(l, axis=-1, keepdims=True)
        return inv[:tile], lam * inv[tile:]

    (c0_lo, c1_lo), (c0_hi, c1_hi) = coefs(l_lo), coefs(l_hi)
    for i in range(nvis):
        e = s_ref[i]
        a = e[:tile] * pick(i, c0_lo, c0_hi) - e[tile:] * pick(i, c1_lo, c1_hi)
        o_ref[i] = _dot(a.astype(BF16), v_ref[key_rows(i), :])

    o_lo, o_hi = reduce_tiles(o_ref, jnp.add, 0.0)
    for o, out in ((o_lo, olo_ref), (o_hi, ohi_ref)):
        out[...] = (o * lax.rsqrt(jnp.mean(o * o, axis=-1, keepdims=True) + HEAD_EPS) * out_scale).astype(BF16)


def attn_a(proj, lam, tb, cfg):
    tile = cfg.attn_tile
    s = cfg.seq
    nq = s // tile
    assert nq % 2 == 0
    npair = nq // 2
    dv = 2 * cfg.da_qk
    assert dv == LANES
    kcol = cfg.ha
    vcol = 2 * cfg.ha
    kern = functools.partial(_attn_a_kernel, tile=tile, half=cfg.da_qk, nq=nq, out_scale=1.0 - cfg.lam_init)
    half_shape = jax.ShapeDtypeStruct((cfg.batch * npair * tile, cfg.ha * dv), BF16)
    y_lo, y_hi = pl.pallas_call(
        kern,
        grid=(cfg.batch, cfg.ha, npair),
        in_specs=[pl.BlockSpec(memory_space=pltpu.SMEM),
                  pl.BlockSpec((tile, LANES), lambda b, h, p: (b * nq + p, h)),
                  pl.BlockSpec((tile, LANES), lambda b, h, p: (b * nq + nq - 1 - p, h)),
                  pl.BlockSpec((s, LANES), lambda b, h, p: (b, kcol + h)),
                  pl.BlockSpec((s, LANES), lambda b, h, p: (b, vcol + h)),
                  pl.BlockSpec((1, 3, tile, tile), lambda b, h, p: (h, 0, 0, 0))],
        out_specs=[pl.BlockSpec((tile, LANES), lambda b, h, p: (b * npair + p, h)),
                   pl.BlockSpec((tile, LANES), lambda b, h, p: (b * npair + npair - 1 - p, h))],
        out_shape=[half_shape, half_shape],
        scratch_shapes=[pltpu.VMEM((nq + 1, 2 * tile, tile), F32),
                        pltpu.VMEM((nq + 1, 2 * tile, LANES), F32),
                        pltpu.VMEM((nq + 1, 2 * tile, LANES), F32),
                        pltpu.VMEM((nq + 1, tile, LANES), F32)],
        compiler_params=_params(3),
        name="attn_a",
    )(lam, proj, proj, proj, proj, tb)
    w = cfg.ha * dv
    rows = npair * tile
    return jnp.concatenate([y_lo.reshape(cfg.batch, rows, w), y_hi.reshape(cfg.batch, rows, w)],
                           axis=1).reshape(cfg.batch * s, w)


ATTN_B_TILES = 2


def _attn_b_kernel(q_ref, k_ref, v_ref, tb_ref, o_ref, *, tile, nwin):
    for u in range(ATTN_B_TILES):
        qi = pl.program_id(2) * ATTN_B_TILES + u
        q = q_ref[u * tile:(u + 1) * tile, :]
        ss, vs = [], []
        for j in range(nwin):
            kb = qi - (nwin - 1) + j
            start = pl.multiple_of(jnp.maximum(kb, 0) * tile, tile)
            ss.append(_dot_t(q, k_ref[pl.ds(start, tile), :]) + tb_ref[0, jnp.where(kb >= 0, j, nwin)])
            vs.append(v_ref[pl.ds(start, tile), :])
        m = jnp.max(ss[0], axis=-1, keepdims=True)
        for s in ss[1:]:
            m = jnp.maximum(m, jnp.max(s, axis=-1, keepdims=True))
        ps = [jnp.exp2(s - m) for s in ss]
        l = jnp.sum(ps[0], axis=-1, keepdims=True)
        for p in ps[1:]:
            l = l + jnp.sum(p, axis=-1, keepdims=True)
        acc = _dot(ps[0].astype(BF16), vs[0])
        for p, v in zip(ps[1:], vs[1:]):
            acc = acc + _dot(p.astype(BF16), v)
        o_ref[u * tile:(u + 1) * tile, :] = (acc * (1.0 / l)).astype(BF16)


def attn_b(proj, tb, cfg):
    t = proj.shape[0]
    tile = cfg.attn_tile
    s = cfg.seq
    nq = s // tile
    assert cfg.db == LANES and nq % ATTN_B_TILES == 0
    nstep = nq // ATTN_B_TILES
    rows = ATTN_B_TILES * tile
    nwin = (cfg.left_chunks * cfg.chunk) // tile + 1
    base = (2 * cfg.ha * 2 * cfg.da_qk + cfg.ha * 2 * cfg.da_qk) // LANES
    kern = functools.partial(_attn_b_kernel, tile=tile, nwin=nwin)
    return pl.pallas_call(
        kern,
        grid=(cfg.batch, cfg.hb, nstep),
        in_specs=[pl.BlockSpec((rows, LANES), lambda b, h, i: (b * nstep + i, base + h)),
                  pl.BlockSpec((s, LANES), lambda b, h, i: (b, base + cfg.hb + h)),
                  pl.BlockSpec((s, LANES), lambda b, h, i: (b, base + 2 * cfg.hb + h)),
                  pl.BlockSpec((1, nwin + 1, tile, tile), lambda b, h, i: (h, 0, 0, 0))],
        out_specs=pl.BlockSpec((rows, LANES), lambda b, h, i: (b * nstep + i, h)),
        out_shape=jax.ShapeDtypeStruct((t, cfg.hb * cfg.db), BF16),
        compiler_params=_params(3),
        name="attn_b",
    )(proj, proj, proj, tb)


def _merge_kernel(ya_ref, yb_ref, ga_ref, gb_ref, x_ref, gta_ref, scm_ref, shm_ref, gmoe_ref,
                  wua_ref, wub_ref, wo_ref, wrh_ref, wrl_ref, h1_ref, v_ref, lg_ref):
    a = _dot(ya_ref[...], wua_ref[...])
    b = _dot(yb_ref[...], wub_ref[...])
    merged = _sigmoid(ga_ref[...].astype(F32)) * a + _sigmoid(gb_ref[...].astype(F32)) * b
    att = _dot(merged.astype(BF16), wo_ref[...])
    h1 = x_ref[...] + gta_ref[0] * att
    h1_ref[...] = h1
    y = h1 * lax.rsqrt(jnp.mean(h1 * h1, axis=-1, keepdims=True) + EPS) * gmoe_ref[...]
    v = y * (1.0 + scm_ref[0]) + shm_ref[0]
    _store_slabs(v_ref, _pack_pairs(v), v_ref.shape[0] // v.shape[0])
    v_hi, v_lo = _split_bf16(v)
    wrh = wrh_ref[...]
    lg_ref[...] = _dot_t(wrh, v_hi) + _dot_t(wrh, v_lo) + _dot_t(wrl_ref[...], v_hi)


def merge(ya, yb, proj, x2, gta, scm, shm, gmoe, wua, wub, wo, wrh, wrl, cfg):
    t, d = x2.shape
    tm = min(cfg.merge_tm, cfg.seq)
    per_b = cfg.seq // tm
    wa = ya.shape[1]
    wb = yb.shape[1]
    gate0 = (proj.shape[1] - 2 * d) // d
    ne = wrh.shape[0]
    ns = d // (2 * LANES)
    const = lambda i: (0, 0)
    row3 = lambda i: (i // per_b, 0, 0)
    return pl.pallas_call(
        _merge_kernel,
        grid=(t // tm,),
        in_specs=[pl.BlockSpec((tm, wa), lambda i: (i, 0)),
                  pl.BlockSpec((tm, wb), lambda i: (i, 0)),
                  pl.BlockSpec((tm, d), lambda i: (i, gate0)),
                  pl.BlockSpec((tm, d), lambda i: (i, gate0 + 1)),
                  pl.BlockSpec((tm, d), lambda i: (i, 0)),
                  pl.BlockSpec((1, 1, d), row3),
                  pl.BlockSpec((1, 1, d), row3),
                  pl.BlockSpec((1, 1, d), row3),
                  pl.BlockSpec((1, d), const),
                  pl.BlockSpec((wa, d), const),
                  pl.BlockSpec((wb, d), const),
                  pl.BlockSpec((d, d), const),
                  pl.BlockSpec((ne, d), const),
                  pl.BlockSpec((ne, d), const)],
        out_specs=[pl.BlockSpec((tm, d), lambda i: (i, 0)),
                   pl.BlockSpec((tm * ns, LANES), lambda i: (i, 0)),
                   pl.BlockSpec((ne, tm), lambda i: (0, i))],
        out_shape=[jax.ShapeDtypeStruct((t, d), F32),
                   jax.ShapeDtypeStruct((t * ns, LANES), U32),
                   jax.ShapeDtypeStruct((ne, t), F32)],
        compiler_params=_params(1),
        name="merge",
    )(ya, yb, proj, proj, x2, gta, scm, shm, gmoe, wua, wub, wo, wrh, wrl)


def _route_kernel(lg_ref, rb_ref, tri_ref, w_ref, lin_ref, kin_ref, gend_ref, carry_ref,
                  *, n_exp, n_groups, topk_groups, top_k, scale):
    gsz = n_exp // n_groups
    tc = lg_ref.shape[1]
    ninf = -jnp.inf
    step = pl.program_id(0)

    @pl.when(step == 0)
    def _():
        carry_ref[...] = jnp.zeros(carry_ref.shape, F32)

    row = lax.broadcasted_iota(jnp.int32, (gsz, tc), 0).astype(F32)
    sc, bi = [], []
    for g in range(n_groups):
        s = _sigmoid(lg_ref[g * gsz:(g + 1) * gsz, :])
        sc.append(s)
        bi.append(s + rb_ref[g * gsz:(g + 1) * gsz, :])
    gs = []
    for x in bi:
        m1 = jnp.max(x, axis=0, keepdims=True)
        eq = x == m1
        cnt = jnp.sum(jnp.where(eq, 1.0, 0.0), axis=0, keepdims=True)
        second = jnp.max(jnp.where(eq, ninf, x), axis=0, keepdims=True)
        gs.append(m1 + jnp.where(cnt >= 2.0, m1, second))
    gmat = jnp.concatenate(gs, axis=0)
    grow = lax.broadcasted_iota(jnp.int32, gmat.shape, 0).astype(F32)
    gsel = jnp.zeros(gmat.shape, F32)
    for _ in range(topk_groups):
        m = jnp.max(gmat, axis=0, keepdims=True)
        idx = jnp.min(jnp.where(gmat == m, grow, float(n_groups)), axis=0, keepdims=True)
        hit = grow == idx
        gsel = jnp.where(hit, 1.0, gsel)
        gmat = jnp.where(hit, ninf, gmat)
    cur = [jnp.where(gsel[g:g + 1, :] > 0.5, bi[g], ninf) for g in range(n_groups)]
    ids = [row + float(g * gsz) for g in range(n_groups)]
    w_rows = []
    chosen = [jnp.zeros((gsz, tc), F32) for _ in range(n_groups)]
    kplus = [jnp.zeros((gsz, tc), F32) for _ in range(n_groups)]
    for k in range(top_k):
        mg = cur[0]
        for c in cur[1:]:
            mg = jnp.maximum(mg, c)
        m = jnp.max(mg, axis=0, keepdims=True)
        cand = jnp.where(cur[0] == m, ids[0], float(n_exp))
        for c, i in zip(cur[1:], ids[1:]):
            cand = jnp.minimum(cand, jnp.where(c == m, i, float(n_exp)))
        idx = jnp.min(cand, axis=0, keepdims=True)
        wsum = jnp.zeros((gsz, tc), F32)
        for g in range(n_groups):
            hit = ids[g] == idx
            wsum = wsum + jnp.where(hit, sc[g], 0.0)
            cur[g] = jnp.where(hit, ninf, cur[g])
            chosen[g] = jnp.where(hit, 1.0, chosen[g])
            kplus[g] = jnp.where(hit, float(k + 1), kplus[g])
        w_rows.append(jnp.sum(wsum, axis=0, keepdims=True))
    w = jnp.concatenate(w_rows, axis=0)
    w_ref[...] = w / jnp.sum(w, axis=0, keepdims=True) * scale

    onehot = jnp.concatenate(chosen, axis=0)
    incl = _dot(onehot.astype(BF16), tri_ref[...])
    lin_ref[...] = incl.astype(BF16)
    kin_ref[...] = jnp.concatenate(kplus, axis=0).astype(BF16)
    gps = tc // LANES
    lane = lax.broadcasted_iota(jnp.int32, (n_exp, LANES), 1)
    run = carry_ref[...]
    ends = jnp.zeros((n_exp, LANES), F32)
    for g in range(gps):
        ends = jnp.where(lane == gps + g, run, ends)
        run = run + incl[:, (g + 1) * LANES - 1:(g + 1) * LANES]
        ends = jnp.where(lane == g, run, ends)
    carry_ref[...] = run
    gend_ref[0] = ends


def route(lg, rb, cfg):
    ne, t = lg.shape
    nx = cfg.n_experts
    tc = min(cfg.route_tc, t)
    nsteps = t // tc
    tri = jnp.asarray(np.kron(np.eye(tc // LANES), np.triu(np.ones((LANES, LANES)))), BF16)
    kern = functools.partial(_route_kernel, n_exp=cfg.n_experts, n_groups=cfg.n_groups,
                             topk_groups=cfg.topk_groups, top_k=cfg.top_k, scale=cfg.routed_scale)
    return pl.pallas_call(
        kern,
        grid=(nsteps,),
        in_specs=[pl.BlockSpec((ne, tc), lambda i: (0, i)),
                  pl.BlockSpec((ne, tc), lambda i: (0, 0)),
                  pl.BlockSpec((tc, tc), lambda i: (0, 0))],
        out_specs=[pl.BlockSpec((cfg.top_k, tc), lambda i: (0, i)),
                   pl.BlockSpec((nx, tc), lambda i: (0, i)),
                   pl.BlockSpec((nx, tc), lambda i: (0, i)),
                   pl.BlockSpec((1, nx, LANES), lambda i: (i, 0, 0))],
        out_shape=[jax.ShapeDtypeStruct((cfg.top_k, t), F32),
                   jax.ShapeDtypeStruct((nx, t), BF16),
                   jax.ShapeDtypeStruct((nx, t), BF16),
                   jax.ShapeDtypeStruct((nsteps, nx, LANES), F32)],
        scratch_shapes=[pltpu.VMEM((cfg.n_experts, LANES), F32)],
        compiler_params=_params(1),
        name="route",
    )(lg, rb, tri)


SLOT_BLOCKS = 4


def _slots_kernel(be_ref, r0_ref, cnt_ref, gstart_ref, gend_ref, lin_ref, kin_ref, tok_ref, dst_ref,
                  *, bm, top_k, n_tok, n_assign):
    step = pl.program_id(0)
    rows = lax.broadcasted_iota(jnp.int32, (bm, LANES), 0)
    lane = lax.broadcasted_iota(jnp.int32, (bm, LANES), 1).astype(F32)
    ones = jnp.ones((16, LANES), BF16)
    gids = lax.broadcasted_iota(jnp.int32, (16, LANES), 1).astype(F32).astype(BF16)
    lane_row = lax.broadcasted_iota(jnp.int32, (1, bm), 1)
    for u in range(SLOT_BLOCKS):
        b = step * SLOT_BLOCKS + u
        e = be_ref[b]
        r0 = r0_ref[b]
        cnt = cnt_ref[b]
        r = (rows + r0).astype(F32)
        gstart = gstart_ref[pl.ds(e, 1), :]
        in_group = jnp.logical_and(gstart <= r, r < gend_ref[pl.ds(e, 1), :])
        group = jnp.where(in_group, 1.0, 0.0).astype(BF16)
        want = r - jnp.sum(jnp.where(in_group, gstart, 0.0), axis=1, keepdims=True) + 1.0
        kin = _dot(group, kin_ref[e])
        hit = jnp.logical_and(_dot(group, lin_ref[e]) == want, kin > 0.5)
        t_loc = _dot_t(ones, jnp.where(hit, lane, 0.0).astype(BF16))[0:1]
        kp1 = _dot_t(ones, jnp.where(hit, kin, 0.0).astype(BF16))[0:1]
        gid = _dot_t(gids, group)[0:1]
        tok = (gid * float(LANES) + t_loc).astype(jnp.int32)
        kk = kp1.astype(jnp.int32) - 1
        valid = (lane_row + r0) < cnt
        dummy = n_assign + lax.rem(b, 2) * bm + lane_row
        tok_ref[u] = jnp.where(valid, tok, 0)
        dst_ref[u] = jnp.where(valid, kk * n_tok + tok, dummy)


def slots(block_e, r0, cntb, gstart, gend, lin, kin, cfg, n_tok):
    nb = block_e.shape[0]
    bm = cfg.moe_rows
    assert nb % SLOT_BLOCKS == 0 and gstart.shape[1] == LANES and lin.shape[1:] == (LANES, LANES)
    kern = functools.partial(_slots_kernel, bm=bm, top_k=cfg.top_k, n_tok=n_tok, n_assign=n_tok * cfg.top_k)
    whole = lambda a: pl.BlockSpec(a.shape, lambda i, be, r0, cn: (0,) * a.ndim)
    out_blk = pl.BlockSpec((SLOT_BLOCKS, 1, bm), lambda i, be, r0, cn: (i, 0, 0))
    grid_spec = pltpu.PrefetchScalarGridSpec(
        num_scalar_prefetch=3,
        grid=(nb // SLOT_BLOCKS,),
        in_specs=[whole(gstart), whole(gend), whole(lin), whole(kin)],
        out_specs=[out_blk, out_blk],
    )
    return pl.pallas_call(
        kern,
        grid_spec=grid_spec,
        out_shape=[jax.ShapeDtypeStruct((nb, 1, bm), jnp.int32), jax.ShapeDtypeStruct((nb, 1, bm), jnp.int32)],
        compiler_params=_params(1),
        name="slots",
    )(block_e, r0, cntb, gstart, gend, lin, kin)


MOE_RING = 3


def _moe_kernel(be_ref, tok0_ref, tok1_ref, tokn_ref, dstp_ref, dstc_ref, v_hbm, wg_ref, wu_ref, wd_ref, yk_hbm,
                xbuf, ybuf, wgb, wub, wdb, gsem, ssem, *, bm, n_assign):
    b = pl.program_id(0)
    nb = pl.num_programs(0)
    slot = lax.rem(b, MOE_RING)
    ns = v_hbm.shape[1]

    def gather_copy(t, r, s):
        return pltpu.make_async_copy(v_hbm.at[t], xbuf.at[s, pl.ds(r * ns, ns), :], gsem.at[s])

    def scatter_copy(r, d, s):
        return pltpu.make_async_copy(ybuf.at[s, pl.ds(r * ns, ns), :], yk_hbm.at[d], ssem.at[s])

    def wait_gather(s):
        for r in range(bm):
            gather_copy(0, r, s).wait()

    def wait_scatter(s):
        for r in range(bm):
            scatter_copy(r, 0, s).wait()

    @pl.when(b == 0)
    def _():
        ybuf[...] = jnp.zeros(ybuf.shape, U32)
        for s in range(2):
            for r in range(bm):
                scatter_copy(r, n_assign + s * bm + r, s).start(priority=r % 2)
        for r in range(bm):
            gather_copy(tok0_ref[0, 0, r], r, 0).start(priority=r % 2)
        for r in range(bm):
            gather_copy(tok1_ref[0, 0, r], r, 1).start(priority=r % 2)

    e = be_ref[b]
    eprev = be_ref[jnp.maximum(b - 1, 0)]

    @pl.when(jnp.logical_or(b == 0, e != eprev))
    def _():
        wgb[...] = wg_ref[0].astype(BF16)
        wub[...] = wu_ref[0].astype(BF16)
        wdb[...] = wd_ref[0].astype(BF16)

    def block_step(s):
        p = (s + 2) % MOE_RING
        wait_gather(s)
        wait_scatter(s)
        for r in range(bm):
            scatter_copy(r, dstp_ref[0, 0, r], p).start(priority=r % 2)
        for r in range(bm):
            gather_copy(tokn_ref[0, 0, r], r, p).start(priority=r % 2)
        lo, hi = _unpack_pairs(_load_slabs(xbuf.at[s], bm, ns))
        x = jnp.concatenate([lo, hi], axis=1).astype(BF16)
        g = _dot(x, wgb[...])
        u = _dot(x, wub[...])
        h = (g * _sigmoid(g)) * u
        _store_slabs(ybuf.at[s], _pack_pairs(_dot(h.astype(BF16), wdb[...])), ns)

    last = be_ref[nb] - 1
    for s in range(MOE_RING):
        pl.when(jnp.logical_and(slot == s, b <= last))(functools.partial(block_step, s))

    @pl.when(b == last)
    def _():
        def drain(s):
            p, q = (s + 2) % MOE_RING, (s + 1) % MOE_RING
            wait_scatter(q)
            for r in range(bm):
                scatter_copy(r, dstc_ref[0, 0, r], s).start(priority=r % 2)
            wait_gather(p)
            wait_gather(q)
            wait_scatter(p)
            wait_scatter(s)

        for s in range(MOE_RING):
            pl.when(slot == s)(functools.partial(drain, s))


def moe(vp, block_e, n_used, slot_tok, slot_dst, wg, wu, wd, cfg):
    d = wg.shape[1]
    ns = d // (2 * LANES)
    t = vp.shape[0] // ns
    bm = cfg.moe_rows
    v3 = vp.reshape(t, ns, LANES)
    nb = block_e.shape[0]
    assert nb >= 2
    de = wg.shape[2]
    n_assign = t * cfg.top_k
    first = (n_assign + 2 * bm + jnp.arange(bm, dtype=jnp.int32)).reshape(1, 1, bm)
    dst_ext = jnp.concatenate([first, slot_dst], axis=0)
    kern = functools.partial(_moe_kernel, bm=bm, n_assign=n_assign)
    smem_blk = lambda f: pl.BlockSpec((1, 1, bm), f, memory_space=pltpu.SMEM)
    grid_spec = pltpu.PrefetchScalarGridSpec(
        num_scalar_prefetch=1,
        grid=(nb,),
        in_specs=[smem_blk(lambda b, be: (0, 0, 0)),
                  smem_blk(lambda b, be: (1, 0, 0)),
                  smem_blk(lambda b, be: (jnp.minimum(b + 2, nb - 1), 0, 0)),
                  smem_blk(lambda b, be: (b, 0, 0)),
                  smem_blk(lambda b, be: (b + 1, 0, 0)),
                  pl.BlockSpec(memory_space=pl.ANY),
                  pl.BlockSpec((1, d, de), lambda b, be: (be[b], 0, 0)),
                  pl.BlockSpec((1, d, de), lambda b, be: (be[b], 0, 0)),
                  pl.BlockSpec((1, de, d), lambda b, be: (be[b], 0, 0))],
        out_specs=pl.BlockSpec(memory_space=pl.ANY),
        scratch_shapes=[pltpu.VMEM((MOE_RING, bm * ns, LANES), U32),
                        pltpu.VMEM((MOE_RING, bm * ns, LANES), U32),
                        pltpu.VMEM((d, de), BF16), pltpu.VMEM((d, de), BF16), pltpu.VMEM((de, d), BF16),
                        pltpu.SemaphoreType.DMA((MOE_RING,)), pltpu.SemaphoreType.DMA((MOE_RING,))],
    )
    return pl.pallas_call(
        kern,
        grid_spec=grid_spec,
        out_shape=jax.ShapeDtypeStruct((n_assign + 3 * bm, ns, LANES), U32),
        compiler_params=_params(1),
        name="moe",
    )(jnp.concatenate([block_e, n_used.reshape(1)]), slot_tok, slot_tok, slot_tok, dst_ext, dst_ext, v3, wg, wu, wd)


def _final_kernel(w_ref, v_ref, h1_ref, gtm_ref, gfin_ref, sg_ref, su_ref, sd_ref, *rest, top_k):
    yk_refs, o_ref, lo_ref, hi_ref = rest[:top_k], rest[top_k], rest[top_k + 1], rest[top_k + 2]
    tm = h1_ref.shape[0]
    ns = yk_refs[0].shape[1]

    def combine(t, carry):
        lo, hi = _unpack_pairs(yk_refs[0][t])
        acc_lo, acc_hi = w_ref[t, 0] * lo, w_ref[t, 0] * hi
        for k in range(1, top_k):
            lo, hi = _unpack_pairs(yk_refs[k][t])
            acc_lo, acc_hi = acc_lo + w_ref[t, k] * lo, acc_hi + w_ref[t, k] * hi
        rows = pl.ds(pl.multiple_of(t * ns, ns), ns)
        lo_ref[rows, :] = acc_lo
        hi_ref[rows, :] = acc_hi
        return carry

    lax.fori_loop(0, tm, combine, 0, unroll=8)
    routed = jnp.concatenate([_load_slabs(lo_ref, tm, ns), _load_slabs(hi_ref, tm, ns)], axis=1)

    vb = jnp.concatenate(_unpack_pairs(_load_slabs(v_ref, tm, ns)), axis=1).astype(BF16)
    g = _dot(vb, sg_ref[...])
    u = _dot(vb, su_ref[...])
    y = routed + _dot(((g * _sigmoid(g)) * u).astype(BF16), sd_ref[...])
    h2 = h1_ref[...] + gtm_ref[0] * y
    o_ref[...] = h2 * lax.rsqrt(jnp.mean(h2 * h2, axis=-1, keepdims=True) + EPS) * gfin_ref[...]


def final(vp, h1, yk3, w_tk, gtm, gfin, sg, su, sd, cfg):
    t, d = h1.shape
    tm = min(cfg.final_tm, cfg.seq)
    per_b = cfg.seq // tm
    nt = t // tm
    ds = sg.shape[1]
    nslab = d // (2 * LANES)
    const = lambda i: (0, 0)
    kern = functools.partial(_final_kernel, top_k=cfg.top_k)
    plane = lambda k: pl.BlockSpec((tm, nslab, LANES), lambda i: (k * nt + i, 0, 0))
    return pl.pallas_call(
        kern,
        grid=(nt,),
        in_specs=[pl.BlockSpec((tm, cfg.top_k), lambda i: (i, 0), memory_space=pltpu.SMEM),
                  pl.BlockSpec((tm * nslab, LANES), lambda i: (i, 0)),
                  pl.BlockSpec((tm, d), lambda i: (i, 0)),
                  pl.BlockSpec((1, 1, d), lambda i: (i // per_b, 0, 0)),
                  pl.BlockSpec((1, d), const),
                  pl.BlockSpec((d, ds), const),
                  pl.BlockSpec((d, ds), const),
                  pl.BlockSpec((ds, d), const)] + [plane(k) for k in range(cfg.top_k)],
        out_specs=pl.BlockSpec((tm, d), lambda i: (i, 0)),
        out_shape=jax.ShapeDtypeStruct((t, d), F32),
        scratch_shapes=[pltpu.VMEM((tm * nslab, LANES), F32), pltpu.VMEM((tm * nslab, LANES), F32)],
        compiler_params=_params(1),
        name="final",
    )(w_tk, vp, h1, gtm, gfin, sg, su, sd, *([yk3] * cfg.top_k))


def _t5_bucket(rel, cfg):
    nb = cfg.t5_buckets // 2
    ret = (rel > 0).astype(np.int32) * nb
    n = np.abs(rel)
    max_exact = nb // 2
    large = max_exact + (np.log(np.maximum(n, 1) / max_exact) / math.log(cfg.t5_max_dist / max_exact)
                         * (nb - max_exact)).astype(np.int32)
    large = np.minimum(large, nb - 1)
    return (ret + np.where(n < max_exact, n, large)).astype(np.int32)


def _toeplitz(vec, n):
    lead = vec.shape[:-1]
    flat = jnp.tile(vec, (1,) * len(lead) + (n,))[..., :n * (2 * n - 1)]
    return flat.reshape(lead + (n, 2 * n - 1))[..., :n]


def _rel_offsets(n):
    m = np.arange(2 * n)
    return np.where(m < n, m, m - 2 * n)


def _bias_tables_a(t5_bias, cfg):
    tile = cfg.attn_tile
    assert tile + 1 >= cfg.t5_max_dist and tile % cfg.chunk == 0
    ql = np.arange(tile)[:, None]
    kl = np.arange(tile)[None, :]
    allowed = (kl // cfg.chunk) <= (ql // cfg.chunk)
    r = _rel_offsets(tile)
    t5 = t5_bias.astype(F32)
    diag = _toeplitz(t5[_t5_bucket(r, cfg)].T, tile)
    prev = _toeplitz(t5[_t5_bucket(r - tile, cfg)].T, tile)
    far = jnp.broadcast_to(t5[_t5_bucket(np.array(-2 * tile), cfg)][:, None, None], diag.shape)
    return jnp.stack([jnp.where(allowed[None], diag, NEG), prev, far], axis=1)


def _bias_tables_b(rel_table, cfg):
    tile = cfg.attn_tile
    assert (cfg.left_chunks * cfg.chunk) % tile == 0 and tile % cfg.chunk == 0
    nwin = (cfg.left_chunks * cfg.chunk) // tile + 1
    ql = np.arange(tile)[:, None]
    kl = np.arange(tile)[None, :]
    r = _rel_offsets(tile)
    rt = rel_table.astype(F32)
    tabs = []
    for j in range(nwin):
        off = (j - (nwin - 1)) * tile
        idx = np.clip(r + off, -cfg.rel_clip, cfg.rel_clip) + cfg.rel_clip
        dchunk = ql // cfg.chunk - (kl + off) // cfg.chunk
        band = (dchunk >= 0) & (dchunk <= cfg.left_chunks)
        tabs.append(jnp.where(band[None], _toeplitz(rt[idx].T, tile), NEG))
    tabs.append(jnp.full_like(tabs[0], NEG))
    return jnp.stack(tabs, axis=1)


def _block_tables(counts, n_assign, cfg):
    e_ = cfg.n_experts
    bm = cfg.moe_rows
    nb = -(-(n_assign + e_ * (bm - 1)) // bm)
    pcounts = (counts + bm - 1) // bm * bm
    pends = jnp.cumsum(pcounts)
    pstarts = pends - pcounts
    start_b = jnp.arange(nb, dtype=jnp.int32) * bm
    block_e = jnp.minimum(jnp.sum((pends[None, :] <= start_b[:, None]).astype(jnp.int32), axis=1), e_ - 1)
    mine = jnp.arange(e_, dtype=jnp.int32)[None, :] == block_e[:, None]
    pstart_b = jnp.sum(jnp.where(mine, pstarts[None, :], 0), axis=1)
    count_b = jnp.sum(jnp.where(mine, counts[None, :], 0), axis=1)
    n_used = (pends[e_ - 1] // bm).astype(jnp.int32)
    return block_e.astype(jnp.int32), (start_b - pstart_b).astype(jnp.int32), count_b.astype(jnp.int32), n_used


def _forward(cfg, x, c, w_ada, b_ada, g_attn, w_in, lambda_qk, t5_bias, rel_bias_b, w_up_a, w_up_b, w_o,
             g_moe, w_router, router_bias, w_exp_gate, w_exp_up, w_exp_down,
             w_sh_gate, w_sh_up, w_sh_down, g_final):
    b_, s_, d_ = x.shape
    t_ = b_ * s_
    l = 0
    qk_a = cfg.ha * 2 * cfg.da_qk
    width_a = cfg.ha * 2 * cfg.da_qk
    width_b = cfg.hb * cfg.db
    in_cols = w_in.shape[2]

    rows = max(16, b_)
    c_pad = jnp.zeros((rows, d_), F32).at[:b_].set(c)
    mod = ada_mod(c_pad, w_ada[l], b_ada[l], cfg)[:b_]
    sh_a, sc_a, gt_a, sh_m, sc_m, gt_m = [m.reshape(b_, 1, d_) for m in jnp.split(mod, cfg.n_mod, axis=-1)]

    colscale = np.ones((1, in_cols), np.float32)
    colscale[0, :qk_a] = cfg.da_qk ** -0.5 * LOG2E
    qb0 = 2 * qk_a + width_a
    colscale[0, qb0:qb0 + width_b] = cfg.db ** -0.5 * LOG2E
    x2 = x.reshape(t_, d_)
    proj = in_proj(x2, g_attn[l].reshape(1, d_), sc_a, sh_a, w_in[l].astype(BF16), jnp.asarray(colscale), cfg)

    lq = lambda_qk[l].astype(F32)
    lam = jnp.exp(jnp.sum(lq[0] * lq[1])) - jnp.exp(jnp.sum(lq[2] * lq[3])) + cfg.lam_init
    ya = attn_a(proj, lam.reshape(1), _bias_tables_a(t5_bias * LOG2E, cfg), cfg)
    yb = attn_b(proj, _bias_tables_b(rel_bias_b[l] * LOG2E, cfg), cfg)

    ne_pad = max(LANES, cfg.n_experts)
    wrt = jnp.zeros((ne_pad, d_), F32).at[:cfg.n_experts].set(w_router[l].T)
    wrh, wrl = _split_bf16(wrt)
    h1, v, lg = merge(ya, yb, proj, x2, gt_a, sc_m, sh_m, g_moe[l].reshape(1, d_),
                      w_up_a[l].astype(BF16), w_up_b[l].astype(BF16), w_o[l].astype(BF16), wrh, wrl, cfg)

    tc = min(cfg.route_tc, t_)
    rb = jnp.zeros((ne_pad, tc), F32).at[:cfg.n_experts].set(
        jnp.broadcast_to(router_bias[l].astype(F32)[:, None], (cfg.n_experts, tc)))
    w_kt, lin, kin, gend_raw = route(lg, rb, cfg)
    gps, n_grp, nx = tc // LANES, t_ // LANES, cfg.n_experts
    assert n_grp <= LANES
    per_group = lambda a: jnp.pad(a.transpose(1, 0, 2).reshape(nx, n_grp), ((0, 0), (0, LANES - n_grp)),
                                  constant_values=1e9)
    gend, gstart = per_group(gend_raw[:, :, :gps]), per_group(gend_raw[:, :, gps:2 * gps])
    by_group = lambda a: jnp.pad(a.reshape(nx, n_grp, LANES), ((0, 0), (0, LANES - n_grp), (0, 0)))
    counts = gend[:, n_grp - 1].astype(jnp.int32)
    block_e, r0, count_b, n_used = _block_tables(counts, t_ * cfg.top_k, cfg)
    slot_tok, slot_dst = slots(block_e, r0, count_b, gstart, gend, by_group(lin), by_group(kin), cfg, t_)

    yk = moe(v, block_e, n_used, slot_tok, slot_dst, w_exp_gate[l], w_exp_up[l], w_exp_down[l], cfg)
    out = final(v, h1, yk, w_kt.T, gt_m, g_final.reshape(1, d_),
                w_sh_gate[l].astype(BF16), w_sh_up[l].astype(BF16), w_sh_down[l].astype(BF16), cfg)
    return out.reshape(b_, s_, d_)


def kernel(x, c, w_ada, b_ada, g_attn, w_in, lambda_qk, t5_bias, rel_bias_b, w_up_a, w_up_b, w_o, g_moe,
           w_router, router_bias, w_exp_gate, w_exp_up, w_exp_down, w_sh_gate, w_sh_up, w_sh_down, g_final):
    return _forward(Cfg(), x, c, w_ada, b_ada, g_attn, w_in, lambda_qk, t5_bias, rel_bias_b, w_up_a, w_up_b,
                    w_o, g_moe, w_router, router_bias, w_exp_gate, w_exp_up, w_exp_down,
                    w_sh_gate, w_sh_up, w_sh_down, g_final)
```

```python
import functools
import math
from typing import NamedTuple

import numpy as np
import jax
import jax.numpy as jnp
from jax import lax
from jax.experimental import pallas as pl
from jax.experimental.pallas import tpu as pltpu

F32 = jnp.float32
BF16 = jnp.bfloat16
NEG = -1e30
EPS = 1e-6
HEAD_EPS = 1e-5
LOG2E = 1.4426950408889634
LANES = 128
VMEM_LIMIT = 56 * 1024 * 1024


class Cfg(NamedTuple):
    batch: int = 8
    seq: int = 2048
    d_model: int = 2048
    chunk: int = 64
    ha: int = 8
    da_qk: int = 64
    hb: int = 8
    db: int = 128
    left_chunks: int = 8
    rel_clip: int = 256
    t5_buckets: int = 32
    t5_max_dist: int = 128
    n_experts: int = 64
    top_k: int = 8
    n_groups: int = 8
    topk_groups: int = 4
    d_expert: int = 512
    d_shared: int = 512
    routed_scale: float = 2.5
    n_mod: int = 6
    lam_init: float = 0.8 - 0.6 * math.exp(-0.3 * 0)
    attn_tile: int = 256
    moe_rows: int = 256
    proj_tm: int = 1024
    proj_tn: int = 1024
    merge_tm: int = 256
    route_tc: int = 512
    final_tm: int = 256
    ada_tn: int = 1024


def _sigmoid(x):
    return 1.0 / (1.0 + jnp.exp(-x))


def _split_bf16(x):
    hi = x.astype(BF16)
    lo = (x - hi.astype(F32)).astype(BF16)
    return hi, lo


def _dot(a, b):
    return jnp.dot(a, b, preferred_element_type=F32)


def _dot_t(a, b):
    return lax.dot_general(a, b, (((1,), (1,)), ((), ())), preferred_element_type=F32)


U32 = jnp.uint32
HI_MASK = 0xFFFF0000


def _pack_pairs(x):
    n = x.shape[1] // 2
    bits = pltpu.bitcast(x.astype(BF16).astype(F32), U32)
    return (bits[:, :n] >> 16) | (bits[:, n:] & U32(HI_MASK))


def _unpack_pairs(p):
    return pltpu.bitcast(p << 16, F32), pltpu.bitcast(p & U32(HI_MASK), F32)


def _load_slabs(ref, m, ns):
    return jnp.concatenate([ref[pl.ds(j, m, stride=ns), :] for j in range(ns)], axis=1)


def _store_slabs(ref, val, ns):
    m = val.shape[0]
    for j in range(ns):
        ref[pl.ds(j, m, stride=ns), :] = val[:, j * LANES:(j + 1) * LANES]


def _params(n_axes):
    return pltpu.CompilerParams(dimension_semantics=("arbitrary",) * n_axes,
                                vmem_limit_bytes=VMEM_LIMIT)


def _ada_kernel(c_ref, w_ref, b_ref, o_ref):
    c = c_ref[...]
    s_hi, s_lo = _split_bf16(c * _sigmoid(c))
    w_hi, w_lo = _split_bf16(w_ref[...])
    acc = _dot(s_hi, w_hi) + _dot(s_hi, w_lo) + _dot(s_lo, w_hi)
    o_ref[...] = acc + b_ref[...]


def ada_mod(c_pad, w_ada, b_ada, cfg):
    rows, d = c_pad.shape
    n = w_ada.shape[1]
    tn = min(cfg.ada_tn, n)
    return pl.pallas_call(
        _ada_kernel,
        grid=(n // tn,),
        in_specs=[pl.BlockSpec((rows, d), lambda j: (0, 0)),
                  pl.BlockSpec((d, tn), lambda j: (0, j)),
                  pl.BlockSpec((1, tn), lambda j: (0, j))],
        out_specs=pl.BlockSpec((rows, tn), lambda j: (0, j)),
        out_shape=jax.ShapeDtypeStruct((rows, n), F32),
        compiler_params=_params(1),
        name="ada_mod",
    )(c_pad, w_ada, b_ada.reshape(1, n))


def _inproj_kernel(x_ref, g_ref, sc_ref, sh_ref, w_ref, cs_ref, o_ref, u_ref):
    @pl.when(pl.program_id(1) == 0)
    def _():
        x = x_ref[...]
        y = x * lax.rsqrt(jnp.mean(x * x, axis=-1, keepdims=True) + EPS) * g_ref[...]
        u_ref[...] = (y * (1.0 + sc_ref[0]) + sh_ref[0]).astype(BF16)

    acc = _dot(u_ref[...], w_ref[...])
    o_ref[...] = (acc * cs_ref[...]).astype(BF16)


def in_proj(x2, g, sc, sh, w_bf, colscale, cfg):
    t, d = x2.shape
    n = w_bf.shape[1]
    tm = min(cfg.proj_tm, cfg.seq)
    tn = min(cfg.proj_tn, n)
    per_b = cfg.seq // tm
    return pl.pallas_call(
        _inproj_kernel,
        grid=(t // tm, n // tn),
        in_specs=[pl.BlockSpec((tm, d), lambda i, j: (i, 0)),
                  pl.BlockSpec((1, d), lambda i, j: (0, 0)),
                  pl.BlockSpec((1, 1, d), lambda i, j: (i // per_b, 0, 0)),
                  pl.BlockSpec((1, 1, d), lambda i, j: (i // per_b, 0, 0)),
                  pl.BlockSpec((d, tn), lambda i, j: (0, j)),
                  pl.BlockSpec((1, tn), lambda i, j: (0, j))],
        out_specs=pl.BlockSpec((tm, tn), lambda i, j: (i, j)),
        out_shape=jax.ShapeDtypeStruct((t, n), BF16),
        scratch_shapes=[pltpu.VMEM((tm, d), BF16)],
        compiler_params=_params(2),
        name="in_proj",
    )(x2, g, sc, sh, w_bf, colscale)


def _attn_a_kernel(lam_ref, qlo_ref, qhi_ref, k_ref, v_ref, tb_ref, olo_ref, ohi_ref,
                   s_ref, m_ref, l_ref, o_ref, *, tile, half, nq, out_scale):
    p = pl.program_id(2)
    hq = nq - 1 - p
    nvis = nq + 1
    ninf = -jnp.inf

    def stacked(q):
        lane = lax.broadcasted_iota(jnp.int32, q.shape, 1)
        zero = jnp.zeros_like(q)
        return jnp.concatenate([jnp.where(lane < half, q, zero), jnp.where(lane >= half, q, zero)], axis=0)

    qq_lo, qq_hi = stacked(qlo_ref[...]), stacked(qhi_ref[...])

    def fold(x, op):
        r = x[:, :LANES]
        for c in range(1, tile // LANES):
            r = op(r, x[:, c * LANES:(c + 1) * LANES])
        return r

    def pick(i, lo, hi):
        return jnp.where(i <= p, lo, hi)

    def key_tile(i):
        return jnp.where(i <= p, i, i - p - 1)

    def key_rows(i):
        return pl.ds(pl.multiple_of(key_tile(i) * tile, tile), tile)

    for i in range(nvis):
        bias = tb_ref[0, jnp.minimum(pick(i, p, hq) - key_tile(i), 2)]
        s = _dot_t(pick(i, qq_lo, qq_hi), k_ref[key_rows(i), :]) + jnp.concatenate([bias, bias], axis=0)
        s_ref[i] = s
        m_ref[i] = fold(s, jnp.maximum)

    def reduce_tiles(ref, op, init):
        lo = jnp.full(ref.shape[1:], init, F32)
        hi = lo
        for i in range(nvis):
            lo = op(lo, jnp.where(i <= p, ref[i], init))
            hi = op(hi, jnp.where(i <= p, init, ref[i]))
        return lo, hi

    m_lo, m_hi = reduce_tiles(m_ref, jnp.maximum, ninf)
    m_lo = jnp.max(m_lo, axis=-1, keepdims=True)
    m_hi = jnp.max(m_hi, axis=-1, keepdims=True)
    for i in range(nvis):
        e = jnp.exp2(s_ref[i] - pick(i, m_lo, m_hi))
        s_ref[i] = e
        l_ref[i] = fold(e, jnp.add)

    l_lo, l_hi = reduce_tiles(l_ref, jnp.add, 0.0)
    lam = lam_ref[0]

    def coefs(l):
        inv = 1.0 / jnp.sum(l, axis=-1, keepdims=True)
        return inv[:tile], lam * inv[tile:]

    (c0_lo, c1_lo), (c0_hi, c1_hi) = coefs(l_lo), coefs(l_hi)
    for i in range(nvis):
        e = s_ref[i]
        a = e[:tile] * pick(i, c0_lo, c0_hi) - e[tile:] * pick(i, c1_lo, c1_hi)
        o_ref[i] = _dot(a.astype(BF16), v_ref[key_rows(i), :])

    o_lo, o_hi = reduce_tiles(o_ref, jnp.add, 0.0)
    for o, out in ((o_lo, olo_ref), (o_hi, ohi_ref)):
        out[...] = (o * lax.rsqrt(jnp.mean(o * o, axis=-1, keepdims=True) + HEAD_EPS) * out_scale).astype(BF16)


def attn_a(proj, lam, tb, cfg):
    tile = cfg.attn_tile
    s = cfg.seq
    nq = s // tile
    assert nq % 2 == 0
    npair = nq // 2
    dv = 2 * cfg.da_qk
    assert dv == LANES
    kcol = cfg.ha
    vcol = 2 * cfg.ha
    kern = functools.partial(_attn_a_kernel, tile=tile, half=cfg.da_qk, nq=nq, out_scale=1.0 - cfg.lam_init)
    half_shape = jax.ShapeDtypeStruct((cfg.batch * npair * tile, cfg.ha * dv), BF16)
    y_lo, y_hi = pl.pallas_call(
        kern,
        grid=(cfg.batch, cfg.ha, npair),
        in_specs=[pl.BlockSpec(memory_space=pltpu.SMEM),
                  pl.BlockSpec((tile, LANES), lambda b, h, p: (b * nq + p, h)),
                  pl.BlockSpec((tile, LANES), lambda b, h, p: (b * nq + nq - 1 - p, h)),
                  pl.BlockSpec((s, LANES), lambda b, h, p: (b, kcol + h)),
                  pl.BlockSpec((s, LANES), lambda b, h, p: (b, vcol + h)),
                  pl.BlockSpec((1, 3, tile, tile), lambda b, h, p: (h, 0, 0, 0))],
        out_specs=[pl.BlockSpec((tile, LANES), lambda b, h, p: (b * npair + p, h)),
                   pl.BlockSpec((tile, LANES), lambda b, h, p: (b * npair + npair - 1 - p, h))],
        out_shape=[half_shape, half_shape],
        scratch_shapes=[pltpu.VMEM((nq + 1, 2 * tile, tile), F32),
                        pltpu.VMEM((nq + 1, 2 * tile, LANES), F32),
                        pltpu.VMEM((nq + 1, 2 * tile, LANES), F32),
                        pltpu.VMEM((nq + 1, tile, LANES), F32)],
        compiler_params=_params(3),
        name="attn_a",
    )(lam, proj, proj, proj, proj, tb)
    w = cfg.ha * dv
    rows = npair * tile
    return jnp.concatenate([y_lo.reshape(cfg.batch, rows, w), y_hi.reshape(cfg.batch, rows, w)],
                           axis=1).reshape(cfg.batch * s, w)


ATTN_B_TILES = 2


def _attn_b_kernel(q_ref, k_ref, v_ref, tb_ref, o_ref, *, tile, nwin):
    for u in range(ATTN_B_TILES):
        qi = pl.program_id(2) * ATTN_B_TILES + u
        q = q_ref[u * tile:(u + 1) * tile, :]
        ss, vs = [], []
        for j in range(nwin):
            kb = qi - (nwin - 1) + j
            start = pl.multiple_of(jnp.maximum(kb, 0) * tile, tile)
            ss.append(_dot_t(q, k_ref[pl.ds(start, tile), :]) + tb_ref[0, jnp.where(kb >= 0, j, nwin)])
            vs.append(v_ref[pl.ds(start, tile), :])
        m = jnp.max(ss[0], axis=-1, keepdims=True)
        for s in ss[1:]:
            m = jnp.maximum(m, jnp.max(s, axis=-1, keepdims=True))
        ps = [jnp.exp2(s - m) for s in ss]
        l = jnp.sum(ps[0], axis=-1, keepdims=True)
        for p in ps[1:]:
            l = l + jnp.sum(p, axis=-1, keepdims=True)
        acc = _dot(ps[0].astype(BF16), vs[0])
        for p, v in zip(ps[1:], vs[1:]):
            acc = acc + _dot(p.astype(BF16), v)
        o_ref[u * tile:(u + 1) * tile, :] = (acc * (1.0 / l)).astype(BF16)


def attn_b(proj, tb, cfg):
    t = proj.shape[0]
    tile = cfg.attn_tile
    s = cfg.seq
    nq = s // tile
    assert cfg.db == LANES and nq % ATTN_B_TILES == 0
    nstep = nq // ATTN_B_TILES
    rows = ATTN_B_TILES * tile
    nwin = (cfg.left_chunks * cfg.chunk) // tile + 1
    base = (2 * cfg.ha * 2 * cfg.da_qk + cfg.ha * 2 * cfg.da_qk) // LANES
    kern = functools.partial(_attn_b_kernel, tile=tile, nwin=nwin)
    return pl.pallas_call(
        kern,
        grid=(cfg.batch, cfg.hb, nstep),
        in_specs=[pl.BlockSpec((rows, LANES), lambda b, h, i: (b * nstep + i, base + h)),
                  pl.BlockSpec((s, LANES), lambda b, h, i: (b, base + cfg.hb + h)),
                  pl.BlockSpec((s, LANES), lambda b, h, i: (b, base + 2 * cfg.hb + h)),
                  pl.BlockSpec((1, nwin + 1, tile, tile), lambda b, h, i: (h, 0, 0, 0))],
        out_specs=pl.BlockSpec((rows, LANES), lambda b, h, i: (b * nstep + i, h)),
        out_shape=jax.ShapeDtypeStruct((t, cfg.hb * cfg.db), BF16),
        compiler_params=_params(3),
        name="attn_b",
    )(proj, proj, proj, tb)


def _merge_kernel(ya_ref, yb_ref, ga_ref, gb_ref, x_ref, gta_ref, scm_ref, shm_ref, gmoe_ref,
                  wua_ref, wub_ref, wo_ref, wrh_ref, wrl_ref, h1_ref, v_ref, lg_ref):
    a = _dot(ya_ref[...], wua_ref[...])
    b = _dot(yb_ref[...], wub_ref[...])
    merged = _sigmoid(ga_ref[...].astype(F32)) * a + _sigmoid(gb_ref[...].astype(F32)) * b
    att = _dot(merged.astype(BF16), wo_ref[...])
    h1 = x_ref[...] + gta_ref[0] * att
    h1_ref[...] = h1
    y = h1 * lax.rsqrt(jnp.mean(h1 * h1, axis=-1, keepdims=True) + EPS) * gmoe_ref[...]
    v = y * (1.0 + scm_ref[0]) + shm_ref[0]
    _store_slabs(v_ref, _pack_pairs(v), v_ref.shape[0] // v.shape[0])
    v_hi, v_lo = _split_bf16(v)
    wrh = wrh_ref[...]
    lg_ref[...] = _dot_t(wrh, v_hi) + _dot_t(wrh, v_lo) + _dot_t(wrl_ref[...], v_hi)


def merge(ya, yb, proj, x2, gta, scm, shm, gmoe, wua, wub, wo, wrh, wrl, cfg):
    t, d = x2.shape
    tm = min(cfg.merge_tm, cfg.seq)
    per_b = cfg.seq // tm
    wa = ya.shape[1]
    wb = yb.shape[1]
    gate0 = (proj.shape[1] - 2 * d) // d
    ne = wrh.shape[0]
    ns = d // (2 * LANES)
    const = lambda i: (0, 0)
    row3 = lambda i: (i // per_b, 0, 0)
    return pl.pallas_call(
        _merge_kernel,
        grid=(t // tm,),
        in_specs=[pl.BlockSpec((tm, wa), lambda i: (i, 0)),
                  pl.BlockSpec((tm, wb), lambda i: (i, 0)),
                  pl.BlockSpec((tm, d), lambda i: (i, gate0)),
                  pl.BlockSpec((tm, d), lambda i: (i, gate0 + 1)),
                  pl.BlockSpec((tm, d), lambda i: (i, 0)),
                  pl.BlockSpec((1, 1, d), row3),
                  pl.BlockSpec((1, 1, d), row3),
                  pl.BlockSpec((1, 1, d), row3),
                  pl.BlockSpec((1, d), const),
                  pl.BlockSpec((wa, d), const),
                  pl.BlockSpec((wb, d), const),
                  pl.BlockSpec((d, d), const),
                  pl.BlockSpec((ne, d), const),
                  pl.BlockSpec((ne, d), const)],
        out_specs=[pl.BlockSpec((tm, d), lambda i: (i, 0)),
                   pl.BlockSpec((tm * ns, LANES), lambda i: (i, 0)),
                   pl.BlockSpec((ne, tm), lambda i: (0, i))],
        out_shape=[jax.ShapeDtypeStruct((t, d), F32),
                   jax.ShapeDtypeStruct((t * ns, LANES), U32),
                   jax.ShapeDtypeStruct((ne, t), F32)],
        compiler_params=_params(1),
        name="merge",
    )(ya, yb, proj, proj, x2, gta, scm, shm, gmoe, wua, wub, wo, wrh, wrl)


def _route_kernel(lg_ref, rb_ref, tri_ref, w_ref, lin_ref, kin_ref, gend_ref, carry_ref,
                  *, n_exp, n_groups, topk_groups, top_k, scale):
    gsz = n_exp // n_groups
    tc = lg_ref.shape[1]
    ninf = -jnp.inf
    step = pl.program_id(0)

    @pl.when(step == 0)
    def _():
        carry_ref[...] = jnp.zeros(carry_ref.shape, F32)

    row = lax.broadcasted_iota(jnp.int32, (gsz, tc), 0).astype(F32)
    sc, bi = [], []
    for g in range(n_groups):
        s = _sigmoid(lg_ref[g * gsz:(g + 1) * gsz, :])
        sc.append(s)
        bi.append(s + rb_ref[g * gsz:(g + 1) * gsz, :])
    gs = []
    for x in bi:
        m1 = jnp.max(x, axis=0, keepdims=True)
        eq = x == m1
        cnt = jnp.sum(jnp.where(eq, 1.0, 0.0), axis=0, keepdims=True)
        second = jnp.max(jnp.where(eq, ninf, x), axis=0, keepdims=True)
        gs.append(m1 + jnp.where(cnt >= 2.0, m1, second))
    gmat = jnp.concatenate(gs, axis=0)
    grow = lax.broadcasted_iota(jnp.int32, gmat.shape, 0).astype(F32)
    gsel = jnp.zeros(gmat.shape, F32)
    for _ in range(topk_groups):
        m = jnp.max(gmat, axis=0, keepdims=True)
        idx = jnp.min(jnp.where(gmat == m, grow, float(n_groups)), axis=0, keepdims=True)
        hit = grow == idx
        gsel = jnp.where(hit, 1.0, gsel)
        gmat = jnp.where(hit, ninf, gmat)
    cur = [jnp.where(gsel[g:g + 1, :] > 0.5, bi[g], ninf) for g in range(n_groups)]
    ids = [row + float(g * gsz) for g in range(n_groups)]
    w_rows = []
    chosen = [jnp.zeros((gsz, tc), F32) for _ in range(n_groups)]
    kplus = [jnp.zeros((gsz, tc), F32) for _ in range(n_groups)]
    for k in range(top_k):
        mg = cur[0]
        for c in cur[1:]:
            mg = jnp.maximum(mg, c)
        m = jnp.max(mg, axis=0, keepdims=True)
        cand = jnp.where(cur[0] == m, ids[0], float(n_exp))
        for c, i in zip(cur[1:], ids[1:]):
            cand = jnp.minimum(cand, jnp.where(c == m, i, float(n_exp)))
        idx = jnp.min(cand, axis=0, keepdims=True)
        wsum = jnp.zeros((gsz, tc), F32)
        for g in range(n_groups):
            hit = ids[g] == idx
            wsum = wsum + jnp.where(hit, sc[g], 0.0)
            cur[g] = jnp.where(hit, ninf, cur[g])
            chosen[g] = jnp.where(hit, 1.0, chosen[g])
            kplus[g] = jnp.where(hit, float(k + 1), kplus[g])
        w_rows.append(jnp.sum(wsum, axis=0, keepdims=True))
    w = jnp.concatenate(w_rows, axis=0)
    w_ref[...] = w / jnp.sum(w, axis=0, keepdims=True) * scale

    onehot = jnp.concatenate(chosen, axis=0)
    incl = _dot(onehot.astype(BF16), tri_ref[...])
    lin_ref[...] = incl.astype(BF16)
    kin_ref[...] = jnp.concatenate(kplus, axis=0).astype(BF16)
    gps = tc // LANES
    lane = lax.broadcasted_iota(jnp.int32, (n_exp, LANES), 1)
    run = carry_ref[...]
    ends = jnp.zeros((n_exp, LANES), F32)
    for g in range(gps):
        ends = jnp.where(lane == gps + g, run, ends)
        run = run + incl[:, (g + 1) * LANES - 1:(g + 1) * LANES]
        ends = jnp.where(lane == g, run, ends)
    carry_ref[...] = run
    gend_ref[0] = ends


def route(lg, rb, cfg):
    ne, t = lg.shape
    nx = cfg.n_experts
    tc = min(cfg.route_tc, t)
    nsteps = t // tc
    tri = jnp.asarray(np.kron(np.eye(tc // LANES), np.triu(np.ones((LANES, LANES)))), BF16)
    kern = functools.partial(_route_kernel, n_exp=cfg.n_experts, n_groups=cfg.n_groups,
                             topk_groups=cfg.topk_groups, top_k=cfg.top_k, scale=cfg.routed_scale)
    return pl.pallas_call(
        kern,
        grid=(nsteps,),
        in_specs=[pl.BlockSpec((ne, tc), lambda i: (0, i)),
                  pl.BlockSpec((ne, tc), lambda i: (0, 0)),
                  pl.BlockSpec((tc, tc), lambda i: (0, 0))],
        out_specs=[pl.BlockSpec((cfg.top_k, tc), lambda i: (0, i)),
                   pl.BlockSpec((nx, tc), lambda i: (0, i)),
                   pl.BlockSpec((nx, tc), lambda i: (0, i)),
                   pl.BlockSpec((1, nx, LANES), lambda i: (i, 0, 0))],
        out_shape=[jax.ShapeDtypeStruct((cfg.top_k, t), F32),
                   jax.ShapeDtypeStruct((nx, t), BF16),
                   jax.ShapeDtypeStruct((nx, t), BF16),
                   jax.ShapeDtypeStruct((nsteps, nx, LANES), F32)],
        scratch_shapes=[pltpu.VMEM((cfg.n_experts, LANES), F32)],
        compiler_params=_params(1),
        name="route",
    )(lg, rb, tri)


SLOT_BLOCKS = 4


def _slots_kernel(be_ref, r0_ref, cnt_ref, gstart_ref, gend_ref, lin_ref, kin_ref, tok_ref, dst_ref,
                  *, bm, top_k, n_tok, n_assign):
    step = pl.program_id(0)
    rows = lax.broadcasted_iota(jnp.int32, (bm, LANES), 0)
    lane = lax.broadcasted_iota(jnp.int32, (bm, LANES), 1).astype(F32)
    ones = jnp.ones((16, LANES), BF16)
    gids = lax.broadcasted_iota(jnp.int32, (16, LANES), 1).astype(F32).astype(BF16)
    lane_row = lax.broadcasted_iota(jnp.int32, (1, bm), 1)
    for u in range(SLOT_BLOCKS):
        b = step * SLOT_BLOCKS + u
        e = be_ref[b]
        r0 = r0_ref[b]
        cnt = cnt_ref[b]
        r = (rows + r0).astype(F32)
        gstart = gstart_ref[pl.ds(e, 1), :]
        in_group = jnp.logical_and(gstart <= r, r < gend_ref[pl.ds(e, 1), :])
        group = jnp.where(in_group, 1.0, 0.0).astype(BF16)
        want = r - jnp.sum(jnp.where(in_group, gstart, 0.0), axis=1, keepdims=True) + 1.0
        kin = _dot(group, kin_ref[e])
        hit = jnp.logical_and(_dot(group, lin_ref[e]) == want, kin > 0.5)
        t_loc = _dot_t(ones, jnp.where(hit, lane, 0.0).astype(BF16))[0:1]
        kp1 = _dot_t(ones, jnp.where(hit, kin, 0.0).astype(BF16))[0:1]
        gid = _dot_t(gids, group)[0:1]
        tok = (gid * float(LANES) + t_loc).astype(jnp.int32)
        kk = kp1.astype(jnp.int32) - 1
        valid = (lane_row + r0) < cnt
        dummy = n_assign + lax.rem(b, 2) * bm + lane_row
        tok_ref[u] = jnp.where(valid, tok, lax.rem(b * bm + lane_row, n_tok))
        dst_ref[u] = jnp.where(valid, kk * n_tok + tok, dummy)


def slots(block_e, r0, cntb, gstart, gend, lin, kin, cfg, n_tok):
    nb = block_e.shape[0]
    bm = cfg.moe_rows
    assert nb % SLOT_BLOCKS == 0 and gstart.shape[1] == LANES and lin.shape[1:] == (LANES, LANES)
    kern = functools.partial(_slots_kernel, bm=bm, top_k=cfg.top_k, n_tok=n_tok, n_assign=n_tok * cfg.top_k)
    whole = lambda a: pl.BlockSpec(a.shape, lambda i, be, r0, cn: (0,) * a.ndim)
    out_blk = pl.BlockSpec((SLOT_BLOCKS, 1, bm), lambda i, be, r0, cn: (i, 0, 0))
    grid_spec = pltpu.PrefetchScalarGridSpec(
        num_scalar_prefetch=3,
        grid=(nb // SLOT_BLOCKS,),
        in_specs=[whole(gstart), whole(gend), whole(lin), whole(kin)],
        out_specs=[out_blk, out_blk],
    )
    return pl.pallas_call(
        kern,
        grid_spec=grid_spec,
        out_shape=[jax.ShapeDtypeStruct((nb, 1, bm), jnp.int32), jax.ShapeDtypeStruct((nb, 1, bm), jnp.int32)],
        compiler_params=_params(1),
        name="slots",
    )(block_e, r0, cntb, gstart, gend, lin, kin)


MOE_RING = 3


def _moe_kernel(be_ref, tok0_ref, tok1_ref, tokn_ref, dstp_ref, dstc_ref, v_hbm, wg_ref, wu_ref, wd_ref, yk_hbm,
                xbuf, ybuf, wgb, wub, wdb, gsem, ssem, *, bm, n_assign):
    b = pl.program_id(0)
    nb = pl.num_programs(0)
    slot = lax.rem(b, MOE_RING)
    ns = v_hbm.shape[1]

    def gather_copy(t, r, s):
        return pltpu.make_async_copy(v_hbm.at[t], xbuf.at[s, pl.ds(r * ns, ns), :], gsem.at[s])

    def scatter_copy(r, d, s):
        return pltpu.make_async_copy(ybuf.at[s, pl.ds(r * ns, ns), :], yk_hbm.at[d], ssem.at[s])

    def wait_gather(s):
        for r in range(bm):
            gather_copy(0, r, s).wait()

    def wait_scatter(s):
        for r in range(bm):
            scatter_copy(r, 0, s).wait()

    @pl.when(b == 0)
    def _():
        ybuf[...] = jnp.zeros(ybuf.shape, U32)
        for s in range(2):
            for r in range(bm):
                scatter_copy(r, n_assign + s * bm + r, s).start(priority=r % 2)
        for r in range(bm):
            gather_copy(tok0_ref[0, 0, r], r, 0).start(priority=r % 2)
        for r in range(bm):
            gather_copy(tok1_ref[0, 0, r], r, 1).start(priority=r % 2)

    e = be_ref[b]
    eprev = be_ref[jnp.maximum(b - 1, 0)]

    @pl.when(jnp.logical_or(b == 0, e != eprev))
    def _():
        wgb[...] = wg_ref[0].astype(BF16)
        wub[...] = wu_ref[0].astype(BF16)
        wdb[...] = wd_ref[0].astype(BF16)

    def block_step(s):
        p = (s + 2) % MOE_RING
        wait_gather(s)
        wait_scatter(s)
        for r in range(bm):
            scatter_copy(r, dstp_ref[0, 0, r], p).start(priority=r % 2)
        for r in range(bm):
            gather_copy(tokn_ref[0, 0, r], r, p).start(priority=r % 2)
        lo, hi = _unpack_pairs(_load_slabs(xbuf.at[s], bm, ns))
        x = jnp.concatenate([lo, hi], axis=1).astype(BF16)
        g = _dot(x, wgb[...])
        u = _dot(x, wub[...])
        h = (g * _sigmoid(g)) * u
        _store_slabs(ybuf.at[s], _pack_pairs(_dot(h.astype(BF16), wdb[...])), ns)

    last = be_ref[nb] - 1
    for s in range(MOE_RING):
        pl.when(jnp.logical_and(slot == s, b <= last))(functools.partial(block_step, s))

    @pl.when(b == last)
    def _():
        def drain(s):
            p, q = (s + 2) % MOE_RING, (s + 1) % MOE_RING
            wait_scatter(q)
            for r in range(bm):
                scatter_copy(r, dstc_ref[0, 0, r], s).start(priority=r % 2)
            wait_gather(p)
            wait_gather(q)
            wait_scatter(p)
            wait_scatter(s)

        for s in range(MOE_RING):
            pl.when(slot == s)(functools.partial(drain, s))


def moe(vp, block_e, n_used, slot_tok, slot_dst, wg, wu, wd, cfg):
    d = wg.shape[1]
    ns = d // (2 * LANES)
    t = vp.shape[0] // ns
    bm = cfg.moe_rows
    v3 = vp.reshape(t, ns, LANES)
    nb = block_e.shape[0]
    assert nb >= 2
    de = wg.shape[2]
    n_assign = t * cfg.top_k
    first = (n_assign + 2 * bm + jnp.arange(bm, dtype=jnp.int32)).reshape(1, 1, bm)
    dst_ext = jnp.concatenate([first, slot_dst], axis=0)
    kern = functools.partial(_moe_kernel, bm=bm, n_assign=n_assign)
    smem_blk = lambda f: pl.BlockSpec((1, 1, bm), f, memory_space=pltpu.SMEM)
    grid_spec = pltpu.PrefetchScalarGridSpec(
        num_scalar_prefetch=1,
        grid=(nb,),
        in_specs=[smem_blk(lambda b, be: (0, 0, 0)),
                  smem_blk(lambda b, be: (1, 0, 0)),
                  smem_blk(lambda b, be: (jnp.minimum(b + 2, nb - 1), 0, 0)),
                  smem_blk(lambda b, be: (b, 0, 0)),
                  smem_blk(lambda b, be: (b + 1, 0, 0)),
                  pl.BlockSpec(memory_space=pl.ANY),
                  pl.BlockSpec((1, d, de), lambda b, be: (be[b], 0, 0)),
                  pl.BlockSpec((1, d, de), lambda b, be: (be[b], 0, 0)),
                  pl.BlockSpec((1, de, d), lambda b, be: (be[b], 0, 0))],
        out_specs=pl.BlockSpec(memory_space=pl.ANY),
        scratch_shapes=[pltpu.VMEM((MOE_RING, bm * ns, LANES), U32),
                        pltpu.VMEM((MOE_RING, bm * ns, LANES), U32),
                        pltpu.VMEM((d, de), BF16), pltpu.VMEM((d, de), BF16), pltpu.VMEM((de, d), BF16),
                        pltpu.SemaphoreType.DMA((MOE_RING,)), pltpu.SemaphoreType.DMA((MOE_RING,))],
    )
    return pl.pallas_call(
        kern,
        grid_spec=grid_spec,
        out_shape=jax.ShapeDtypeStruct((n_assign + 3 * bm, ns, LANES), U32),
        compiler_params=_params(1),
        name="moe",
    )(jnp.concatenate([block_e, n_used.reshape(1)]), slot_tok, slot_tok, slot_tok, dst_ext, dst_ext, v3, wg, wu, wd)


def _final_kernel(w_ref, v_ref, h1_ref, gtm_ref, gfin_ref, sg_ref, su_ref, sd_ref, *rest, top_k):
    yk_refs, o_ref, lo_ref, hi_ref = rest[:top_k], rest[top_k], rest[top_k + 1], rest[top_k + 2]
    tm = h1_ref.shape[0]
    ns = yk_refs[0].shape[1]

    def combine(t, carry):
        lo, hi = _unpack_pairs(yk_refs[0][t])
        acc_lo, acc_hi = w_ref[t, 0] * lo, w_ref[t, 0] * hi
        for k in range(1, top_k):
            lo, hi = _unpack_pairs(yk_refs[k][t])
            acc_lo, acc_hi = acc_lo + w_ref[t, k] * lo, acc_hi + w_ref[t, k] * hi
        rows = pl.ds(pl.multiple_of(t * ns, ns), ns)
        lo_ref[rows, :] = acc_lo
        hi_ref[rows, :] = acc_hi
        return carry

    lax.fori_loop(0, tm, combine, 0, unroll=8)
    routed = jnp.concatenate([_load_slabs(lo_ref, tm, ns), _load_slabs(hi_ref, tm, ns)], axis=1)

    vb = jnp.concatenate(_unpack_pairs(_load_slabs(v_ref, tm, ns)), axis=1).astype(BF16)
    g = _dot(vb, sg_ref[...])
    u = _dot(vb, su_ref[...])
    y = routed + _dot(((g * _sigmoid(g)) * u).astype(BF16), sd_ref[...])
    h2 = h1_ref[...] + gtm_ref[0] * y
    o_ref[...] = h2 * lax.rsqrt(jnp.mean(h2 * h2, axis=-1, keepdims=True) + EPS) * gfin_ref[...]


def final(vp, h1, yk3, w_tk, gtm, gfin, sg, su, sd, cfg):
    t, d = h1.shape
    tm = min(cfg.final_tm, cfg.seq)
    per_b = cfg.seq // tm
    nt = t // tm
    ds = sg.shape[1]
    nslab = d // (2 * LANES)
    const = lambda i: (0, 0)
    kern = functools.partial(_final_kernel, top_k=cfg.top_k)
    plane = lambda k: pl.BlockSpec((tm, nslab, LANES), lambda i: (k * nt + i, 0, 0))
    return pl.pallas_call(
        kern,
        grid=(nt,),
        in_specs=[pl.BlockSpec((tm, cfg.top_k), lambda i: (i, 0), memory_space=pltpu.SMEM),
                  pl.BlockSpec((tm * nslab, LANES), lambda i: (i, 0)),
                  pl.BlockSpec((tm, d), lambda i: (i, 0)),
                  pl.BlockSpec((1, 1, d), lambda i: (i // per_b, 0, 0)),
                  pl.BlockSpec((1, d), const),
                  pl.BlockSpec((d, ds), const),
                  pl.BlockSpec((d, ds), const),
                  pl.BlockSpec((ds, d), const)] + [plane(k) for k in range(cfg.top_k)],
        out_specs=pl.BlockSpec((tm, d), lambda i: (i, 0)),
        out_shape=jax.ShapeDtypeStruct((t, d), F32),
        scratch_shapes=[pltpu.VMEM((tm * nslab, LANES), F32), pltpu.VMEM((tm * nslab, LANES), F32)],
        compiler_params=_params(1),
        name="final",
    )(w_tk, vp, h1, gtm, gfin, sg, su, sd, *([yk3] * cfg.top_k))


def _t5_bucket(rel, cfg):
    nb = cfg.t5_buckets // 2
    ret = (rel > 0).astype(np.int32) * nb
    n = np.abs(rel)
    max_exact = nb // 2
    large = max_exact + (np.log(np.maximum(n, 1) / max_exact) / math.log(cfg.t5_max_dist / max_exact)
                         * (nb - max_exact)).astype(np.int32)
    large = np.minimum(large, nb - 1)
    return (ret + np.where(n < max_exact, n, large)).astype(np.int32)


def _toeplitz(vec, n):
    lead = vec.shape[:-1]
    flat = jnp.tile(vec, (1,) * len(lead) + (n,))[..., :n * (2 * n - 1)]
    return flat.reshape(lead + (n, 2 * n - 1))[..., :n]


def _rel_offsets(n):
    m = np.arange(2 * n)
    return np.where(m < n, m, m - 2 * n)


def _bias_tables_a(t5_bias, cfg):
    tile = cfg.attn_tile
    assert tile + 1 >= cfg.t5_max_dist and tile % cfg.chunk == 0
    ql = np.arange(tile)[:, None]
    kl = np.arange(tile)[None, :]
    allowed = (kl // cfg.chunk) <= (ql // cfg.chunk)
    r = _rel_offsets(tile)
    t5 = t5_bias.astype(F32)
    diag = _toeplitz(t5[_t5_bucket(r, cfg)].T, tile)
    prev = _toeplitz(t5[_t5_bucket(r - tile, cfg)].T, tile)
    far = jnp.broadcast_to(t5[_t5_bucket(np.array(-2 * tile), cfg)][:, None, None], diag.shape)
    return jnp.stack([jnp.where(allowed[None], diag, NEG), prev, far], axis=1)


def _bias_tables_b(rel_table, cfg):
    tile = cfg.attn_tile
    assert (cfg.left_chunks * cfg.chunk) % tile == 0 and tile % cfg.chunk == 0
    nwin = (cfg.left_chunks * cfg.chunk) // tile + 1
    ql = np.arange(tile)[:, None]
    kl = np.arange(tile)[None, :]
    r = _rel_offsets(tile)
    rt = rel_table.astype(F32)
    tabs = []
    for j in range(nwin):
        off = (j - (nwin - 1)) * tile
        idx = np.clip(r + off, -cfg.rel_clip, cfg.rel_clip) + cfg.rel_clip
        dchunk = ql // cfg.chunk - (kl + off) // cfg.chunk
        band = (dchunk >= 0) & (dchunk <= cfg.left_chunks)
        tabs.append(jnp.where(band[None], _toeplitz(rt[idx].T, tile), NEG))
    tabs.append(jnp.full_like(tabs[0], NEG))
    return jnp.stack(tabs, axis=1)


def _block_tables(counts, n_assign, cfg):
    e_ = cfg.n_experts
    bm = cfg.moe_rows
    nb = -(-(n_assign + e_ * (bm - 1)) // bm)
    pcounts = (counts + bm - 1) // bm * bm
    pends = jnp.cumsum(pcounts)
    pstarts = pends - pcounts
    start_b = jnp.arange(nb, dtype=jnp.int32) * bm
    block_e = jnp.minimum(jnp.sum((pends[None, :] <= start_b[:, None]).astype(jnp.int32), axis=1), e_ - 1)
    mine = jnp.arange(e_, dtype=jnp.int32)[None, :] == block_e[:, None]
    pstart_b = jnp.sum(jnp.where(mine, pstarts[None, :], 0), axis=1)
    count_b = jnp.sum(jnp.where(mine, counts[None, :], 0), axis=1)
    n_used = (pends[e_ - 1] // bm).astype(jnp.int32)
    return block_e.astype(jnp.int32), (start_b - pstart_b).astype(jnp.int32), count_b.astype(jnp.int32), n_used


def _forward(cfg, x, c, w_ada, b_ada, g_attn, w_in, lambda_qk, t5_bias, rel_bias_b, w_up_a, w_up_b, w_o,
             g_moe, w_router, router_bias, w_exp_gate, w_exp_up, w_exp_down,
             w_sh_gate, w_sh_up, w_sh_down, g_final):
    b_, s_, d_ = x.shape
    t_ = b_ * s_
    l = 0
    qk_a = cfg.ha * 2 * cfg.da_qk
    width_a = cfg.ha * 2 * cfg.da_qk
    width_b = cfg.hb * cfg.db
    in_cols = w_in.shape[2]

    rows = max(16, b_)
    c_pad = jnp.zeros((rows, d_), F32).at[:b_].set(c)
    mod = ada_mod(c_pad, w_ada[l], b_ada[l], cfg)[:b_]
    sh_a, sc_a, gt_a, sh_m, sc_m, gt_m = [m.reshape(b_, 1, d_) for m in jnp.split(mod, cfg.n_mod, axis=-1)]

    colscale = np.ones((1, in_cols), np.float32)
    colscale[0, :qk_a] = cfg.da_qk ** -0.5 * LOG2E
    qb0 = 2 * qk_a + width_a
    colscale[0, qb0:qb0 + width_b] = cfg.db ** -0.5 * LOG2E
    x2 = x.reshape(t_, d_)
    proj = in_proj(x2, g_attn[l].reshape(1, d_), sc_a, sh_a, w_in[l].astype(BF16), jnp.asarray(colscale), cfg)

    lq = lambda_qk[l].astype(F32)
    lam = jnp.exp(jnp.sum(lq[0] * lq[1])) - jnp.exp(jnp.sum(lq[2] * lq[3])) + cfg.lam_init
    ya = attn_a(proj, lam.reshape(1), _bias_tables_a(t5_bias * LOG2E, cfg), cfg)
    yb = attn_b(proj, _bias_tables_b(rel_bias_b[l] * LOG2E, cfg), cfg)

    ne_pad = max(LANES, cfg.n_experts)
    wrt = jnp.zeros((ne_pad, d_), F32).at[:cfg.n_experts].set(w_router[l].T)
    wrh, wrl = _split_bf16(wrt)
    h1, v, lg = merge(ya, yb, proj, x2, gt_a, sc_m, sh_m, g_moe[l].reshape(1, d_),
                      w_up_a[l].astype(BF16), w_up_b[l].astype(BF16), w_o[l].astype(BF16), wrh, wrl, cfg)

    tc = min(cfg.route_tc, t_)
    rb = jnp.zeros((ne_pad, tc), F32).at[:cfg.n_experts].set(
        jnp.broadcast_to(router_bias[l].astype(F32)[:, None], (cfg.n_experts, tc)))
    w_kt, lin, kin, gend_raw = route(lg, rb, cfg)
    gps, n_grp, nx = tc // LANES, t_ // LANES, cfg.n_experts
    assert n_grp <= LANES
    per_group = lambda a: jnp.pad(a.transpose(1, 0, 2).reshape(nx, n_grp), ((0, 0), (0, LANES - n_grp)),
                                  constant_values=1e9)
    gend, gstart = per_group(gend_raw[:, :, :gps]), per_group(gend_raw[:, :, gps:2 * gps])
    by_group = lambda a: jnp.pad(a.reshape(nx, n_grp, LANES), ((0, 0), (0, LANES - n_grp), (0, 0)))
    counts = gend[:, n_grp - 1].astype(jnp.int32)
    block_e, r0, count_b, n_used = _block_tables(counts, t_ * cfg.top_k, cfg)
    slot_tok, slot_dst = slots(block_e, r0, count_b, gstart, gend, by_group(lin), by_group(kin), cfg, t_)

    yk = moe(v, block_e, n_used, slot_tok, slot_dst, w_exp_gate[l], w_exp_up[l], w_exp_down[l], cfg)
    out = final(v, h1, yk, w_kt.T, gt_m, g_final.reshape(1, d_),
                w_sh_gate[l].astype(BF16), w_sh_up[l].astype(BF16), w_sh_down[l].astype(BF16), cfg)
    return out.reshape(b_, s_, d_)


def kernel(x, c, w_ada, b_ada, g_attn, w_in, lambda_qk, t5_bias, rel_bias_b, w_up_a, w_up_b, w_o, g_moe,
           w_router, router_bias, w_exp_gate, w_exp_up, w_exp_down, w_sh_gate, w_sh_up, w_sh_down, g_final):
    return _forward(Cfg(), x, c, w_ada, b_ada, g_attn, w_in, lambda_qk, t5_bias, rel_bias_b, w_up_a, w_up_b,
                    w_o, g_moe, w_router, router_bias, w_exp_gate, w_exp_up, w_exp_down,
                    w_sh_gate, w_sh_up, w_sh_down, g_final)
```

```python
import functools
import math
from typing import NamedTuple

import numpy as np
import jax
import jax.numpy as jnp
from jax import lax
from jax.experimental import pallas as pl
from jax.experimental.pallas import tpu as pltpu

F32 = jnp.float32
BF16 = jnp.bfloat16
NEG = -1e30
EPS = 1e-6
HEAD_EPS = 1e-5
LOG2E = 1.4426950408889634
LANES = 128
VMEM_LIMIT = 56 * 1024 * 1024


class Cfg(NamedTuple):
    batch: int = 8
    seq: int = 2048
    d_model: int = 2048
    chunk: int = 64
    ha: int = 8
    da_qk: int = 64
    hb: int = 8
    db: int = 128
    left_chunks: int = 8
    rel_clip: int = 256
    t5_buckets: int = 32
    t5_max_dist: int = 128
    n_experts: int = 64
    top_k: int = 8
    n_groups: int = 8
    topk_groups: int = 4
    d_expert: int = 512
    d_shared: int = 512
    routed_scale: float = 2.5
    n_mod: int = 6
    lam_init: float = 0.8 - 0.6 * math.exp(-0.3 * 0)
    attn_tile: int = 256
    moe_rows: int = 256
    proj_tm: int = 1024
    proj_tn: int = 1024
    merge_tm: int = 256
    route_tc: int = 512
    final_tm: int = 256
    ada_tn: int = 1024


def _sigmoid(x):
    return 1.0 / (1.0 + jnp.exp(-x))


def _split_bf16(x):
    hi = x.astype(BF16)
    lo = (x - hi.astype(F32)).astype(BF16)
    return hi, lo


def _dot(a, b):
    return jnp.dot(a, b, preferred_element_type=F32)


def _dot_t(a, b):
    return lax.dot_general(a, b, (((1,), (1,)), ((), ())), preferred_element_type=F32)


U32 = jnp.uint32
HI_MASK = 0xFFFF0000


def _pack_pairs(x):
    n = x.shape[1] // 2
    bits = pltpu.bitcast(x.astype(BF16).astype(F32), U32)
    return (bits[:, :n] >> 16) | (bits[:, n:] & U32(HI_MASK))


def _unpack_pairs(p):
    return pltpu.bitcast(p << 16, F32), pltpu.bitcast(p & U32(HI_MASK), F32)


def _load_slabs(ref, m, ns):
    return jnp.concatenate([ref[pl.ds(j, m, stride=ns), :] for j in range(ns)], axis=1)


def _store_slabs(ref, val, ns):
    m = val.shape[0]
    for j in range(ns):
        ref[pl.ds(j, m, stride=ns), :] = val[:, j * LANES:(j + 1) * LANES]


def _params(n_axes):
    return pltpu.CompilerParams(dimension_semantics=("arbitrary",) * n_axes,
                                vmem_limit_bytes=VMEM_LIMIT)


def _ada_kernel(c_ref, w_ref, b_ref, o_ref):
    c = c_ref[...]
    s_hi, s_lo = _split_bf16(c * _sigmoid(c))
    w_hi, w_lo = _split_bf16(w_ref[...])
    acc = _dot(s_hi, w_hi) + _dot(s_hi, w_lo) + _dot(s_lo, w_hi)
    o_ref[...] = acc + b_ref[...]


def ada_mod(c_pad, w_ada, b_ada, cfg):
    rows, d = c_pad.shape
    n = w_ada.shape[1]
    tn = min(cfg.ada_tn, n)
    return pl.pallas_call(
        _ada_kernel,
        grid=(n // tn,),
        in_specs=[pl.BlockSpec((rows, d), lambda j: (0, 0)),
                  pl.BlockSpec((d, tn), lambda j: (0, j)),
                  pl.BlockSpec((1, tn), lambda j: (0, j))],
        out_specs=pl.BlockSpec((rows, tn), lambda j: (0, j)),
        out_shape=jax.ShapeDtypeStruct((rows, n), F32),
        compiler_params=_params(1),
        name="ada_mod",
    )(c_pad, w_ada, b_ada.reshape(1, n))


def _inproj_kernel(x_ref, g_ref, sc_ref, sh_ref, w_ref, cs_ref, o_ref, u_ref):
    @pl.when(pl.program_id(1) == 0)
    def _():
        x = x_ref[...]
        y = x * lax.rsqrt(jnp.mean(x * x, axis=-1, keepdims=True) + EPS) * g_ref[...]
        u_ref[...] = (y * (1.0 + sc_ref[0]) + sh_ref[0]).astype(BF16)

    acc = _dot(u_ref[...], w_ref[...])
    o_ref[...] = (acc * cs_ref[...]).astype(BF16)


def in_proj(x2, g, sc, sh, w_bf, colscale, cfg):
    t, d = x2.shape
    n = w_bf.shape[1]
    tm = min(cfg.proj_tm, cfg.seq)
    tn = min(cfg.proj_tn, n)
    per_b = cfg.seq // tm
    return pl.pallas_call(
        _inproj_kernel,
        grid=(t // tm, n // tn),
        in_specs=[pl.BlockSpec((tm, d), lambda i, j: (i, 0)),
                  pl.BlockSpec((1, d), lambda i, j: (0, 0)),
                  pl.BlockSpec((1, 1, d), lambda i, j: (i // per_b, 0, 0)),
                  pl.BlockSpec((1, 1, d), lambda i, j: (i // per_b, 0, 0)),
                  pl.BlockSpec((d, tn), lambda i, j: (0, j)),
                  pl.BlockSpec((1, tn), lambda i, j: (0, j))],
        out_specs=pl.BlockSpec((tm, tn), lambda i, j: (i, j)),
        out_shape=jax.ShapeDtypeStruct((t, n), BF16),
        scratch_shapes=[pltpu.VMEM((tm, d), BF16)],
        compiler_params=_params(2),
        name="in_proj",
    )(x2, g, sc, sh, w_bf, colscale)


def _attn_a_kernel(lam_ref, qlo_ref, qhi_ref, k_ref, v_ref, tb_ref, olo_ref, ohi_ref,
                   s_ref, m_ref, l_ref, o_ref, *, tile, half, nq, out_scale):
    p = pl.program_id(2)
    hq = nq - 1 - p
    nvis = nq + 1
    ninf = -jnp.inf

    def stacked(q):
        lane = lax.broadcasted_iota(jnp.int32, q.shape, 1)
        zero = jnp.zeros_like(q)
        return jnp.concatenate([jnp.where(lane < half, q, zero), jnp.where(lane >= half, q, zero)], axis=0)

    qq_lo, qq_hi = stacked(qlo_ref[...]), stacked(qhi_ref[...])

    def fold(x, op):
        r = x[:, :LANES]
        for c in range(1, tile // LANES):
            r = op(r, x[:, c * LANES:(c + 1) * LANES])
        return r

    def pick(i, lo, hi):
        return jnp.where(i <= p, lo, hi)

    def key_tile(i):
        return jnp.where(i <= p, i, i - p - 1)

    def key_rows(i):
        return pl.ds(pl.multiple_of(key_tile(i) * tile, tile), tile)

    for i in range(nvis):
        bias = tb_ref[0, jnp.minimum(pick(i, p, hq) - key_tile(i), 2)]
        s = _dot_t(pick(i, qq_lo, qq_hi), k_ref[key_rows(i), :]) + jnp.concatenate([bias, bias], axis=0)
        s_ref[i] = s
        m_ref[i] = fold(s, jnp.maximum)

    def reduce_tiles(ref, op, init):
        lo = jnp.full(ref.shape[1:], init, F32)
        hi = lo
        for i in range(nvis):
            lo = op(lo, jnp.where(i <= p, ref[i], init))
            hi = op(hi, jnp.where(i <= p, init, ref[i]))
        return lo, hi

    m_lo, m_hi = reduce_tiles(m_ref, jnp.maximum, ninf)
    m_lo = jnp.max(m_lo, axis=-1, keepdims=True)
    m_hi = jnp.max(m_hi, axis=-1, keepdims=True)
    for i in range(nvis):
        e = jnp.exp2(s_ref[i] - pick(i, m_lo, m_hi))
        s_ref[i] = e
        l_ref[i] = fold(e, jnp.add)

    l_lo, l_hi = reduce_tiles(l_ref, jnp.add, 0.0)
    lam = lam_ref[0]

    def coefs(l):
        inv = 1.0 / jnp.sum(l, axis=-1, keepdims=True)
        return inv[:tile], lam * inv[tile:]

    (c0_lo, c1_lo), (c0_hi, c1_hi) = coefs(l_lo), coefs(l_hi)
    for i in range(nvis):
        e = s_ref[i]
        a = e[:tile] * pick(i, c0_lo, c0_hi) - e[tile:] * pick(i, c1_lo, c1_hi)
        o_ref[i] = _dot(a.astype(BF16), v_ref[key_rows(i), :])

    o_lo, o_hi = reduce_tiles(o_ref, jnp.add, 0.0)
    for o, out in ((o_lo, olo_ref), (o_hi, ohi_ref)):
        out[...] = (o * lax.rsqrt(jnp.mean(o * o, axis=-1, keepdims=True) + HEAD_EPS) * out_scale).astype(BF16)


def attn_a(proj, lam, tb, cfg):
    tile = cfg.attn_tile
    s = cfg.seq
    nq = s // tile
    assert nq % 2 == 0
    npair = nq // 2
    dv = 2 * cfg.da_qk
    assert dv == LANES
    kcol = cfg.ha
    vcol = 2 * cfg.ha
    kern = functools.partial(_attn_a_kernel, tile=tile, half=cfg.da_qk, nq=nq, out_scale=1.0 - cfg.lam_init)
    half_shape = jax.ShapeDtypeStruct((cfg.batch * npair * tile, cfg.ha * dv), BF16)
    y_lo, y_hi = pl.pallas_call(
        kern,
        grid=(cfg.batch, cfg.ha, npair),
        in_specs=[pl.BlockSpec(memory_space=pltpu.SMEM),
                  pl.BlockSpec((tile, LANES), lambda b, h, p: (b * nq + p, h)),
                  pl.BlockSpec((tile, LANES), lambda b, h, p: (b * nq + nq - 1 - p, h)),
                  pl.BlockSpec((s, LANES), lambda b, h, p: (b, kcol + h)),
                  pl.BlockSpec((s, LANES), lambda b, h, p: (b, vcol + h)),
                  pl.BlockSpec((1, 3, tile, tile), lambda b, h, p: (h, 0, 0, 0))],
        out_specs=[pl.BlockSpec((tile, LANES), lambda b, h, p: (b * npair + p, h)),
                   pl.BlockSpec((tile, LANES), lambda b, h, p: (b * npair + npair - 1 - p, h))],
        out_shape=[half_shape, half_shape],
        scratch_shapes=[pltpu.VMEM((nq + 1, 2 * tile, tile), F32),
                        pltpu.VMEM((nq + 1, 2 * tile, LANES), F32),
                        pltpu.VMEM((nq + 1, 2 * tile, LANES), F32),
                        pltpu.VMEM((nq + 1, tile, LANES), F32)],
        compiler_params=_params(3),
        name="attn_a",
    )(lam, proj, proj, proj, proj, tb)
    w = cfg.ha * dv
    rows = npair * tile
    return jnp.concatenate([y_lo.reshape(cfg.batch, rows, w), y_hi.reshape(cfg.batch, rows, w)],
                           axis=1).reshape(cfg.batch * s, w)


ATTN_B_TILES = 2


def _attn_b_kernel(q_ref, k_ref, v_ref, tb_ref, o_ref, *, tile, nwin):
    for u in range(ATTN_B_TILES):
        qi = pl.program_id(2) * ATTN_B_TILES + u
        q = q_ref[u * tile:(u + 1) * tile, :]
        ss, vs = [], []
        for j in range(nwin):
            kb = qi - (nwin - 1) + j
            start = pl.multiple_of(jnp.maximum(kb, 0) * tile, tile)
            ss.append(_dot_t(q, k_ref[pl.ds(start, tile), :]) + tb_ref[0, jnp.where(kb >= 0, j, nwin)])
            vs.append(v_ref[pl.ds(start, tile), :])
        m = jnp.max(ss[0], axis=-1, keepdims=True)
        for s in ss[1:]:
            m = jnp.maximum(m, jnp.max(s, axis=-1, keepdims=True))
        ps = [jnp.exp2(s - m) for s in ss]
        l = jnp.sum(ps[0], axis=-1, keepdims=True)
        for p in ps[1:]:
            l = l + jnp.sum(p, axis=-1, keepdims=True)
        acc = _dot(ps[0].astype(BF16), vs[0])
        for p, v in zip(ps[1:], vs[1:]):
            acc = acc + _dot(p.astype(BF16), v)
        o_ref[u * tile:(u + 1) * tile, :] = (acc * (1.0 / l)).astype(BF16)


def attn_b(proj, tb, cfg):
    t = proj.shape[0]
    tile = cfg.attn_tile
    s = cfg.seq
    nq = s // tile
    assert cfg.db == LANES and nq % ATTN_B_TILES == 0
    nstep = nq // ATTN_B_TILES
    rows = ATTN_B_TILES * tile
    nwin = (cfg.left_chunks * cfg.chunk) // tile + 1
    base = (2 * cfg.ha * 2 * cfg.da_qk + cfg.ha * 2 * cfg.da_qk) // LANES
    kern = functools.partial(_attn_b_kernel, tile=tile, nwin=nwin)
    return pl.pallas_call(
        kern,
        grid=(cfg.batch, cfg.hb, nstep),
        in_specs=[pl.BlockSpec((rows, LANES), lambda b, h, i: (b * nstep + i, base + h)),
                  pl.BlockSpec((s, LANES), lambda b, h, i: (b, base + cfg.hb + h)),
                  pl.BlockSpec((s, LANES), lambda b, h, i: (b, base + 2 * cfg.hb + h)),
                  pl.BlockSpec((1, nwin + 1, tile, tile), lambda b, h, i: (h, 0, 0, 0))],
        out_specs=pl.BlockSpec((rows, LANES), lambda b, h, i: (b * nstep + i, h)),
        out_shape=jax.ShapeDtypeStruct((t, cfg.hb * cfg.db), BF16),
        compiler_params=_params(3),
        name="attn_b",
    )(proj, proj, proj, tb)


def _merge_kernel(ya_ref, yb_ref, ga_ref, gb_ref, x_ref, gta_ref, scm_ref, shm_ref, gmoe_ref,
                  wua_ref, wub_ref, wo_ref, wrh_ref, wrl_ref, h1_ref, v_ref, lg_ref):
    a = _dot(ya_ref[...], wua_ref[...])
    b = _dot(yb_ref[...], wub_ref[...])
    merged = _sigmoid(ga_ref[...].astype(F32)) * a + _sigmoid(gb_ref[...].astype(F32)) * b
    att = _dot(merged.astype(BF16), wo_ref[...])
    h1 = x_ref[...] + gta_ref[0] * att
    h1_ref[...] = h1
    y = h1 * lax.rsqrt(jnp.mean(h1 * h1, axis=-1, keepdims=True) + EPS) * gmoe_ref[...]
    v = y * (1.0 + scm_ref[0]) + shm_ref[0]
    _store_slabs(v_ref, _pack_pairs(v), v_ref.shape[0] // v.shape[0])
    v_hi, v_lo = _split_bf16(v)
    wrh = wrh_ref[...]
    lg_ref[...] = _dot_t(wrh, v_hi) + _dot_t(wrh, v_lo) + _dot_t(wrl_ref[...], v_hi)


def merge(ya, yb, proj, x2, gta, scm, shm, gmoe, wua, wub, wo, wrh, wrl, cfg):
    t, d = x2.shape
    tm = min(cfg.merge_tm, cfg.seq)
    per_b = cfg.seq // tm
    wa = ya.shape[1]
    wb = yb.shape[1]
    gate0 = (proj.shape[1] - 2 * d) // d
    ne = wrh.shape[0]
    ns = d // (2 * LANES)
    const = lambda i: (0, 0)
    row3 = lambda i: (i // per_b, 0, 0)
    return pl.pallas_call(
        _merge_kernel,
        grid=(t // tm,),
        in_specs=[pl.BlockSpec((tm, wa), lambda i: (i, 0)),
                  pl.BlockSpec((tm, wb), lambda i: (i, 0)),
                  pl.BlockSpec((tm, d), lambda i: (i, gate0)),
                  pl.BlockSpec((tm, d), lambda i: (i, gate0 + 1)),
                  pl.BlockSpec((tm, d), lambda i: (i, 0)),
                  pl.BlockSpec((1, 1, d), row3),
                  pl.BlockSpec((1, 1, d), row3),
                  pl.BlockSpec((1, 1, d), row3),
                  pl.BlockSpec((1, d), const),
                  pl.BlockSpec((wa, d), const),
                  pl.BlockSpec((wb, d), const),
                  pl.BlockSpec((d, d), const),
                  pl.BlockSpec((ne, d), const),
                  pl.BlockSpec((ne, d), const)],
        out_specs=[pl.BlockSpec((tm, d), lambda i: (i, 0)),
                   pl.BlockSpec((tm * ns, LANES), lambda i: (i, 0)),
                   pl.BlockSpec((ne, tm), lambda i: (0, i))],
        out_shape=[jax.ShapeDtypeStruct((t, d), F32),
                   jax.ShapeDtypeStruct((t * ns, LANES), U32),
                   jax.ShapeDtypeStruct((ne, t), F32)],
        compiler_params=_params(1),
        name="merge",
    )(ya, yb, proj, proj, x2, gta, scm, shm, gmoe, wua, wub, wo, wrh, wrl)


def _route_kernel(lg_ref, rb_ref, tri_ref, w_ref, lin_ref, kin_ref, gend_ref, carry_ref,
                  *, n_exp, n_groups, topk_groups, top_k, scale):
    gsz = n_exp // n_groups
    tc = lg_ref.shape[1]
    ninf = -jnp.inf
    step = pl.program_id(0)

    @pl.when(step == 0)
    def _():
        carry_ref[...] = jnp.zeros(carry_ref.shape, F32)

    row = lax.broadcasted_iota(jnp.int32, (gsz, tc), 0).astype(F32)
    sc, bi = [], []
    for g in range(n_groups):
        s = _sigmoid(lg_ref[g * gsz:(g + 1) * gsz, :])
        sc.append(s)
        bi.append(s + rb_ref[g * gsz:(g + 1) * gsz, :])
    gs = []
    for x in bi:
        m1 = jnp.max(x, axis=0, keepdims=True)
        eq = x == m1
        cnt = jnp.sum(jnp.where(eq, 1.0, 0.0), axis=0, keepdims=True)
        second = jnp.max(jnp.where(eq, ninf, x), axis=0, keepdims=True)
        gs.append(m1 + jnp.where(cnt >= 2.0, m1, second))
    gmat = jnp.concatenate(gs, axis=0)
    grow = lax.broadcasted_iota(jnp.int32, gmat.shape, 0).astype(F32)
    gsel = jnp.zeros(gmat.shape, F32)
    for _ in range(topk_groups):
        m = jnp.max(gmat, axis=0, keepdims=True)
        idx = jnp.min(jnp.where(gmat == m, grow, float(n_groups)), axis=0, keepdims=True)
        hit = grow == idx
        gsel = jnp.where(hit, 1.0, gsel)
        gmat = jnp.where(hit, ninf, gmat)
    cur = [jnp.where(gsel[g:g + 1, :] > 0.5, bi[g], ninf) for g in range(n_groups)]
    ids = [row + float(g * gsz) for g in range(n_groups)]
    w_rows = []
    chosen = [jnp.zeros((gsz, tc), F32) for _ in range(n_groups)]
    kplus = [jnp.zeros((gsz, tc), F32) for _ in range(n_groups)]
    for k in range(top_k):
        mg = cur[0]
        for c in cur[1:]:
            mg = jnp.maximum(mg, c)
        m = jnp.max(mg, axis=0, keepdims=True)
        cand = jnp.where(cur[0] == m, ids[0], float(n_exp))
        for c, i in zip(cur[1:], ids[1:]):
            cand = jnp.minimum(cand, jnp.where(c == m, i, float(n_exp)))
        idx = jnp.min(cand, axis=0, keepdims=True)
        wsum = jnp.zeros((gsz, tc), F32)
        for g in range(n_groups):
            hit = ids[g] == idx
            wsum = wsum + jnp.where(hit, sc[g], 0.0)
            cur[g] = jnp.where(hit, ninf, cur[g])
            chosen[g] = jnp.where(hit, 1.0, chosen[g])
            kplus[g] = jnp.where(hit, float(k + 1), kplus[g])
        w_rows.append(jnp.sum(wsum, axis=0, keepdims=True))
    w = jnp.concatenate(w_rows, axis=0)
    w_ref[...] = w / jnp.sum(w, axis=0, keepdims=True) * scale

    onehot = jnp.concatenate(chosen, axis=0)
    incl = _dot(onehot.astype(BF16), tri_ref[...])
    lin_ref[...] = incl.astype(BF16)
    kin_ref[...] = jnp.concatenate(kplus, axis=0).astype(BF16)
    gps = tc // LANES
    lane = lax.broadcasted_iota(jnp.int32, (n_exp, LANES), 1)
    run = carry_ref[...]
    ends = jnp.zeros((n_exp, LANES), F32)
    for g in range(gps):
        ends = jnp.where(lane == gps + g, run, ends)
        run = run + incl[:, (g + 1) * LANES - 1:(g + 1) * LANES]
        ends = jnp.where(lane == g, run, ends)
    carry_ref[...] = run
    gend_ref[0] = ends


def route(lg, rb, cfg):
    ne, t = lg.shape
    nx = cfg.n_experts
    tc = min(cfg.route_tc, t)
    nsteps = t // tc
    tri = jnp.asarray(np.kron(np.eye(tc // LANES), np.triu(np.ones((LANES, LANES)))), BF16)
    kern = functools.partial(_route_kernel, n_exp=cfg.n_experts, n_groups=cfg.n_groups,
                             topk_groups=cfg.topk_groups, top_k=cfg.top_k, scale=cfg.routed_scale)
    return pl.pallas_call(
        kern,
        grid=(nsteps,),
        in_specs=[pl.BlockSpec((ne, tc), lambda i: (0, i)),
                  pl.BlockSpec((ne, tc), lambda i: (0, 0)),
                  pl.BlockSpec((tc, tc), lambda i: (0, 0))],
        out_specs=[pl.BlockSpec((cfg.top_k, tc), lambda i: (0, i)),
                   pl.BlockSpec((nx, tc), lambda i: (0, i)),
                   pl.BlockSpec((nx, tc), lambda i: (0, i)),
                   pl.BlockSpec((1, nx, LANES), lambda i: (i, 0, 0))],
        out_shape=[jax.ShapeDtypeStruct((cfg.top_k, t), F32),
                   jax.ShapeDtypeStruct((nx, t), BF16),
                   jax.ShapeDtypeStruct((nx, t), BF16),
                   jax.ShapeDtypeStruct((nsteps, nx, LANES), F32)],
        scratch_shapes=[pltpu.VMEM((cfg.n_experts, LANES), F32)],
        compiler_params=_params(1),
        name="route",
    )(lg, rb, tri)


SLOT_BLOCKS = 4


def _slots_kernel(be_ref, r0_ref, cnt_ref, gstart_ref, gend_ref, lin_ref, kin_ref, tok_ref, dst_ref,
                  *, bm, top_k, n_tok, n_assign):
    step = pl.program_id(0)
    rows = lax.broadcasted_iota(jnp.int32, (bm, LANES), 0)
    lane = lax.broadcasted_iota(jnp.int32, (bm, LANES), 1).astype(F32)
    ones = jnp.ones((16, LANES), BF16)
    gids = lax.broadcasted_iota(jnp.int32, (16, LANES), 1).astype(F32).astype(BF16)
    lane_row = lax.broadcasted_iota(jnp.int32, (1, bm), 1)
    for u in range(SLOT_BLOCKS):
        b = step * SLOT_BLOCKS + u
        e = be_ref[b]
        r0 = r0_ref[b]
        cnt = cnt_ref[b]
        r = (rows + r0).astype(F32)
        gstart = gstart_ref[pl.ds(e, 1), :]
        in_group = jnp.logical_and(gstart <= r, r < gend_ref[pl.ds(e, 1), :])
        group = jnp.where(in_group, 1.0, 0.0).astype(BF16)
        want = r - jnp.sum(jnp.where(in_group, gstart, 0.0), axis=1, keepdims=True) + 1.0
        kin = _dot(group, kin_ref[e])
        hit = jnp.logical_and(_dot(group, lin_ref[e]) == want, kin > 0.5)
        t_loc = _dot_t(ones, jnp.where(hit, lane, 0.0).astype(BF16))[0:1]
        kp1 = _dot_t(ones, jnp.where(hit, kin, 0.0).astype(BF16))[0:1]
        gid = _dot_t(gids, group)[0:1]
        tok = (gid * float(LANES) + t_loc).astype(jnp.int32)
        kk = kp1.astype(jnp.int32) - 1
        valid = (lane_row + r0) < cnt
        dummy = n_assign + lax.rem(b, 4) * bm + lane_row
        tok_ref[u] = jnp.where(valid, tok, lax.rem(b * bm + lane_row, n_tok))
        dst_ref[u] = jnp.where(valid, kk * n_tok + tok, dummy)


def slots(block_e, r0, cntb, gstart, gend, lin, kin, cfg, n_tok):
    nb = block_e.shape[0]
    bm = cfg.moe_rows
    assert nb % SLOT_BLOCKS == 0 and gstart.shape[1] == LANES and lin.shape[1:] == (LANES, LANES)
    kern = functools.partial(_slots_kernel, bm=bm, top_k=cfg.top_k, n_tok=n_tok, n_assign=n_tok * cfg.top_k)
    whole = lambda a: pl.BlockSpec(a.shape, lambda i, be, r0, cn: (0,) * a.ndim)
    out_blk = pl.BlockSpec((SLOT_BLOCKS, 1, bm), lambda i, be, r0, cn: (i, 0, 0))
    grid_spec = pltpu.PrefetchScalarGridSpec(
        num_scalar_prefetch=3,
        grid=(nb // SLOT_BLOCKS,),
        in_specs=[whole(gstart), whole(gend), whole(lin), whole(kin)],
        out_specs=[out_blk, out_blk],
    )
    return pl.pallas_call(
        kern,
        grid_spec=grid_spec,
        out_shape=[jax.ShapeDtypeStruct((nb, 1, bm), jnp.int32), jax.ShapeDtypeStruct((nb, 1, bm), jnp.int32)],
        compiler_params=_params(1),
        name="slots",
    )(block_e, r0, cntb, gstart, gend, lin, kin)


MOE_RING = 4
MOE_DUMMY = 4


def _moe_kernel(be_ref, tok0_ref, tok1_ref, tok2_ref, tokn_ref, dstp_ref, dstc_ref, v_hbm, wg_ref, wu_ref, wd_ref,
                yk_hbm, xbuf, ybuf, xbf, wgb, wub, wdb, gsem, ssem, *, bm, n_assign):
    b = pl.program_id(0)
    nb = pl.num_programs(0)
    slot = lax.rem(b, MOE_RING)
    ns = v_hbm.shape[1]

    def unpack_block(s):
        lo, hi = _unpack_pairs(_load_slabs(xbuf.at[s], bm, ns))
        xbf[s % 2] = jnp.concatenate([lo, hi], axis=1).astype(BF16)

    def gather_copy(t, r, s):
        return pltpu.make_async_copy(v_hbm.at[t], xbuf.at[s, pl.ds(r * ns, ns), :], gsem.at[s])

    def scatter_copy(r, d, s):
        return pltpu.make_async_copy(ybuf.at[s, pl.ds(r * ns, ns), :], yk_hbm.at[d], ssem.at[s])

    def wait_gather(s):
        for r in range(bm):
            gather_copy(0, r, s).wait()

    def wait_scatter(s):
        for r in range(bm):
            scatter_copy(r, 0, s).wait()

    @pl.when(b == 0)
    def _():
        ybuf[...] = jnp.zeros(ybuf.shape, U32)
        for s in range(MOE_RING - 1):
            for r in range(bm):
                scatter_copy(r, n_assign + s * bm + r, s).start(priority=r % 2)
        for s, tok_ref in enumerate((tok0_ref, tok1_ref, tok2_ref)):
            for r in range(bm):
                gather_copy(tok_ref[0, 0, r], r, s).start(priority=r % 2)
        wait_gather(0)
        unpack_block(0)

    e = be_ref[b]
    eprev = be_ref[jnp.maximum(b - 1, 0)]

    @pl.when(jnp.logical_or(b == 0, e != eprev))
    def _():
        wgb[...] = wg_ref[0].astype(BF16)
        wub[...] = wu_ref[0].astype(BF16)
        wdb[...] = wd_ref[0].astype(BF16)

    def block_step(s):
        nxt, prv = (s + 1) % MOE_RING, (s + 3) % MOE_RING
        wait_gather(nxt)
        wait_scatter(s)
        for r in range(bm):
            scatter_copy(r, dstp_ref[0, 0, r], prv).start(priority=r % 2)
        for r in range(bm):
            gather_copy(tokn_ref[0, 0, r], r, prv).start(priority=r % 2)
        x = xbf[s % 2]
        g = _dot(x, wgb[...])
        u = _dot(x, wub[...])
        h = (g * _sigmoid(g)) * u
        unpack_block(nxt)
        _store_slabs(ybuf.at[s], _pack_pairs(_dot(h.astype(BF16), wdb[...])), ns)

    last = be_ref[nb] - 1
    for s in range(MOE_RING):
        pl.when(jnp.logical_and(slot == s, b <= last))(functools.partial(block_step, s))

    @pl.when(b == last)
    def _():
        def drain(s):
            for r in range(bm):
                scatter_copy(r, dstc_ref[0, 0, r], s).start(priority=r % 2)
            for j in (2, 3):
                wait_gather((s + j) % MOE_RING)
            for j in (1, 2, 3, 0):
                wait_scatter((s + j) % MOE_RING)

        for s in range(MOE_RING):
            pl.when(slot == s)(functools.partial(drain, s))


def moe(vp, block_e, n_used, slot_tok, slot_dst, wg, wu, wd, cfg):
    d = wg.shape[1]
    ns = d // (2 * LANES)
    t = vp.shape[0] // ns
    bm = cfg.moe_rows
    v3 = vp.reshape(t, ns, LANES)
    nb = block_e.shape[0]
    assert nb >= 2
    de = wg.shape[2]
    n_assign = t * cfg.top_k
    first = (n_assign + 3 * bm + jnp.arange(bm, dtype=jnp.int32)).reshape(1, 1, bm)
    dst_ext = jnp.concatenate([first, slot_dst], axis=0)
    kern = functools.partial(_moe_kernel, bm=bm, n_assign=n_assign)
    smem_blk = lambda f: pl.BlockSpec((1, 1, bm), f, memory_space=pltpu.SMEM)
    grid_spec = pltpu.PrefetchScalarGridSpec(
        num_scalar_prefetch=1,
        grid=(nb,),
        in_specs=[smem_blk(lambda b, be: (0, 0, 0)),
                  smem_blk(lambda b, be: (1, 0, 0)),
                  smem_blk(lambda b, be: (min(2, nb - 1), 0, 0)),
                  smem_blk(lambda b, be: (jnp.minimum(b + 3, nb - 1), 0, 0)),
                  smem_blk(lambda b, be: (b, 0, 0)),
                  smem_blk(lambda b, be: (b + 1, 0, 0)),
                  pl.BlockSpec(memory_space=pl.ANY),
                  pl.BlockSpec((1, d, de), lambda b, be: (be[b], 0, 0)),
                  pl.BlockSpec((1, d, de), lambda b, be: (be[b], 0, 0)),
                  pl.BlockSpec((1, de, d), lambda b, be: (be[b], 0, 0))],
        out_specs=pl.BlockSpec(memory_space=pl.ANY),
        scratch_shapes=[pltpu.VMEM((MOE_RING, bm * ns, LANES), U32),
                        pltpu.VMEM((MOE_RING, bm * ns, LANES), U32),
                        pltpu.VMEM((2, bm, d), BF16),
                        pltpu.VMEM((d, de), BF16), pltpu.VMEM((d, de), BF16), pltpu.VMEM((de, d), BF16),
                        pltpu.SemaphoreType.DMA((MOE_RING,)), pltpu.SemaphoreType.DMA((MOE_RING,))],
    )
    return pl.pallas_call(
        kern,
        grid_spec=grid_spec,
        out_shape=jax.ShapeDtypeStruct((n_assign + MOE_DUMMY * bm, ns, LANES), U32),
        compiler_params=_params(1),
        name="moe",
    )(jnp.concatenate([block_e, n_used.reshape(1)]), slot_tok, slot_tok, slot_tok, slot_tok, dst_ext, dst_ext,
      v3, wg, wu, wd)


def _final_kernel(w_ref, v_ref, h1_ref, gtm_ref, gfin_ref, sg_ref, su_ref, sd_ref, *rest, top_k):
    yk_refs, o_ref, lo_ref, hi_ref = rest[:top_k], rest[top_k], rest[top_k + 1], rest[top_k + 2]
    tm = h1_ref.shape[0]
    ns = yk_refs[0].shape[1]

    def combine(t, carry):
        lo, hi = _unpack_pairs(yk_refs[0][t])
        acc_lo, acc_hi = w_ref[t, 0] * lo, w_ref[t, 0] * hi
        for k in range(1, top_k):
            lo, hi = _unpack_pairs(yk_refs[k][t])
            acc_lo, acc_hi = acc_lo + w_ref[t, k] * lo, acc_hi + w_ref[t, k] * hi
        rows = pl.ds(pl.multiple_of(t * ns, ns), ns)
        lo_ref[rows, :] = acc_lo
        hi_ref[rows, :] = acc_hi
        return carry

    lax.fori_loop(0, tm, combine, 0, unroll=8)
    routed = jnp.concatenate([_load_slabs(lo_ref, tm, ns), _load_slabs(hi_ref, tm, ns)], axis=1)

    vb = jnp.concatenate(_unpack_pairs(_load_slabs(v_ref, tm, ns)), axis=1).astype(BF16)
    g = _dot(vb, sg_ref[...])
    u = _dot(vb, su_ref[...])
    y = routed + _dot(((g * _sigmoid(g)) * u).astype(BF16), sd_ref[...])
    h2 = h1_ref[...] + gtm_ref[0] * y
    o_ref[...] = h2 * lax.rsqrt(jnp.mean(h2 * h2, axis=-1, keepdims=True) + EPS) * gfin_ref[...]


def final(vp, h1, yk3, w_tk, gtm, gfin, sg, su, sd, cfg):
    t, d = h1.shape
    tm = min(cfg.final_tm, cfg.seq)
    per_b = cfg.seq // tm
    nt = t // tm
    ds = sg.shape[1]
    nslab = d // (2 * LANES)
    const = lambda i: (0, 0)
    kern = functools.partial(_final_kernel, top_k=cfg.top_k)
    plane = lambda k: pl.BlockSpec((tm, nslab, LANES), lambda i: (k * nt + i, 0, 0))
    return pl.pallas_call(
        kern,
        grid=(nt,),
        in_specs=[pl.BlockSpec((tm, cfg.top_k), lambda i: (i, 0), memory_space=pltpu.SMEM),
                  pl.BlockSpec((tm * nslab, LANES), lambda i: (i, 0)),
                  pl.BlockSpec((tm, d), lambda i: (i, 0)),
                  pl.BlockSpec((1, 1, d), lambda i: (i // per_b, 0, 0)),
                  pl.BlockSpec((1, d), const),
                  pl.BlockSpec((d, ds), const),
                  pl.BlockSpec((d, ds), const),
                  pl.BlockSpec((ds, d), const)] + [plane(k) for k in range(cfg.top_k)],
        out_specs=pl.BlockSpec((tm, d), lambda i: (i, 0)),
        out_shape=jax.ShapeDtypeStruct((t, d), F32),
        scratch_shapes=[pltpu.VMEM((tm * nslab, LANES), F32), pltpu.VMEM((tm * nslab, LANES), F32)],
        compiler_params=_params(1),
        name="final",
    )(w_tk, vp, h1, gtm, gfin, sg, su, sd, *([yk3] * cfg.top_k))


def _t5_bucket(rel, cfg):
    nb = cfg.t5_buckets // 2
    ret = (rel > 0).astype(np.int32) * nb
    n = np.abs(rel)
    max_exact = nb // 2
    large = max_exact + (np.log(np.maximum(n, 1) / max_exact) / math.log(cfg.t5_max_dist / max_exact)
                         * (nb - max_exact)).astype(np.int32)
    large = np.minimum(large, nb - 1)
    return (ret + np.where(n < max_exact, n, large)).astype(np.int32)


def _toeplitz(vec, n):
    lead = vec.shape[:-1]
    flat = jnp.tile(vec, (1,) * len(lead) + (n,))[..., :n * (2 * n - 1)]
    return flat.reshape(lead + (n, 2 * n - 1))[..., :n]


def _rel_offsets(n):
    m = np.arange(2 * n)
    return np.where(m < n, m, m - 2 * n)


def _bias_tables_a(t5_bias, cfg):
    tile = cfg.attn_tile
    assert tile + 1 >= cfg.t5_max_dist and tile % cfg.chunk == 0
    ql = np.arange(tile)[:, None]
    kl = np.arange(tile)[None, :]
    allowed = (kl // cfg.chunk) <= (ql // cfg.chunk)
    r = _rel_offsets(tile)
    t5 = t5_bias.astype(F32)
    diag = _toeplitz(t5[_t5_bucket(r, cfg)].T, tile)
    prev = _toeplitz(t5[_t5_bucket(r - tile, cfg)].T, tile)
    far = jnp.broadcast_to(t5[_t5_bucket(np.array(-2 * tile), cfg)][:, None, None], diag.shape)
    return jnp.stack([jnp.where(allowed[None], diag, NEG), prev, far], axis=1)


def _bias_tables_b(rel_table, cfg):
    tile = cfg.attn_tile
    assert (cfg.left_chunks * cfg.chunk) % tile == 0 and tile % cfg.chunk == 0
    nwin = (cfg.left_chunks * cfg.chunk) // tile + 1
    ql = np.arange(tile)[:, None]
    kl = np.arange(tile)[None, :]
    r = _rel_offsets(tile)
    rt = rel_table.astype(F32)
    tabs = []
    for j in range(nwin):
        off = (j - (nwin - 1)) * tile
        idx = np.clip(r + off, -cfg.rel_clip, cfg.rel_clip) + cfg.rel_clip
        dchunk = ql // cfg.chunk - (kl + off) // cfg.chunk
        band = (dchunk >= 0) & (dchunk <= cfg.left_chunks)
        tabs.append(jnp.where(band[None], _toeplitz(rt[idx].T, tile), NEG))
    tabs.append(jnp.full_like(tabs[0], NEG))
    return jnp.stack(tabs, axis=1)


def _block_tables(counts, n_assign, cfg):
    e_ = cfg.n_experts
    bm = cfg.moe_rows
    nb = -(-(n_assign + e_ * (bm - 1)) // bm)
    pcounts = (counts + bm - 1) // bm * bm
    pends = jnp.cumsum(pcounts)
    pstarts = pends - pcounts
    start_b = jnp.arange(nb, dtype=jnp.int32) * bm
    block_e = jnp.minimum(jnp.sum((pends[None, :] <= start_b[:, None]).astype(jnp.int32), axis=1), e_ - 1)
    mine = jnp.arange(e_, dtype=jnp.int32)[None, :] == block_e[:, None]
    pstart_b = jnp.sum(jnp.where(mine, pstarts[None, :], 0), axis=1)
    count_b = jnp.sum(jnp.where(mine, counts[None, :], 0), axis=1)
    n_used = (pends[e_ - 1] // bm).astype(jnp.int32)
    return block_e.astype(jnp.int32), (start_b - pstart_b).astype(jnp.int32), count_b.astype(jnp.int32), n_used


def _forward(cfg, x, c, w_ada, b_ada, g_attn, w_in, lambda_qk, t5_bias, rel_bias_b, w_up_a, w_up_b, w_o,
             g_moe, w_router, router_bias, w_exp_gate, w_exp_up, w_exp_down,
             w_sh_gate, w_sh_up, w_sh_down, g_final):
    b_, s_, d_ = x.shape
    t_ = b_ * s_
    l = 0
    qk_a = cfg.ha * 2 * cfg.da_qk
    width_a = cfg.ha * 2 * cfg.da_qk
    width_b = cfg.hb * cfg.db
    in_cols = w_in.shape[2]

    rows = max(16, b_)
    c_pad = jnp.zeros((rows, d_), F32).at[:b_].set(c)
    mod = ada_mod(c_pad, w_ada[l], b_ada[l], cfg)[:b_]
    sh_a, sc_a, gt_a, sh_m, sc_m, gt_m = [m.reshape(b_, 1, d_) for m in jnp.split(mod, cfg.n_mod, axis=-1)]

    colscale = np.ones((1, in_cols), np.float32)
    colscale[0, :qk_a] = cfg.da_qk ** -0.5 * LOG2E
    qb0 = 2 * qk_a + width_a
    colscale[0, qb0:qb0 + width_b] = cfg.db ** -0.5 * LOG2E
    x2 = x.reshape(t_, d_)
    proj = in_proj(x2, g_attn[l].reshape(1, d_), sc_a, sh_a, w_in[l].astype(BF16), jnp.asarray(colscale), cfg)

    lq = lambda_qk[l].astype(F32)
    lam = jnp.exp(jnp.sum(lq[0] * lq[1])) - jnp.exp(jnp.sum(lq[2] * lq[3])) + cfg.lam_init
    ya = attn_a(proj, lam.reshape(1), _bias_tables_a(t5_bias * LOG2E, cfg), cfg)
    yb = attn_b(proj, _bias_tables_b(rel_bias_b[l] * LOG2E, cfg), cfg)

    ne_pad = max(LANES, cfg.n_experts)
    wrt = jnp.zeros((ne_pad, d_), F32).at[:cfg.n_experts].set(w_router[l].T)
    wrh, wrl = _split_bf16(wrt)
    h1, v, lg = merge(ya, yb, proj, x2, gt_a, sc_m, sh_m, g_moe[l].reshape(1, d_),
                      w_up_a[l].astype(BF16), w_up_b[l].astype(BF16), w_o[l].astype(BF16), wrh, wrl, cfg)

    tc = min(cfg.route_tc, t_)
    rb = jnp.zeros((ne_pad, tc), F32).at[:cfg.n_experts].set(
        jnp.broadcast_to(router_bias[l].astype(F32)[:, None], (cfg.n_experts, tc)))
    w_kt, lin, kin, gend_raw = route(lg, rb, cfg)
    gps, n_grp, nx = tc // LANES, t_ // LANES, cfg.n_experts
    assert n_grp <= LANES
    per_group = lambda a: jnp.pad(a.transpose(1, 0, 2).reshape(nx, n_grp), ((0, 0), (0, LANES - n_grp)),
                                  constant_values=1e9)
    gend, gstart = per_group(gend_raw[:, :, :gps]), per_group(gend_raw[:, :, gps:2 * gps])
    by_group = lambda a: jnp.pad(a.reshape(nx, n_grp, LANES), ((0, 0), (0, LANES - n_grp), (0, 0)))
    counts = gend[:, n_grp - 1].astype(jnp.int32)
    block_e, r0, count_b, n_used = _block_tables(counts, t_ * cfg.top_k, cfg)
    slot_tok, slot_dst = slots(block_e, r0, count_b, gstart, gend, by_group(lin), by_group(kin), cfg, t_)

    yk = moe(v, block_e, n_used, slot_tok, slot_dst, w_exp_gate[l], w_exp_up[l], w_exp_down[l], cfg)
    out = final(v, h1, yk, w_kt.T, gt_m, g_final.reshape(1, d_),
                w_sh_gate[l].astype(BF16), w_sh_up[l].astype(BF16), w_sh_down[l].astype(BF16), cfg)
    return out.reshape(b_, s_, d_)


def kernel(x, c, w_ada, b_ada, g_attn, w_in, lambda_qk, t5_bias, rel_bias_b, w_up_a, w_up_b, w_o, g_moe,
           w_router, router_bias, w_exp_gate, w_exp_up, w_exp_down, w_sh_gate, w_sh_up, w_sh_down, g_final):
    return _forward(Cfg(), x, c, w_ada, b_ada, g_attn, w_in, lambda_qk, t5_bias, rel_bias_b, w_up_a, w_up_b,
                    w_o, g_moe, w_router, router_bias, w_exp_gate, w_exp_up, w_exp_down,
                    w_sh_gate, w_sh_up, w_sh_down, g_final)
```

```python
import functools
import math
from typing import NamedTuple

import numpy as np
import jax
import jax.numpy as jnp
from jax import lax
from jax.experimental import pallas as pl
from jax.experimental.pallas import tpu as pltpu

F32 = jnp.float32
BF16 = jnp.bfloat16
NEG = -1e30
EPS = 1e-6
HEAD_EPS = 1e-5
LOG2E = 1.4426950408889634
LANES = 128
VMEM_LIMIT = 56 * 1024 * 1024


class Cfg(NamedTuple):
    batch: int = 8
    seq: int = 2048
    d_model: int = 2048
    chunk: int = 64
    ha: int = 8
    da_qk: int = 64
    hb: int = 8
    db: int = 128
    left_chunks: int = 8
    rel_clip: int = 256
    t5_buckets: int = 32
    t5_max_dist: int = 128
    n_experts: int = 64
    top_k: int = 8
    n_groups: int = 8
    topk_groups: int = 4
    d_expert: int = 512
    d_shared: int = 512
    routed_scale: float = 2.5
    n_mod: int = 6
    lam_init: float = 0.8 - 0.6 * math.exp(-0.3 * 0)
    attn_tile: int = 256
    moe_rows: int = 256
    proj_tm: int = 1024
    proj_tn: int = 1024
    merge_tm: int = 512
    merge_sub: int = 256
    route_tc: int = 512
    final_tm: int = 256
    ada_tn: int = 1024


def _sigmoid(x):
    return 1.0 / (1.0 + jnp.exp(-x))


def _split_bf16(x):
    hi = x.astype(BF16)
    lo = (x - hi.astype(F32)).astype(BF16)
    return hi, lo


def _dot(a, b):
    return jnp.dot(a, b, preferred_element_type=F32)


def _dot_t(a, b):
    return lax.dot_general(a, b, (((1,), (1,)), ((), ())), preferred_element_type=F32)


U32 = jnp.uint32
HI_MASK = 0xFFFF0000


def _pack_pairs(x):
    n = x.shape[1] // 2
    bits = pltpu.bitcast(x.astype(BF16).astype(F32), U32)
    return (bits[:, :n] >> 16) | (bits[:, n:] & U32(HI_MASK))


def _unpack_pairs(p):
    return pltpu.bitcast(p << 16, F32), pltpu.bitcast(p & U32(HI_MASK), F32)


def _load_slabs(ref, m, ns):
    return jnp.concatenate([ref[pl.ds(j, m, stride=ns), :] for j in range(ns)], axis=1)


def _store_slabs(ref, val, ns):
    m = val.shape[0]
    for j in range(ns):
        ref[pl.ds(j, m, stride=ns), :] = val[:, j * LANES:(j + 1) * LANES]


def _params(n_axes):
    return pltpu.CompilerParams(dimension_semantics=("arbitrary",) * n_axes,
                                vmem_limit_bytes=VMEM_LIMIT)


def _ada_kernel(c_ref, w_ref, b_ref, o_ref):
    c = c_ref[...]
    s_hi, s_lo = _split_bf16(c * _sigmoid(c))
    w_hi, w_lo = _split_bf16(w_ref[...])
    acc = _dot(s_hi, w_hi) + _dot(s_hi, w_lo) + _dot(s_lo, w_hi)
    o_ref[...] = acc + b_ref[...]


def ada_mod(c_pad, w_ada, b_ada, cfg):
    rows, d = c_pad.shape
    n = w_ada.shape[1]
    tn = min(cfg.ada_tn, n)
    return pl.pallas_call(
        _ada_kernel,
        grid=(n // tn,),
        in_specs=[pl.BlockSpec((rows, d), lambda j: (0, 0)),
                  pl.BlockSpec((d, tn), lambda j: (0, j)),
                  pl.BlockSpec((1, tn), lambda j: (0, j))],
        out_specs=pl.BlockSpec((rows, tn), lambda j: (0, j)),
        out_shape=jax.ShapeDtypeStruct((rows, n), F32),
        compiler_params=_params(1),
        name="ada_mod",
    )(c_pad, w_ada, b_ada.reshape(1, n))


def _inproj_kernel(x_ref, g_ref, sc_ref, sh_ref, w_ref, cs_ref, o_ref, u_ref):
    @pl.when(pl.program_id(1) == 0)
    def _():
        x = x_ref[...]
        y = x * lax.rsqrt(jnp.mean(x * x, axis=-1, keepdims=True) + EPS) * g_ref[...]
        u_ref[...] = (y * (1.0 + sc_ref[0]) + sh_ref[0]).astype(BF16)

    acc = _dot(u_ref[...], w_ref[...])
    o_ref[...] = (acc * cs_ref[...]).astype(BF16)


def in_proj(x2, g, sc, sh, w_bf, colscale, cfg):
    t, d = x2.shape
    n = w_bf.shape[1]
    tm = min(cfg.proj_tm, cfg.seq)
    tn = min(cfg.proj_tn, n)
    per_b = cfg.seq // tm
    return pl.pallas_call(
        _inproj_kernel,
        grid=(t // tm, n // tn),
        in_specs=[pl.BlockSpec((tm, d), lambda i, j: (i, 0)),
                  pl.BlockSpec((1, d), lambda i, j: (0, 0)),
                  pl.BlockSpec((1, 1, d), lambda i, j: (i // per_b, 0, 0)),
                  pl.BlockSpec((1, 1, d), lambda i, j: (i // per_b, 0, 0)),
                  pl.BlockSpec((d, tn), lambda i, j: (0, j)),
                  pl.BlockSpec((1, tn), lambda i, j: (0, j))],
        out_specs=pl.BlockSpec((tm, tn), lambda i, j: (i, j)),
        out_shape=jax.ShapeDtypeStruct((t, n), BF16),
        scratch_shapes=[pltpu.VMEM((tm, d), BF16)],
        compiler_params=_params(2),
        name="in_proj",
    )(x2, g, sc, sh, w_bf, colscale)


def _attn_a_kernel(lam_ref, qlo_ref, qhi_ref, k_ref, v_ref, tb_ref, olo_ref, ohi_ref,
                   s_ref, m_ref, l_ref, o_ref, *, tile, half, nq, out_scale):
    p = pl.program_id(2)
    hq = nq - 1 - p
    nvis = nq + 1
    ninf = -jnp.inf

    def stacked(q):
        lane = lax.broadcasted_iota(jnp.int32, q.shape, 1)
        zero = jnp.zeros_like(q)
        return jnp.concatenate([jnp.where(lane < half, q, zero), jnp.where(lane >= half, q, zero)], axis=0)

    qq_lo, qq_hi = stacked(qlo_ref[...]), stacked(qhi_ref[...])

    def fold(x, op):
        r = x[:, :LANES]
        for c in range(1, tile // LANES):
            r = op(r, x[:, c * LANES:(c + 1) * LANES])
        return r

    def pick(i, lo, hi):
        return jnp.where(i <= p, lo, hi)

    def key_tile(i):
        return jnp.where(i <= p, i, i - p - 1)

    def key_rows(i):
        return pl.ds(pl.multiple_of(key_tile(i) * tile, tile), tile)

    for i in range(nvis):
        bias = tb_ref[0, jnp.minimum(pick(i, p, hq) - key_tile(i), 2)]
        s = _dot_t(pick(i, qq_lo, qq_hi), k_ref[key_rows(i), :]) + jnp.concatenate([bias, bias], axis=0)
        s_ref[i] = s
        m_ref[i] = fold(s, jnp.maximum)

    def reduce_tiles(ref, op, init):
        lo = jnp.full(ref.shape[1:], init, F32)
        hi = lo
        for i in range(nvis):
            lo = op(lo, jnp.where(i <= p, ref[i], init))
            hi = op(hi, jnp.where(i <= p, init, ref[i]))
        return lo, hi

    m_lo, m_hi = reduce_tiles(m_ref, jnp.maximum, ninf)
    m_lo = jnp.max(m_lo, axis=-1, keepdims=True)
    m_hi = jnp.max(m_hi, axis=-1, keepdims=True)
    for i in range(nvis):
        e = jnp.exp2(s_ref[i] - pick(i, m_lo, m_hi))
        s_ref[i] = e
        l_ref[i] = fold(e, jnp.add)

    l_lo, l_hi = reduce_tiles(l_ref, jnp.add, 0.0)
    lam = lam_ref[0]

    def coefs(l):
        inv = 1.0 / jnp.sum(l, axis=-1, keepdims=True)
        return inv[:tile], lam * inv[tile:]

    (c0_lo, c1_lo), (c0_hi, c1_hi) = coefs(l_lo), coefs(l_hi)
    for i in range(nvis):
        e = s_ref[i]
        a = e[:tile] * pick(i, c0_lo, c0_hi) - e[tile:] * pick(i, c1_lo, c1_hi)
        o_ref[i] = _dot(a.astype(BF16), v_ref[key_rows(i), :])

    o_lo, o_hi = reduce_tiles(o_ref, jnp.add, 0.0)
    for o, out in ((o_lo, olo_ref), (o_hi, ohi_ref)):
        out[...] = (o * lax.rsqrt(jnp.mean(o * o, axis=-1, keepdims=True) + HEAD_EPS) * out_scale).astype(BF16)


def attn_a(proj, lam, tb, cfg):
    tile = cfg.attn_tile
    s = cfg.seq
    nq = s // tile
    assert nq % 2 == 0
    npair = nq // 2
    dv = 2 * cfg.da_qk
    assert dv == LANES
    kcol = cfg.ha
    vcol = 2 * cfg.ha
    kern = functools.partial(_attn_a_kernel, tile=tile, half=cfg.da_qk, nq=nq, out_scale=1.0 - cfg.lam_init)
    half_shape = jax.ShapeDtypeStruct((cfg.batch * npair * tile, cfg.ha * dv), BF16)
    y_lo, y_hi = pl.pallas_call(
        kern,
        grid=(cfg.batch, cfg.ha, npair),
        in_specs=[pl.BlockSpec(memory_space=pltpu.SMEM),
                  pl.BlockSpec((tile, LANES), lambda b, h, p: (b * nq + p, h)),
                  pl.BlockSpec((tile, LANES), lambda b, h, p: (b * nq + nq - 1 - p, h)),
                  pl.BlockSpec((s, LANES), lambda b, h, p: (b, kcol + h)),
                  pl.BlockSpec((s, LANES), lambda b, h, p: (b, vcol + h)),
                  pl.BlockSpec((1, 3, tile, tile), lambda b, h, p: (h, 0, 0, 0))],
        out_specs=[pl.BlockSpec((tile, LANES), lambda b, h, p: (b * npair + p, h)),
                   pl.BlockSpec((tile, LANES), lambda b, h, p: (b * npair + npair - 1 - p, h))],
        out_shape=[half_shape, half_shape],
        scratch_shapes=[pltpu.VMEM((nq + 1, 2 * tile, tile), F32),
                        pltpu.VMEM((nq + 1, 2 * tile, LANES), F32),
                        pltpu.VMEM((nq + 1, 2 * tile, LANES), F32),
                        pltpu.VMEM((nq + 1, tile, LANES), F32)],
        compiler_params=_params(3),
        name="attn_a",
    )(lam, proj, proj, proj, proj, tb)
    w = cfg.ha * dv
    rows = npair * tile
    return jnp.concatenate([y_lo.reshape(cfg.batch, rows, w), y_hi.reshape(cfg.batch, rows, w)],
                           axis=1).reshape(cfg.batch * s, w)


ATTN_B_TILES = 2


def _attn_b_kernel(q_ref, k_ref, v_ref, tb_ref, o_ref, *, tile, nwin):
    for u in range(ATTN_B_TILES):
        qi = pl.program_id(2) * ATTN_B_TILES + u
        q = q_ref[u * tile:(u + 1) * tile, :]
        ss, vs = [], []
        for j in range(nwin):
            kb = qi - (nwin - 1) + j
            start = pl.multiple_of(jnp.maximum(kb, 0) * tile, tile)
            ss.append(_dot_t(q, k_ref[pl.ds(start, tile), :]) + tb_ref[0, jnp.where(kb >= 0, j, nwin)])
            vs.append(v_ref[pl.ds(start, tile), :])
        m = jnp.max(ss[0], axis=-1, keepdims=True)
        for s in ss[1:]:
            m = jnp.maximum(m, jnp.max(s, axis=-1, keepdims=True))
        ps = [jnp.exp2(s - m) for s in ss]
        l = jnp.sum(ps[0], axis=-1, keepdims=True)
        for p in ps[1:]:
            l = l + jnp.sum(p, axis=-1, keepdims=True)
        acc = _dot(ps[0].astype(BF16), vs[0])
        for p, v in zip(ps[1:], vs[1:]):
            acc = acc + _dot(p.astype(BF16), v)
        o_ref[u * tile:(u + 1) * tile, :] = (acc * (1.0 / l)).astype(BF16)


def attn_b(proj, tb, cfg):
    t = proj.shape[0]
    tile = cfg.attn_tile
    s = cfg.seq
    nq = s // tile
    assert cfg.db == LANES and nq % ATTN_B_TILES == 0
    nstep = nq // ATTN_B_TILES
    rows = ATTN_B_TILES * tile
    nwin = (cfg.left_chunks * cfg.chunk) // tile + 1
    base = (2 * cfg.ha * 2 * cfg.da_qk + cfg.ha * 2 * cfg.da_qk) // LANES
    kern = functools.partial(_attn_b_kernel, tile=tile, nwin=nwin)
    return pl.pallas_call(
        kern,
        grid=(cfg.batch, cfg.hb, nstep),
        in_specs=[pl.BlockSpec((rows, LANES), lambda b, h, i: (b * nstep + i, base + h)),
                  pl.BlockSpec((s, LANES), lambda b, h, i: (b, base + cfg.hb + h)),
                  pl.BlockSpec((s, LANES), lambda b, h, i: (b, base + 2 * cfg.hb + h)),
                  pl.BlockSpec((1, nwin + 1, tile, tile), lambda b, h, i: (h, 0, 0, 0))],
        out_specs=pl.BlockSpec((rows, LANES), lambda b, h, i: (b * nstep + i, h)),
        out_shape=jax.ShapeDtypeStruct((t, cfg.hb * cfg.db), BF16),
        compiler_params=_params(3),
        name="attn_b",
    )(proj, proj, proj, tb)


def _merge_kernel(ya_ref, yb_ref, ga_ref, gb_ref, x_ref, gta_ref, scm_ref, shm_ref, gmoe_ref,
                  wua_ref, wub_ref, wo_ref, wrh_ref, wrl_ref, h1_ref, v_ref, lg_ref, *, sub):
    ns = v_ref.shape[0] // h1_ref.shape[0]
    for r0 in range(0, h1_ref.shape[0], sub):
        rows = slice(r0, r0 + sub)
        a = _dot(ya_ref[rows, :], wua_ref[...])
        b = _dot(yb_ref[rows, :], wub_ref[...])
        merged = _sigmoid(ga_ref[rows, :].astype(F32)) * a + _sigmoid(gb_ref[rows, :].astype(F32)) * b
        att = _dot(merged.astype(BF16), wo_ref[...])
        h1 = x_ref[rows, :] + gta_ref[0] * att
        h1_ref[rows, :] = h1
        y = h1 * lax.rsqrt(jnp.mean(h1 * h1, axis=-1, keepdims=True) + EPS) * gmoe_ref[...]
        v = y * (1.0 + scm_ref[0]) + shm_ref[0]
        _store_slabs(v_ref.at[pl.ds(r0 * ns, sub * ns), :], _pack_pairs(v), ns)
        v_hi, v_lo = _split_bf16(v)
        wrh = wrh_ref[...]
        lg_ref[:, rows] = _dot_t(wrh, v_hi) + _dot_t(wrh, v_lo) + _dot_t(wrl_ref[...], v_hi)


def merge(ya, yb, proj, x2, gta, scm, shm, gmoe, wua, wub, wo, wrh, wrl, cfg):
    t, d = x2.shape
    tm = min(cfg.merge_tm, cfg.seq)
    per_b = cfg.seq // tm
    wa = ya.shape[1]
    wb = yb.shape[1]
    gate0 = (proj.shape[1] - 2 * d) // d
    ne = wrh.shape[0]
    ns = d // (2 * LANES)
    const = lambda i: (0, 0)
    row3 = lambda i: (i // per_b, 0, 0)
    resident = lambda shape: pl.BlockSpec(shape, const, pipeline_mode=pl.Buffered(1))
    return pl.pallas_call(
        functools.partial(_merge_kernel, sub=min(cfg.merge_sub, tm)),
        grid=(t // tm,),
        in_specs=[pl.BlockSpec((tm, wa), lambda i: (i, 0)),
                  pl.BlockSpec((tm, wb), lambda i: (i, 0)),
                  pl.BlockSpec((tm, d), lambda i: (i, gate0)),
                  pl.BlockSpec((tm, d), lambda i: (i, gate0 + 1)),
                  pl.BlockSpec((tm, d), lambda i: (i, 0)),
                  pl.BlockSpec((1, 1, d), row3),
                  pl.BlockSpec((1, 1, d), row3),
                  pl.BlockSpec((1, 1, d), row3),
                  pl.BlockSpec((1, d), const),
                  resident((wa, d)),
                  resident((wb, d)),
                  resident((d, d)),
                  resident((ne, d)),
                  resident((ne, d))],
        out_specs=[pl.BlockSpec((tm, d), lambda i: (i, 0)),
                   pl.BlockSpec((tm * ns, LANES), lambda i: (i, 0)),
                   pl.BlockSpec((ne, tm), lambda i: (0, i))],
        out_shape=[jax.ShapeDtypeStruct((t, d), F32),
                   jax.ShapeDtypeStruct((t * ns, LANES), U32),
                   jax.ShapeDtypeStruct((ne, t), F32)],
        compiler_params=_params(1),
        name="merge",
    )(ya, yb, proj, proj, x2, gta, scm, shm, gmoe, wua, wub, wo, wrh, wrl)


def _route_kernel(lg_ref, rb_ref, tri_ref, w_ref, lin_ref, kin_ref, gend_ref, carry_ref,
                  *, n_exp, n_groups, topk_groups, top_k, scale):
    gsz = n_exp // n_groups
    tc = lg_ref.shape[1]
    ninf = -jnp.inf
    step = pl.program_id(0)

    @pl.when(step == 0)
    def _():
        carry_ref[...] = jnp.zeros(carry_ref.shape, F32)

    row = lax.broadcasted_iota(jnp.int32, (gsz, tc), 0).astype(F32)
    sc, bi = [], []
    for g in range(n_groups):
        s = _sigmoid(lg_ref[g * gsz:(g + 1) * gsz, :])
        sc.append(s)
        bi.append(s + rb_ref[g * gsz:(g + 1) * gsz, :])
    gs = []
    for x in bi:
        m1 = jnp.max(x, axis=0, keepdims=True)
        eq = x == m1
        cnt = jnp.sum(jnp.where(eq, 1.0, 0.0), axis=0, keepdims=True)
        second = jnp.max(jnp.where(eq, ninf, x), axis=0, keepdims=True)
        gs.append(m1 + jnp.where(cnt >= 2.0, m1, second))
    gmat = jnp.concatenate(gs, axis=0)
    grow = lax.broadcasted_iota(jnp.int32, gmat.shape, 0).astype(F32)
    gsel = jnp.zeros(gmat.shape, F32)
    for _ in range(topk_groups):
        m = jnp.max(gmat, axis=0, keepdims=True)
        idx = jnp.min(jnp.where(gmat == m, grow, float(n_groups)), axis=0, keepdims=True)
        hit = grow == idx
        gsel = jnp.where(hit, 1.0, gsel)
        gmat = jnp.where(hit, ninf, gmat)
    cur = [jnp.where(gsel[g:g + 1, :] > 0.5, bi[g], ninf) for g in range(n_groups)]
    ids = [row + float(g * gsz) for g in range(n_groups)]
    w_rows = []
    chosen = [jnp.zeros((gsz, tc), F32) for _ in range(n_groups)]
    kplus = [jnp.zeros((gsz, tc), F32) for _ in range(n_groups)]
    for k in range(top_k):
        mg = cur[0]
        for c in cur[1:]:
            mg = jnp.maximum(mg, c)
        m = jnp.max(mg, axis=0, keepdims=True)
        cand = jnp.where(cur[0] == m, ids[0], float(n_exp))
        for c, i in zip(cur[1:], ids[1:]):
            cand = jnp.minimum(cand, jnp.where(c == m, i, float(n_exp)))
        idx = jnp.min(cand, axis=0, keepdims=True)
        wsum = jnp.zeros((gsz, tc), F32)
        for g in range(n_groups):
            hit = ids[g] == idx
            wsum = wsum + jnp.where(hit, sc[g], 0.0)
            cur[g] = jnp.where(hit, ninf, cur[g])
            chosen[g] = jnp.where(hit, 1.0, chosen[g])
            kplus[g] = jnp.where(hit, float(k + 1), kplus[g])
        w_rows.append(jnp.sum(wsum, axis=0, keepdims=True))
    w = jnp.concatenate(w_rows, axis=0)
    w_ref[...] = w / jnp.sum(w, axis=0, keepdims=True) * scale

    onehot = jnp.concatenate(chosen, axis=0)
    incl = _dot(onehot.astype(BF16), tri_ref[...])
    lin_ref[...] = incl.astype(BF16)
    kin_ref[...] = jnp.concatenate(kplus, axis=0).astype(BF16)
    gps = tc // LANES
    lane = lax.broadcasted_iota(jnp.int32, (n_exp, LANES), 1)
    run = carry_ref[...]
    ends = jnp.zeros((n_exp, LANES), F32)
    for g in range(gps):
        ends = jnp.where(lane == gps + g, run, ends)
        run = run + incl[:, (g + 1) * LANES - 1:(g + 1) * LANES]
        ends = jnp.where(lane == g, run, ends)
    carry_ref[...] = run
    gend_ref[0] = ends


def route(lg, rb, cfg):
    ne, t = lg.shape
    nx = cfg.n_experts
    tc = min(cfg.route_tc, t)
    nsteps = t // tc
    tri = jnp.asarray(np.kron(np.eye(tc // LANES), np.triu(np.ones((LANES, LANES)))), BF16)
    kern = functools.partial(_route_kernel, n_exp=cfg.n_experts, n_groups=cfg.n_groups,
                             topk_groups=cfg.topk_groups, top_k=cfg.top_k, scale=cfg.routed_scale)
    return pl.pallas_call(
        kern,
        grid=(nsteps,),
        in_specs=[pl.BlockSpec((ne, tc), lambda i: (0, i)),
                  pl.BlockSpec((ne, tc), lambda i: (0, 0)),
                  pl.BlockSpec((tc, tc), lambda i: (0, 0))],
        out_specs=[pl.BlockSpec((cfg.top_k, tc), lambda i: (0, i)),
                   pl.BlockSpec((nx, tc), lambda i: (0, i)),
                   pl.BlockSpec((nx, tc), lambda i: (0, i)),
                   pl.BlockSpec((1, nx, LANES), lambda i: (i, 0, 0))],
        out_shape=[jax.ShapeDtypeStruct((cfg.top_k, t), F32),
                   jax.ShapeDtypeStruct((nx, t), BF16),
                   jax.ShapeDtypeStruct((nx, t), BF16),
                   jax.ShapeDtypeStruct((nsteps, nx, LANES), F32)],
        scratch_shapes=[pltpu.VMEM((cfg.n_experts, LANES), F32)],
        compiler_params=_params(1),
        name="route",
    )(lg, rb, tri)


SLOT_BLOCKS = 4


def _slots_kernel(be_ref, r0_ref, cnt_ref, gstart_ref, gend_ref, lin_ref, kin_ref, tok_ref, dst_ref,
                  *, bm, top_k, n_tok, n_assign):
    step = pl.program_id(0)
    rows = lax.broadcasted_iota(jnp.int32, (bm, LANES), 0)
    lane = lax.broadcasted_iota(jnp.int32, (bm, LANES), 1).astype(F32)
    ones = jnp.ones((16, LANES), BF16)
    gids = lax.broadcasted_iota(jnp.int32, (16, LANES), 1).astype(F32).astype(BF16)
    lane_row = lax.broadcasted_iota(jnp.int32, (1, bm), 1)
    for u in range(SLOT_BLOCKS):
        b = step * SLOT_BLOCKS + u
        e = be_ref[b]
        r0 = r0_ref[b]
        cnt = cnt_ref[b]
        r = (rows + r0).astype(F32)
        gstart = gstart_ref[pl.ds(e, 1), :]
        in_group = jnp.logical_and(gstart <= r, r < gend_ref[pl.ds(e, 1), :])
        group = jnp.where(in_group, 1.0, 0.0).astype(BF16)
        want = r - jnp.sum(jnp.where(in_group, gstart, 0.0), axis=1, keepdims=True) + 1.0
        kin = _dot(group, kin_ref[e])
        hit = jnp.logical_and(_dot(group, lin_ref[e]) == want, kin > 0.5)
        t_loc = _dot_t(ones, jnp.where(hit, lane, 0.0).astype(BF16))[0:1]
        kp1 = _dot_t(ones, jnp.where(hit, kin, 0.0).astype(BF16))[0:1]
        gid = _dot_t(gids, group)[0:1]
        tok = (gid * float(LANES) + t_loc).astype(jnp.int32)
        kk = kp1.astype(jnp.int32) - 1
        valid = (lane_row + r0) < cnt
        dummy = n_assign + lax.rem(b, 4) * bm + lane_row
        tok_ref[u] = jnp.where(valid, tok, lax.rem(b * bm + lane_row, n_tok))
        dst_ref[u] = jnp.where(valid, kk * n_tok + tok, dummy)


def slots(block_e, r0, cntb, gstart, gend, lin, kin, cfg, n_tok):
    nb = block_e.shape[0]
    bm = cfg.moe_rows
    assert nb % SLOT_BLOCKS == 0 and gstart.shape[1] == LANES and lin.shape[1:] == (LANES, LANES)
    kern = functools.partial(_slots_kernel, bm=bm, top_k=cfg.top_k, n_tok=n_tok, n_assign=n_tok * cfg.top_k)
    whole = lambda a: pl.BlockSpec(a.shape, lambda i, be, r0, cn: (0,) * a.ndim)
    out_blk = pl.BlockSpec((SLOT_BLOCKS, 1, bm), lambda i, be, r0, cn: (i, 0, 0))
    grid_spec = pltpu.PrefetchScalarGridSpec(
        num_scalar_prefetch=3,
        grid=(nb // SLOT_BLOCKS,),
        in_specs=[whole(gstart), whole(gend), whole(lin), whole(kin)],
        out_specs=[out_blk, out_blk],
    )
    return pl.pallas_call(
        kern,
        grid_spec=grid_spec,
        out_shape=[jax.ShapeDtypeStruct((nb, 1, bm), jnp.int32), jax.ShapeDtypeStruct((nb, 1, bm), jnp.int32)],
        compiler_params=_params(1),
        name="slots",
    )(block_e, r0, cntb, gstart, gend, lin, kin)


MOE_RING = 4
MOE_DUMMY = 4


def _moe_kernel(be_ref, tok0_ref, tok1_ref, tok2_ref, tokn_ref, dstp_ref, dstc_ref, v_hbm, wg_ref, wu_ref, wd_ref,
                yk_hbm, xbuf, ybuf, xbf, wgb, wub, wdb, gsem, ssem, *, bm, n_assign):
    b = pl.program_id(0)
    nb = pl.num_programs(0)
    slot = lax.rem(b, MOE_RING)
    ns = v_hbm.shape[1]

    def unpack_block(s):
        lo, hi = _unpack_pairs(_load_slabs(xbuf.at[s], bm, ns))
        xbf[s % 2] = jnp.concatenate([lo, hi], axis=1).astype(BF16)

    def gather_copy(t, r, s):
        return pltpu.make_async_copy(v_hbm.at[t], xbuf.at[s, pl.ds(r * ns, ns), :], gsem.at[s])

    def scatter_copy(r, d, s):
        return pltpu.make_async_copy(ybuf.at[s, pl.ds(r * ns, ns), :], yk_hbm.at[d], ssem.at[s])

    def wait_gather(s):
        for r in range(bm):
            gather_copy(0, r, s).wait()

    def wait_scatter(s):
        for r in range(bm):
            scatter_copy(r, 0, s).wait()

    @pl.when(b == 0)
    def _():
        ybuf[...] = jnp.zeros(ybuf.shape, U32)
        for s in range(MOE_RING - 1):
            for r in range(bm):
                scatter_copy(r, n_assign + s * bm + r, s).start(priority=r % 2)
        for s, tok_ref in enumerate((tok0_ref, tok1_ref, tok2_ref)):
            for r in range(bm):
                gather_copy(tok_ref[0, 0, r], r, s).start(priority=r % 2)
        wait_gather(0)
        unpack_block(0)

    e = be_ref[b]
    eprev = be_ref[jnp.maximum(b - 1, 0)]

    @pl.when(jnp.logical_or(b == 0, e != eprev))
    def _():
        wgb[...] = wg_ref[0].astype(BF16)
        wub[...] = wu_ref[0].astype(BF16)
        wdb[...] = wd_ref[0].astype(BF16)

    def block_step(s):
        nxt, prv = (s + 1) % MOE_RING, (s + 3) % MOE_RING
        wait_gather(nxt)
        wait_scatter(s)
        for r in range(bm):
            scatter_copy(r, dstp_ref[0, 0, r], prv).start(priority=r % 2)
        for r in range(bm):
            gather_copy(tokn_ref[0, 0, r], r, prv).start(priority=r % 2)
        x = xbf[s % 2]
        g = _dot(x, wgb[...])
        u = _dot(x, wub[...])
        h = (g * _sigmoid(g)) * u
        unpack_block(nxt)
        _store_slabs(ybuf.at[s], _pack_pairs(_dot(h.astype(BF16), wdb[...])), ns)

    last = be_ref[nb] - 1
    for s in range(MOE_RING):
        pl.when(jnp.logical_and(slot == s, b <= last))(functools.partial(block_step, s))

    @pl.when(b == last)
    def _():
        def drain(s):
            for r in range(bm):
                scatter_copy(r, dstc_ref[0, 0, r], s).start(priority=r % 2)
            for j in (2, 3):
                wait_gather((s + j) % MOE_RING)
            for j in (1, 2, 3, 0):
                wait_scatter((s + j) % MOE_RING)

        for s in range(MOE_RING):
            pl.when(slot == s)(functools.partial(drain, s))


def moe(vp, block_e, n_used, slot_tok, slot_dst, wg, wu, wd, cfg):
    d = wg.shape[1]
    ns = d // (2 * LANES)
    t = vp.shape[0] // ns
    bm = cfg.moe_rows
    v3 = vp.reshape(t, ns, LANES)
    nb = block_e.shape[0]
    assert nb >= 2
    de = wg.shape[2]
    n_assign = t * cfg.top_k
    first = (n_assign + 3 * bm + jnp.arange(bm, dtype=jnp.int32)).reshape(1, 1, bm)
    dst_ext = jnp.concatenate([first, slot_dst], axis=0)
    kern = functools.partial(_moe_kernel, bm=bm, n_assign=n_assign)
    smem_blk = lambda f: pl.BlockSpec((1, 1, bm), f, memory_space=pltpu.SMEM)
    grid_spec = pltpu.PrefetchScalarGridSpec(
        num_scalar_prefetch=1,
        grid=(nb,),
        in_specs=[smem_blk(lambda b, be: (0, 0, 0)),
                  smem_blk(lambda b, be: (1, 0, 0)),
                  smem_blk(lambda b, be: (min(2, nb - 1), 0, 0)),
                  smem_blk(lambda b, be: (jnp.minimum(b + 3, nb - 1), 0, 0)),
                  smem_blk(lambda b, be: (b, 0, 0)),
                  smem_blk(lambda b, be: (b + 1, 0, 0)),
                  pl.BlockSpec(memory_space=pl.ANY),
                  pl.BlockSpec((1, d, de), lambda b, be: (be[b], 0, 0)),
                  pl.BlockSpec((1, d, de), lambda b, be: (be[b], 0, 0)),
                  pl.BlockSpec((1, de, d), lambda b, be: (be[b], 0, 0))],
        out_specs=pl.BlockSpec(memory_space=pl.ANY),
        scratch_shapes=[pltpu.VMEM((MOE_RING, bm * ns, LANES), U32),
                        pltpu.VMEM((MOE_RING, bm * ns, LANES), U32),
                        pltpu.VMEM((2, bm, d), BF16),
                        pltpu.VMEM((d, de), BF16), pltpu.VMEM((d, de), BF16), pltpu.VMEM((de, d), BF16),
                        pltpu.SemaphoreType.DMA((MOE_RING,)), pltpu.SemaphoreType.DMA((MOE_RING,))],
    )
    return pl.pallas_call(
        kern,
        grid_spec=grid_spec,
        out_shape=jax.ShapeDtypeStruct((n_assign + MOE_DUMMY * bm, ns, LANES), U32),
        compiler_params=_params(1),
        name="moe",
    )(jnp.concatenate([block_e, n_used.reshape(1)]), slot_tok, slot_tok, slot_tok, slot_tok, dst_ext, dst_ext,
      v3, wg, wu, wd)


def _final_kernel(w_ref, v_ref, h1_ref, gtm_ref, gfin_ref, sg_ref, su_ref, sd_ref, *rest, top_k):
    yk_refs, o_ref, lo_ref, hi_ref = rest[:top_k], rest[top_k], rest[top_k + 1], rest[top_k + 2]
    tm = h1_ref.shape[0]
    ns = yk_refs[0].shape[1]

    def combine(t, carry):
        lo, hi = _unpack_pairs(yk_refs[0][t])
        acc_lo, acc_hi = w_ref[t, 0] * lo, w_ref[t, 0] * hi
        for k in range(1, top_k):
            lo, hi = _unpack_pairs(yk_refs[k][t])
            acc_lo, acc_hi = acc_lo + w_ref[t, k] * lo, acc_hi + w_ref[t, k] * hi
        rows = pl.ds(pl.multiple_of(t * ns, ns), ns)
        lo_ref[rows, :] = acc_lo
        hi_ref[rows, :] = acc_hi
        return carry

    lax.fori_loop(0, tm, combine, 0, unroll=8)
    routed = jnp.concatenate([_load_slabs(lo_ref, tm, ns), _load_slabs(hi_ref, tm, ns)], axis=1)

    vb = jnp.concatenate(_unpack_pairs(_load_slabs(v_ref, tm, ns)), axis=1).astype(BF16)
    g = _dot(vb, sg_ref[...])
    u = _dot(vb, su_ref[...])
    y = routed + _dot(((g * _sigmoid(g)) * u).astype(BF16), sd_ref[...])
    h2 = h1_ref[...] + gtm_ref[0] * y
    o_ref[...] = h2 * lax.rsqrt(jnp.mean(h2 * h2, axis=-1, keepdims=True) + EPS) * gfin_ref[...]


def final(vp, h1, yk3, w_tk, gtm, gfin, sg, su, sd, cfg):
    t, d = h1.shape
    tm = min(cfg.final_tm, cfg.seq)
    per_b = cfg.seq // tm
    nt = t // tm
    ds = sg.shape[1]
    nslab = d // (2 * LANES)
    const = lambda i: (0, 0)
    kern = functools.partial(_final_kernel, top_k=cfg.top_k)
    plane = lambda k: pl.BlockSpec((tm, nslab, LANES), lambda i: (k * nt + i, 0, 0))
    return pl.pallas_call(
        kern,
        grid=(nt,),
        in_specs=[pl.BlockSpec((tm, cfg.top_k), lambda i: (i, 0), memory_space=pltpu.SMEM),
                  pl.BlockSpec((tm * nslab, LANES), lambda i: (i, 0)),
                  pl.BlockSpec((tm, d), lambda i: (i, 0)),
                  pl.BlockSpec((1, 1, d), lambda i: (i // per_b, 0, 0)),
                  pl.BlockSpec((1, d), const),
                  pl.BlockSpec((d, ds), const),
                  pl.BlockSpec((d, ds), const),
                  pl.BlockSpec((ds, d), const)] + [plane(k) for k in range(cfg.top_k)],
        out_specs=pl.BlockSpec((tm, d), lambda i: (i, 0)),
        out_shape=jax.ShapeDtypeStruct((t, d), F32),
        scratch_shapes=[pltpu.VMEM((tm * nslab, LANES), F32), pltpu.VMEM((tm * nslab, LANES), F32)],
        compiler_params=_params(1),
        name="final",
    )(w_tk, vp, h1, gtm, gfin, sg, su, sd, *([yk3] * cfg.top_k))


def _t5_bucket(rel, cfg):
    nb = cfg.t5_buckets // 2
    ret = (rel > 0).astype(np.int32) * nb
    n = np.abs(rel)
    max_exact = nb // 2
    large = max_exact + (np.log(np.maximum(n, 1) / max_exact) / math.log(cfg.t5_max_dist / max_exact)
                         * (nb - max_exact)).astype(np.int32)
    large = np.minimum(large, nb - 1)
    return (ret + np.where(n < max_exact, n, large)).astype(np.int32)


def _toeplitz(vec, n):
    lead = vec.shape[:-1]
    flat = jnp.tile(vec, (1,) * len(lead) + (n,))[..., :n * (2 * n - 1)]
    return flat.reshape(lead + (n, 2 * n - 1))[..., :n]


def _rel_offsets(n):
    m = np.arange(2 * n)
    return np.where(m < n, m, m - 2 * n)


def _bias_tables_a(t5_bias, cfg):
    tile = cfg.attn_tile
    assert tile + 1 >= cfg.t5_max_dist and tile % cfg.chunk == 0
    ql = np.arange(tile)[:, None]
    kl = np.arange(tile)[None, :]
    allowed = (kl // cfg.chunk) <= (ql // cfg.chunk)
    r = _rel_offsets(tile)
    t5 = t5_bias.astype(F32)
    diag = _toeplitz(t5[_t5_bucket(r, cfg)].T, tile)
    prev = _toeplitz(t5[_t5_bucket(r - tile, cfg)].T, tile)
    far = jnp.broadcast_to(t5[_t5_bucket(np.array(-2 * tile), cfg)][:, None, None], diag.shape)
    return jnp.stack([jnp.where(allowed[None], diag, NEG), prev, far], axis=1)


def _bias_tables_b(rel_table, cfg):
    tile = cfg.attn_tile
    assert (cfg.left_chunks * cfg.chunk) % tile == 0 and tile % cfg.chunk == 0
    nwin = (cfg.left_chunks * cfg.chunk) // tile + 1
    ql = np.arange(tile)[:, None]
    kl = np.arange(tile)[None, :]
    r = _rel_offsets(tile)
    rt = rel_table.astype(F32)
    tabs = []
    for j in range(nwin):
        off = (j - (nwin - 1)) * tile
        idx = np.clip(r + off, -cfg.rel_clip, cfg.rel_clip) + cfg.rel_clip
        dchunk = ql // cfg.chunk - (kl + off) // cfg.chunk
        band = (dchunk >= 0) & (dchunk <= cfg.left_chunks)
        tabs.append(jnp.where(band[None], _toeplitz(rt[idx].T, tile), NEG))
    tabs.append(jnp.full_like(tabs[0], NEG))
    return jnp.stack(tabs, axis=1)


def _block_tables(counts, n_assign, cfg):
    e_ = cfg.n_experts
    bm = cfg.moe_rows
    nb = -(-(n_assign + e_ * (bm - 1)) // bm)
    pcounts = (counts + bm - 1) // bm * bm
    pends = jnp.cumsum(pcounts)
    pstarts = pends - pcounts
    start_b = jnp.arange(nb, dtype=jnp.int32) * bm
    block_e = jnp.minimum(jnp.sum((pends[None, :] <= start_b[:, None]).astype(jnp.int32), axis=1), e_ - 1)
    mine = jnp.arange(e_, dtype=jnp.int32)[None, :] == block_e[:, None]
    pstart_b = jnp.sum(jnp.where(mine, pstarts[None, :], 0), axis=1)
    count_b = jnp.sum(jnp.where(mine, counts[None, :], 0), axis=1)
    n_used = (pends[e_ - 1] // bm).astype(jnp.int32)
    return block_e.astype(jnp.int32), (start_b - pstart_b).astype(jnp.int32), count_b.astype(jnp.int32), n_used


def _forward(cfg, x, c, w_ada, b_ada, g_attn, w_in, lambda_qk, t5_bias, rel_bias_b, w_up_a, w_up_b, w_o,
             g_moe, w_router, router_bias, w_exp_gate, w_exp_up, w_exp_down,
             w_sh_gate, w_sh_up, w_sh_down, g_final):
    b_, s_, d_ = x.shape
    t_ = b_ * s_
    l = 0
    qk_a = cfg.ha * 2 * cfg.da_qk
    width_a = cfg.ha * 2 * cfg.da_qk
    width_b = cfg.hb * cfg.db
    in_cols = w_in.shape[2]

    rows = max(16, b_)
    c_pad = jnp.zeros((rows, d_), F32).at[:b_].set(c)
    mod = ada_mod(c_pad, w_ada[l], b_ada[l], cfg)[:b_]
    sh_a, sc_a, gt_a, sh_m, sc_m, gt_m = [m.reshape(b_, 1, d_) for m in jnp.split(mod, cfg.n_mod, axis=-1)]

    colscale = np.ones((1, in_cols), np.float32)
    colscale[0, :qk_a] = cfg.da_qk ** -0.5 * LOG2E
    qb0 = 2 * qk_a + width_a
    colscale[0, qb0:qb0 + width_b] = cfg.db ** -0.5 * LOG2E
    x2 = x.reshape(t_, d_)
    proj = in_proj(x2, g_attn[l].reshape(1, d_), sc_a, sh_a, w_in[l].astype(BF16), jnp.asarray(colscale), cfg)

    lq = lambda_qk[l].astype(F32)
    lam = jnp.exp(jnp.sum(lq[0] * lq[1])) - jnp.exp(jnp.sum(lq[2] * lq[3])) + cfg.lam_init
    ya = attn_a(proj, lam.reshape(1), _bias_tables_a(t5_bias * LOG2E, cfg), cfg)
    yb = attn_b(proj, _bias_tables_b(rel_bias_b[l] * LOG2E, cfg), cfg)

    ne_pad = max(LANES, cfg.n_experts)
    wrt = jnp.zeros((ne_pad, d_), F32).at[:cfg.n_experts].set(w_router[l].T)
    wrh, wrl = _split_bf16(wrt)
    h1, v, lg = merge(ya, yb, proj, x2, gt_a, sc_m, sh_m, g_moe[l].reshape(1, d_),
                      w_up_a[l].astype(BF16), w_up_b[l].astype(BF16), w_o[l].astype(BF16), wrh, wrl, cfg)

    tc = min(cfg.route_tc, t_)
    rb = jnp.zeros((ne_pad, tc), F32).at[:cfg.n_experts].set(
        jnp.broadcast_to(router_bias[l].astype(F32)[:, None], (cfg.n_experts, tc)))
    w_kt, lin, kin, gend_raw = route(lg, rb, cfg)
    gps, n_grp, nx = tc // LANES, t_ // LANES, cfg.n_experts
    assert n_grp <= LANES
    per_group = lambda a: jnp.pad(a.transpose(1, 0, 2).reshape(nx, n_grp), ((0, 0), (0, LANES - n_grp)),
                                  constant_values=1e9)
    gend, gstart = per_group(gend_raw[:, :, :gps]), per_group(gend_raw[:, :, gps:2 * gps])
    by_group = lambda a: jnp.pad(a.reshape(nx, n_grp, LANES), ((0, 0), (0, LANES - n_grp), (0, 0)))
    counts = gend[:, n_grp - 1].astype(jnp.int32)
    block_e, r0, count_b, n_used = _block_tables(counts, t_ * cfg.top_k, cfg)
    slot_tok, slot_dst = slots(block_e, r0, count_b, gstart, gend, by_group(lin), by_group(kin), cfg, t_)

    yk = moe(v, block_e, n_used, slot_tok, slot_dst, w_exp_gate[l], w_exp_up[l], w_exp_down[l], cfg)
    out = final(v, h1, yk, w_kt.T, gt_m, g_final.reshape(1, d_),
                w_sh_gate[l].astype(BF16), w_sh_up[l].astype(BF16), w_sh_down[l].astype(BF16), cfg)
    return out.reshape(b_, s_, d_)


def kernel(x, c, w_ada, b_ada, g_attn, w_in, lambda_qk, t5_bias, rel_bias_b, w_up_a, w_up_b, w_o, g_moe,
           w_router, router_bias, w_exp_gate, w_exp_up, w_exp_down, w_sh_gate, w_sh_up, w_sh_down, g_final):
    return _forward(Cfg(), x, c, w_ada, b_ada, g_attn, w_in, lambda_qk, t5_bias, rel_bias_b, w_up_a, w_up_b,
                    w_o, g_moe, w_router, router_bias, w_exp_gate, w_exp_up, w_exp_down,
                    w_sh_gate, w_sh_up, w_sh_down, g_final)
```

```python
import functools
import math
from typing import NamedTuple

import numpy as np
import jax
import jax.numpy as jnp
from jax import lax
from jax.experimental import pallas as pl
from jax.experimental.pallas import tpu as pltpu

F32 = jnp.float32
BF16 = jnp.bfloat16
NEG = -1e30
EPS = 1e-6
HEAD_EPS = 1e-5
LOG2E = 1.4426950408889634
LANES = 128
VMEM_LIMIT = 56 * 1024 * 1024


class Cfg(NamedTuple):
    batch: int = 8
    seq: int = 2048
    d_model: int = 2048
    chunk: int = 64
    ha: int = 8
    da_qk: int = 64
    hb: int = 8
    db: int = 128
    left_chunks: int = 8
    rel_clip: int = 256
    t5_buckets: int = 32
    t5_max_dist: int = 128
    n_experts: int = 64
    top_k: int = 8
    n_groups: int = 8
    topk_groups: int = 4
    d_expert: int = 512
    d_shared: int = 512
    routed_scale: float = 2.5
    n_mod: int = 6
    lam_init: float = 0.8 - 0.6 * math.exp(-0.3 * 0)
    attn_tile: int = 256
    moe_rows: int = 256
    proj_tm: int = 1024
    proj_tn: int = 1024
    merge_tm: int = 512
    merge_sub: int = 256
    route_tc: int = 512
    final_tm: int = 256
    ada_tn: int = 1024


def _sigmoid(x):
    return 1.0 / (1.0 + jnp.exp(-x))


def _split_bf16(x):
    hi = x.astype(BF16)
    lo = (x - hi.astype(F32)).astype(BF16)
    return hi, lo


def _dot(a, b):
    return jnp.dot(a, b, preferred_element_type=F32)


def _dot_t(a, b):
    return lax.dot_general(a, b, (((1,), (1,)), ((), ())), preferred_element_type=F32)


U32 = jnp.uint32
HI_MASK = 0xFFFF0000


def _pack_pairs(x):
    n = x.shape[1] // 2
    bits = pltpu.bitcast(x.astype(BF16).astype(F32), U32)
    return (bits[:, :n] >> 16) | (bits[:, n:] & U32(HI_MASK))


def _unpack_pairs(p):
    return pltpu.bitcast(p << 16, F32), pltpu.bitcast(p & U32(HI_MASK), F32)


def _load_slabs(ref, m, ns):
    return jnp.concatenate([ref[pl.ds(j, m, stride=ns), :] for j in range(ns)], axis=1)


def _store_slabs(ref, val, ns):
    m = val.shape[0]
    for j in range(ns):
        ref[pl.ds(j, m, stride=ns), :] = val[:, j * LANES:(j + 1) * LANES]


def _params(n_axes):
    return pltpu.CompilerParams(dimension_semantics=("arbitrary",) * n_axes,
                                vmem_limit_bytes=VMEM_LIMIT)


def _ada_kernel(c_ref, w_ref, b_ref, o_ref):
    c = c_ref[...]
    s_hi, s_lo = _split_bf16(c * _sigmoid(c))
    w_hi, w_lo = _split_bf16(w_ref[...])
    acc = _dot(s_hi, w_hi) + _dot(s_hi, w_lo) + _dot(s_lo, w_hi)
    o_ref[...] = acc + b_ref[...]


def ada_mod(c_pad, w_ada, b_ada, cfg):
    rows, d = c_pad.shape
    n = w_ada.shape[1]
    tn = min(cfg.ada_tn, n)
    return pl.pallas_call(
        _ada_kernel,
        grid=(n // tn,),
        in_specs=[pl.BlockSpec((rows, d), lambda j: (0, 0)),
                  pl.BlockSpec((d, tn), lambda j: (0, j)),
                  pl.BlockSpec((1, tn), lambda j: (0, j))],
        out_specs=pl.BlockSpec((rows, tn), lambda j: (0, j)),
        out_shape=jax.ShapeDtypeStruct((rows, n), F32),
        compiler_params=_params(1),
        name="ada_mod",
    )(c_pad, w_ada, b_ada.reshape(1, n))


def _inproj_kernel(x_ref, g_ref, sc_ref, sh_ref, w_ref, cs_ref, o_ref, u_ref):
    @pl.when(pl.program_id(1) == 0)
    def _():
        x = x_ref[...]
        y = x * lax.rsqrt(jnp.mean(x * x, axis=-1, keepdims=True) + EPS) * g_ref[...]
        u_ref[...] = (y * (1.0 + sc_ref[0]) + sh_ref[0]).astype(BF16)

    acc = _dot(u_ref[...], w_ref[...])
    o_ref[...] = (acc * cs_ref[...]).astype(BF16)


def in_proj(x2, g, sc, sh, w_bf, colscale, cfg):
    t, d = x2.shape
    n = w_bf.shape[1]
    tm = min(cfg.proj_tm, cfg.seq)
    tn = min(cfg.proj_tn, n)
    per_b = cfg.seq // tm
    return pl.pallas_call(
        _inproj_kernel,
        grid=(t // tm, n // tn),
        in_specs=[pl.BlockSpec((tm, d), lambda i, j: (i, 0)),
                  pl.BlockSpec((1, d), lambda i, j: (0, 0)),
                  pl.BlockSpec((1, 1, d), lambda i, j: (i // per_b, 0, 0)),
                  pl.BlockSpec((1, 1, d), lambda i, j: (i // per_b, 0, 0)),
                  pl.BlockSpec((d, tn), lambda i, j: (0, j)),
                  pl.BlockSpec((1, tn), lambda i, j: (0, j))],
        out_specs=pl.BlockSpec((tm, tn), lambda i, j: (i, j)),
        out_shape=jax.ShapeDtypeStruct((t, n), BF16),
        scratch_shapes=[pltpu.VMEM((tm, d), BF16)],
        compiler_params=_params(2),
        name="in_proj",
    )(x2, g, sc, sh, w_bf, colscale)


def _attn_a_kernel(lam_ref, qlo_ref, qhi_ref, k_ref, v_ref, tb_ref, olo_ref, ohi_ref,
                   s_ref, m_ref, l_ref, o_ref, *, tile, half, nq, out_scale):
    p = pl.program_id(2)
    hq = nq - 1 - p
    nvis = nq + 1
    ninf = -jnp.inf

    def stacked(q):
        lane = lax.broadcasted_iota(jnp.int32, q.shape, 1)
        zero = jnp.zeros_like(q)
        return jnp.concatenate([jnp.where(lane < half, q, zero), jnp.where(lane >= half, q, zero)], axis=0)

    qq_lo, qq_hi = stacked(qlo_ref[...]), stacked(qhi_ref[...])

    def fold(x, op):
        r = x[:, :LANES]
        for c in range(1, tile // LANES):
            r = op(r, x[:, c * LANES:(c + 1) * LANES])
        return r

    def pick(i, lo, hi):
        return jnp.where(i <= p, lo, hi)

    def key_tile(i):
        return jnp.where(i <= p, i, i - p - 1)

    def key_rows(i):
        return pl.ds(pl.multiple_of(key_tile(i) * tile, tile), tile)

    for i in range(nvis):
        bias = tb_ref[0, jnp.minimum(pick(i, p, hq) - key_tile(i), 2)]
        s = _dot_t(pick(i, qq_lo, qq_hi), k_ref[key_rows(i), :]) + jnp.concatenate([bias, bias], axis=0)
        s_ref[i] = s
        m_ref[i] = fold(s, jnp.maximum)

    def reduce_tiles(ref, op, init):
        lo = jnp.full(ref.shape[1:], init, F32)
        hi = lo
        for i in range(nvis):
            lo = op(lo, jnp.where(i <= p, ref[i], init))
            hi = op(hi, jnp.where(i <= p, init, ref[i]))
        return lo, hi

    m_lo, m_hi = reduce_tiles(m_ref, jnp.maximum, ninf)
    m_lo = jnp.max(m_lo, axis=-1, keepdims=True)
    m_hi = jnp.max(m_hi, axis=-1, keepdims=True)
    for i in range(nvis):
        e = jnp.exp2(s_ref[i] - pick(i, m_lo, m_hi))
        s_ref[i] = e
        l_ref[i] = fold(e, jnp.add)

    l_lo, l_hi = reduce_tiles(l_ref, jnp.add, 0.0)
    lam = lam_ref[0]

    def coefs(l):
        inv = 1.0 / jnp.sum(l, axis=-1, keepdims=True)
        return inv[:tile], lam * inv[tile:]

    (c0_lo, c1_lo), (c0_hi, c1_hi) = coefs(l_lo), coefs(l_hi)
    for i in range(nvis):
        e = s_ref[i]
        a = e[:tile] * pick(i, c0_lo, c0_hi) - e[tile:] * pick(i, c1_lo, c1_hi)
        o_ref[i] = _dot(a.astype(BF16), v_ref[key_rows(i), :])

    o_lo, o_hi = reduce_tiles(o_ref, jnp.add, 0.0)
    for o, out in ((o_lo, olo_ref), (o_hi, ohi_ref)):
        out[...] = (o * lax.rsqrt(jnp.mean(o * o, axis=-1, keepdims=True) + HEAD_EPS) * out_scale).astype(BF16)


def attn_a(proj, lam, tb, cfg):
    tile = cfg.attn_tile
    s = cfg.seq
    nq = s // tile
    assert nq % 2 == 0
    npair = nq // 2
    dv = 2 * cfg.da_qk
    assert dv == LANES
    kcol = cfg.ha
    vcol = 2 * cfg.ha
    kern = functools.partial(_attn_a_kernel, tile=tile, half=cfg.da_qk, nq=nq, out_scale=1.0 - cfg.lam_init)
    half_shape = jax.ShapeDtypeStruct((cfg.batch * npair * tile, cfg.ha * dv), BF16)
    y_lo, y_hi = pl.pallas_call(
        kern,
        grid=(cfg.batch, cfg.ha, npair),
        in_specs=[pl.BlockSpec(memory_space=pltpu.SMEM),
                  pl.BlockSpec((tile, LANES), lambda b, h, p: (b * nq + p, h)),
                  pl.BlockSpec((tile, LANES), lambda b, h, p: (b * nq + nq - 1 - p, h)),
                  pl.BlockSpec((s, LANES), lambda b, h, p: (b, kcol + h)),
                  pl.BlockSpec((s, LANES), lambda b, h, p: (b, vcol + h)),
                  pl.BlockSpec((1, 3, tile, tile), lambda b, h, p: (h, 0, 0, 0))],
        out_specs=[pl.BlockSpec((tile, LANES), lambda b, h, p: (b * npair + p, h)),
                   pl.BlockSpec((tile, LANES), lambda b, h, p: (b * npair + npair - 1 - p, h))],
        out_shape=[half_shape, half_shape],
        scratch_shapes=[pltpu.VMEM((nq + 1, 2 * tile, tile), F32),
                        pltpu.VMEM((nq + 1, 2 * tile, LANES), F32),
                        pltpu.VMEM((nq + 1, 2 * tile, LANES), F32),
                        pltpu.VMEM((nq + 1, tile, LANES), F32)],
        compiler_params=_params(3),
        name="attn_a",
    )(lam, proj, proj, proj, proj, tb)
    w = cfg.ha * dv
    rows = npair * tile
    return jnp.concatenate([y_lo.reshape(cfg.batch, rows, w), y_hi.reshape(cfg.batch, rows, w)],
                           axis=1).reshape(cfg.batch * s, w)


ATTN_B_TILES = 2


def _attn_b_kernel(q_ref, k_ref, v_ref, tb_ref, o_ref, *, tile, nwin):
    for u in range(ATTN_B_TILES):
        qi = pl.program_id(2) * ATTN_B_TILES + u
        q = q_ref[u * tile:(u + 1) * tile, :]
        ss, vs = [], []
        for j in range(nwin):
            kb = qi - (nwin - 1) + j
            start = pl.multiple_of(jnp.maximum(kb, 0) * tile, tile)
            ss.append(_dot_t(q, k_ref[pl.ds(start, tile), :]) + tb_ref[0, jnp.where(kb >= 0, j, nwin)])
            vs.append(v_ref[pl.ds(start, tile), :])
        m = jnp.max(ss[0], axis=-1, keepdims=True)
        for s in ss[1:]:
            m = jnp.maximum(m, jnp.max(s, axis=-1, keepdims=True))
        ps = [jnp.exp2(s - m) for s in ss]
        l = jnp.sum(ps[0], axis=-1, keepdims=True)
        for p in ps[1:]:
            l = l + jnp.sum(p, axis=-1, keepdims=True)
        acc = _dot(ps[0].astype(BF16), vs[0])
        for p, v in zip(ps[1:], vs[1:]):
            acc = acc + _dot(p.astype(BF16), v)
        o_ref[u * tile:(u + 1) * tile, :] = (acc * (1.0 / l)).astype(BF16)


def attn_b(proj, tb, cfg):
    t = proj.shape[0]
    tile = cfg.attn_tile
    s = cfg.seq
    nq = s // tile
    assert cfg.db == LANES and nq % ATTN_B_TILES == 0
    nstep = nq // ATTN_B_TILES
    rows = ATTN_B_TILES * tile
    nwin = (cfg.left_chunks * cfg.chunk) // tile + 1
    base = (2 * cfg.ha * 2 * cfg.da_qk + cfg.ha * 2 * cfg.da_qk) // LANES
    kern = functools.partial(_attn_b_kernel, tile=tile, nwin=nwin)
    return pl.pallas_call(
        kern,
        grid=(cfg.batch, cfg.hb, nstep),
        in_specs=[pl.BlockSpec((rows, LANES), lambda b, h, i: (b * nstep + i, base + h)),
                  pl.BlockSpec((s, LANES), lambda b, h, i: (b, base + cfg.hb + h)),
                  pl.BlockSpec((s, LANES), lambda b, h, i: (b, base + 2 * cfg.hb + h)),
                  pl.BlockSpec((1, nwin + 1, tile, tile), lambda b, h, i: (h, 0, 0, 0))],
        out_specs=pl.BlockSpec((rows, LANES), lambda b, h, i: (b * nstep + i, h)),
        out_shape=jax.ShapeDtypeStruct((t, cfg.hb * cfg.db), BF16),
        compiler_params=_params(3),
        name="attn_b",
    )(proj, proj, proj, tb)


def _merge_kernel(ya_ref, yb_ref, ga_ref, gb_ref, x_ref, gta_ref, scm_ref, shm_ref, gmoe_ref,
                  wua_ref, wub_ref, wo_ref, wrh_ref, wrl_ref, h1_ref, v_ref, lg_ref, *, sub):
    ns = v_ref.shape[0] // h1_ref.shape[0]
    for r0 in range(0, h1_ref.shape[0], sub):
        rows = slice(r0, r0 + sub)
        a = _dot(ya_ref[rows, :], wua_ref[...])
        b = _dot(yb_ref[rows, :], wub_ref[...])
        merged = _sigmoid(ga_ref[rows, :].astype(F32)) * a + _sigmoid(gb_ref[rows, :].astype(F32)) * b
        att = _dot(merged.astype(BF16), wo_ref[...])
        h1 = x_ref[rows, :] + gta_ref[0] * att
        h1_ref[rows, :] = h1
        y = h1 * lax.rsqrt(jnp.mean(h1 * h1, axis=-1, keepdims=True) + EPS) * gmoe_ref[...]
        v = y * (1.0 + scm_ref[0]) + shm_ref[0]
        _store_slabs(v_ref.at[pl.ds(r0 * ns, sub * ns), :], _pack_pairs(v), ns)
        v_hi, v_lo = _split_bf16(v)
        wrh = wrh_ref[...]
        lg_ref[:, rows] = _dot_t(wrh, v_hi) + _dot_t(wrh, v_lo) + _dot_t(wrl_ref[...], v_hi)


def merge(ya, yb, proj, x2, gta, scm, shm, gmoe, wua, wub, wo, wrh, wrl, cfg):
    t, d = x2.shape
    tm = min(cfg.merge_tm, cfg.seq)
    per_b = cfg.seq // tm
    wa = ya.shape[1]
    wb = yb.shape[1]
    gate0 = (proj.shape[1] - 2 * d) // d
    ne = wrh.shape[0]
    ns = d // (2 * LANES)
    const = lambda i: (0, 0)
    row3 = lambda i: (i // per_b, 0, 0)
    resident = lambda shape: pl.BlockSpec(shape, const, pipeline_mode=pl.Buffered(1))
    return pl.pallas_call(
        functools.partial(_merge_kernel, sub=min(cfg.merge_sub, tm)),
        grid=(t // tm,),
        in_specs=[pl.BlockSpec((tm, wa), lambda i: (i, 0)),
                  pl.BlockSpec((tm, wb), lambda i: (i, 0)),
                  pl.BlockSpec((tm, d), lambda i: (i, gate0)),
                  pl.BlockSpec((tm, d), lambda i: (i, gate0 + 1)),
                  pl.BlockSpec((tm, d), lambda i: (i, 0)),
                  pl.BlockSpec((1, 1, d), row3),
                  pl.BlockSpec((1, 1, d), row3),
                  pl.BlockSpec((1, 1, d), row3),
                  pl.BlockSpec((1, d), const),
                  resident((wa, d)),
                  resident((wb, d)),
                  resident((d, d)),
                  resident((ne, d)),
                  resident((ne, d))],
        out_specs=[pl.BlockSpec((tm, d), lambda i: (i, 0)),
                   pl.BlockSpec((tm * ns, LANES), lambda i: (i, 0)),
                   pl.BlockSpec((ne, tm), lambda i: (0, i))],
        out_shape=[jax.ShapeDtypeStruct((t, d), F32),
                   jax.ShapeDtypeStruct((t * ns, LANES), U32),
                   jax.ShapeDtypeStruct((ne, t), F32)],
        compiler_params=_params(1),
        name="merge",
    )(ya, yb, proj, proj, x2, gta, scm, shm, gmoe, wua, wub, wo, wrh, wrl)


def _route_kernel(lg_ref, rb_ref, tri_ref, w_ref, lin_ref, kin_ref, gend_ref, carry_ref,
                  *, n_exp, n_groups, topk_groups, top_k, scale):
    gsz = n_exp // n_groups
    tc = lg_ref.shape[1]
    ninf = -jnp.inf
    step = pl.program_id(0)

    @pl.when(step == 0)
    def _():
        carry_ref[...] = jnp.zeros(carry_ref.shape, F32)

    row = lax.broadcasted_iota(jnp.int32, (gsz, tc), 0).astype(F32)
    sc, bi = [], []
    for g in range(n_groups):
        s = _sigmoid(lg_ref[g * gsz:(g + 1) * gsz, :])
        sc.append(s)
        bi.append(s + rb_ref[g * gsz:(g + 1) * gsz, :])
    gs = []
    for x in bi:
        m1 = jnp.max(x, axis=0, keepdims=True)
        eq = x == m1
        cnt = jnp.sum(jnp.where(eq, 1.0, 0.0), axis=0, keepdims=True)
        second = jnp.max(jnp.where(eq, ninf, x), axis=0, keepdims=True)
        gs.append(m1 + jnp.where(cnt >= 2.0, m1, second))
    gmat = jnp.concatenate(gs, axis=0)
    grow = lax.broadcasted_iota(jnp.int32, gmat.shape, 0).astype(F32)
    gsel = jnp.zeros(gmat.shape, F32)
    for _ in range(topk_groups):
        m = jnp.max(gmat, axis=0, keepdims=True)
        idx = jnp.min(jnp.where(gmat == m, grow, float(n_groups)), axis=0, keepdims=True)
        hit = grow == idx
        gsel = jnp.where(hit, 1.0, gsel)
        gmat = jnp.where(hit, ninf, gmat)
    cur = [jnp.where(gsel[g:g + 1, :] > 0.5, bi[g], ninf) for g in range(n_groups)]
    ids = [row + float(g * gsz) for g in range(n_groups)]
    w_rows = []
    chosen = [jnp.zeros((gsz, tc), F32) for _ in range(n_groups)]
    kplus = [jnp.zeros((gsz, tc), F32) for _ in range(n_groups)]
    for k in range(top_k):
        mg = cur[0]
        for c in cur[1:]:
            mg = jnp.maximum(mg, c)
        m = jnp.max(mg, axis=0, keepdims=True)
        cand = jnp.where(cur[0] == m, ids[0], float(n_exp))
        for c, i in zip(cur[1:], ids[1:]):
            cand = jnp.minimum(cand, jnp.where(c == m, i, float(n_exp)))
        idx = jnp.min(cand, axis=0, keepdims=True)
        wsum = jnp.zeros((gsz, tc), F32)
        for g in range(n_groups):
            hit = ids[g] == idx
            wsum = wsum + jnp.where(hit, sc[g], 0.0)
            cur[g] = jnp.where(hit, ninf, cur[g])
            chosen[g] = jnp.where(hit, 1.0, chosen[g])
            kplus[g] = jnp.where(hit, float(k + 1), kplus[g])
        w_rows.append(jnp.sum(wsum, axis=0, keepdims=True))
    w = jnp.concatenate(w_rows, axis=0)
    w_ref[...] = w / jnp.sum(w, axis=0, keepdims=True) * scale

    onehot = jnp.concatenate(chosen, axis=0)
    incl = _dot(onehot.astype(BF16), tri_ref[...])
    lin_ref[...] = incl.astype(BF16)
    kin_ref[...] = jnp.concatenate(kplus, axis=0).astype(BF16)
    gps = tc // LANES
    lane = lax.broadcasted_iota(jnp.int32, (n_exp, LANES), 1)
    run = carry_ref[...]
    ends = jnp.zeros((n_exp, LANES), F32)
    for g in range(gps):
        ends = jnp.where(lane == gps + g, run, ends)
        run = run + incl[:, (g + 1) * LANES - 1:(g + 1) * LANES]
        ends = jnp.where(lane == g, run, ends)
    carry_ref[...] = run
    gend_ref[0] = ends


def route(lg, rb, cfg):
    ne, t = lg.shape
    nx = cfg.n_experts
    tc = min(cfg.route_tc, t)
    nsteps = t // tc
    tri = jnp.asarray(np.kron(np.eye(tc // LANES), np.triu(np.ones((LANES, LANES)))), BF16)
    kern = functools.partial(_route_kernel, n_exp=cfg.n_experts, n_groups=cfg.n_groups,
                             topk_groups=cfg.topk_groups, top_k=cfg.top_k, scale=cfg.routed_scale)
    return pl.pallas_call(
        kern,
        grid=(nsteps,),
        in_specs=[pl.BlockSpec((ne, tc), lambda i: (0, i)),
                  pl.BlockSpec((ne, tc), lambda i: (0, 0)),
                  pl.BlockSpec((tc, tc), lambda i: (0, 0))],
        out_specs=[pl.BlockSpec((cfg.top_k, tc), lambda i: (0, i)),
                   pl.BlockSpec((nx, tc), lambda i: (0, i)),
                   pl.BlockSpec((nx, tc), lambda i: (0, i)),
                   pl.BlockSpec((1, nx, LANES), lambda i: (i, 0, 0))],
        out_shape=[jax.ShapeDtypeStruct((cfg.top_k, t), F32),
                   jax.ShapeDtypeStruct((nx, t), BF16),
                   jax.ShapeDtypeStruct((nx, t), BF16),
                   jax.ShapeDtypeStruct((nsteps, nx, LANES), F32)],
        scratch_shapes=[pltpu.VMEM((cfg.n_experts, LANES), F32)],
        compiler_params=_params(1),
        name="route",
    )(lg, rb, tri)


SLOT_BLOCKS = 4


def _slots_kernel(be_ref, r0_ref, cnt_ref, gstart_ref, gend_ref, lin_ref, kin_ref, tok_ref, dst_ref,
                  *, bm, top_k, n_tok, n_assign):
    step = pl.program_id(0)
    rows = lax.broadcasted_iota(jnp.int32, (bm, LANES), 0)
    lane = lax.broadcasted_iota(jnp.int32, (bm, LANES), 1).astype(F32)
    ones = jnp.ones((16, LANES), BF16)
    gids = lax.broadcasted_iota(jnp.int32, (16, LANES), 1).astype(F32).astype(BF16)
    lane_row = lax.broadcasted_iota(jnp.int32, (1, bm), 1)
    for u in range(SLOT_BLOCKS):
        b = step * SLOT_BLOCKS + u
        e = be_ref[b]
        r0 = r0_ref[b]
        cnt = cnt_ref[b]
        r = (rows + r0).astype(F32)
        gstart = gstart_ref[pl.ds(e, 1), :]
        in_group = jnp.logical_and(gstart <= r, r < gend_ref[pl.ds(e, 1), :])
        group = jnp.where(in_group, 1.0, 0.0).astype(BF16)
        want = r - jnp.sum(jnp.where(in_group, gstart, 0.0), axis=1, keepdims=True) + 1.0
        kin = _dot(group, kin_ref[e])
        hit = jnp.logical_and(_dot(group, lin_ref[e]) == want, kin > 0.5)
        t_loc = _dot_t(ones, jnp.where(hit, lane, 0.0).astype(BF16))[0:1]
        kp1 = _dot_t(ones, jnp.where(hit, kin, 0.0).astype(BF16))[0:1]
        gid = _dot_t(gids, group)[0:1]
        tok = (gid * float(LANES) + t_loc).astype(jnp.int32)
        kk = kp1.astype(jnp.int32) - 1
        valid = (lane_row + r0) < cnt
        dummy = n_assign + lax.rem(b, 4) * bm + lane_row
        tok_ref[u] = jnp.where(valid, tok, lax.rem(b * bm + lane_row, n_tok))
        dst_ref[u] = jnp.where(valid, kk * n_tok + tok, dummy)


def slots(block_e, r0, cntb, gstart, gend, lin, kin, cfg, n_tok):
    nb = block_e.shape[0]
    bm = cfg.moe_rows
    assert nb % SLOT_BLOCKS == 0 and gstart.shape[1] == LANES and lin.shape[1:] == (LANES, LANES)
    kern = functools.partial(_slots_kernel, bm=bm, top_k=cfg.top_k, n_tok=n_tok, n_assign=n_tok * cfg.top_k)
    whole = lambda a: pl.BlockSpec(a.shape, lambda i, be, r0, cn: (0,) * a.ndim)
    out_blk = pl.BlockSpec((SLOT_BLOCKS, 1, bm), lambda i, be, r0, cn: (i, 0, 0))
    grid_spec = pltpu.PrefetchScalarGridSpec(
        num_scalar_prefetch=3,
        grid=(nb // SLOT_BLOCKS,),
        in_specs=[whole(gstart), whole(gend), whole(lin), whole(kin)],
        out_specs=[out_blk, out_blk],
    )
    return pl.pallas_call(
        kern,
        grid_spec=grid_spec,
        out_shape=[jax.ShapeDtypeStruct((nb, 1, bm), jnp.int32), jax.ShapeDtypeStruct((nb, 1, bm), jnp.int32)],
        compiler_params=_params(1),
        name="slots",
    )(block_e, r0, cntb, gstart, gend, lin, kin)


MOE_RING = 4
MOE_DUMMY = 4


def _moe_kernel(be_ref, tok0_ref, tok1_ref, tok2_ref, tokn_ref, dstp_ref, dstc_ref, v_hbm, wg_hbm, wu_hbm, wd_hbm,
                yk_hbm, xbuf, ybuf, xbf, wst_g, wst_u, wst_d, wgb, wub, wdb, gsem, ssem, wsem, *, bm, n_assign):
    b = pl.program_id(0)
    nb = pl.num_programs(0)
    slot = lax.rem(b, MOE_RING)
    ns = v_hbm.shape[1]

    def weight_copies(e, ws):
        return [pltpu.make_async_copy(src.at[e], dst.at[ws], wsem.at[ws])
                for src, dst in ((wg_hbm, wst_g), (wu_hbm, wst_u), (wd_hbm, wst_d))]

    def unpack_block(s):
        lo, hi = _unpack_pairs(_load_slabs(xbuf.at[s], bm, ns))
        xbf[s % 2] = jnp.concatenate([lo, hi], axis=1).astype(BF16)

    def gather_copy(t, r, s):
        return pltpu.make_async_copy(v_hbm.at[t], xbuf.at[s, pl.ds(r * ns, ns), :], gsem.at[s])

    def scatter_copy(r, d, s):
        return pltpu.make_async_copy(ybuf.at[s, pl.ds(r * ns, ns), :], yk_hbm.at[d], ssem.at[s])

    def wait_gather(s):
        for r in range(bm):
            gather_copy(0, r, s).wait()

    def wait_scatter(s):
        for r in range(bm):
            scatter_copy(r, 0, s).wait()

    @pl.when(b == 0)
    def _():
        ybuf[...] = jnp.zeros(ybuf.shape, U32)
        for s in range(MOE_RING - 1):
            for r in range(bm):
                scatter_copy(r, n_assign + s * bm + r, s).start(priority=r % 2)
        for s, tok_ref in enumerate((tok0_ref, tok1_ref, tok2_ref)):
            for r in range(bm):
                gather_copy(tok_ref[0, 0, r], r, s).start(priority=r % 2)
        wait_gather(0)
        unpack_block(0)
        for c in weight_copies(be_ref[0], 0):
            c.start()

    @pl.when(be_ref[nb + 1 + b] == 1)
    def _():
        ws = be_ref[2 * nb + 1 + b]
        nxt_e = be_ref[3 * nb + 1 + b]
        for c in weight_copies(be_ref[b], ws):
            c.wait()

        @pl.when(nxt_e >= 0)
        def _():
            for c in weight_copies(nxt_e, 1 - ws):
                c.start()

        wgb[...] = wst_g[ws].astype(BF16)
        wub[...] = wst_u[ws].astype(BF16)
        wdb[...] = wst_d[ws].astype(BF16)

    def block_step(s):
        nxt, prv = (s + 1) % MOE_RING, (s + 3) % MOE_RING
        wait_gather(nxt)
        wait_scatter(s)
        for r in range(bm):
            scatter_copy(r, dstp_ref[0, 0, r], prv).start(priority=r % 2)
        for r in range(bm):
            gather_copy(tokn_ref[0, 0, r], r, prv).start(priority=r % 2)
        x = xbf[s % 2]
        g = _dot(x, wgb[...])
        u = _dot(x, wub[...])
        h = (g * _sigmoid(g)) * u
        unpack_block(nxt)
        _store_slabs(ybuf.at[s], _pack_pairs(_dot(h.astype(BF16), wdb[...])), ns)

    last = be_ref[nb] - 1
    for s in range(MOE_RING):
        pl.when(jnp.logical_and(slot == s, b <= last))(functools.partial(block_step, s))

    @pl.when(b == last)
    def _():
        def drain(s):
            for r in range(bm):
                scatter_copy(r, dstc_ref[0, 0, r], s).start(priority=r % 2)
            for j in (2, 3):
                wait_gather((s + j) % MOE_RING)
            for j in (1, 2, 3, 0):
                wait_scatter((s + j) % MOE_RING)

        for s in range(MOE_RING):
            pl.when(slot == s)(functools.partial(drain, s))


def moe(vp, block_e, n_used, slot_tok, slot_dst, wg, wu, wd, cfg):
    d = wg.shape[1]
    ns = d // (2 * LANES)
    t = vp.shape[0] // ns
    bm = cfg.moe_rows
    v3 = vp.reshape(t, ns, LANES)
    nb = block_e.shape[0]
    assert nb >= 2
    de = wg.shape[2]
    n_assign = t * cfg.top_k
    first = (n_assign + 3 * bm + jnp.arange(bm, dtype=jnp.int32)).reshape(1, 1, bm)
    dst_ext = jnp.concatenate([first, slot_dst], axis=0)
    kern = functools.partial(_moe_kernel, bm=bm, n_assign=n_assign)
    smem_blk = lambda f: pl.BlockSpec((1, 1, bm), f, memory_space=pltpu.SMEM)
    grid_spec = pltpu.PrefetchScalarGridSpec(
        num_scalar_prefetch=1,
        grid=(nb,),
        in_specs=[smem_blk(lambda b, be: (0, 0, 0)),
                  smem_blk(lambda b, be: (1, 0, 0)),
                  smem_blk(lambda b, be: (min(2, nb - 1), 0, 0)),
                  smem_blk(lambda b, be: (jnp.minimum(b + 3, nb - 1), 0, 0)),
                  smem_blk(lambda b, be: (b, 0, 0)),
                  smem_blk(lambda b, be: (b + 1, 0, 0)),
                  pl.BlockSpec(memory_space=pl.ANY)] + [pl.BlockSpec(memory_space=pl.ANY)] * 3,
        out_specs=pl.BlockSpec(memory_space=pl.ANY),
        scratch_shapes=[pltpu.VMEM((MOE_RING, bm * ns, LANES), U32),
                        pltpu.VMEM((MOE_RING, bm * ns, LANES), U32),
                        pltpu.VMEM((2, bm, d), BF16),
                        pltpu.VMEM((2, d, de), F32), pltpu.VMEM((2, d, de), F32), pltpu.VMEM((2, de, d), F32),
                        pltpu.VMEM((d, de), BF16), pltpu.VMEM((d, de), BF16), pltpu.VMEM((de, d), BF16),
                        pltpu.SemaphoreType.DMA((MOE_RING,)), pltpu.SemaphoreType.DMA((MOE_RING,)),
                        pltpu.SemaphoreType.DMA((2,))],
    )
    idx = jnp.arange(nb, dtype=jnp.int32)
    is_first = jnp.logical_and(idx < n_used, block_e != jnp.concatenate([jnp.full((1,), -1, jnp.int32), block_e[:-1]]))
    stage = (jnp.cumsum(is_first.astype(jnp.int32)) - 1) % 2
    later_first = jnp.flip(lax.cummin(jnp.flip(jnp.where(is_first, idx, nb))))
    nxt_pos = jnp.concatenate([later_first[1:], jnp.full((1,), nb, jnp.int32)])
    nxt_e = jnp.sum(jnp.where(idx[None, :] == nxt_pos[:, None], block_e[None, :] + 1, 0), axis=1) - 1
    sched = jnp.concatenate([block_e, n_used.reshape(1), is_first.astype(jnp.int32), stage.astype(jnp.int32),
                             nxt_e.astype(jnp.int32)])
    return pl.pallas_call(
        kern,
        grid_spec=grid_spec,
        out_shape=jax.ShapeDtypeStruct((n_assign + MOE_DUMMY * bm, ns, LANES), U32),
        compiler_params=_params(1),
        name="moe",
    )(sched, slot_tok, slot_tok, slot_tok, slot_tok, dst_ext, dst_ext, v3, wg, wu, wd)


def _final_kernel(w_ref, v_ref, h1_ref, gtm_ref, gfin_ref, sg_ref, su_ref, sd_ref, *rest, top_k):
    yk_refs, o_ref, lo_ref, hi_ref = rest[:top_k], rest[top_k], rest[top_k + 1], rest[top_k + 2]
    tm = h1_ref.shape[0]
    ns = yk_refs[0].shape[1]

    def combine(t, carry):
        lo, hi = _unpack_pairs(yk_refs[0][t])
        acc_lo, acc_hi = w_ref[t, 0] * lo, w_ref[t, 0] * hi
        for k in range(1, top_k):
            lo, hi = _unpack_pairs(yk_refs[k][t])
            acc_lo, acc_hi = acc_lo + w_ref[t, k] * lo, acc_hi + w_ref[t, k] * hi
        rows = pl.ds(pl.multiple_of(t * ns, ns), ns)
        lo_ref[rows, :] = acc_lo
        hi_ref[rows, :] = acc_hi
        return carry

    lax.fori_loop(0, tm, combine, 0, unroll=8)
    routed = jnp.concatenate([_load_slabs(lo_ref, tm, ns), _load_slabs(hi_ref, tm, ns)], axis=1)

    vb = jnp.concatenate(_unpack_pairs(_load_slabs(v_ref, tm, ns)), axis=1).astype(BF16)
    g = _dot(vb, sg_ref[...])
    u = _dot(vb, su_ref[...])
    y = routed + _dot(((g * _sigmoid(g)) * u).astype(BF16), sd_ref[...])
    h2 = h1_ref[...] + gtm_ref[0] * y
    o_ref[...] = h2 * lax.rsqrt(jnp.mean(h2 * h2, axis=-1, keepdims=True) + EPS) * gfin_ref[...]


def final(vp, h1, yk3, w_tk, gtm, gfin, sg, su, sd, cfg):
    t, d = h1.shape
    tm = min(cfg.final_tm, cfg.seq)
    per_b = cfg.seq // tm
    nt = t // tm
    ds = sg.shape[1]
    nslab = d // (2 * LANES)
    const = lambda i: (0, 0)
    kern = functools.partial(_final_kernel, top_k=cfg.top_k)
    plane = lambda k: pl.BlockSpec((tm, nslab, LANES), lambda i: (k * nt + i, 0, 0))
    return pl.pallas_call(
        kern,
        grid=(nt,),
        in_specs=[pl.BlockSpec((tm, cfg.top_k), lambda i: (i, 0), memory_space=pltpu.SMEM),
                  pl.BlockSpec((tm * nslab, LANES), lambda i: (i, 0)),
                  pl.BlockSpec((tm, d), lambda i: (i, 0)),
                  pl.BlockSpec((1, 1, d), lambda i: (i // per_b, 0, 0)),
                  pl.BlockSpec((1, d), const),
                  pl.BlockSpec((d, ds), const),
                  pl.BlockSpec((d, ds), const),
                  pl.BlockSpec((ds, d), const)] + [plane(k) for k in range(cfg.top_k)],
        out_specs=pl.BlockSpec((tm, d), lambda i: (i, 0)),
        out_shape=jax.ShapeDtypeStruct((t, d), F32),
        scratch_shapes=[pltpu.VMEM((tm * nslab, LANES), F32), pltpu.VMEM((tm * nslab, LANES), F32)],
        compiler_params=_params(1),
        name="final",
    )(w_tk, vp, h1, gtm, gfin, sg, su, sd, *([yk3] * cfg.top_k))


def _t5_bucket(rel, cfg):
    nb = cfg.t5_buckets // 2
    ret = (rel > 0).astype(np.int32) * nb
    n = np.abs(rel)
    max_exact = nb // 2
    large = max_exact + (np.log(np.maximum(n, 1) / max_exact) / math.log(cfg.t5_max_dist / max_exact)
                         * (nb - max_exact)).astype(np.int32)
    large = np.minimum(large, nb - 1)
    return (ret + np.where(n < max_exact, n, large)).astype(np.int32)


def _toeplitz(vec, n):
    lead = vec.shape[:-1]
    flat = jnp.tile(vec, (1,) * len(lead) + (n,))[..., :n * (2 * n - 1)]
    return flat.reshape(lead + (n, 2 * n - 1))[..., :n]


def _rel_offsets(n):
    m = np.arange(2 * n)
    return np.where(m < n, m, m - 2 * n)


def _bias_tables_a(t5_bias, cfg):
    tile = cfg.attn_tile
    assert tile + 1 >= cfg.t5_max_dist and tile % cfg.chunk == 0
    ql = np.arange(tile)[:, None]
    kl = np.arange(tile)[None, :]
    allowed = (kl // cfg.chunk) <= (ql // cfg.chunk)
    r = _rel_offsets(tile)
    t5 = t5_bias.astype(F32)
    diag = _toeplitz(t5[_t5_bucket(r, cfg)].T, tile)
    prev = _toeplitz(t5[_t5_bucket(r - tile, cfg)].T, tile)
    far = jnp.broadcast_to(t5[_t5_bucket(np.array(-2 * tile), cfg)][:, None, None], diag.shape)
    return jnp.stack([jnp.where(allowed[None], diag, NEG), prev, far], axis=1)


def _bias_tables_b(rel_table, cfg):
    tile = cfg.attn_tile
    assert (cfg.left_chunks * cfg.chunk) % tile == 0 and tile % cfg.chunk == 0
    nwin = (cfg.left_chunks * cfg.chunk) // tile + 1
    ql = np.arange(tile)[:, None]
    kl = np.arange(tile)[None, :]
    r = _rel_offsets(tile)
    rt = rel_table.astype(F32)
    tabs = []
    for j in range(nwin):
        off = (j - (nwin - 1)) * tile
        idx = np.clip(r + off, -cfg.rel_clip, cfg.rel_clip) + cfg.rel_clip
        dchunk = ql // cfg.chunk - (kl + off) // cfg.chunk
        band = (dchunk >= 0) & (dchunk <= cfg.left_chunks)
        tabs.append(jnp.where(band[None], _toeplitz(rt[idx].T, tile), NEG))
    tabs.append(jnp.full_like(tabs[0], NEG))
    return jnp.stack(tabs, axis=1)


def _block_tables(counts, n_assign, cfg):
    e_ = cfg.n_experts
    bm = cfg.moe_rows
    nb = -(-(n_assign + e_ * (bm - 1)) // bm)
    pcounts = (counts + bm - 1) // bm * bm
    pends = jnp.cumsum(pcounts)
    pstarts = pends - pcounts
    start_b = jnp.arange(nb, dtype=jnp.int32) * bm
    block_e = jnp.minimum(jnp.sum((pends[None, :] <= start_b[:, None]).astype(jnp.int32), axis=1), e_ - 1)
    mine = jnp.arange(e_, dtype=jnp.int32)[None, :] == block_e[:, None]
    pstart_b = jnp.sum(jnp.where(mine, pstarts[None, :], 0), axis=1)
    count_b = jnp.sum(jnp.where(mine, counts[None, :], 0), axis=1)
    n_used = (pends[e_ - 1] // bm).astype(jnp.int32)
    return block_e.astype(jnp.int32), (start_b - pstart_b).astype(jnp.int32), count_b.astype(jnp.int32), n_used


def _forward(cfg, x, c, w_ada, b_ada, g_attn, w_in, lambda_qk, t5_bias, rel_bias_b, w_up_a, w_up_b, w_o,
             g_moe, w_router, router_bias, w_exp_gate, w_exp_up, w_exp_down,
             w_sh_gate, w_sh_up, w_sh_down, g_final):
    b_, s_, d_ = x.shape
    t_ = b_ * s_
    l = 0
    qk_a = cfg.ha * 2 * cfg.da_qk
    width_a = cfg.ha * 2 * cfg.da_qk
    width_b = cfg.hb * cfg.db
    in_cols = w_in.shape[2]

    rows = max(16, b_)
    c_pad = jnp.zeros((rows, d_), F32).at[:b_].set(c)
    mod = ada_mod(c_pad, w_ada[l], b_ada[l], cfg)[:b_]
    sh_a, sc_a, gt_a, sh_m, sc_m, gt_m = [m.reshape(b_, 1, d_) for m in jnp.split(mod, cfg.n_mod, axis=-1)]

    colscale = np.ones((1, in_cols), np.float32)
    colscale[0, :qk_a] = cfg.da_qk ** -0.5 * LOG2E
    qb0 = 2 * qk_a + width_a
    colscale[0, qb0:qb0 + width_b] = cfg.db ** -0.5 * LOG2E
    x2 = x.reshape(t_, d_)
    proj = in_proj(x2, g_attn[l].reshape(1, d_), sc_a, sh_a, w_in[l].astype(BF16), jnp.asarray(colscale), cfg)

    lq = lambda_qk[l].astype(F32)
    lam = jnp.exp(jnp.sum(lq[0] * lq[1])) - jnp.exp(jnp.sum(lq[2] * lq[3])) + cfg.lam_init
    ya = attn_a(proj, lam.reshape(1), _bias_tables_a(t5_bias * LOG2E, cfg), cfg)
    yb = attn_b(proj, _bias_tables_b(rel_bias_b[l] * LOG2E, cfg), cfg)

    ne_pad = max(LANES, cfg.n_experts)
    wrt = jnp.zeros((ne_pad, d_), F32).at[:cfg.n_experts].set(w_router[l].T)
    wrh, wrl = _split_bf16(wrt)
    h1, v, lg = merge(ya, yb, proj, x2, gt_a, sc_m, sh_m, g_moe[l].reshape(1, d_),
                      w_up_a[l].astype(BF16), w_up_b[l].astype(BF16), w_o[l].astype(BF16), wrh, wrl, cfg)

    tc = min(cfg.route_tc, t_)
    rb = jnp.zeros((ne_pad, tc), F32).at[:cfg.n_experts].set(
        jnp.broadcast_to(router_bias[l].astype(F32)[:, None], (cfg.n_experts, tc)))
    w_kt, lin, kin, gend_raw = route(lg, rb, cfg)
    gps, n_grp, nx = tc // LANES, t_ // LANES, cfg.n_experts
    assert n_grp <= LANES
    per_group = lambda a: jnp.pad(a.transpose(1, 0, 2).reshape(nx, n_grp), ((0, 0), (0, LANES - n_grp)),
                                  constant_values=1e9)
    gend, gstart = per_group(gend_raw[:, :, :gps]), per_group(gend_raw[:, :, gps:2 * gps])
    by_group = lambda a: jnp.pad(a.reshape(nx, n_grp, LANES), ((0, 0), (0, LANES - n_grp), (0, 0)))
    counts = gend[:, n_grp - 1].astype(jnp.int32)
    block_e, r0, count_b, n_used = _block_tables(counts, t_ * cfg.top_k, cfg)
    slot_tok, slot_dst = slots(block_e, r0, count_b, gstart, gend, by_group(lin), by_group(kin), cfg, t_)

    yk = moe(v, block_e, n_used, slot_tok, slot_dst, w_exp_gate[l], w_exp_up[l], w_exp_down[l], cfg)
    out = final(v, h1, yk, w_kt.T, gt_m, g_final.reshape(1, d_),
                w_sh_gate[l].astype(BF16), w_sh_up[l].astype(BF16), w_sh_down[l].astype(BF16), cfg)
    return out.reshape(b_, s_, d_)


def kernel(x, c, w_ada, b_ada, g_attn, w_in, lambda_qk, t5_bias, rel_bias_b, w_up_a, w_up_b, w_o, g_moe,
           w_router, router_bias, w_exp_gate, w_exp_up, w_exp_down, w_sh_gate, w_sh_up, w_sh_down, g_final):
    return _forward(Cfg(), x, c, w_ada, b_ada, g_attn, w_in, lambda_qk, t5_bias, rel_bias_b, w_up_a, w_up_b,
                    w_o, g_moe, w_router, router_bias, w_exp_gate, w_exp_up, w_exp_down,
                    w_sh_gate, w_sh_up, w_sh_down, g_final)
```

```python
import functools
import math
from typing import NamedTuple

import numpy as np
import jax
import jax.numpy as jnp
from jax import lax
from jax.experimental import pallas as pl
from jax.experimental.pallas import tpu as pltpu

F32 = jnp.float32
BF16 = jnp.bfloat16
NEG = -1e30
EPS = 1e-6
HEAD_EPS = 1e-5
LOG2E = 1.4426950408889634
LANES = 128
VMEM_LIMIT = 56 * 1024 * 1024


class Cfg(NamedTuple):
    batch: int = 8
    seq: int = 2048
    d_model: int = 2048
    chunk: int = 64
    ha: int = 8
    da_qk: int = 64
    hb: int = 8
    db: int = 128
    left_chunks: int = 8
    rel_clip: int = 256
    t5_buckets: int = 32
    t5_max_dist: int = 128
    n_experts: int = 64
    top_k: int = 8
    n_groups: int = 8
    topk_groups: int = 4
    d_expert: int = 512
    d_shared: int = 512
    routed_scale: float = 2.5
    n_mod: int = 6
    lam_init: float = 0.8 - 0.6 * math.exp(-0.3 * 0)
    attn_tile: int = 256
    moe_rows: int = 256
    proj_tm: int = 1024
    proj_tn: int = 1024
    merge_tm: int = 512
    merge_sub: int = 256
    route_tc: int = 512
    final_tm: int = 256
    ada_tn: int = 1024


def _sigmoid(x):
    return 1.0 / (1.0 + jnp.exp(-x))


def _split_bf16(x):
    hi = x.astype(BF16)
    lo = (x - hi.astype(F32)).astype(BF16)
    return hi, lo


def _dot(a, b):
    return jnp.dot(a, b, preferred_element_type=F32)


def _dot_t(a, b):
    return lax.dot_general(a, b, (((1,), (1,)), ((), ())), preferred_element_type=F32)


U32 = jnp.uint32
HI_MASK = 0xFFFF0000


def _pack_pairs(x):
    n = x.shape[1] // 2
    bits = pltpu.bitcast(x.astype(BF16).astype(F32), U32)
    return (bits[:, :n] >> 16) | (bits[:, n:] & U32(HI_MASK))


def _unpack_pairs(p):
    return pltpu.bitcast(p << 16, F32), pltpu.bitcast(p & U32(HI_MASK), F32)


def _load_slabs(ref, m, ns):
    return jnp.concatenate([ref[pl.ds(j, m, stride=ns), :] for j in range(ns)], axis=1)


def _store_slabs(ref, val, ns):
    m = val.shape[0]
    for j in range(ns):
        ref[pl.ds(j, m, stride=ns), :] = val[:, j * LANES:(j + 1) * LANES]


def _params(n_axes):
    return pltpu.CompilerParams(dimension_semantics=("arbitrary",) * n_axes,
                                vmem_limit_bytes=VMEM_LIMIT)


def _ada_kernel(c_ref, w_ref, b_ref, o_ref):
    c = c_ref[...]
    s_hi, s_lo = _split_bf16(c * _sigmoid(c))
    w_hi, w_lo = _split_bf16(w_ref[...])
    acc = _dot(s_hi, w_hi) + _dot(s_hi, w_lo) + _dot(s_lo, w_hi)
    o_ref[...] = acc + b_ref[...]


def ada_mod(c_pad, w_ada, b_ada, cfg):
    rows, d = c_pad.shape
    n = w_ada.shape[1]
    tn = min(cfg.ada_tn, n)
    return pl.pallas_call(
        _ada_kernel,
        grid=(n // tn,),
        in_specs=[pl.BlockSpec((rows, d), lambda j: (0, 0)),
                  pl.BlockSpec((d, tn), lambda j: (0, j)),
                  pl.BlockSpec((1, tn), lambda j: (0, j))],
        out_specs=pl.BlockSpec((rows, tn), lambda j: (0, j)),
        out_shape=jax.ShapeDtypeStruct((rows, n), F32),
        compiler_params=_params(1),
        name="ada_mod",
    )(c_pad, w_ada, b_ada.reshape(1, n))


def _inproj_kernel(x_ref, g_ref, sc_ref, sh_ref, w_ref, cs_ref, o_ref, u_ref):
    @pl.when(pl.program_id(1) == 0)
    def _():
        x = x_ref[...]
        y = x * lax.rsqrt(jnp.mean(x * x, axis=-1, keepdims=True) + EPS) * g_ref[...]
        u_ref[...] = (y * (1.0 + sc_ref[0]) + sh_ref[0]).astype(BF16)

    acc = _dot(u_ref[...], w_ref[...])
    o_ref[...] = (acc * cs_ref[...]).astype(BF16)


def in_proj(x2, g, sc, sh, w_bf, colscale, cfg):
    t, d = x2.shape
    n = w_bf.shape[1]
    tm = min(cfg.proj_tm, cfg.seq)
    tn = min(cfg.proj_tn, n)
    per_b = cfg.seq // tm
    return pl.pallas_call(
        _inproj_kernel,
        grid=(t // tm, n // tn),
        in_specs=[pl.BlockSpec((tm, d), lambda i, j: (i, 0)),
                  pl.BlockSpec((1, d), lambda i, j: (0, 0)),
                  pl.BlockSpec((1, 1, d), lambda i, j: (i // per_b, 0, 0)),
                  pl.BlockSpec((1, 1, d), lambda i, j: (i // per_b, 0, 0)),
                  pl.BlockSpec((d, tn), lambda i, j: (0, j)),
                  pl.BlockSpec((1, tn), lambda i, j: (0, j))],
        out_specs=pl.BlockSpec((tm, tn), lambda i, j: (i, j)),
        out_shape=jax.ShapeDtypeStruct((t, n), BF16),
        scratch_shapes=[pltpu.VMEM((tm, d), BF16)],
        compiler_params=_params(2),
        name="in_proj",
    )(x2, g, sc, sh, w_bf, colscale)


def _attn_a_kernel(lam_ref, qlo_ref, qhi_ref, k_ref, v_ref, tb_ref, olo_ref, ohi_ref,
                   s_ref, m_ref, l_ref, o_ref, *, tile, half, nq, out_scale):
    p = pl.program_id(2)
    hq = nq - 1 - p
    nvis = nq + 1
    ninf = -jnp.inf

    def stacked(q):
        lane = lax.broadcasted_iota(jnp.int32, q.shape, 1)
        zero = jnp.zeros_like(q)
        return jnp.concatenate([jnp.where(lane < half, q, zero), jnp.where(lane >= half, q, zero)], axis=0)

    qq_lo, qq_hi = stacked(qlo_ref[...]), stacked(qhi_ref[...])

    def fold(x, op):
        r = x[:, :LANES]
        for c in range(1, tile // LANES):
            r = op(r, x[:, c * LANES:(c + 1) * LANES])
        return r

    npair = nq // 2

    def pick(i, lo, hi):
        return hi if i >= npair else jnp.where(i <= p, lo, hi)

    def key_tile(i):
        return i - p - 1 if i >= npair else jnp.where(i <= p, i, i - p - 1)

    def key_rows(i):
        return pl.ds(pl.multiple_of(key_tile(i) * tile, tile), tile)

    for i in range(nvis):
        dist = nq - i if i >= npair else pick(i, p, hq) - key_tile(i)
        bias = tb_ref[0, min(dist, 2) if i >= npair else jnp.minimum(dist, 2)]
        s = _dot_t(pick(i, qq_lo, qq_hi), k_ref[key_rows(i), :]) + jnp.concatenate([bias, bias], axis=0)
        s_ref[i] = s
        m_ref[i] = fold(s, jnp.maximum)

    def reduce_tiles(ref, op, init):
        lo = jnp.full(ref.shape[1:], init, F32)
        hi = lo
        for i in range(nvis):
            if i >= npair:
                hi = op(hi, ref[i])
            else:
                lo = op(lo, jnp.where(i <= p, ref[i], init))
                hi = op(hi, jnp.where(i <= p, init, ref[i]))
        return lo, hi

    m_lo, m_hi = reduce_tiles(m_ref, jnp.maximum, ninf)
    m_lo = jnp.max(m_lo, axis=-1, keepdims=True)
    m_hi = jnp.max(m_hi, axis=-1, keepdims=True)
    for i in range(nvis):
        e = jnp.exp2(s_ref[i] - pick(i, m_lo, m_hi))
        s_ref[i] = e
        l_ref[i] = fold(e, jnp.add)

    l_lo, l_hi = reduce_tiles(l_ref, jnp.add, 0.0)
    lam = lam_ref[0]

    def coefs(l):
        inv = 1.0 / jnp.sum(l, axis=-1, keepdims=True)
        return inv[:tile], lam * inv[tile:]

    (c0_lo, c1_lo), (c0_hi, c1_hi) = coefs(l_lo), coefs(l_hi)
    for i in range(nvis):
        e = s_ref[i]
        a = e[:tile] * pick(i, c0_lo, c0_hi) - e[tile:] * pick(i, c1_lo, c1_hi)
        o_ref[i] = _dot(a.astype(BF16), v_ref[key_rows(i), :])

    o_lo, o_hi = reduce_tiles(o_ref, jnp.add, 0.0)
    for o, out in ((o_lo, olo_ref), (o_hi, ohi_ref)):
        out[...] = (o * lax.rsqrt(jnp.mean(o * o, axis=-1, keepdims=True) + HEAD_EPS) * out_scale).astype(BF16)


def attn_a(proj, lam, tb, cfg):
    tile = cfg.attn_tile
    s = cfg.seq
    nq = s // tile
    assert nq % 2 == 0
    npair = nq // 2
    dv = 2 * cfg.da_qk
    assert dv == LANES
    kcol = cfg.ha
    vcol = 2 * cfg.ha
    kern = functools.partial(_attn_a_kernel, tile=tile, half=cfg.da_qk, nq=nq, out_scale=1.0 - cfg.lam_init)
    half_shape = jax.ShapeDtypeStruct((cfg.batch * npair * tile, cfg.ha * dv), BF16)
    y_lo, y_hi = pl.pallas_call(
        kern,
        grid=(cfg.batch, cfg.ha, npair),
        in_specs=[pl.BlockSpec(memory_space=pltpu.SMEM),
                  pl.BlockSpec((tile, LANES), lambda b, h, p: (b * nq + p, h)),
                  pl.BlockSpec((tile, LANES), lambda b, h, p: (b * nq + nq - 1 - p, h)),
                  pl.BlockSpec((s, LANES), lambda b, h, p: (b, kcol + h)),
                  pl.BlockSpec((s, LANES), lambda b, h, p: (b, vcol + h)),
                  pl.BlockSpec((1, 3, tile, tile), lambda b, h, p: (h, 0, 0, 0))],
        out_specs=[pl.BlockSpec((tile, LANES), lambda b, h, p: (b * npair + p, h)),
                   pl.BlockSpec((tile, LANES), lambda b, h, p: (b * npair + npair - 1 - p, h))],
        out_shape=[half_shape, half_shape],
        scratch_shapes=[pltpu.VMEM((nq + 1, 2 * tile, tile), F32),
                        pltpu.VMEM((nq + 1, 2 * tile, LANES), F32),
                        pltpu.VMEM((nq + 1, 2 * tile, LANES), F32),
                        pltpu.VMEM((nq + 1, tile, LANES), F32)],
        compiler_params=_params(3),
        name="attn_a",
    )(lam, proj, proj, proj, proj, tb)
    w = cfg.ha * dv
    rows = npair * tile
    return jnp.concatenate([y_lo.reshape(cfg.batch, rows, w), y_hi.reshape(cfg.batch, rows, w)],
                           axis=1).reshape(cfg.batch * s, w)


ATTN_B_TILES = 2


def _attn_b_kernel(q_ref, k_ref, v_ref, tb_ref, o_ref, *, tile, nwin):
    for u in range(ATTN_B_TILES):
        qi = pl.program_id(2) * ATTN_B_TILES + u
        q = q_ref[u * tile:(u + 1) * tile, :]
        ss, vs = [], []
        for j in range(nwin):
            kb = qi - (nwin - 1) + j
            start = pl.multiple_of(jnp.maximum(kb, 0) * tile, tile)
            ss.append(_dot_t(q, k_ref[pl.ds(start, tile), :]) + tb_ref[0, jnp.where(kb >= 0, j, nwin)])
            vs.append(v_ref[pl.ds(start, tile), :])
        m = jnp.max(ss[0], axis=-1, keepdims=True)
        for s in ss[1:]:
            m = jnp.maximum(m, jnp.max(s, axis=-1, keepdims=True))
        ps = [jnp.exp2(s - m) for s in ss]
        l = jnp.sum(ps[0], axis=-1, keepdims=True)
        for p in ps[1:]:
            l = l + jnp.sum(p, axis=-1, keepdims=True)
        acc = _dot(ps[0].astype(BF16), vs[0])
        for p, v in zip(ps[1:], vs[1:]):
            acc = acc + _dot(p.astype(BF16), v)
        o_ref[u * tile:(u + 1) * tile, :] = (acc * (1.0 / l)).astype(BF16)


def attn_b(proj, tb, cfg):
    t = proj.shape[0]
    tile = cfg.attn_tile
    s = cfg.seq
    nq = s // tile
    assert cfg.db == LANES and nq % ATTN_B_TILES == 0
    nstep = nq // ATTN_B_TILES
    rows = ATTN_B_TILES * tile
    nwin = (cfg.left_chunks * cfg.chunk) // tile + 1
    base = (2 * cfg.ha * 2 * cfg.da_qk + cfg.ha * 2 * cfg.da_qk) // LANES
    kern = functools.partial(_attn_b_kernel, tile=tile, nwin=nwin)
    return pl.pallas_call(
        kern,
        grid=(cfg.batch, cfg.hb, nstep),
        in_specs=[pl.BlockSpec((rows, LANES), lambda b, h, i: (b * nstep + i, base + h)),
                  pl.BlockSpec((s, LANES), lambda b, h, i: (b, base + cfg.hb + h)),
                  pl.BlockSpec((s, LANES), lambda b, h, i: (b, base + 2 * cfg.hb + h)),
                  pl.BlockSpec((1, nwin + 1, tile, tile), lambda b, h, i: (h, 0, 0, 0))],
        out_specs=pl.BlockSpec((rows, LANES), lambda b, h, i: (b * nstep + i, h)),
        out_shape=jax.ShapeDtypeStruct((t, cfg.hb * cfg.db), BF16),
        compiler_params=_params(3),
        name="attn_b",
    )(proj, proj, proj, tb)


def _merge_kernel(ya_ref, yb_ref, ga_ref, gb_ref, x_ref, gta_ref, scm_ref, shm_ref, gmoe_ref,
                  wua_ref, wub_ref, wo_ref, wrh_ref, wrl_ref, h1_ref, v_ref, lg_ref, *, sub):
    ns = v_ref.shape[0] // h1_ref.shape[0]
    for r0 in range(0, h1_ref.shape[0], sub):
        rows = slice(r0, r0 + sub)
        a = _dot(ya_ref[rows, :], wua_ref[...])
        b = _dot(yb_ref[rows, :], wub_ref[...])
        merged = _sigmoid(ga_ref[rows, :].astype(F32)) * a + _sigmoid(gb_ref[rows, :].astype(F32)) * b
        att = _dot(merged.astype(BF16), wo_ref[...])
        h1 = x_ref[rows, :] + gta_ref[0] * att
        h1_ref[rows, :] = h1
        y = h1 * lax.rsqrt(jnp.mean(h1 * h1, axis=-1, keepdims=True) + EPS) * gmoe_ref[...]
        v = y * (1.0 + scm_ref[0]) + shm_ref[0]
        _store_slabs(v_ref.at[pl.ds(r0 * ns, sub * ns), :], _pack_pairs(v), ns)
        v_hi, v_lo = _split_bf16(v)
        wrh = wrh_ref[...]
        lg_ref[:, rows] = _dot_t(wrh, v_hi) + _dot_t(wrh, v_lo) + _dot_t(wrl_ref[...], v_hi)


def merge(ya, yb, proj, x2, gta, scm, shm, gmoe, wua, wub, wo, wrh, wrl, cfg):
    t, d = x2.shape
    tm = min(cfg.merge_tm, cfg.seq)
    per_b = cfg.seq // tm
    wa = ya.shape[1]
    wb = yb.shape[1]
    gate0 = (proj.shape[1] - 2 * d) // d
    ne = wrh.shape[0]
    ns = d // (2 * LANES)
    const = lambda i: (0, 0)
    row3 = lambda i: (i // per_b, 0, 0)
    resident = lambda shape: pl.BlockSpec(shape, const, pipeline_mode=pl.Buffered(1))
    return pl.pallas_call(
        functools.partial(_merge_kernel, sub=min(cfg.merge_sub, tm)),
        grid=(t // tm,),
        in_specs=[pl.BlockSpec((tm, wa), lambda i: (i, 0)),
                  pl.BlockSpec((tm, wb), lambda i: (i, 0)),
                  pl.BlockSpec((tm, d), lambda i: (i, gate0)),
                  pl.BlockSpec((tm, d), lambda i: (i, gate0 + 1)),
                  pl.BlockSpec((tm, d), lambda i: (i, 0)),
                  pl.BlockSpec((1, 1, d), row3),
                  pl.BlockSpec((1, 1, d), row3),
                  pl.BlockSpec((1, 1, d), row3),
                  pl.BlockSpec((1, d), const),
                  resident((wa, d)),
                  resident((wb, d)),
                  resident((d, d)),
                  resident((ne, d)),
                  resident((ne, d))],
        out_specs=[pl.BlockSpec((tm, d), lambda i: (i, 0)),
                   pl.BlockSpec((tm * ns, LANES), lambda i: (i, 0)),
                   pl.BlockSpec((ne, tm), lambda i: (0, i))],
        out_shape=[jax.ShapeDtypeStruct((t, d), F32),
                   jax.ShapeDtypeStruct((t * ns, LANES), U32),
                   jax.ShapeDtypeStruct((ne, t), F32)],
        compiler_params=_params(1),
        name="merge",
    )(ya, yb, proj, proj, x2, gta, scm, shm, gmoe, wua, wub, wo, wrh, wrl)


def _route_kernel(lg_ref, rb_ref, tri_ref, w_ref, lin_ref, kin_ref, gend_ref, carry_ref,
                  *, n_exp, n_groups, topk_groups, top_k, scale):
    gsz = n_exp // n_groups
    tc = lg_ref.shape[1]
    ninf = -jnp.inf
    step = pl.program_id(0)

    @pl.when(step == 0)
    def _():
        carry_ref[...] = jnp.zeros(carry_ref.shape, F32)

    row = lax.broadcasted_iota(jnp.int32, (gsz, tc), 0).astype(F32)
    sc, bi = [], []
    for g in range(n_groups):
        s = _sigmoid(lg_ref[g * gsz:(g + 1) * gsz, :])
        sc.append(s)
        bi.append(s + rb_ref[g * gsz:(g + 1) * gsz, :])
    gs = []
    for x in bi:
        m1 = jnp.max(x, axis=0, keepdims=True)
        eq = x == m1
        cnt = jnp.sum(jnp.where(eq, 1.0, 0.0), axis=0, keepdims=True)
        second = jnp.max(jnp.where(eq, ninf, x), axis=0, keepdims=True)
        gs.append(m1 + jnp.where(cnt >= 2.0, m1, second))
    gmat = jnp.concatenate(gs, axis=0)
    grow = lax.broadcasted_iota(jnp.int32, gmat.shape, 0).astype(F32)
    gsel = jnp.zeros(gmat.shape, F32)
    for _ in range(topk_groups):
        m = jnp.max(gmat, axis=0, keepdims=True)
        idx = jnp.min(jnp.where(gmat == m, grow, float(n_groups)), axis=0, keepdims=True)
        hit = grow == idx
        gsel = jnp.where(hit, 1.0, gsel)
        gmat = jnp.where(hit, ninf, gmat)
    cur = [jnp.where(gsel[g:g + 1, :] > 0.5, bi[g], ninf) for g in range(n_groups)]
    ids = [row + float(g * gsz) for g in range(n_groups)]
    w_rows = []
    chosen = [jnp.zeros((gsz, tc), F32) for _ in range(n_groups)]
    kplus = [jnp.zeros((gsz, tc), F32) for _ in range(n_groups)]
    for k in range(top_k):
        mg = cur[0]
        for c in cur[1:]:
            mg = jnp.maximum(mg, c)
        m = jnp.max(mg, axis=0, keepdims=True)
        cand = jnp.where(cur[0] == m, ids[0], float(n_exp))
        for c, i in zip(cur[1:], ids[1:]):
            cand = jnp.minimum(cand, jnp.where(c == m, i, float(n_exp)))
        idx = jnp.min(cand, axis=0, keepdims=True)
        wsum = jnp.zeros((gsz, tc), F32)
        for g in range(n_groups):
            hit = ids[g] == idx
            wsum = wsum + jnp.where(hit, sc[g], 0.0)
            cur[g] = jnp.where(hit, ninf, cur[g])
            chosen[g] = jnp.where(hit, 1.0, chosen[g])
            kplus[g] = jnp.where(hit, float(k + 1), kplus[g])
        w_rows.append(jnp.sum(wsum, axis=0, keepdims=True))
    w = jnp.concatenate(w_rows, axis=0)
    w_ref[...] = w / jnp.sum(w, axis=0, keepdims=True) * scale

    onehot = jnp.concatenate(chosen, axis=0)
    incl = _dot(onehot.astype(BF16), tri_ref[...])
    lin_ref[...] = incl.astype(BF16)
    kin_ref[...] = jnp.concatenate(kplus, axis=0).astype(BF16)
    gps = tc // LANES
    lane = lax.broadcasted_iota(jnp.int32, (n_exp, LANES), 1)
    run = carry_ref[...]
    ends = jnp.zeros((n_exp, LANES), F32)
    for g in range(gps):
        ends = jnp.where(lane == gps + g, run, ends)
        run = run + incl[:, (g + 1) * LANES - 1:(g + 1) * LANES]
        ends = jnp.where(lane == g, run, ends)
    carry_ref[...] = run
    gend_ref[0] = ends


def route(lg, rb, cfg):
    ne, t = lg.shape
    nx = cfg.n_experts
    tc = min(cfg.route_tc, t)
    nsteps = t // tc
    tri = jnp.asarray(np.kron(np.eye(tc // LANES), np.triu(np.ones((LANES, LANES)))), BF16)
    kern = functools.partial(_route_kernel, n_exp=cfg.n_experts, n_groups=cfg.n_groups,
                             topk_groups=cfg.topk_groups, top_k=cfg.top_k, scale=cfg.routed_scale)
    return pl.pallas_call(
        kern,
        grid=(nsteps,),
        in_specs=[pl.BlockSpec((ne, tc), lambda i: (0, i)),
                  pl.BlockSpec((ne, tc), lambda i: (0, 0)),
                  pl.BlockSpec((tc, tc), lambda i: (0, 0))],
        out_specs=[pl.BlockSpec((cfg.top_k, tc), lambda i: (0, i)),
                   pl.BlockSpec((nx, tc), lambda i: (0, i)),
                   pl.BlockSpec((nx, tc), lambda i: (0, i)),
                   pl.BlockSpec((1, nx, LANES), lambda i: (i, 0, 0))],
        out_shape=[jax.ShapeDtypeStruct((cfg.top_k, t), F32),
                   jax.ShapeDtypeStruct((nx, t), BF16),
                   jax.ShapeDtypeStruct((nx, t), BF16),
                   jax.ShapeDtypeStruct((nsteps, nx, LANES), F32)],
        scratch_shapes=[pltpu.VMEM((cfg.n_experts, LANES), F32)],
        compiler_params=_params(1),
        name="route",
    )(lg, rb, tri)


SLOT_BLOCKS = 4


def _slots_kernel(be_ref, r0_ref, cnt_ref, gstart_ref, gend_ref, lin_ref, kin_ref, tok_ref, dst_ref,
                  *, bm, top_k, n_tok, n_assign):
    step = pl.program_id(0)
    rows = lax.broadcasted_iota(jnp.int32, (bm, LANES), 0)
    lane = lax.broadcasted_iota(jnp.int32, (bm, LANES), 1).astype(F32)
    ones = jnp.ones((16, LANES), BF16)
    gids = lax.broadcasted_iota(jnp.int32, (16, LANES), 1).astype(F32).astype(BF16)
    lane_row = lax.broadcasted_iota(jnp.int32, (1, bm), 1)
    for u in range(SLOT_BLOCKS):
        b = step * SLOT_BLOCKS + u
        e = be_ref[b]
        r0 = r0_ref[b]
        cnt = cnt_ref[b]
        r = (rows + r0).astype(F32)
        gstart = gstart_ref[pl.ds(e, 1), :]
        in_group = jnp.logical_and(gstart <= r, r < gend_ref[pl.ds(e, 1), :])
        group = jnp.where(in_group, 1.0, 0.0).astype(BF16)
        want = r - jnp.sum(jnp.where(in_group, gstart, 0.0), axis=1, keepdims=True) + 1.0
        kin = _dot(group, kin_ref[e])
        hit = jnp.logical_and(_dot(group, lin_ref[e]) == want, kin > 0.5)
        t_loc = _dot_t(ones, jnp.where(hit, lane, 0.0).astype(BF16))[0:1]
        kp1 = _dot_t(ones, jnp.where(hit, kin, 0.0).astype(BF16))[0:1]
        gid = _dot_t(gids, group)[0:1]
        tok = (gid * float(LANES) + t_loc).astype(jnp.int32)
        kk = kp1.astype(jnp.int32) - 1
        valid = (lane_row + r0) < cnt
        dummy = n_assign + lax.rem(b, 4) * bm + lane_row
        tok_ref[u] = jnp.where(valid, tok, lax.rem(b * bm + lane_row, n_tok))
        dst_ref[u] = jnp.where(valid, kk * n_tok + tok, dummy)


def slots(block_e, r0, cntb, gstart, gend, lin, kin, cfg, n_tok):
    nb = block_e.shape[0]
    bm = cfg.moe_rows
    assert nb % SLOT_BLOCKS == 0 and gstart.shape[1] == LANES and lin.shape[1:] == (LANES, LANES)
    kern = functools.partial(_slots_kernel, bm=bm, top_k=cfg.top_k, n_tok=n_tok, n_assign=n_tok * cfg.top_k)
    whole = lambda a: pl.BlockSpec(a.shape, lambda i, be, r0, cn: (0,) * a.ndim)
    out_blk = pl.BlockSpec((SLOT_BLOCKS, 1, bm), lambda i, be, r0, cn: (i, 0, 0))
    grid_spec = pltpu.PrefetchScalarGridSpec(
        num_scalar_prefetch=3,
        grid=(nb // SLOT_BLOCKS,),
        in_specs=[whole(gstart), whole(gend), whole(lin), whole(kin)],
        out_specs=[out_blk, out_blk],
    )
    return pl.pallas_call(
        kern,
        grid_spec=grid_spec,
        out_shape=[jax.ShapeDtypeStruct((nb, 1, bm), jnp.int32), jax.ShapeDtypeStruct((nb, 1, bm), jnp.int32)],
        compiler_params=_params(1),
        name="slots",
    )(block_e, r0, cntb, gstart, gend, lin, kin)


MOE_RING = 4
MOE_DUMMY = 4


def _moe_kernel(be_ref, tok0_ref, tok1_ref, tok2_ref, tokn_ref, dstp_ref, dstc_ref, v_hbm, wg_hbm, wu_hbm, wd_hbm,
                yk_hbm, xbuf, ybuf, xbf, wst_g, wst_u, wst_d, wgb, wub, wdb, gsem, ssem, wsem, *, bm, n_assign):
    b = pl.program_id(0)
    nb = pl.num_programs(0)
    slot = lax.rem(b, MOE_RING)
    ns = v_hbm.shape[1]

    def weight_copies(e, ws):
        return [pltpu.make_async_copy(src.at[e], dst.at[ws], wsem.at[ws])
                for src, dst in ((wg_hbm, wst_g), (wu_hbm, wst_u), (wd_hbm, wst_d))]

    def unpack_block(s):
        lo, hi = _unpack_pairs(_load_slabs(xbuf.at[s], bm, ns))
        xbf[s % 2] = jnp.concatenate([lo, hi], axis=1).astype(BF16)

    def gather_copy(t, r, s):
        return pltpu.make_async_copy(v_hbm.at[t], xbuf.at[s, pl.ds(r * ns, ns), :], gsem.at[s])

    def scatter_copy(r, d, s):
        return pltpu.make_async_copy(ybuf.at[s, pl.ds(r * ns, ns), :], yk_hbm.at[d], ssem.at[s])

    def wait_gather(s):
        for r in range(bm):
            gather_copy(0, r, s).wait()

    def wait_scatter(s):
        for r in range(bm):
            scatter_copy(r, 0, s).wait()

    @pl.when(b == 0)
    def _():
        ybuf[...] = jnp.zeros(ybuf.shape, U32)
        for s in range(MOE_RING - 1):
            for r in range(bm):
                scatter_copy(r, n_assign + s * bm + r, s).start(priority=r % 2)
        for s, tok_ref in enumerate((tok0_ref, tok1_ref, tok2_ref)):
            for r in range(bm):
                gather_copy(tok_ref[0, 0, r], r, s).start(priority=r % 2)
        wait_gather(0)
        unpack_block(0)
        for c in weight_copies(be_ref[0], 0):
            c.start()

    @pl.when(be_ref[nb + 1 + b] == 1)
    def _():
        ws = be_ref[2 * nb + 1 + b]
        nxt_e = be_ref[3 * nb + 1 + b]
        for c in weight_copies(be_ref[b], ws):
            c.wait()

        @pl.when(nxt_e >= 0)
        def _():
            for c in weight_copies(nxt_e, 1 - ws):
                c.start()

        wgb[...] = wst_g[ws].astype(BF16)
        wub[...] = wst_u[ws].astype(BF16)
        wdb[...] = wst_d[ws].astype(BF16)

    def block_step(s):
        nxt, prv = (s + 1) % MOE_RING, (s + 3) % MOE_RING
        wait_gather(nxt)
        wait_scatter(s)
        for r in range(bm):
            scatter_copy(r, dstp_ref[0, 0, r], prv).start(priority=r % 2)
        for r in range(bm):
            gather_copy(tokn_ref[0, 0, r], r, prv).start(priority=r % 2)
        x = xbf[s % 2]
        g = _dot(x, wgb[...])
        u = _dot(x, wub[...])
        h = (g * _sigmoid(g)) * u
        unpack_block(nxt)
        _store_slabs(ybuf.at[s], _pack_pairs(_dot(h.astype(BF16), wdb[...])), ns)

    last = be_ref[nb] - 1
    for s in range(MOE_RING):
        pl.when(jnp.logical_and(slot == s, b <= last))(functools.partial(block_step, s))

    @pl.when(b == last)
    def _():
        def drain(s):
            for r in range(bm):
                scatter_copy(r, dstc_ref[0, 0, r], s).start(priority=r % 2)
            for j in (2, 3):
                wait_gather((s + j) % MOE_RING)
            for j in (1, 2, 3, 0):
                wait_scatter((s + j) % MOE_RING)

        for s in range(MOE_RING):
            pl.when(slot == s)(functools.partial(drain, s))


def moe(vp, block_e, n_used, slot_tok, slot_dst, wg, wu, wd, cfg):
    d = wg.shape[1]
    ns = d // (2 * LANES)
    t = vp.shape[0] // ns
    bm = cfg.moe_rows
    v3 = vp.reshape(t, ns, LANES)
    nb = block_e.shape[0]
    assert nb >= 2
    de = wg.shape[2]
    n_assign = t * cfg.top_k
    first = (n_assign + 3 * bm + jnp.arange(bm, dtype=jnp.int32)).reshape(1, 1, bm)
    dst_ext = jnp.concatenate([first, slot_dst], axis=0)
    kern = functools.partial(_moe_kernel, bm=bm, n_assign=n_assign)
    smem_blk = lambda f: pl.BlockSpec((1, 1, bm), f, memory_space=pltpu.SMEM)
    grid_spec = pltpu.PrefetchScalarGridSpec(
        num_scalar_prefetch=1,
        grid=(nb,),
        in_specs=[smem_blk(lambda b, be: (0, 0, 0)),
                  smem_blk(lambda b, be: (1, 0, 0)),
                  smem_blk(lambda b, be: (min(2, nb - 1), 0, 0)),
                  smem_blk(lambda b, be: (jnp.minimum(b + 3, nb - 1), 0, 0)),
                  smem_blk(lambda b, be: (b, 0, 0)),
                  smem_blk(lambda b, be: (b + 1, 0, 0)),
                  pl.BlockSpec(memory_space=pl.ANY)] + [pl.BlockSpec(memory_space=pl.ANY)] * 3,
        out_specs=pl.BlockSpec(memory_space=pl.ANY),
        scratch_shapes=[pltpu.VMEM((MOE_RING, bm * ns, LANES), U32),
                        pltpu.VMEM((MOE_RING, bm * ns, LANES), U32),
                        pltpu.VMEM((2, bm, d), BF16),
                        pltpu.VMEM((2, d, de), F32), pltpu.VMEM((2, d, de), F32), pltpu.VMEM((2, de, d), F32),
                        pltpu.VMEM((d, de), BF16), pltpu.VMEM((d, de), BF16), pltpu.VMEM((de, d), BF16),
                        pltpu.SemaphoreType.DMA((MOE_RING,)), pltpu.SemaphoreType.DMA((MOE_RING,)),
                        pltpu.SemaphoreType.DMA((2,))],
    )
    idx = jnp.arange(nb, dtype=jnp.int32)
    is_first = jnp.logical_and(idx < n_used, block_e != jnp.concatenate([jnp.full((1,), -1, jnp.int32), block_e[:-1]]))
    stage = (jnp.cumsum(is_first.astype(jnp.int32)) - 1) % 2
    later_first = jnp.flip(lax.cummin(jnp.flip(jnp.where(is_first, idx, nb))))
    nxt_pos = jnp.concatenate([later_first[1:], jnp.full((1,), nb, jnp.int32)])
    nxt_e = jnp.sum(jnp.where(idx[None, :] == nxt_pos[:, None], block_e[None, :] + 1, 0), axis=1) - 1
    sched = jnp.concatenate([block_e, n_used.reshape(1), is_first.astype(jnp.int32), stage.astype(jnp.int32),
                             nxt_e.astype(jnp.int32)])
    return pl.pallas_call(
        kern,
        grid_spec=grid_spec,
        out_shape=jax.ShapeDtypeStruct((n_assign + MOE_DUMMY * bm, ns, LANES), U32),
        compiler_params=_params(1),
        name="moe",
    )(sched, slot_tok, slot_tok, slot_tok, slot_tok, dst_ext, dst_ext, v3, wg, wu, wd)


def _final_kernel(w_ref, v_ref, h1_ref, gtm_ref, gfin_ref, sg_ref, su_ref, sd_ref, *rest, top_k):
    yk_refs, o_ref, lo_ref, hi_ref = rest[:top_k], rest[top_k], rest[top_k + 1], rest[top_k + 2]
    tm = h1_ref.shape[0]
    ns = yk_refs[0].shape[1]

    def combine(t, carry):
        lo, hi = _unpack_pairs(yk_refs[0][t])
        acc_lo, acc_hi = w_ref[t, 0] * lo, w_ref[t, 0] * hi
        for k in range(1, top_k):
            lo, hi = _unpack_pairs(yk_refs[k][t])
            acc_lo, acc_hi = acc_lo + w_ref[t, k] * lo, acc_hi + w_ref[t, k] * hi
        rows = pl.ds(pl.multiple_of(t * ns, ns), ns)
        lo_ref[rows, :] = acc_lo
        hi_ref[rows, :] = acc_hi
        return carry

    lax.fori_loop(0, tm, combine, 0, unroll=8)
    routed = jnp.concatenate([_load_slabs(lo_ref, tm, ns), _load_slabs(hi_ref, tm, ns)], axis=1)

    vb = jnp.concatenate(_unpack_pairs(_load_slabs(v_ref, tm, ns)), axis=1).astype(BF16)
    g = _dot(vb, sg_ref[...])
    u = _dot(vb, su_ref[...])
    y = routed + _dot(((g * _sigmoid(g)) * u).astype(BF16), sd_ref[...])
    h2 = h1_ref[...] + gtm_ref[0] * y
    o_ref[...] = h2 * lax.rsqrt(jnp.mean(h2 * h2, axis=-1, keepdims=True) + EPS) * gfin_ref[...]


def final(vp, h1, yk3, w_tk, gtm, gfin, sg, su, sd, cfg):
    t, d = h1.shape
    tm = min(cfg.final_tm, cfg.seq)
    per_b = cfg.seq // tm
    nt = t // tm
    ds = sg.shape[1]
    nslab = d // (2 * LANES)
    const = lambda i: (0, 0)
    kern = functools.partial(_final_kernel, top_k=cfg.top_k)
    plane = lambda k: pl.BlockSpec((tm, nslab, LANES), lambda i: (k * nt + i, 0, 0))
    return pl.pallas_call(
        kern,
        grid=(nt,),
        in_specs=[pl.BlockSpec((tm, cfg.top_k), lambda i: (i, 0), memory_space=pltpu.SMEM),
                  pl.BlockSpec((tm * nslab, LANES), lambda i: (i, 0)),
                  pl.BlockSpec((tm, d), lambda i: (i, 0)),
                  pl.BlockSpec((1, 1, d), lambda i: (i // per_b, 0, 0)),
                  pl.BlockSpec((1, d), const),
                  pl.BlockSpec((d, ds), const),
                  pl.BlockSpec((d, ds), const),
                  pl.BlockSpec((ds, d), const)] + [plane(k) for k in range(cfg.top_k)],
        out_specs=pl.BlockSpec((tm, d), lambda i: (i, 0)),
        out_shape=jax.ShapeDtypeStruct((t, d), F32),
        scratch_shapes=[pltpu.VMEM((tm * nslab, LANES), F32), pltpu.VMEM((tm * nslab, LANES), F32)],
        compiler_params=_params(1),
        name="final",
    )(w_tk, vp, h1, gtm, gfin, sg, su, sd, *([yk3] * cfg.top_k))


def _t5_bucket(rel, cfg):
    nb = cfg.t5_buckets // 2
    ret = (rel > 0).astype(np.int32) * nb
    n = np.abs(rel)
    max_exact = nb // 2
    large = max_exact + (np.log(np.maximum(n, 1) / max_exact) / math.log(cfg.t5_max_dist / max_exact)
                         * (nb - max_exact)).astype(np.int32)
    large = np.minimum(large, nb - 1)
    return (ret + np.where(n < max_exact, n, large)).astype(np.int32)


def _toeplitz(vec, n):
    lead = vec.shape[:-1]
    flat = jnp.tile(vec, (1,) * len(lead) + (n,))[..., :n * (2 * n - 1)]
    return flat.reshape(lead + (n, 2 * n - 1))[..., :n]


def _rel_offsets(n):
    m = np.arange(2 * n)
    return np.where(m < n, m, m - 2 * n)


def _bias_tables_a(t5_bias, cfg):
    tile = cfg.attn_tile
    assert tile + 1 >= cfg.t5_max_dist and tile % cfg.chunk == 0
    ql = np.arange(tile)[:, None]
    kl = np.arange(tile)[None, :]
    allowed = (kl // cfg.chunk) <= (ql // cfg.chunk)
    r = _rel_offsets(tile)
    t5 = t5_bias.astype(F32)
    diag = _toeplitz(t5[_t5_bucket(r, cfg)].T, tile)
    prev = _toeplitz(t5[_t5_bucket(r - tile, cfg)].T, tile)
    far = jnp.broadcast_to(t5[_t5_bucket(np.array(-2 * tile), cfg)][:, None, None], diag.shape)
    return jnp.stack([jnp.where(allowed[None], diag, NEG), prev, far], axis=1)


def _bias_tables_b(rel_table, cfg):
    tile = cfg.attn_tile
    assert (cfg.left_chunks * cfg.chunk) % tile == 0 and tile % cfg.chunk == 0
    nwin = (cfg.left_chunks * cfg.chunk) // tile + 1
    ql = np.arange(tile)[:, None]
    kl = np.arange(tile)[None, :]
    r = _rel_offsets(tile)
    rt = rel_table.astype(F32)
    tabs = []
    for j in range(nwin):
        off = (j - (nwin - 1)) * tile
        idx = np.clip(r + off, -cfg.rel_clip, cfg.rel_clip) + cfg.rel_clip
        dchunk = ql // cfg.chunk - (kl + off) // cfg.chunk
        band = (dchunk >= 0) & (dchunk <= cfg.left_chunks)
        tabs.append(jnp.where(band[None], _toeplitz(rt[idx].T, tile), NEG))
    tabs.append(jnp.full_like(tabs[0], NEG))
    return jnp.stack(tabs, axis=1)


def _block_tables(counts, n_assign, cfg):
    e_ = cfg.n_experts
    bm = cfg.moe_rows
    nb = -(-(n_assign + e_ * (bm - 1)) // bm)
    pcounts = (counts + bm - 1) // bm * bm
    pends = jnp.cumsum(pcounts)
    pstarts = pends - pcounts
    start_b = jnp.arange(nb, dtype=jnp.int32) * bm
    block_e = jnp.minimum(jnp.sum((pends[None, :] <= start_b[:, None]).astype(jnp.int32), axis=1), e_ - 1)
    mine = jnp.arange(e_, dtype=jnp.int32)[None, :] == block_e[:, None]
    pstart_b = jnp.sum(jnp.where(mine, pstarts[None, :], 0), axis=1)
    count_b = jnp.sum(jnp.where(mine, counts[None, :], 0), axis=1)
    n_used = (pends[e_ - 1] // bm).astype(jnp.int32)
    return block_e.astype(jnp.int32), (start_b - pstart_b).astype(jnp.int32), count_b.astype(jnp.int32), n_used


def _forward(cfg, x, c, w_ada, b_ada, g_attn, w_in, lambda_qk, t5_bias, rel_bias_b, w_up_a, w_up_b, w_o,
             g_moe, w_router, router_bias, w_exp_gate, w_exp_up, w_exp_down,
             w_sh_gate, w_sh_up, w_sh_down, g_final):
    b_, s_, d_ = x.shape
    t_ = b_ * s_
    l = 0
    qk_a = cfg.ha * 2 * cfg.da_qk
    width_a = cfg.ha * 2 * cfg.da_qk
    width_b = cfg.hb * cfg.db
    in_cols = w_in.shape[2]

    rows = max(16, b_)
    c_pad = jnp.zeros((rows, d_), F32).at[:b_].set(c)
    mod = ada_mod(c_pad, w_ada[l], b_ada[l], cfg)[:b_]
    sh_a, sc_a, gt_a, sh_m, sc_m, gt_m = [m.reshape(b_, 1, d_) for m in jnp.split(mod, cfg.n_mod, axis=-1)]

    colscale = np.ones((1, in_cols), np.float32)
    colscale[0, :qk_a] = cfg.da_qk ** -0.5 * LOG2E
    qb0 = 2 * qk_a + width_a
    colscale[0, qb0:qb0 + width_b] = cfg.db ** -0.5 * LOG2E
    x2 = x.reshape(t_, d_)
    proj = in_proj(x2, g_attn[l].reshape(1, d_), sc_a, sh_a, w_in[l].astype(BF16), jnp.asarray(colscale), cfg)

    lq = lambda_qk[l].astype(F32)
    lam = jnp.exp(jnp.sum(lq[0] * lq[1])) - jnp.exp(jnp.sum(lq[2] * lq[3])) + cfg.lam_init
    ya = attn_a(proj, lam.reshape(1), _bias_tables_a(t5_bias * LOG2E, cfg), cfg)
    yb = attn_b(proj, _bias_tables_b(rel_bias_b[l] * LOG2E, cfg), cfg)

    ne_pad = max(LANES, cfg.n_experts)
    wrt = jnp.zeros((ne_pad, d_), F32).at[:cfg.n_experts].set(w_router[l].T)
    wrh, wrl = _split_bf16(wrt)
    h1, v, lg = merge(ya, yb, proj, x2, gt_a, sc_m, sh_m, g_moe[l].reshape(1, d_),
                      w_up_a[l].astype(BF16), w_up_b[l].astype(BF16), w_o[l].astype(BF16), wrh, wrl, cfg)

    tc = min(cfg.route_tc, t_)
    rb = jnp.zeros((ne_pad, tc), F32).at[:cfg.n_experts].set(
        jnp.broadcast_to(router_bias[l].astype(F32)[:, None], (cfg.n_experts, tc)))
    w_kt, lin, kin, gend_raw = route(lg, rb, cfg)
    gps, n_grp, nx = tc // LANES, t_ // LANES, cfg.n_experts
    assert n_grp <= LANES
    per_group = lambda a: jnp.pad(a.transpose(1, 0, 2).reshape(nx, n_grp), ((0, 0), (0, LANES - n_grp)),
                                  constant_values=1e9)
    gend, gstart = per_group(gend_raw[:, :, :gps]), per_group(gend_raw[:, :, gps:2 * gps])
    by_group = lambda a: jnp.pad(a.reshape(nx, n_grp, LANES), ((0, 0), (0, LANES - n_grp), (0, 0)))
    counts = gend[:, n_grp - 1].astype(jnp.int32)
    block_e, r0, count_b, n_used = _block_tables(counts, t_ * cfg.top_k, cfg)
    slot_tok, slot_dst = slots(block_e, r0, count_b, gstart, gend, by_group(lin), by_group(kin), cfg, t_)

    yk = moe(v, block_e, n_used, slot_tok, slot_dst, w_exp_gate[l], w_exp_up[l], w_exp_down[l], cfg)
    out = final(v, h1, yk, w_kt.T, gt_m, g_final.reshape(1, d_),
                w_sh_gate[l].astype(BF16), w_sh_up[l].astype(BF16), w_sh_down[l].astype(BF16), cfg)
    return out.reshape(b_, s_, d_)


def kernel(x, c, w_ada, b_ada, g_attn, w_in, lambda_qk, t5_bias, rel_bias_b, w_up_a, w_up_b, w_o, g_moe,
           w_router, router_bias, w_exp_gate, w_exp_up, w_exp_down, w_sh_gate, w_sh_up, w_sh_down, g_final):
    return _forward(Cfg(), x, c, w_ada, b_ada, g_attn, w_in, lambda_qk, t5_bias, rel_bias_b, w_up_a, w_up_b,
                    w_o, g_moe, w_router, router_bias, w_exp_gate, w_exp_up, w_exp_down,
                    w_sh_gate, w_sh_up, w_sh_down, g_final)
```

```python
import functools
import math
from typing import NamedTuple

import numpy as np
import jax
import jax.numpy as jnp
from jax import lax
from jax.experimental import pallas as pl
from jax.experimental.pallas import tpu as pltpu

F32 = jnp.float32
BF16 = jnp.bfloat16
NEG = -1e30
EPS = 1e-6
HEAD_EPS = 1e-5
LOG2E = 1.4426950408889634
LANES = 128
VMEM_LIMIT = 56 * 1024 * 1024


class Cfg(NamedTuple):
    batch: int = 8
    seq: int = 2048
    d_model: int = 2048
    chunk: int = 64
    ha: int = 8
    da_qk: int = 64
    hb: int = 8
    db: int = 128
    left_chunks: int = 8
    rel_clip: int = 256
    t5_buckets: int = 32
    t5_max_dist: int = 128
    n_experts: int = 64
    top_k: int = 8
    n_groups: int = 8
    topk_groups: int = 4
    d_expert: int = 512
    d_shared: int = 512
    routed_scale: float = 2.5
    n_mod: int = 6
    lam_init: float = 0.8 - 0.6 * math.exp(-0.3 * 0)
    attn_tile: int = 256
    moe_rows: int = 256
    proj_tm: int = 1024
    proj_tn: int = 1024
    merge_tm: int = 512
    merge_sub: int = 256
    route_tc: int = 512
    final_tm: int = 256
    ada_tn: int = 1024


def _sigmoid(x):
    return 1.0 / (1.0 + jnp.exp(-x))


def _split_bf16(x):
    hi = x.astype(BF16)
    lo = (x - hi.astype(F32)).astype(BF16)
    return hi, lo


def _dot(a, b):
    return jnp.dot(a, b, preferred_element_type=F32)


def _dot_t(a, b):
    return lax.dot_general(a, b, (((1,), (1,)), ((), ())), preferred_element_type=F32)


U32 = jnp.uint32
HI_MASK = 0xFFFF0000


def _pack_pairs(x):
    n = x.shape[1] // 2
    bits = pltpu.bitcast(x.astype(BF16).astype(F32), U32)
    return (bits[:, :n] >> 16) | (bits[:, n:] & U32(HI_MASK))


def _unpack_pairs(p):
    return pltpu.bitcast(p << 16, F32), pltpu.bitcast(p & U32(HI_MASK), F32)


def _load_slabs(ref, m, ns):
    return jnp.concatenate([ref[pl.ds(j, m, stride=ns), :] for j in range(ns)], axis=1)


def _store_slabs(ref, val, ns):
    m = val.shape[0]
    for j in range(ns):
        ref[pl.ds(j, m, stride=ns), :] = val[:, j * LANES:(j + 1) * LANES]


def _params(n_axes):
    return pltpu.CompilerParams(dimension_semantics=("arbitrary",) * n_axes,
                                vmem_limit_bytes=VMEM_LIMIT)


def _ada_kernel(c_ref, w_ref, b_ref, o_ref):
    c = c_ref[...]
    s_hi, s_lo = _split_bf16(c * _sigmoid(c))
    w_hi, w_lo = _split_bf16(w_ref[...])
    acc = _dot(s_hi, w_hi) + _dot(s_hi, w_lo) + _dot(s_lo, w_hi)
    o_ref[...] = acc + b_ref[...]


def ada_mod(c_pad, w_ada, b_ada, cfg):
    rows, d = c_pad.shape
    n = w_ada.shape[1]
    tn = min(cfg.ada_tn, n)
    return pl.pallas_call(
        _ada_kernel,
        grid=(n // tn,),
        in_specs=[pl.BlockSpec((rows, d), lambda j: (0, 0)),
                  pl.BlockSpec((d, tn), lambda j: (0, j)),
                  pl.BlockSpec((1, tn), lambda j: (0, j))],
        out_specs=pl.BlockSpec((rows, tn), lambda j: (0, j)),
        out_shape=jax.ShapeDtypeStruct((rows, n), F32),
        compiler_params=_params(1),
        name="ada_mod",
    )(c_pad, w_ada, b_ada.reshape(1, n))


def _inproj_kernel(x_ref, g_ref, sc_ref, sh_ref, w_ref, cs_ref, o_ref, u_ref):
    @pl.when(pl.program_id(1) == 0)
    def _():
        x = x_ref[...]
        y = x * lax.rsqrt(jnp.mean(x * x, axis=-1, keepdims=True) + EPS) * g_ref[...]
        u_ref[...] = (y * (1.0 + sc_ref[0]) + sh_ref[0]).astype(BF16)

    acc = _dot(u_ref[...], w_ref[...])
    o_ref[...] = (acc * cs_ref[...]).astype(BF16)


def in_proj(x2, g, sc, sh, w_bf, colscale, cfg):
    t, d = x2.shape
    n = w_bf.shape[1]
    tm = min(cfg.proj_tm, cfg.seq)
    tn = min(cfg.proj_tn, n)
    per_b = cfg.seq // tm
    return pl.pallas_call(
        _inproj_kernel,
        grid=(t // tm, n // tn),
        in_specs=[pl.BlockSpec((tm, d), lambda i, j: (i, 0)),
                  pl.BlockSpec((1, d), lambda i, j: (0, 0)),
                  pl.BlockSpec((1, 1, d), lambda i, j: (i // per_b, 0, 0)),
                  pl.BlockSpec((1, 1, d), lambda i, j: (i // per_b, 0, 0)),
                  pl.BlockSpec((d, tn), lambda i, j: (0, j)),
                  pl.BlockSpec((1, tn), lambda i, j: (0, j))],
        out_specs=pl.BlockSpec((tm, tn), lambda i, j: (i, j)),
        out_shape=jax.ShapeDtypeStruct((t, n), BF16),
        scratch_shapes=[pltpu.VMEM((tm, d), BF16)],
        compiler_params=_params(2),
        name="in_proj",
    )(x2, g, sc, sh, w_bf, colscale)


def _attn_a_kernel(lam_ref, qlo_ref, qhi_ref, k_ref, v_ref, tb_ref, olo_ref, ohi_ref,
                   s_ref, m_ref, l_ref, o_ref, *, tile, half, nq, out_scale):
    def stacked(q):
        lane = lax.broadcasted_iota(jnp.int32, q.shape, 1)
        zero = jnp.zeros_like(q)
        return jnp.concatenate([jnp.where(lane < half, q, zero), jnp.where(lane >= half, q, zero)], axis=0)

    def fold(x, op):
        r = x[:, :LANES]
        for c in range(1, tile // LANES):
            r = op(r, x[:, c * LANES:(c + 1) * LANES])
        return r

    def body(p):
        qtile = (p, nq - 1 - p)
        visits = [(0, kb) for kb in range(p + 1)] + [(1, kb) for kb in range(nq - p)]
        qq = (stacked(qlo_ref[...]), stacked(qhi_ref[...]))
        rows = lambda kb: slice(kb * tile, (kb + 1) * tile)
        for i, (w, kb) in enumerate(visits):
            bias = tb_ref[0, min(qtile[w] - kb, 2)]
            s = _dot_t(qq[w], k_ref[rows(kb), :]) + jnp.concatenate([bias, bias], axis=0)
            s_ref[i] = s
            m_ref[i] = fold(s, jnp.maximum)

        def per_tile(ref, op):
            out = [None, None]
            for i, (w, _) in enumerate(visits):
                out[w] = ref[i] if out[w] is None else op(out[w], ref[i])
            return out

        m = [jnp.max(x, axis=-1, keepdims=True) for x in per_tile(m_ref, jnp.maximum)]
        for i, (w, _) in enumerate(visits):
            e = jnp.exp2(s_ref[i] - m[w])
            s_ref[i] = e
            l_ref[i] = fold(e, jnp.add)

        lam = lam_ref[0]
        inv = [1.0 / jnp.sum(x, axis=-1, keepdims=True) for x in per_tile(l_ref, jnp.add)]
        c0 = [x[:tile] for x in inv]
        c1 = [lam * x[tile:] for x in inv]
        for i, (w, kb) in enumerate(visits):
            e = s_ref[i]
            a = e[:tile] * c0[w] - e[tile:] * c1[w]
            o_ref[i] = _dot(a.astype(BF16), v_ref[rows(kb), :])

        for o, out in zip(per_tile(o_ref, jnp.add), (olo_ref, ohi_ref)):
            out[...] = (o * lax.rsqrt(jnp.mean(o * o, axis=-1, keepdims=True) + HEAD_EPS) * out_scale).astype(BF16)

    for k in range(nq // 2):
        pl.when(pl.program_id(2) == k)(functools.partial(body, k))


def attn_a(proj, lam, tb, cfg):
    tile = cfg.attn_tile
    s = cfg.seq
    nq = s // tile
    assert nq % 2 == 0
    npair = nq // 2
    dv = 2 * cfg.da_qk
    assert dv == LANES
    kcol = cfg.ha
    vcol = 2 * cfg.ha
    kern = functools.partial(_attn_a_kernel, tile=tile, half=cfg.da_qk, nq=nq, out_scale=1.0 - cfg.lam_init)
    half_shape = jax.ShapeDtypeStruct((cfg.batch * npair * tile, cfg.ha * dv), BF16)
    y_lo, y_hi = pl.pallas_call(
        kern,
        grid=(cfg.ha, cfg.batch, npair),
        in_specs=[pl.BlockSpec(memory_space=pltpu.SMEM),
                  pl.BlockSpec((tile, LANES), lambda h, b, p: (b * nq + p, h)),
                  pl.BlockSpec((tile, LANES), lambda h, b, p: (b * nq + nq - 1 - p, h)),
                  pl.BlockSpec((s, LANES), lambda h, b, p: (b, kcol + h)),
                  pl.BlockSpec((s, LANES), lambda h, b, p: (b, vcol + h)),
                  pl.BlockSpec((1, 3, tile, tile), lambda h, b, p: (h, 0, 0, 0))],
        out_specs=[pl.BlockSpec((tile, LANES), lambda h, b, p: (b * npair + p, h)),
                   pl.BlockSpec((tile, LANES), lambda h, b, p: (b * npair + npair - 1 - p, h))],
        out_shape=[half_shape, half_shape],
        scratch_shapes=[pltpu.VMEM((nq + 1, 2 * tile, tile), F32),
                        pltpu.VMEM((nq + 1, 2 * tile, LANES), F32),
                        pltpu.VMEM((nq + 1, 2 * tile, LANES), F32),
                        pltpu.VMEM((nq + 1, tile, LANES), F32)],
        compiler_params=_params(3),
        name="attn_a",
    )(lam, proj, proj, proj, proj, tb)
    w = cfg.ha * dv
    rows = npair * tile
    return jnp.concatenate([y_lo.reshape(cfg.batch, rows, w), y_hi.reshape(cfg.batch, rows, w)],
                           axis=1).reshape(cfg.batch * s, w)


ATTN_B_TILES = 2


def _attn_b_kernel(q_ref, k_ref, v_ref, tb_ref, o_ref, *, tile, nwin):
    for u in range(ATTN_B_TILES):
        qi = pl.program_id(2) * ATTN_B_TILES + u
        q = q_ref[u * tile:(u + 1) * tile, :]
        ss, vs = [], []
        for j in range(nwin):
            kb = qi - (nwin - 1) + j
            start = pl.multiple_of(jnp.maximum(kb, 0) * tile, tile)
            ss.append(_dot_t(q, k_ref[pl.ds(start, tile), :]) + tb_ref[0, jnp.where(kb >= 0, j, nwin)])
            vs.append(v_ref[pl.ds(start, tile), :])
        m = jnp.max(ss[0], axis=-1, keepdims=True)
        for s in ss[1:]:
            m = jnp.maximum(m, jnp.max(s, axis=-1, keepdims=True))
        ps = [jnp.exp2(s - m) for s in ss]
        l = jnp.sum(ps[0], axis=-1, keepdims=True)
        for p in ps[1:]:
            l = l + jnp.sum(p, axis=-1, keepdims=True)
        acc = _dot(ps[0].astype(BF16), vs[0])
        for p, v in zip(ps[1:], vs[1:]):
            acc = acc + _dot(p.astype(BF16), v)
        o_ref[u * tile:(u + 1) * tile, :] = (acc * (1.0 / l)).astype(BF16)


def attn_b(proj, tb, cfg):
    t = proj.shape[0]
    tile = cfg.attn_tile
    s = cfg.seq
    nq = s // tile
    assert cfg.db == LANES and nq % ATTN_B_TILES == 0
    nstep = nq // ATTN_B_TILES
    rows = ATTN_B_TILES * tile
    nwin = (cfg.left_chunks * cfg.chunk) // tile + 1
    base = (2 * cfg.ha * 2 * cfg.da_qk + cfg.ha * 2 * cfg.da_qk) // LANES
    kern = functools.partial(_attn_b_kernel, tile=tile, nwin=nwin)
    return pl.pallas_call(
        kern,
        grid=(cfg.hb, cfg.batch, nstep),
        in_specs=[pl.BlockSpec((rows, LANES), lambda h, b, i: (b * nstep + i, base + h)),
                  pl.BlockSpec((s, LANES), lambda h, b, i: (b, base + cfg.hb + h)),
                  pl.BlockSpec((s, LANES), lambda h, b, i: (b, base + 2 * cfg.hb + h)),
                  pl.BlockSpec((1, nwin + 1, tile, tile), lambda h, b, i: (h, 0, 0, 0))],
        out_specs=pl.BlockSpec((rows, LANES), lambda h, b, i: (b * nstep + i, h)),
        out_shape=jax.ShapeDtypeStruct((t, cfg.hb * cfg.db), BF16),
        compiler_params=_params(3),
        name="attn_b",
    )(proj, proj, proj, tb)


def _merge_kernel(ya_ref, yb_ref, ga_ref, gb_ref, x_ref, gta_ref, scm_ref, shm_ref, gmoe_ref,
                  wua_ref, wub_ref, wo_ref, wrh_ref, wrl_ref, h1_ref, v_ref, lg_ref, *, sub):
    ns = v_ref.shape[0] // h1_ref.shape[0]
    for r0 in range(0, h1_ref.shape[0], sub):
        rows = slice(r0, r0 + sub)
        a = _dot(ya_ref[rows, :], wua_ref[...])
        b = _dot(yb_ref[rows, :], wub_ref[...])
        merged = _sigmoid(ga_ref[rows, :].astype(F32)) * a + _sigmoid(gb_ref[rows, :].astype(F32)) * b
        att = _dot(merged.astype(BF16), wo_ref[...])
        h1 = x_ref[rows, :] + gta_ref[0] * att
        h1_ref[rows, :] = h1
        y = h1 * lax.rsqrt(jnp.mean(h1 * h1, axis=-1, keepdims=True) + EPS) * gmoe_ref[...]
        v = y * (1.0 + scm_ref[0]) + shm_ref[0]
        _store_slabs(v_ref.at[pl.ds(r0 * ns, sub * ns), :], _pack_pairs(v), ns)
        v_hi, v_lo = _split_bf16(v)
        wrh = wrh_ref[...]
        lg_ref[:, rows] = _dot_t(wrh, v_hi) + _dot_t(wrh, v_lo) + _dot_t(wrl_ref[...], v_hi)


def merge(ya, yb, proj, x2, gta, scm, shm, gmoe, wua, wub, wo, wrh, wrl, cfg):
    t, d = x2.shape
    tm = min(cfg.merge_tm, cfg.seq)
    per_b = cfg.seq // tm
    wa = ya.shape[1]
    wb = yb.shape[1]
    gate0 = (proj.shape[1] - 2 * d) // d
    ne = wrh.shape[0]
    ns = d // (2 * LANES)
    const = lambda i: (0, 0)
    row3 = lambda i: (i // per_b, 0, 0)
    resident = lambda shape: pl.BlockSpec(shape, const, pipeline_mode=pl.Buffered(1))
    return pl.pallas_call(
        functools.partial(_merge_kernel, sub=min(cfg.merge_sub, tm)),
        grid=(t // tm,),
        in_specs=[pl.BlockSpec((tm, wa), lambda i: (i, 0)),
                  pl.BlockSpec((tm, wb), lambda i: (i, 0)),
                  pl.BlockSpec((tm, d), lambda i: (i, gate0)),
                  pl.BlockSpec((tm, d), lambda i: (i, gate0 + 1)),
                  pl.BlockSpec((tm, d), lambda i: (i, 0)),
                  pl.BlockSpec((1, 1, d), row3),
                  pl.BlockSpec((1, 1, d), row3),
                  pl.BlockSpec((1, 1, d), row3),
                  pl.BlockSpec((1, d), const),
                  resident((wa, d)),
                  resident((wb, d)),
                  resident((d, d)),
                  resident((ne, d)),
                  resident((ne, d))],
        out_specs=[pl.BlockSpec((tm, d), lambda i: (i, 0)),
                   pl.BlockSpec((tm * ns, LANES), lambda i: (i, 0)),
                   pl.BlockSpec((ne, tm), lambda i: (0, i))],
        out_shape=[jax.ShapeDtypeStruct((t, d), F32),
                   jax.ShapeDtypeStruct((t * ns, LANES), U32),
                   jax.ShapeDtypeStruct((ne, t), F32)],
        compiler_params=_params(1),
        name="merge",
    )(ya, yb, proj, proj, x2, gta, scm, shm, gmoe, wua, wub, wo, wrh, wrl)


def _route_kernel(lg_ref, rb_ref, tri_ref, w_ref, lin_ref, kin_ref, gend_ref, carry_ref,
                  *, n_exp, n_groups, topk_groups, top_k, scale):
    gsz = n_exp // n_groups
    tc = lg_ref.shape[1]
    ninf = -jnp.inf
    step = pl.program_id(0)

    @pl.when(step == 0)
    def _():
        carry_ref[...] = jnp.zeros(carry_ref.shape, F32)

    row = lax.broadcasted_iota(jnp.int32, (gsz, tc), 0).astype(F32)
    sc, bi = [], []
    for g in range(n_groups):
        s = _sigmoid(lg_ref[g * gsz:(g + 1) * gsz, :])
        sc.append(s)
        bi.append(s + rb_ref[g * gsz:(g + 1) * gsz, :])
    gs = []
    for x in bi:
        m1 = jnp.max(x, axis=0, keepdims=True)
        eq = x == m1
        cnt = jnp.sum(jnp.where(eq, 1.0, 0.0), axis=0, keepdims=True)
        second = jnp.max(jnp.where(eq, ninf, x), axis=0, keepdims=True)
        gs.append(m1 + jnp.where(cnt >= 2.0, m1, second))
    gmat = jnp.concatenate(gs, axis=0)
    grow = lax.broadcasted_iota(jnp.int32, gmat.shape, 0).astype(F32)
    gsel = jnp.zeros(gmat.shape, F32)
    for _ in range(topk_groups):
        m = jnp.max(gmat, axis=0, keepdims=True)
        idx = jnp.min(jnp.where(gmat == m, grow, float(n_groups)), axis=0, keepdims=True)
        hit = grow == idx
        gsel = jnp.where(hit, 1.0, gsel)
        gmat = jnp.where(hit, ninf, gmat)
    cur = [jnp.where(gsel[g:g + 1, :] > 0.5, bi[g], ninf) for g in range(n_groups)]
    ids = [row + float(g * gsz) for g in range(n_groups)]
    w_rows = []
    chosen = [jnp.zeros((gsz, tc), F32) for _ in range(n_groups)]
    kplus = [jnp.zeros((gsz, tc), F32) for _ in range(n_groups)]
    for k in range(top_k):
        mg = cur[0]
        for c in cur[1:]:
            mg = jnp.maximum(mg, c)
        m = jnp.max(mg, axis=0, keepdims=True)
        cand = jnp.where(cur[0] == m, ids[0], float(n_exp))
        for c, i in zip(cur[1:], ids[1:]):
            cand = jnp.minimum(cand, jnp.where(c == m, i, float(n_exp)))
        idx = jnp.min(cand, axis=0, keepdims=True)
        wsum = jnp.zeros((gsz, tc), F32)
        for g in range(n_groups):
            hit = ids[g] == idx
            wsum = wsum + jnp.where(hit, sc[g], 0.0)
            cur[g] = jnp.where(hit, ninf, cur[g])
            chosen[g] = jnp.where(hit, 1.0, chosen[g])
            kplus[g] = jnp.where(hit, float(k + 1), kplus[g])
        w_rows.append(jnp.sum(wsum, axis=0, keepdims=True))
    w = jnp.concatenate(w_rows, axis=0)
    w_ref[...] = w / jnp.sum(w, axis=0, keepdims=True) * scale

    onehot = jnp.concatenate(chosen, axis=0)
    incl = _dot(onehot.astype(BF16), tri_ref[...])
    lin_ref[...] = incl.astype(BF16)
    kin_ref[...] = jnp.concatenate(kplus, axis=0).astype(BF16)
    gps = tc // LANES
    lane = lax.broadcasted_iota(jnp.int32, (n_exp, LANES), 1)
    run = carry_ref[...]
    ends = jnp.zeros((n_exp, LANES), F32)
    for g in range(gps):
        ends = jnp.where(lane == gps + g, run, ends)
        run = run + incl[:, (g + 1) * LANES - 1:(g + 1) * LANES]
        ends = jnp.where(lane == g, run, ends)
    carry_ref[...] = run
    gend_ref[0] = ends


def route(lg, rb, cfg):
    ne, t = lg.shape
    nx = cfg.n_experts
    tc = min(cfg.route_tc, t)
    nsteps = t // tc
    tri = jnp.asarray(np.kron(np.eye(tc // LANES), np.triu(np.ones((LANES, LANES)))), BF16)
    kern = functools.partial(_route_kernel, n_exp=cfg.n_experts, n_groups=cfg.n_groups,
                             topk_groups=cfg.topk_groups, top_k=cfg.top_k, scale=cfg.routed_scale)
    return pl.pallas_call(
        kern,
        grid=(nsteps,),
        in_specs=[pl.BlockSpec((ne, tc), lambda i: (0, i)),
                  pl.BlockSpec((ne, tc), lambda i: (0, 0)),
                  pl.BlockSpec((tc, tc), lambda i: (0, 0))],
        out_specs=[pl.BlockSpec((cfg.top_k, tc), lambda i: (0, i)),
                   pl.BlockSpec((nx, tc), lambda i: (0, i)),
                   pl.BlockSpec((nx, tc), lambda i: (0, i)),
                   pl.BlockSpec((1, nx, LANES), lambda i: (i, 0, 0))],
        out_shape=[jax.ShapeDtypeStruct((cfg.top_k, t), F32),
                   jax.ShapeDtypeStruct((nx, t), BF16),
                   jax.ShapeDtypeStruct((nx, t), BF16),
                   jax.ShapeDtypeStruct((nsteps, nx, LANES), F32)],
        scratch_shapes=[pltpu.VMEM((cfg.n_experts, LANES), F32)],
        compiler_params=_params(1),
        name="route",
    )(lg, rb, tri)


SLOT_BLOCKS = 4


def _slots_kernel(be_ref, r0_ref, cnt_ref, gstart_ref, gend_ref, lin_ref, kin_ref, tok_ref, dst_ref,
                  *, bm, top_k, n_tok, n_assign):
    step = pl.program_id(0)
    rows = lax.broadcasted_iota(jnp.int32, (bm, LANES), 0)
    lane = lax.broadcasted_iota(jnp.int32, (bm, LANES), 1).astype(F32)
    ones = jnp.ones((16, LANES), BF16)
    gids = lax.broadcasted_iota(jnp.int32, (16, LANES), 1).astype(F32).astype(BF16)
    lane_row = lax.broadcasted_iota(jnp.int32, (1, bm), 1)
    for u in range(SLOT_BLOCKS):
        b = step * SLOT_BLOCKS + u
        e = be_ref[b]
        r0 = r0_ref[b]
        cnt = cnt_ref[b]
        r = (rows + r0).astype(F32)
        gstart = gstart_ref[pl.ds(e, 1), :]
        in_group = jnp.logical_and(gstart <= r, r < gend_ref[pl.ds(e, 1), :])
        group = jnp.where(in_group, 1.0, 0.0).astype(BF16)
        want = r - jnp.sum(jnp.where(in_group, gstart, 0.0), axis=1, keepdims=True) + 1.0
        kin = _dot(group, kin_ref[e])
        hit = jnp.logical_and(_dot(group, lin_ref[e]) == want, kin > 0.5)
        t_loc = _dot_t(ones, jnp.where(hit, lane, 0.0).astype(BF16))[0:1]
        kp1 = _dot_t(ones, jnp.where(hit, kin, 0.0).astype(BF16))[0:1]
        gid = _dot_t(gids, group)[0:1]
        tok = (gid * float(LANES) + t_loc).astype(jnp.int32)
        kk = kp1.astype(jnp.int32) - 1
        valid = (lane_row + r0) < cnt
        dummy = n_assign + lax.rem(b, 4) * bm + lane_row
        tok_ref[u] = jnp.where(valid, tok, lax.rem(b * bm + lane_row, n_tok))
        dst_ref[u] = jnp.where(valid, kk * n_tok + tok, dummy)


def slots(block_e, r0, cntb, gstart, gend, lin, kin, cfg, n_tok):
    nb = block_e.shape[0]
    bm = cfg.moe_rows
    assert nb % SLOT_BLOCKS == 0 and gstart.shape[1] == LANES and lin.shape[1:] == (LANES, LANES)
    kern = functools.partial(_slots_kernel, bm=bm, top_k=cfg.top_k, n_tok=n_tok, n_assign=n_tok * cfg.top_k)
    whole = lambda a: pl.BlockSpec(a.shape, lambda i, be, r0, cn: (0,) * a.ndim)
    out_blk = pl.BlockSpec((SLOT_BLOCKS, 1, bm), lambda i, be, r0, cn: (i, 0, 0))
    grid_spec = pltpu.PrefetchScalarGridSpec(
        num_scalar_prefetch=3,
        grid=(nb // SLOT_BLOCKS,),
        in_specs=[whole(gstart), whole(gend), whole(lin), whole(kin)],
        out_specs=[out_blk, out_blk],
    )
    return pl.pallas_call(
        kern,
        grid_spec=grid_spec,
        out_shape=[jax.ShapeDtypeStruct((nb, 1, bm), jnp.int32), jax.ShapeDtypeStruct((nb, 1, bm), jnp.int32)],
        compiler_params=_params(1),
        name="slots",
    )(block_e, r0, cntb, gstart, gend, lin, kin)


MOE_RING = 4
MOE_DUMMY = 4


def _moe_kernel(be_ref, tok0_ref, tok1_ref, tok2_ref, tokn_ref, dstp_ref, dstc_ref, v_hbm, wg_hbm, wu_hbm, wd_hbm,
                yk_hbm, xbuf, ybuf, xbf, wst_g, wst_u, wst_d, wgb, wub, wdb, gsem, ssem, wsem, *, bm, n_assign):
    b = pl.program_id(0)
    nb = pl.num_programs(0)
    slot = lax.rem(b, MOE_RING)
    ns = v_hbm.shape[1]

    def weight_copies(e, ws):
        return [pltpu.make_async_copy(src.at[e], dst.at[ws], wsem.at[ws])
                for src, dst in ((wg_hbm, wst_g), (wu_hbm, wst_u), (wd_hbm, wst_d))]

    def unpack_block(s):
        lo, hi = _unpack_pairs(_load_slabs(xbuf.at[s], bm, ns))
        xbf[s % 2] = jnp.concatenate([lo, hi], axis=1).astype(BF16)

    def gather_copy(t, r, s):
        return pltpu.make_async_copy(v_hbm.at[t], xbuf.at[s, pl.ds(r * ns, ns), :], gsem.at[s])

    def scatter_copy(r, d, s):
        return pltpu.make_async_copy(ybuf.at[s, pl.ds(r * ns, ns), :], yk_hbm.at[d], ssem.at[s])

    def wait_gather(s):
        for r in range(bm):
            gather_copy(0, r, s).wait()

    def wait_scatter(s):
        for r in range(bm):
            scatter_copy(r, 0, s).wait()

    @pl.when(b == 0)
    def _():
        ybuf[...] = jnp.zeros(ybuf.shape, U32)
        for s in range(MOE_RING - 1):
            for r in range(bm):
                scatter_copy(r, n_assign + s * bm + r, s).start(priority=r % 2)
        for s, tok_ref in enumerate((tok0_ref, tok1_ref, tok2_ref)):
            for r in range(bm):
                gather_copy(tok_ref[0, 0, r], r, s).start(priority=r % 2)
        wait_gather(0)
        unpack_block(0)
        for c in weight_copies(be_ref[0], 0):
            c.start()

    @pl.when(be_ref[nb + 1 + b] == 1)
    def _():
        ws = be_ref[2 * nb + 1 + b]
        nxt_e = be_ref[3 * nb + 1 + b]
        for c in weight_copies(be_ref[b], ws):
            c.wait()

        @pl.when(nxt_e >= 0)
        def _():
            for c in weight_copies(nxt_e, 1 - ws):
                c.start()

        wgb[...] = wst_g[ws].astype(BF16)
        wub[...] = wst_u[ws].astype(BF16)
        wdb[...] = wst_d[ws].astype(BF16)

    def block_step(s):
        nxt, prv = (s + 1) % MOE_RING, (s + 3) % MOE_RING
        wait_gather(nxt)
        wait_scatter(s)
        for r in range(bm):
            scatter_copy(r, dstp_ref[0, 0, r], prv).start(priority=r % 2)
        for r in range(bm):
            gather_copy(tokn_ref[0, 0, r], r, prv).start(priority=r % 2)
        x = xbf[s % 2]
        g = _dot(x, wgb[...])
        u = _dot(x, wub[...])
        h = (g * _sigmoid(g)) * u
        unpack_block(nxt)
        _store_slabs(ybuf.at[s], _pack_pairs(_dot(h.astype(BF16), wdb[...])), ns)

    last = be_ref[nb] - 1
    for s in range(MOE_RING):
        pl.when(jnp.logical_and(slot == s, b <= last))(functools.partial(block_step, s))

    @pl.when(b == last)
    def _():
        def drain(s):
            for r in range(bm):
                scatter_copy(r, dstc_ref[0, 0, r], s).start(priority=r % 2)
            for j in (2, 3):
                wait_gather((s + j) % MOE_RING)
            for j in (1, 2, 3, 0):
                wait_scatter((s + j) % MOE_RING)

        for s in range(MOE_RING):
            pl.when(slot == s)(functools.partial(drain, s))


def moe(vp, block_e, n_used, slot_tok, slot_dst, wg, wu, wd, cfg):
    d = wg.shape[1]
    ns = d // (2 * LANES)
    t = vp.shape[0] // ns
    bm = cfg.moe_rows
    v3 = vp.reshape(t, ns, LANES)
    nb = block_e.shape[0]
    assert nb >= 2
    de = wg.shape[2]
    n_assign = t * cfg.top_k
    first = (n_assign + 3 * bm + jnp.arange(bm, dtype=jnp.int32)).reshape(1, 1, bm)
    dst_ext = jnp.concatenate([first, slot_dst], axis=0)
    kern = functools.partial(_moe_kernel, bm=bm, n_assign=n_assign)
    smem_blk = lambda f: pl.BlockSpec((1, 1, bm), f, memory_space=pltpu.SMEM)
    grid_spec = pltpu.PrefetchScalarGridSpec(
        num_scalar_prefetch=1,
        grid=(nb,),
        in_specs=[smem_blk(lambda b, be: (0, 0, 0)),
                  smem_blk(lambda b, be: (1, 0, 0)),
                  smem_blk(lambda b, be: (min(2, nb - 1), 0, 0)),
                  smem_blk(lambda b, be: (jnp.minimum(b + 3, nb - 1), 0, 0)),
                  smem_blk(lambda b, be: (b, 0, 0)),
                  smem_blk(lambda b, be: (b + 1, 0, 0)),
                  pl.BlockSpec(memory_space=pl.ANY)] + [pl.BlockSpec(memory_space=pl.ANY)] * 3,
        out_specs=pl.BlockSpec(memory_space=pl.ANY),
        scratch_shapes=[pltpu.VMEM((MOE_RING, bm * ns, LANES), U32),
                        pltpu.VMEM((MOE_RING, bm * ns, LANES), U32),
                        pltpu.VMEM((2, bm, d), BF16),
                        pltpu.VMEM((2, d, de), F32), pltpu.VMEM((2, d, de), F32), pltpu.VMEM((2, de, d), F32),
                        pltpu.VMEM((d, de), BF16), pltpu.VMEM((d, de), BF16), pltpu.VMEM((de, d), BF16),
                        pltpu.SemaphoreType.DMA((MOE_RING,)), pltpu.SemaphoreType.DMA((MOE_RING,)),
                        pltpu.SemaphoreType.DMA((2,))],
    )
    idx = jnp.arange(nb, dtype=jnp.int32)
    is_first = jnp.logical_and(idx < n_used, block_e != jnp.concatenate([jnp.full((1,), -1, jnp.int32), block_e[:-1]]))
    stage = (jnp.cumsum(is_first.astype(jnp.int32)) - 1) % 2
    later_first = jnp.flip(lax.cummin(jnp.flip(jnp.where(is_first, idx, nb))))
    nxt_pos = jnp.concatenate([later_first[1:], jnp.full((1,), nb, jnp.int32)])
    nxt_e = jnp.sum(jnp.where(idx[None, :] == nxt_pos[:, None], block_e[None, :] + 1, 0), axis=1) - 1
    sched = jnp.concatenate([block_e, n_used.reshape(1), is_first.astype(jnp.int32), stage.astype(jnp.int32),
                             nxt_e.astype(jnp.int32)])
    return pl.pallas_call(
        kern,
        grid_spec=grid_spec,
        out_shape=jax.ShapeDtypeStruct((n_assign + MOE_DUMMY * bm, ns, LANES), U32),
        compiler_params=_params(1),
        name="moe",
    )(sched, slot_tok, slot_tok, slot_tok, slot_tok, dst_ext, dst_ext, v3, wg, wu, wd)


def _final_kernel(w_ref, v_ref, h1_ref, gtm_ref, gfin_ref, sg_ref, su_ref, sd_ref, *rest, top_k):
    yk_refs, o_ref, lo_ref, hi_ref = rest[:top_k], rest[top_k], rest[top_k + 1], rest[top_k + 2]
    tm = h1_ref.shape[0]
    ns = yk_refs[0].shape[1]

    def combine(t, carry):
        lo, hi = _unpack_pairs(yk_refs[0][t])
        acc_lo, acc_hi = w_ref[t, 0] * lo, w_ref[t, 0] * hi
        for k in range(1, top_k):
            lo, hi = _unpack_pairs(yk_refs[k][t])
            acc_lo, acc_hi = acc_lo + w_ref[t, k] * lo, acc_hi + w_ref[t, k] * hi
        rows = pl.ds(pl.multiple_of(t * ns, ns), ns)
        lo_ref[rows, :] = acc_lo
        hi_ref[rows, :] = acc_hi
        return carry

    lax.fori_loop(0, tm, combine, 0, unroll=8)
    routed = jnp.concatenate([_load_slabs(lo_ref, tm, ns), _load_slabs(hi_ref, tm, ns)], axis=1)

    vb = jnp.concatenate(_unpack_pairs(_load_slabs(v_ref, tm, ns)), axis=1).astype(BF16)
    g = _dot(vb, sg_ref[...])
    u = _dot(vb, su_ref[...])
    y = routed + _dot(((g * _sigmoid(g)) * u).astype(BF16), sd_ref[...])
    h2 = h1_ref[...] + gtm_ref[0] * y
    o_ref[...] = h2 * lax.rsqrt(jnp.mean(h2 * h2, axis=-1, keepdims=True) + EPS) * gfin_ref[...]


def final(vp, h1, yk3, w_tk, gtm, gfin, sg, su, sd, cfg):
    t, d = h1.shape
    tm = min(cfg.final_tm, cfg.seq)
    per_b = cfg.seq // tm
    nt = t // tm
    ds = sg.shape[1]
    nslab = d // (2 * LANES)
    const = lambda i: (0, 0)
    kern = functools.partial(_final_kernel, top_k=cfg.top_k)
    plane = lambda k: pl.BlockSpec((tm, nslab, LANES), lambda i: (k * nt + i, 0, 0))
    return pl.pallas_call(
        kern,
        grid=(nt,),
        in_specs=[pl.BlockSpec((tm, cfg.top_k), lambda i: (i, 0), memory_space=pltpu.SMEM),
                  pl.BlockSpec((tm * nslab, LANES), lambda i: (i, 0)),
                  pl.BlockSpec((tm, d), lambda i: (i, 0)),
                  pl.BlockSpec((1, 1, d), lambda i: (i // per_b, 0, 0)),
                  pl.BlockSpec((1, d), const),
                  pl.BlockSpec((d, ds), const),
                  pl.BlockSpec((d, ds), const),
                  pl.BlockSpec((ds, d), const)] + [plane(k) for k in range(cfg.top_k)],
        out_specs=pl.BlockSpec((tm, d), lambda i: (i, 0)),
        out_shape=jax.ShapeDtypeStruct((t, d), F32),
        scratch_shapes=[pltpu.VMEM((tm * nslab, LANES), F32), pltpu.VMEM((tm * nslab, LANES), F32)],
        compiler_params=_params(1),
        name="final",
    )(w_tk, vp, h1, gtm, gfin, sg, su, sd, *([yk3] * cfg.top_k))


def _t5_bucket(rel, cfg):
    nb = cfg.t5_buckets // 2
    ret = (rel > 0).astype(np.int32) * nb
    n = np.abs(rel)
    max_exact = nb // 2
    large = max_exact + (np.log(np.maximum(n, 1) / max_exact) / math.log(cfg.t5_max_dist / max_exact)
                         * (nb - max_exact)).astype(np.int32)
    large = np.minimum(large, nb - 1)
    return (ret + np.where(n < max_exact, n, large)).astype(np.int32)


def _toeplitz(vec, n):
    lead = vec.shape[:-1]
    flat = jnp.tile(vec, (1,) * len(lead) + (n,))[..., :n * (2 * n - 1)]
    return flat.reshape(lead + (n, 2 * n - 1))[..., :n]


def _rel_offsets(n):
    m = np.arange(2 * n)
    return np.where(m < n, m, m - 2 * n)


def _bias_tables_a(t5_bias, cfg):
    tile = cfg.attn_tile
    assert tile + 1 >= cfg.t5_max_dist and tile % cfg.chunk == 0
    ql = np.arange(tile)[:, None]
    kl = np.arange(tile)[None, :]
    allowed = (kl // cfg.chunk) <= (ql // cfg.chunk)
    r = _rel_offsets(tile)
    t5 = t5_bias.astype(F32)
    diag = _toeplitz(t5[_t5_bucket(r, cfg)].T, tile)
    prev = _toeplitz(t5[_t5_bucket(r - tile, cfg)].T, tile)
    far = jnp.broadcast_to(t5[_t5_bucket(np.array(-2 * tile), cfg)][:, None, None], diag.shape)
    return jnp.stack([jnp.where(allowed[None], diag, NEG), prev, far], axis=1)


def _bias_tables_b(rel_table, cfg):
    tile = cfg.attn_tile
    assert (cfg.left_chunks * cfg.chunk) % tile == 0 and tile % cfg.chunk == 0
    nwin = (cfg.left_chunks * cfg.chunk) // tile + 1
    ql = np.arange(tile)[:, None]
    kl = np.arange(tile)[None, :]
    r = _rel_offsets(tile)
    rt = rel_table.astype(F32)
    tabs = []
    for j in range(nwin):
        off = (j - (nwin - 1)) * tile
        idx = np.clip(r + off, -cfg.rel_clip, cfg.rel_clip) + cfg.rel_clip
        dchunk = ql // cfg.chunk - (kl + off) // cfg.chunk
        band = (dchunk >= 0) & (dchunk <= cfg.left_chunks)
        tabs.append(jnp.where(band[None], _toeplitz(rt[idx].T, tile), NEG))
    tabs.append(jnp.full_like(tabs[0], NEG))
    return jnp.stack(tabs, axis=1)


def _block_tables(counts, n_assign, cfg):
    e_ = cfg.n_experts
    bm = cfg.moe_rows
    nb = -(-(n_assign + e_ * (bm - 1)) // bm)
    pcounts = (counts + bm - 1) // bm * bm
    pends = jnp.cumsum(pcounts)
    pstarts = pends - pcounts
    start_b = jnp.arange(nb, dtype=jnp.int32) * bm
    block_e = jnp.minimum(jnp.sum((pends[None, :] <= start_b[:, None]).astype(jnp.int32), axis=1), e_ - 1)
    mine = jnp.arange(e_, dtype=jnp.int32)[None, :] == block_e[:, None]
    pstart_b = jnp.sum(jnp.where(mine, pstarts[None, :], 0), axis=1)
    count_b = jnp.sum(jnp.where(mine, counts[None, :], 0), axis=1)
    n_used = (pends[e_ - 1] // bm).astype(jnp.int32)
    return block_e.astype(jnp.int32), (start_b - pstart_b).astype(jnp.int32), count_b.astype(jnp.int32), n_used


def _forward(cfg, x, c, w_ada, b_ada, g_attn, w_in, lambda_qk, t5_bias, rel_bias_b, w_up_a, w_up_b, w_o,
             g_moe, w_router, router_bias, w_exp_gate, w_exp_up, w_exp_down,
             w_sh_gate, w_sh_up, w_sh_down, g_final):
    b_, s_, d_ = x.shape
    t_ = b_ * s_
    l = 0
    qk_a = cfg.ha * 2 * cfg.da_qk
    width_a = cfg.ha * 2 * cfg.da_qk
    width_b = cfg.hb * cfg.db
    in_cols = w_in.shape[2]

    rows = max(16, b_)
    c_pad = jnp.zeros((rows, d_), F32).at[:b_].set(c)
    mod = ada_mod(c_pad, w_ada[l], b_ada[l], cfg)[:b_]
    sh_a, sc_a, gt_a, sh_m, sc_m, gt_m = [m.reshape(b_, 1, d_) for m in jnp.split(mod, cfg.n_mod, axis=-1)]

    colscale = np.ones((1, in_cols), np.float32)
    colscale[0, :qk_a] = cfg.da_qk ** -0.5 * LOG2E
    qb0 = 2 * qk_a + width_a
    colscale[0, qb0:qb0 + width_b] = cfg.db ** -0.5 * LOG2E
    x2 = x.reshape(t_, d_)
    proj = in_proj(x2, g_attn[l].reshape(1, d_), sc_a, sh_a, w_in[l].astype(BF16), jnp.asarray(colscale), cfg)

    lq = lambda_qk[l].astype(F32)
    lam = jnp.exp(jnp.sum(lq[0] * lq[1])) - jnp.exp(jnp.sum(lq[2] * lq[3])) + cfg.lam_init
    ya = attn_a(proj, lam.reshape(1), _bias_tables_a(t5_bias * LOG2E, cfg), cfg)
    yb = attn_b(proj, _bias_tables_b(rel_bias_b[l] * LOG2E, cfg), cfg)

    ne_pad = max(LANES, cfg.n_experts)
    wrt = jnp.zeros((ne_pad, d_), F32).at[:cfg.n_experts].set(w_router[l].T)
    wrh, wrl = _split_bf16(wrt)
    h1, v, lg = merge(ya, yb, proj, x2, gt_a, sc_m, sh_m, g_moe[l].reshape(1, d_),
                      w_up_a[l].astype(BF16), w_up_b[l].astype(BF16), w_o[l].astype(BF16), wrh, wrl, cfg)

    tc = min(cfg.route_tc, t_)
    rb = jnp.zeros((ne_pad, tc), F32).at[:cfg.n_experts].set(
        jnp.broadcast_to(router_bias[l].astype(F32)[:, None], (cfg.n_experts, tc)))
    w_kt, lin, kin, gend_raw = route(lg, rb, cfg)
    gps, n_grp, nx = tc // LANES, t_ // LANES, cfg.n_experts
    assert n_grp <= LANES
    per_group = lambda a: jnp.pad(a.transpose(1, 0, 2).reshape(nx, n_grp), ((0, 0), (0, LANES - n_grp)),
                                  constant_values=1e9)
    gend, gstart = per_group(gend_raw[:, :, :gps]), per_group(gend_raw[:, :, gps:2 * gps])
    by_group = lambda a: jnp.pad(a.reshape(nx, n_grp, LANES), ((0, 0), (0, LANES - n_grp), (0, 0)))
    counts = gend[:, n_grp - 1].astype(jnp.int32)
    block_e, r0, count_b, n_used = _block_tables(counts, t_ * cfg.top_k, cfg)
    slot_tok, slot_dst = slots(block_e, r0, count_b, gstart, gend, by_group(lin), by_group(kin), cfg, t_)

    yk = moe(v, block_e, n_used, slot_tok, slot_dst, w_exp_gate[l], w_exp_up[l], w_exp_down[l], cfg)
    out = final(v, h1, yk, w_kt.T, gt_m, g_final.reshape(1, d_),
                w_sh_gate[l].astype(BF16), w_sh_up[l].astype(BF16), w_sh_down[l].astype(BF16), cfg)
    return out.reshape(b_, s_, d_)


def kernel(x, c, w_ada, b_ada, g_attn, w_in, lambda_qk, t5_bias, rel_bias_b, w_up_a, w_up_b, w_o, g_moe,
           w_router, router_bias, w_exp_gate, w_exp_up, w_exp_down, w_sh_gate, w_sh_up, w_sh_down, g_final):
    return _forward(Cfg(), x, c, w_ada, b_ada, g_attn, w_in, lambda_qk, t5_bias, rel_bias_b, w_up_a, w_up_b,
                    w_o, g_moe, w_router, router_bias, w_exp_gate, w_exp_up, w_exp_down,
                    w_sh_gate, w_sh_up, w_sh_down, g_final)
```

```python
import functools
import math
from typing import NamedTuple

import numpy as np
import jax
import jax.numpy as jnp
from jax import lax
from jax.experimental import pallas as pl
from jax.experimental.pallas import tpu as pltpu

F32 = jnp.float32
BF16 = jnp.bfloat16
NEG = -1e30
EPS = 1e-6
HEAD_EPS = 1e-5
LOG2E = 1.4426950408889634
LANES = 128
VMEM_LIMIT = 56 * 1024 * 1024


class Cfg(NamedTuple):
    batch: int = 8
    seq: int = 2048
    d_model: int = 2048
    chunk: int = 64
    ha: int = 8
    da_qk: int = 64
    hb: int = 8
    db: int = 128
    left_chunks: int = 8
    rel_clip: int = 256
    t5_buckets: int = 32
    t5_max_dist: int = 128
    n_experts: int = 64
    top_k: int = 8
    n_groups: int = 8
    topk_groups: int = 4
    d_expert: int = 512
    d_shared: int = 512
    routed_scale: float = 2.5
    n_mod: int = 6
    lam_init: float = 0.8 - 0.6 * math.exp(-0.3 * 0)
    attn_tile: int = 256
    moe_rows: int = 256
    proj_tm: int = 1024
    proj_tn: int = 1024
    proj_sub: int = 256
    merge_tm: int = 512
    merge_sub: int = 256
    route_tc: int = 512
    final_tm: int = 256
    ada_tn: int = 1024


def _sigmoid(x):
    return 1.0 / (1.0 + jnp.exp(-x))


def _split_bf16(x):
    hi = x.astype(BF16)
    lo = (x - hi.astype(F32)).astype(BF16)
    return hi, lo


def _dot(a, b):
    return jnp.dot(a, b, preferred_element_type=F32)


def _dot_t(a, b):
    return lax.dot_general(a, b, (((1,), (1,)), ((), ())), preferred_element_type=F32)


U32 = jnp.uint32
HI_MASK = 0xFFFF0000


def _pack_pairs(x):
    n = x.shape[1] // 2
    bits = pltpu.bitcast(x.astype(BF16).astype(F32), U32)
    return (bits[:, :n] >> 16) | (bits[:, n:] & U32(HI_MASK))


def _unpack_pairs(p):
    return pltpu.bitcast(p << 16, F32), pltpu.bitcast(p & U32(HI_MASK), F32)


def _load_slabs(ref, m, ns):
    return jnp.concatenate([ref[pl.ds(j, m, stride=ns), :] for j in range(ns)], axis=1)


def _store_slabs(ref, val, ns):
    m = val.shape[0]
    for j in range(ns):
        ref[pl.ds(j, m, stride=ns), :] = val[:, j * LANES:(j + 1) * LANES]


def _params(n_axes):
    return pltpu.CompilerParams(dimension_semantics=("arbitrary",) * n_axes,
                                vmem_limit_bytes=VMEM_LIMIT)


def _ada_kernel(c_ref, w_ref, b_ref, o_ref):
    c = c_ref[...]
    s_hi, s_lo = _split_bf16(c * _sigmoid(c))
    w_hi, w_lo = _split_bf16(w_ref[...])
    acc = _dot(s_hi, w_hi) + _dot(s_hi, w_lo) + _dot(s_lo, w_hi)
    o_ref[...] = acc + b_ref[...]


def ada_mod(c_pad, w_ada, b_ada, cfg):
    rows, d = c_pad.shape
    n = w_ada.shape[1]
    tn = min(cfg.ada_tn, n)
    return pl.pallas_call(
        _ada_kernel,
        grid=(n // tn,),
        in_specs=[pl.BlockSpec((rows, d), lambda j: (0, 0)),
                  pl.BlockSpec((d, tn), lambda j: (0, j)),
                  pl.BlockSpec((1, tn), lambda j: (0, j))],
        out_specs=pl.BlockSpec((rows, tn), lambda j: (0, j)),
        out_shape=jax.ShapeDtypeStruct((rows, n), F32),
        compiler_params=_params(1),
        name="ada_mod",
    )(c_pad, w_ada, b_ada.reshape(1, n))


def _inproj_kernel(x_ref, g_ref, sc_ref, sh_ref, w_ref, cs_ref, o_ref, u_ref, *, sub):
    j = pl.program_id(1)

    @pl.when(j == 0)
    def _():
        for r0 in range(0, x_ref.shape[0], sub):
            rows = slice(r0, r0 + sub)
            x = x_ref[rows, :]
            y = x * lax.rsqrt(jnp.mean(x * x, axis=-1, keepdims=True) + EPS) * g_ref[...]
            u = (y * (1.0 + sc_ref[0]) + sh_ref[0]).astype(BF16)
            u_ref[rows, :] = u
            o_ref[rows, :] = (_dot(u, w_ref[...]) * cs_ref[...]).astype(BF16)

    @pl.when(j > 0)
    def _():
        o_ref[...] = (_dot(u_ref[...], w_ref[...]) * cs_ref[...]).astype(BF16)


def in_proj(x2, g, sc, sh, w_bf, colscale, cfg):
    t, d = x2.shape
    n = w_bf.shape[1]
    tm = min(cfg.proj_tm, cfg.seq)
    tn = min(cfg.proj_tn, n)
    per_b = cfg.seq // tm
    return pl.pallas_call(
        functools.partial(_inproj_kernel, sub=min(cfg.proj_sub, tm)),
        grid=(t // tm, n // tn),
        in_specs=[pl.BlockSpec((tm, d), lambda i, j: (i, 0)),
                  pl.BlockSpec((1, d), lambda i, j: (0, 0)),
                  pl.BlockSpec((1, 1, d), lambda i, j: (i // per_b, 0, 0)),
                  pl.BlockSpec((1, 1, d), lambda i, j: (i // per_b, 0, 0)),
                  pl.BlockSpec((d, tn), lambda i, j: (0, j)),
                  pl.BlockSpec((1, tn), lambda i, j: (0, j))],
        out_specs=pl.BlockSpec((tm, tn), lambda i, j: (i, j)),
        out_shape=jax.ShapeDtypeStruct((t, n), BF16),
        scratch_shapes=[pltpu.VMEM((tm, d), BF16)],
        compiler_params=_params(2),
        name="in_proj",
    )(x2, g, sc, sh, w_bf, colscale)


def _attn_a_kernel(lam_ref, qlo_ref, qhi_ref, k_ref, v_ref, tb_ref, olo_ref, ohi_ref,
                   s_ref, m_ref, l_ref, o_ref, *, tile, half, nq, out_scale):
    def stacked(q):
        lane = lax.broadcasted_iota(jnp.int32, q.shape, 1)
        zero = jnp.zeros_like(q)
        return jnp.concatenate([jnp.where(lane < half, q, zero), jnp.where(lane >= half, q, zero)], axis=0)

    def fold(x, op):
        r = x[:, :LANES]
        for c in range(1, tile // LANES):
            r = op(r, x[:, c * LANES:(c + 1) * LANES])
        return r

    def body(p):
        qtile = (p, nq - 1 - p)
        visits = [(0, kb) for kb in range(p + 1)] + [(1, kb) for kb in range(nq - p)]
        qq = (stacked(qlo_ref[...]), stacked(qhi_ref[...]))
        rows = lambda kb: slice(kb * tile, (kb + 1) * tile)
        for i, (w, kb) in enumerate(visits):
            bias = tb_ref[0, min(qtile[w] - kb, 2)]
            s = _dot_t(qq[w], k_ref[rows(kb), :]) + jnp.concatenate([bias, bias], axis=0)
            s_ref[i] = s
            m_ref[i] = fold(s, jnp.maximum)

        def per_tile(ref, op):
            out = [None, None]
            for i, (w, _) in enumerate(visits):
                out[w] = ref[i] if out[w] is None else op(out[w], ref[i])
            return out

        m = [jnp.max(x, axis=-1, keepdims=True) for x in per_tile(m_ref, jnp.maximum)]
        for i, (w, _) in enumerate(visits):
            e = jnp.exp2(s_ref[i] - m[w])
            s_ref[i] = e
            l_ref[i] = fold(e, jnp.add)

        lam = lam_ref[0]
        inv = [1.0 / jnp.sum(x, axis=-1, keepdims=True) for x in per_tile(l_ref, jnp.add)]
        c0 = [x[:tile] for x in inv]
        c1 = [lam * x[tile:] for x in inv]
        for i, (w, kb) in enumerate(visits):
            e = s_ref[i]
            a = e[:tile] * c0[w] - e[tile:] * c1[w]
            o_ref[i] = _dot(a.astype(BF16), v_ref[rows(kb), :])

        for o, out in zip(per_tile(o_ref, jnp.add), (olo_ref, ohi_ref)):
            out[...] = (o * lax.rsqrt(jnp.mean(o * o, axis=-1, keepdims=True) + HEAD_EPS) * out_scale).astype(BF16)

    for k in range(nq // 2):
        pl.when(pl.program_id(2) == k)(functools.partial(body, k))


def attn_a(proj, lam, tb, cfg):
    tile = cfg.attn_tile
    s = cfg.seq
    nq = s // tile
    assert nq % 2 == 0
    npair = nq // 2
    dv = 2 * cfg.da_qk
    assert dv == LANES
    kcol = cfg.ha
    vcol = 2 * cfg.ha
    kern = functools.partial(_attn_a_kernel, tile=tile, half=cfg.da_qk, nq=nq, out_scale=1.0 - cfg.lam_init)
    half_shape = jax.ShapeDtypeStruct((cfg.batch * npair * tile, cfg.ha * dv), BF16)
    y_lo, y_hi = pl.pallas_call(
        kern,
        grid=(cfg.ha, cfg.batch, npair),
        in_specs=[pl.BlockSpec(memory_space=pltpu.SMEM),
                  pl.BlockSpec((tile, LANES), lambda h, b, p: (b * nq + p, h)),
                  pl.BlockSpec((tile, LANES), lambda h, b, p: (b * nq + nq - 1 - p, h)),
                  pl.BlockSpec((s, LANES), lambda h, b, p: (b, kcol + h)),
                  pl.BlockSpec((s, LANES), lambda h, b, p: (b, vcol + h)),
                  pl.BlockSpec((1, 3, tile, tile), lambda h, b, p: (h, 0, 0, 0))],
        out_specs=[pl.BlockSpec((tile, LANES), lambda h, b, p: (b * npair + p, h)),
                   pl.BlockSpec((tile, LANES), lambda h, b, p: (b * npair + npair - 1 - p, h))],
        out_shape=[half_shape, half_shape],
        scratch_shapes=[pltpu.VMEM((nq + 1, 2 * tile, tile), F32),
                        pltpu.VMEM((nq + 1, 2 * tile, LANES), F32),
                        pltpu.VMEM((nq + 1, 2 * tile, LANES), F32),
                        pltpu.VMEM((nq + 1, tile, LANES), F32)],
        compiler_params=_params(3),
        name="attn_a",
    )(lam, proj, proj, proj, proj, tb)
    w = cfg.ha * dv
    rows = npair * tile
    return jnp.concatenate([y_lo.reshape(cfg.batch, rows, w), y_hi.reshape(cfg.batch, rows, w)],
                           axis=1).reshape(cfg.batch * s, w)


ATTN_B_TILES = 2


def _attn_b_kernel(q_ref, k_ref, v_ref, tb_ref, o_ref, *, tile, nwin):
    for u in range(ATTN_B_TILES):
        qi = pl.program_id(2) * ATTN_B_TILES + u
        q = q_ref[u * tile:(u + 1) * tile, :]
        ss, vs = [], []
        for j in range(nwin):
            kb = qi - (nwin - 1) + j
            start = pl.multiple_of(jnp.maximum(kb, 0) * tile, tile)
            ss.append(_dot_t(q, k_ref[pl.ds(start, tile), :]) + tb_ref[0, jnp.where(kb >= 0, j, nwin)])
            vs.append(v_ref[pl.ds(start, tile), :])
        m = jnp.max(ss[0], axis=-1, keepdims=True)
        for s in ss[1:]:
            m = jnp.maximum(m, jnp.max(s, axis=-1, keepdims=True))
        ps = [jnp.exp2(s - m) for s in ss]
        l = jnp.sum(ps[0], axis=-1, keepdims=True)
        for p in ps[1:]:
            l = l + jnp.sum(p, axis=-1, keepdims=True)
        acc = _dot(ps[0].astype(BF16), vs[0])
        for p, v in zip(ps[1:], vs[1:]):
            acc = acc + _dot(p.astype(BF16), v)
        o_ref[u * tile:(u + 1) * tile, :] = (acc * (1.0 / l)).astype(BF16)


def attn_b(proj, tb, cfg):
    t = proj.shape[0]
    tile = cfg.attn_tile
    s = cfg.seq
    nq = s // tile
    assert cfg.db == LANES and nq % ATTN_B_TILES == 0
    nstep = nq // ATTN_B_TILES
    rows = ATTN_B_TILES * tile
    nwin = (cfg.left_chunks * cfg.chunk) // tile + 1
    base = (2 * cfg.ha * 2 * cfg.da_qk + cfg.ha * 2 * cfg.da_qk) // LANES
    kern = functools.partial(_attn_b_kernel, tile=tile, nwin=nwin)
    return pl.pallas_call(
        kern,
        grid=(cfg.hb, cfg.batch, nstep),
        in_specs=[pl.BlockSpec((rows, LANES), lambda h, b, i: (b * nstep + i, base + h)),
                  pl.BlockSpec((s, LANES), lambda h, b, i: (b, base + cfg.hb + h)),
                  pl.BlockSpec((s, LANES), lambda h, b, i: (b, base + 2 * cfg.hb + h)),
                  pl.BlockSpec((1, nwin + 1, tile, tile), lambda h, b, i: (h, 0, 0, 0))],
        out_specs=pl.BlockSpec((rows, LANES), lambda h, b, i: (b * nstep + i, h)),
        out_shape=jax.ShapeDtypeStruct((t, cfg.hb * cfg.db), BF16),
        compiler_params=_params(3),
        name="attn_b",
    )(proj, proj, proj, tb)


def _merge_kernel(ya_ref, yb_ref, ga_ref, gb_ref, x_ref, gta_ref, scm_ref, shm_ref, gmoe_ref,
                  wua_ref, wub_ref, wo_ref, wrh_ref, wrl_ref, h1_ref, v_ref, lg_ref, *, sub):
    ns = v_ref.shape[0] // h1_ref.shape[0]
    for r0 in range(0, h1_ref.shape[0], sub):
        rows = slice(r0, r0 + sub)
        a = _dot(ya_ref[rows, :], wua_ref[...])
        b = _dot(yb_ref[rows, :], wub_ref[...])
        merged = _sigmoid(ga_ref[rows, :].astype(F32)) * a + _sigmoid(gb_ref[rows, :].astype(F32)) * b
        att = _dot(merged.astype(BF16), wo_ref[...])
        h1 = x_ref[rows, :] + gta_ref[0] * att
        h1_ref[rows, :] = h1
        y = h1 * lax.rsqrt(jnp.mean(h1 * h1, axis=-1, keepdims=True) + EPS) * gmoe_ref[...]
        v = y * (1.0 + scm_ref[0]) + shm_ref[0]
        _store_slabs(v_ref.at[pl.ds(r0 * ns, sub * ns), :], _pack_pairs(v), ns)
        v_hi, v_lo = _split_bf16(v)
        wrh = wrh_ref[...]
        lg_ref[:, rows] = _dot_t(wrh, v_hi) + _dot_t(wrh, v_lo) + _dot_t(wrl_ref[...], v_hi)


def merge(ya, yb, proj, x2, gta, scm, shm, gmoe, wua, wub, wo, wrh, wrl, cfg):
    t, d = x2.shape
    tm = min(cfg.merge_tm, cfg.seq)
    per_b = cfg.seq // tm
    wa = ya.shape[1]
    wb = yb.shape[1]
    gate0 = (proj.shape[1] - 2 * d) // d
    ne = wrh.shape[0]
    ns = d // (2 * LANES)
    const = lambda i: (0, 0)
    row3 = lambda i: (i // per_b, 0, 0)
    resident = lambda shape: pl.BlockSpec(shape, const, pipeline_mode=pl.Buffered(1))
    return pl.pallas_call(
        functools.partial(_merge_kernel, sub=min(cfg.merge_sub, tm)),
        grid=(t // tm,),
        in_specs=[pl.BlockSpec((tm, wa), lambda i: (i, 0)),
                  pl.BlockSpec((tm, wb), lambda i: (i, 0)),
                  pl.BlockSpec((tm, d), lambda i: (i, gate0)),
                  pl.BlockSpec((tm, d), lambda i: (i, gate0 + 1)),
                  pl.BlockSpec((tm, d), lambda i: (i, 0)),
                  pl.BlockSpec((1, 1, d), row3),
                  pl.BlockSpec((1, 1, d), row3),
                  pl.BlockSpec((1, 1, d), row3),
                  pl.BlockSpec((1, d), const),
                  resident((wa, d)),
                  resident((wb, d)),
                  resident((d, d)),
                  resident((ne, d)),
                  resident((ne, d))],
        out_specs=[pl.BlockSpec((tm, d), lambda i: (i, 0)),
                   pl.BlockSpec((tm * ns, LANES), lambda i: (i, 0)),
                   pl.BlockSpec((ne, tm), lambda i: (0, i))],
        out_shape=[jax.ShapeDtypeStruct((t, d), F32),
                   jax.ShapeDtypeStruct((t * ns, LANES), U32),
                   jax.ShapeDtypeStruct((ne, t), F32)],
        compiler_params=_params(1),
        name="merge",
    )(ya, yb, proj, proj, x2, gta, scm, shm, gmoe, wua, wub, wo, wrh, wrl)


def _route_kernel(lg_ref, rb_ref, tri_ref, w_ref, lin_ref, kin_ref, gend_ref, carry_ref,
                  *, n_exp, n_groups, topk_groups, top_k, scale):
    gsz = n_exp // n_groups
    tc = lg_ref.shape[1]
    ninf = -jnp.inf
    step = pl.program_id(0)

    @pl.when(step == 0)
    def _():
        carry_ref[...] = jnp.zeros(carry_ref.shape, F32)

    row = lax.broadcasted_iota(jnp.int32, (gsz, tc), 0).astype(F32)
    sc, bi = [], []
    for g in range(n_groups):
        s = _sigmoid(lg_ref[g * gsz:(g + 1) * gsz, :])
        sc.append(s)
        bi.append(s + rb_ref[g * gsz:(g + 1) * gsz, :])
    gs = []
    for x in bi:
        m1 = jnp.max(x, axis=0, keepdims=True)
        eq = x == m1
        cnt = jnp.sum(jnp.where(eq, 1.0, 0.0), axis=0, keepdims=True)
        second = jnp.max(jnp.where(eq, ninf, x), axis=0, keepdims=True)
        gs.append(m1 + jnp.where(cnt >= 2.0, m1, second))
    gmat = jnp.concatenate(gs, axis=0)
    grow = lax.broadcasted_iota(jnp.int32, gmat.shape, 0).astype(F32)
    gsel = jnp.zeros(gmat.shape, F32)
    for _ in range(topk_groups):
        m = jnp.max(gmat, axis=0, keepdims=True)
        idx = jnp.min(jnp.where(gmat == m, grow, float(n_groups)), axis=0, keepdims=True)
        hit = grow == idx
        gsel = jnp.where(hit, 1.0, gsel)
        gmat = jnp.where(hit, ninf, gmat)
    cur = [jnp.where(gsel[g:g + 1, :] > 0.5, bi[g], ninf) for g in range(n_groups)]
    ids = [row + float(g * gsz) for g in range(n_groups)]
    w_rows = []
    chosen = [jnp.zeros((gsz, tc), F32) for _ in range(n_groups)]
    kplus = [jnp.zeros((gsz, tc), F32) for _ in range(n_groups)]
    for k in range(top_k):
        mg = cur[0]
        for c in cur[1:]:
            mg = jnp.maximum(mg, c)
        m = jnp.max(mg, axis=0, keepdims=True)
        cand = jnp.where(cur[0] == m, ids[0], float(n_exp))
        for c, i in zip(cur[1:], ids[1:]):
            cand = jnp.minimum(cand, jnp.where(c == m, i, float(n_exp)))
        idx = jnp.min(cand, axis=0, keepdims=True)
        wsum = jnp.zeros((gsz, tc), F32)
        for g in range(n_groups):
            hit = ids[g] == idx
            wsum = wsum + jnp.where(hit, sc[g], 0.0)
            cur[g] = jnp.where(hit, ninf, cur[g])
            chosen[g] = jnp.where(hit, 1.0, chosen[g])
            kplus[g] = jnp.where(hit, float(k + 1), kplus[g])
        w_rows.append(jnp.sum(wsum, axis=0, keepdims=True))
    w = jnp.concatenate(w_rows, axis=0)
    w_ref[...] = w / jnp.sum(w, axis=0, keepdims=True) * scale

    onehot = jnp.concatenate(chosen, axis=0)
    incl = _dot(onehot.astype(BF16), tri_ref[...])
    lin_ref[...] = incl.astype(BF16)
    kin_ref[...] = jnp.concatenate(kplus, axis=0).astype(BF16)
    gps = tc // LANES
    lane = lax.broadcasted_iota(jnp.int32, (n_exp, LANES), 1)
    run = carry_ref[...]
    ends = jnp.zeros((n_exp, LANES), F32)
    for g in range(gps):
        ends = jnp.where(lane == gps + g, run, ends)
        run = run + incl[:, (g + 1) * LANES - 1:(g + 1) * LANES]
        ends = jnp.where(lane == g, run, ends)
    carry_ref[...] = run
    gend_ref[0] = ends


def route(lg, rb, cfg):
    ne, t = lg.shape
    nx = cfg.n_experts
    tc = min(cfg.route_tc, t)
    nsteps = t // tc
    tri = jnp.asarray(np.kron(np.eye(tc // LANES), np.triu(np.ones((LANES, LANES)))), BF16)
    kern = functools.partial(_route_kernel, n_exp=cfg.n_experts, n_groups=cfg.n_groups,
                             topk_groups=cfg.topk_groups, top_k=cfg.top_k, scale=cfg.routed_scale)
    return pl.pallas_call(
        kern,
        grid=(nsteps,),
        in_specs=[pl.BlockSpec((ne, tc), lambda i: (0, i)),
                  pl.BlockSpec((ne, tc), lambda i: (0, 0)),
                  pl.BlockSpec((tc, tc), lambda i: (0, 0))],
        out_specs=[pl.BlockSpec((cfg.top_k, tc), lambda i: (0, i)),
                   pl.BlockSpec((nx, tc), lambda i: (0, i)),
                   pl.BlockSpec((nx, tc), lambda i: (0, i)),
                   pl.BlockSpec((1, nx, LANES), lambda i: (i, 0, 0))],
        out_shape=[jax.ShapeDtypeStruct((cfg.top_k, t), F32),
                   jax.ShapeDtypeStruct((nx, t), BF16),
                   jax.ShapeDtypeStruct((nx, t), BF16),
                   jax.ShapeDtypeStruct((nsteps, nx, LANES), F32)],
        scratch_shapes=[pltpu.VMEM((cfg.n_experts, LANES), F32)],
        compiler_params=_params(1),
        name="route",
    )(lg, rb, tri)


SLOT_BLOCKS = 4


def _slots_kernel(be_ref, r0_ref, cnt_ref, gstart_ref, gend_ref, lin_ref, kin_ref, tok_ref, dst_ref,
                  *, bm, top_k, n_tok, n_assign):
    step = pl.program_id(0)
    rows = lax.broadcasted_iota(jnp.int32, (bm, LANES), 0)
    lane = lax.broadcasted_iota(jnp.int32, (bm, LANES), 1).astype(F32)
    ones = jnp.ones((16, LANES), BF16)
    gids = lax.broadcasted_iota(jnp.int32, (16, LANES), 1).astype(F32).astype(BF16)
    lane_row = lax.broadcasted_iota(jnp.int32, (1, bm), 1)
    for u in range(SLOT_BLOCKS):
        b = step * SLOT_BLOCKS + u
        e = be_ref[b]
        r0 = r0_ref[b]
        cnt = cnt_ref[b]
        r = (rows + r0).astype(F32)
        gstart = gstart_ref[pl.ds(e, 1), :]
        in_group = jnp.logical_and(gstart <= r, r < gend_ref[pl.ds(e, 1), :])
        group = jnp.where(in_group, 1.0, 0.0).astype(BF16)
        want = r - jnp.sum(jnp.where(in_group, gstart, 0.0), axis=1, keepdims=True) + 1.0
        kin = _dot(group, kin_ref[e])
        hit = jnp.logical_and(_dot(group, lin_ref[e]) == want, kin > 0.5)
        t_loc = _dot_t(ones, jnp.where(hit, lane, 0.0).astype(BF16))[0:1]
        kp1 = _dot_t(ones, jnp.where(hit, kin, 0.0).astype(BF16))[0:1]
        gid = _dot_t(gids, group)[0:1]
        tok = (gid * float(LANES) + t_loc).astype(jnp.int32)
        kk = kp1.astype(jnp.int32) - 1
        valid = (lane_row + r0) < cnt
        dummy = n_assign + lax.rem(b, 4) * bm + lane_row
        tok_ref[u] = jnp.where(valid, tok, lax.rem(b * bm + lane_row, n_tok))
        dst_ref[u] = jnp.where(valid, kk * n_tok + tok, dummy)


def slots(block_e, r0, cntb, gstart, gend, lin, kin, cfg, n_tok):
    nb = block_e.shape[0]
    bm = cfg.moe_rows
    assert nb % SLOT_BLOCKS == 0 and gstart.shape[1] == LANES and lin.shape[1:] == (LANES, LANES)
    kern = functools.partial(_slots_kernel, bm=bm, top_k=cfg.top_k, n_tok=n_tok, n_assign=n_tok * cfg.top_k)
    whole = lambda a: pl.BlockSpec(a.shape, lambda i, be, r0, cn: (0,) * a.ndim)
    out_blk = pl.BlockSpec((SLOT_BLOCKS, 1, bm), lambda i, be, r0, cn: (i, 0, 0))
    grid_spec = pltpu.PrefetchScalarGridSpec(
        num_scalar_prefetch=3,
        grid=(nb // SLOT_BLOCKS,),
        in_specs=[whole(gstart), whole(gend), whole(lin), whole(kin)],
        out_specs=[out_blk, out_blk],
    )
    return pl.pallas_call(
        kern,
        grid_spec=grid_spec,
        out_shape=[jax.ShapeDtypeStruct((nb, 1, bm), jnp.int32), jax.ShapeDtypeStruct((nb, 1, bm), jnp.int32)],
        compiler_params=_params(1),
        name="slots",
    )(block_e, r0, cntb, gstart, gend, lin, kin)


MOE_RING = 4
MOE_DUMMY = 4


def _moe_kernel(be_ref, tok0_ref, tok1_ref, tok2_ref, tokn_ref, dstp_ref, dstc_ref, v_hbm, wg_hbm, wu_hbm, wd_hbm,
                yk_hbm, xbuf, ybuf, xbf, wst_g, wst_u, wst_d, wgb, wub, wdb, gsem, ssem, wsem, *, bm, n_assign):
    b = pl.program_id(0)
    nb = pl.num_programs(0)
    slot = lax.rem(b, MOE_RING)
    ns = v_hbm.shape[1]

    def weight_copies(e, ws):
        return [pltpu.make_async_copy(src.at[e], dst.at[ws], wsem.at[ws])
                for src, dst in ((wg_hbm, wst_g), (wu_hbm, wst_u), (wd_hbm, wst_d))]

    def unpack_block(s):
        lo, hi = _unpack_pairs(_load_slabs(xbuf.at[s], bm, ns))
        xbf[s % 2] = jnp.concatenate([lo, hi], axis=1).astype(BF16)

    def gather_copy(t, r, s):
        return pltpu.make_async_copy(v_hbm.at[t], xbuf.at[s, pl.ds(r * ns, ns), :], gsem.at[s])

    def scatter_copy(r, d, s):
        return pltpu.make_async_copy(ybuf.at[s, pl.ds(r * ns, ns), :], yk_hbm.at[d], ssem.at[s])

    def wait_gather(s):
        for r in range(bm):
            gather_copy(0, r, s).wait()

    def wait_scatter(s):
        for r in range(bm):
            scatter_copy(r, 0, s).wait()

    @pl.when(b == 0)
    def _():
        ybuf[...] = jnp.zeros(ybuf.shape, U32)
        for s in range(MOE_RING - 1):
            for r in range(bm):
                scatter_copy(r, n_assign + s * bm + r, s).start(priority=r % 2)
        for s, tok_ref in enumerate((tok0_ref, tok1_ref, tok2_ref)):
            for r in range(bm):
                gather_copy(tok_ref[0, 0, r], r, s).start(priority=r % 2)
        wait_gather(0)
        unpack_block(0)
        for c in weight_copies(be_ref[0], 0):
            c.start()

    @pl.when(be_ref[nb + 1 + b] == 1)
    def _():
        ws = be_ref[2 * nb + 1 + b]
        nxt_e = be_ref[3 * nb + 1 + b]
        for c in weight_copies(be_ref[b], ws):
            c.wait()

        @pl.when(nxt_e >= 0)
        def _():
            for c in weight_copies(nxt_e, 1 - ws):
                c.start()

        wgb[...] = wst_g[ws].astype(BF16)
        wub[...] = wst_u[ws].astype(BF16)
        wdb[...] = wst_d[ws].astype(BF16)

    def block_step(s):
        nxt, prv = (s + 1) % MOE_RING, (s + 3) % MOE_RING
        wait_gather(nxt)
        wait_scatter(s)
        for r in range(bm):
            scatter_copy(r, dstp_ref[0, 0, r], prv).start(priority=r % 2)
        for r in range(bm):
            gather_copy(tokn_ref[0, 0, r], r, prv).start(priority=r % 2)
        x = xbf[s % 2]
        g = _dot(x, wgb[...])
        u = _dot(x, wub[...])
        h = (g * _sigmoid(g)) * u
        unpack_block(nxt)
        _store_slabs(ybuf.at[s], _pack_pairs(_dot(h.astype(BF16), wdb[...])), ns)

    last = be_ref[nb] - 1
    for s in range(MOE_RING):
        pl.when(jnp.logical_and(slot == s, b <= last))(functools.partial(block_step, s))

    @pl.when(b == last)
    def _():
        def drain(s):
            for r in range(bm):
                scatter_copy(r, dstc_ref[0, 0, r], s).start(priority=r % 2)
            for j in (2, 3):
                wait_gather((s + j) % MOE_RING)
            for j in (1, 2, 3, 0):
                wait_scatter((s + j) % MOE_RING)

        for s in range(MOE_RING):
            pl.when(slot == s)(functools.partial(drain, s))


def moe(vp, block_e, n_used, slot_tok, slot_dst, wg, wu, wd, cfg):
    d = wg.shape[1]
    ns = d // (2 * LANES)
    t = vp.shape[0] // ns
    bm = cfg.moe_rows
    v3 = vp.reshape(t, ns, LANES)
    nb = block_e.shape[0]
    assert nb >= 2
    de = wg.shape[2]
    n_assign = t * cfg.top_k
    first = (n_assign + 3 * bm + jnp.arange(bm, dtype=jnp.int32)).reshape(1, 1, bm)
    dst_ext = jnp.concatenate([first, slot_dst], axis=0)
    kern = functools.partial(_moe_kernel, bm=bm, n_assign=n_assign)
    smem_blk = lambda f: pl.BlockSpec((1, 1, bm), f, memory_space=pltpu.SMEM)
    grid_spec = pltpu.PrefetchScalarGridSpec(
        num_scalar_prefetch=1,
        grid=(nb,),
        in_specs=[smem_blk(lambda b, be: (0, 0, 0)),
                  smem_blk(lambda b, be: (1, 0, 0)),
                  smem_blk(lambda b, be: (min(2, nb - 1), 0, 0)),
                  smem_blk(lambda b, be: (jnp.minimum(b + 3, nb - 1), 0, 0)),
                  smem_blk(lambda b, be: (b, 0, 0)),
                  smem_blk(lambda b, be: (b + 1, 0, 0)),
                  pl.BlockSpec(memory_space=pl.ANY)] + [pl.BlockSpec(memory_space=pl.ANY)] * 3,
        out_specs=pl.BlockSpec(memory_space=pl.ANY),
        scratch_shapes=[pltpu.VMEM((MOE_RING, bm * ns, LANES), U32),
                        pltpu.VMEM((MOE_RING, bm * ns, LANES), U32),
                        pltpu.VMEM((2, bm, d), BF16),
                        pltpu.VMEM((2, d, de), F32), pltpu.VMEM((2, d, de), F32), pltpu.VMEM((2, de, d), F32),
                        pltpu.VMEM((d, de), BF16), pltpu.VMEM((d, de), BF16), pltpu.VMEM((de, d), BF16),
                        pltpu.SemaphoreType.DMA((MOE_RING,)), pltpu.SemaphoreType.DMA((MOE_RING,)),
                        pltpu.SemaphoreType.DMA((2,))],
    )
    idx = jnp.arange(nb, dtype=jnp.int32)
    is_first = jnp.logical_and(idx < n_used, block_e != jnp.concatenate([jnp.full((1,), -1, jnp.int32), block_e[:-1]]))
    stage = (jnp.cumsum(is_first.astype(jnp.int32)) - 1) % 2
    later_first = jnp.flip(lax.cummin(jnp.flip(jnp.where(is_first, idx, nb))))
    nxt_pos = jnp.concatenate([later_first[1:], jnp.full((1,), nb, jnp.int32)])
    nxt_e = jnp.sum(jnp.where(idx[None, :] == nxt_pos[:, None], block_e[None, :] + 1, 0), axis=1) - 1
    sched = jnp.concatenate([block_e, n_used.reshape(1), is_first.astype(jnp.int32), stage.astype(jnp.int32),
                             nxt_e.astype(jnp.int32)])
    return pl.pallas_call(
        kern,
        grid_spec=grid_spec,
        out_shape=jax.ShapeDtypeStruct((n_assign + MOE_DUMMY * bm, ns, LANES), U32),
        compiler_params=_params(1),
        name="moe",
    )(sched, slot_tok, slot_tok, slot_tok, slot_tok, dst_ext, dst_ext, v3, wg, wu, wd)


def _final_kernel(w_ref, v_ref, h1_ref, gtm_ref, gfin_ref, sg_ref, su_ref, sd_ref, *rest, top_k):
    yk_refs, o_ref, lo_ref, hi_ref = rest[:top_k], rest[top_k], rest[top_k + 1], rest[top_k + 2]
    tm = h1_ref.shape[0]
    ns = yk_refs[0].shape[1]

    def combine(t, carry):
        lo, hi = _unpack_pairs(yk_refs[0][t])
        acc_lo, acc_hi = w_ref[t, 0] * lo, w_ref[t, 0] * hi
        for k in range(1, top_k):
            lo, hi = _unpack_pairs(yk_refs[k][t])
            acc_lo, acc_hi = acc_lo + w_ref[t, k] * lo, acc_hi + w_ref[t, k] * hi
        rows = pl.ds(pl.multiple_of(t * ns, ns), ns)
        lo_ref[rows, :] = acc_lo
        hi_ref[rows, :] = acc_hi
        return carry

    lax.fori_loop(0, tm, combine, 0, unroll=8)
    routed = jnp.concatenate([_load_slabs(lo_ref, tm, ns), _load_slabs(hi_ref, tm, ns)], axis=1)

    vb = jnp.concatenate(_unpack_pairs(_load_slabs(v_ref, tm, ns)), axis=1).astype(BF16)
    g = _dot(vb, sg_ref[...])
    u = _dot(vb, su_ref[...])
    y = routed + _dot(((g * _sigmoid(g)) * u).astype(BF16), sd_ref[...])
    h2 = h1_ref[...] + gtm_ref[0] * y
    o_ref[...] = h2 * lax.rsqrt(jnp.mean(h2 * h2, axis=-1, keepdims=True) + EPS) * gfin_ref[...]


def final(vp, h1, yk3, w_tk, gtm, gfin, sg, su, sd, cfg):
    t, d = h1.shape
    tm = min(cfg.final_tm, cfg.seq)
    per_b = cfg.seq // tm
    nt = t // tm
    ds = sg.shape[1]
    nslab = d // (2 * LANES)
    const = lambda i: (0, 0)
    kern = functools.partial(_final_kernel, top_k=cfg.top_k)
    plane = lambda k: pl.BlockSpec((tm, nslab, LANES), lambda i: (k * nt + i, 0, 0))
    return pl.pallas_call(
        kern,
        grid=(nt,),
        in_specs=[pl.BlockSpec((tm, cfg.top_k), lambda i: (i, 0), memory_space=pltpu.SMEM),
                  pl.BlockSpec((tm * nslab, LANES), lambda i: (i, 0)),
                  pl.BlockSpec((tm, d), lambda i: (i, 0)),
                  pl.BlockSpec((1, 1, d), lambda i: (i // per_b, 0, 0)),
                  pl.BlockSpec((1, d), const),
                  pl.BlockSpec((d, ds), const),
                  pl.BlockSpec((d, ds), const),
                  pl.BlockSpec((ds, d), const)] + [plane(k) for k in range(cfg.top_k)],
        out_specs=pl.BlockSpec((tm, d), lambda i: (i, 0)),
        out_shape=jax.ShapeDtypeStruct((t, d), F32),
        scratch_shapes=[pltpu.VMEM((tm * nslab, LANES), F32), pltpu.VMEM((tm * nslab, LANES), F32)],
        compiler_params=_params(1),
        name="final",
    )(w_tk, vp, h1, gtm, gfin, sg, su, sd, *([yk3] * cfg.top_k))


def _t5_bucket(rel, cfg):
    nb = cfg.t5_buckets // 2
    ret = (rel > 0).astype(np.int32) * nb
    n = np.abs(rel)
    max_exact = nb // 2
    large = max_exact + (np.log(np.maximum(n, 1) / max_exact) / math.log(cfg.t5_max_dist / max_exact)
                         * (nb - max_exact)).astype(np.int32)
    large = np.minimum(large, nb - 1)
    return (ret + np.where(n < max_exact, n, large)).astype(np.int32)


def _toeplitz(vec, n):
    lead = vec.shape[:-1]
    flat = jnp.tile(vec, (1,) * len(lead) + (n,))[..., :n * (2 * n - 1)]
    return flat.reshape(lead + (n, 2 * n - 1))[..., :n]


def _rel_offsets(n):
    m = np.arange(2 * n)
    return np.where(m < n, m, m - 2 * n)


def _bias_tables_a(t5_bias, cfg):
    tile = cfg.attn_tile
    assert tile + 1 >= cfg.t5_max_dist and tile % cfg.chunk == 0
    ql = np.arange(tile)[:, None]
    kl = np.arange(tile)[None, :]
    allowed = (kl // cfg.chunk) <= (ql // cfg.chunk)
    r = _rel_offsets(tile)
    t5 = t5_bias.astype(F32)
    diag = _toeplitz(t5[_t5_bucket(r, cfg)].T, tile)
    prev = _toeplitz(t5[_t5_bucket(r - tile, cfg)].T, tile)
    far = jnp.broadcast_to(t5[_t5_bucket(np.array(-2 * tile), cfg)][:, None, None], diag.shape)
    return jnp.stack([jnp.where(allowed[None], diag, NEG), prev, far], axis=1)


def _bias_tables_b(rel_table, cfg):
    tile = cfg.attn_tile
    assert (cfg.left_chunks * cfg.chunk) % tile == 0 and tile % cfg.chunk == 0
    nwin = (cfg.left_chunks * cfg.chunk) // tile + 1
    ql = np.arange(tile)[:, None]
    kl = np.arange(tile)[None, :]
    r = _rel_offsets(tile)
    rt = rel_table.astype(F32)
    tabs = []
    for j in range(nwin):
        off = (j - (nwin - 1)) * tile
        idx = np.clip(r + off, -cfg.rel_clip, cfg.rel_clip) + cfg.rel_clip
        dchunk = ql // cfg.chunk - (kl + off) // cfg.chunk
        band = (dchunk >= 0) & (dchunk <= cfg.left_chunks)
        tabs.append(jnp.where(band[None], _toeplitz(rt[idx].T, tile), NEG))
    tabs.append(jnp.full_like(tabs[0], NEG))
    return jnp.stack(tabs, axis=1)


def _block_tables(counts, n_assign, cfg):
    e_ = cfg.n_experts
    bm = cfg.moe_rows
    nb = -(-(n_assign + e_ * (bm - 1)) // bm)
    pcounts = (counts + bm - 1) // bm * bm
    pends = jnp.cumsum(pcounts)
    pstarts = pends - pcounts
    start_b = jnp.arange(nb, dtype=jnp.int32) * bm
    block_e = jnp.minimum(jnp.sum((pends[None, :] <= start_b[:, None]).astype(jnp.int32), axis=1), e_ - 1)
    mine = jnp.arange(e_, dtype=jnp.int32)[None, :] == block_e[:, None]
    pstart_b = jnp.sum(jnp.where(mine, pstarts[None, :], 0), axis=1)
    count_b = jnp.sum(jnp.where(mine, counts[None, :], 0), axis=1)
    n_used = (pends[e_ - 1] // bm).astype(jnp.int32)
    return block_e.astype(jnp.int32), (start_b - pstart_b).astype(jnp.int32), count_b.astype(jnp.int32), n_used


def _forward(cfg, x, c, w_ada, b_ada, g_attn, w_in, lambda_qk, t5_bias, rel_bias_b, w_up_a, w_up_b, w_o,
             g_moe, w_router, router_bias, w_exp_gate, w_exp_up, w_exp_down,
             w_sh_gate, w_sh_up, w_sh_down, g_final):
    b_, s_, d_ = x.shape
    t_ = b_ * s_
    l = 0
    qk_a = cfg.ha * 2 * cfg.da_qk
    width_a = cfg.ha * 2 * cfg.da_qk
    width_b = cfg.hb * cfg.db
    in_cols = w_in.shape[2]

    rows = max(16, b_)
    c_pad = jnp.zeros((rows, d_), F32).at[:b_].set(c)
    mod = ada_mod(c_pad, w_ada[l], b_ada[l], cfg)[:b_]
    sh_a, sc_a, gt_a, sh_m, sc_m, gt_m = [m.reshape(b_, 1, d_) for m in jnp.split(mod, cfg.n_mod, axis=-1)]

    colscale = np.ones((1, in_cols), np.float32)
    colscale[0, :qk_a] = cfg.da_qk ** -0.5 * LOG2E
    qb0 = 2 * qk_a + width_a
    colscale[0, qb0:qb0 + width_b] = cfg.db ** -0.5 * LOG2E
    x2 = x.reshape(t_, d_)
    proj = in_proj(x2, g_attn[l].reshape(1, d_), sc_a, sh_a, w_in[l].astype(BF16), jnp.asarray(colscale), cfg)

    lq = lambda_qk[l].astype(F32)
    lam = jnp.exp(jnp.sum(lq[0] * lq[1])) - jnp.exp(jnp.sum(lq[2] * lq[3])) + cfg.lam_init
    ya = attn_a(proj, lam.reshape(1), _bias_tables_a(t5_bias * LOG2E, cfg), cfg)
    yb = attn_b(proj, _bias_tables_b(rel_bias_b[l] * LOG2E, cfg), cfg)

    ne_pad = max(LANES, cfg.n_experts)
    wrt = jnp.zeros((ne_pad, d_), F32).at[:cfg.n_experts].set(w_router[l].T)
    wrh, wrl = _split_bf16(wrt)
    h1, v, lg = merge(ya, yb, proj, x2, gt_a, sc_m, sh_m, g_moe[l].reshape(1, d_),
                      w_up_a[l].astype(BF16), w_up_b[l].astype(BF16), w_o[l].astype(BF16), wrh, wrl, cfg)

    tc = min(cfg.route_tc, t_)
    rb = jnp.zeros((ne_pad, tc), F32).at[:cfg.n_experts].set(
        jnp.broadcast_to(router_bias[l].astype(F32)[:, None], (cfg.n_experts, tc)))
    w_kt, lin, kin, gend_raw = route(lg, rb, cfg)
    gps, n_grp, nx = tc // LANES, t_ // LANES, cfg.n_experts
    assert n_grp <= LANES
    per_group = lambda a: jnp.pad(a.transpose(1, 0, 2).reshape(nx, n_grp), ((0, 0), (0, LANES - n_grp)),
                                  constant_values=1e9)
    gend, gstart = per_group(gend_raw[:, :, :gps]), per_group(gend_raw[:, :, gps:2 * gps])
    by_group = lambda a: jnp.pad(a.reshape(nx, n_grp, LANES), ((0, 0), (0, LANES - n_grp), (0, 0)))
    counts = gend[:, n_grp - 1].astype(jnp.int32)
    block_e, r0, count_b, n_used = _block_tables(counts, t_ * cfg.top_k, cfg)
    slot_tok, slot_dst = slots(block_e, r0, count_b, gstart, gend, by_group(lin), by_group(kin), cfg, t_)

    yk = moe(v, block_e, n_used, slot_tok, slot_dst, w_exp_gate[l], w_exp_up[l], w_exp_down[l], cfg)
    out = final(v, h1, yk, w_kt.T, gt_m, g_final.reshape(1, d_),
                w_sh_gate[l].astype(BF16), w_sh_up[l].astype(BF16), w_sh_down[l].astype(BF16), cfg)
    return out.reshape(b_, s_, d_)


def kernel(x, c, w_ada, b_ada, g_attn, w_in, lambda_qk, t5_bias, rel_bias_b, w_up_a, w_up_b, w_o, g_moe,
           w_router, router_bias, w_exp_gate, w_exp_up, w_exp_down, w_sh_gate, w_sh_up, w_sh_down, g_final):
    return _forward(Cfg(), x, c, w_ada, b_ada, g_attn, w_in, lambda_qk, t5_bias, rel_bias_b, w_up_a, w_up_b,
                    w_o, g_moe, w_router, router_bias, w_exp_gate, w_exp_up, w_exp_down,
                    w_sh_gate, w_sh_up, w_sh_down, g_final)
```

```python
import functools
import math
from typing import NamedTuple

import numpy as np
import jax
import jax.numpy as jnp
from jax import lax
from jax.experimental import pallas as pl
from jax.experimental.pallas import tpu as pltpu

F32 = jnp.float32
BF16 = jnp.bfloat16
NEG = -1e30
EPS = 1e-6
HEAD_EPS = 1e-5
LOG2E = 1.4426950408889634
LANES = 128
VMEM_LIMIT = 56 * 1024 * 1024


class Cfg(NamedTuple):
    batch: int = 8
    seq: int = 2048
    d_model: int = 2048
    chunk: int = 64
    ha: int = 8
    da_qk: int = 64
    hb: int = 8
    db: int = 128
    left_chunks: int = 8
    rel_clip: int = 256
    t5_buckets: int = 32
    t5_max_dist: int = 128
    n_experts: int = 64
    top_k: int = 8
    n_groups: int = 8
    topk_groups: int = 4
    d_expert: int = 512
    d_shared: int = 512
    routed_scale: float = 2.5
    n_mod: int = 6
    lam_init: float = 0.8 - 0.6 * math.exp(-0.3 * 0)
    attn_tile: int = 256
    moe_rows: int = 256
    proj_tm: int = 1024
    proj_tn: int = 1024
    proj_sub: int = 256
    merge_tm: int = 512
    merge_sub: int = 256
    route_tc: int = 512
    final_tm: int = 256
    ada_tn: int = 1024


def _sigmoid(x):
    return 1.0 / (1.0 + jnp.exp(-x))


def _split_bf16(x):
    hi = x.astype(BF16)
    lo = (x - hi.astype(F32)).astype(BF16)
    return hi, lo


def _dot(a, b):
    return jnp.dot(a, b, preferred_element_type=F32)


def _dot_t(a, b):
    return lax.dot_general(a, b, (((1,), (1,)), ((), ())), preferred_element_type=F32)


U32 = jnp.uint32
HI_MASK = 0xFFFF0000


def _pack_pairs(x):
    n = x.shape[1] // 2
    bits = pltpu.bitcast(x.astype(BF16).astype(F32), U32)
    return (bits[:, :n] >> 16) | (bits[:, n:] & U32(HI_MASK))


def _unpack_pairs(p):
    return pltpu.bitcast(p << 16, F32), pltpu.bitcast(p & U32(HI_MASK), F32)


def _load_slabs(ref, m, ns):
    return jnp.concatenate([ref[pl.ds(j, m, stride=ns), :] for j in range(ns)], axis=1)


def _store_slabs(ref, val, ns):
    m = val.shape[0]
    for j in range(ns):
        ref[pl.ds(j, m, stride=ns), :] = val[:, j * LANES:(j + 1) * LANES]


def _params(n_axes):
    return pltpu.CompilerParams(dimension_semantics=("arbitrary",) * n_axes,
                                vmem_limit_bytes=VMEM_LIMIT)


def _ada_kernel(c_ref, w_ref, b_ref, o_ref):
    c = c_ref[...]
    s_hi, s_lo = _split_bf16(c * _sigmoid(c))
    w_hi, w_lo = _split_bf16(w_ref[...])
    acc = _dot(s_hi, w_hi) + _dot(s_hi, w_lo) + _dot(s_lo, w_hi)
    o_ref[...] = acc + b_ref[...]


def ada_mod(c_pad, w_ada, b_ada, cfg):
    rows, d = c_pad.shape
    n = w_ada.shape[1]
    tn = min(cfg.ada_tn, n)
    return pl.pallas_call(
        _ada_kernel,
        grid=(n // tn,),
        in_specs=[pl.BlockSpec((rows, d), lambda j: (0, 0)),
                  pl.BlockSpec((d, tn), lambda j: (0, j)),
                  pl.BlockSpec((1, tn), lambda j: (0, j))],
        out_specs=pl.BlockSpec((rows, tn), lambda j: (0, j)),
        out_shape=jax.ShapeDtypeStruct((rows, n), F32),
        compiler_params=_params(1),
        name="ada_mod",
    )(c_pad, w_ada, b_ada.reshape(1, n))


def _inproj_kernel(x_ref, g_ref, sc_ref, sh_ref, w_ref, cs_ref, o_ref, u_ref, *, sub):
    j = pl.program_id(1)

    @pl.when(j == 0)
    def _():
        for r0 in range(0, x_ref.shape[0], sub):
            rows = slice(r0, r0 + sub)
            x = x_ref[rows, :]
            y = x * lax.rsqrt(jnp.mean(x * x, axis=-1, keepdims=True) + EPS) * g_ref[...]
            u = (y * (1.0 + sc_ref[0]) + sh_ref[0]).astype(BF16)
            u_ref[rows, :] = u
            o_ref[rows, :] = (_dot(u, w_ref[...]) * cs_ref[...]).astype(BF16)

    @pl.when(j > 0)
    def _():
        o_ref[...] = (_dot(u_ref[...], w_ref[...]) * cs_ref[...]).astype(BF16)


def in_proj(x2, g, sc, sh, w_bf, colscale, cfg):
    t, d = x2.shape
    n = w_bf.shape[1]
    tm = min(cfg.proj_tm, cfg.seq)
    tn = min(cfg.proj_tn, n)
    per_b = cfg.seq // tm
    return pl.pallas_call(
        functools.partial(_inproj_kernel, sub=min(cfg.proj_sub, tm)),
        grid=(t // tm, n // tn),
        in_specs=[pl.BlockSpec((tm, d), lambda i, j: (i, 0)),
                  pl.BlockSpec((1, d), lambda i, j: (0, 0)),
                  pl.BlockSpec((1, 1, d), lambda i, j: (i // per_b, 0, 0)),
                  pl.BlockSpec((1, 1, d), lambda i, j: (i // per_b, 0, 0)),
                  pl.BlockSpec((d, tn), lambda i, j: (0, j)),
                  pl.BlockSpec((1, tn), lambda i, j: (0, j))],
        out_specs=pl.BlockSpec((tm, tn), lambda i, j: (i, j)),
        out_shape=jax.ShapeDtypeStruct((t, n), BF16),
        scratch_shapes=[pltpu.VMEM((tm, d), BF16)],
        compiler_params=_params(2),
        name="in_proj",
    )(x2, g, sc, sh, w_bf, colscale)


def _attn_a_kernel(lam_ref, qlo_ref, qhi_ref, k_ref, v_ref, tb_ref, olo_ref, ohi_ref,
                   s_ref, m_ref, l_ref, o_ref, *, tile, half, nq, out_scale):
    def stacked(q):
        lane = lax.broadcasted_iota(jnp.int32, q.shape, 1)
        zero = jnp.zeros_like(q)
        return jnp.concatenate([jnp.where(lane < half, q, zero), jnp.where(lane >= half, q, zero)], axis=0)

    def fold(x, op):
        r = x[:, :LANES]
        for c in range(1, tile // LANES):
            r = op(r, x[:, c * LANES:(c + 1) * LANES])
        return r

    def body(p):
        qtile = (p, nq - 1 - p)
        visits = [(0, kb) for kb in range(p + 1)] + [(1, kb) for kb in range(nq - p)]
        qq = (stacked(qlo_ref[...]), stacked(qhi_ref[...]))
        rows = lambda kb: slice(kb * tile, (kb + 1) * tile)
        for i, (w, kb) in enumerate(visits):
            bias = tb_ref[0, min(qtile[w] - kb, 2)]
            s = _dot_t(qq[w], k_ref[rows(kb), :]) + jnp.concatenate([bias, bias], axis=0)
            s_ref[i] = s
            m_ref[i] = fold(s, jnp.maximum)

        def per_tile(ref, op):
            out = [None, None]
            for i, (w, _) in enumerate(visits):
                out[w] = ref[i] if out[w] is None else op(out[w], ref[i])
            return out

        m = [jnp.max(x, axis=-1, keepdims=True) for x in per_tile(m_ref, jnp.maximum)]
        for i, (w, _) in enumerate(visits):
            e = jnp.exp2(s_ref[i] - m[w])
            s_ref[i] = e
            l_ref[i] = fold(e, jnp.add)

        lam = lam_ref[0]
        inv = [1.0 / jnp.sum(x, axis=-1, keepdims=True) for x in per_tile(l_ref, jnp.add)]
        c0 = [x[:tile] for x in inv]
        c1 = [lam * x[tile:] for x in inv]
        for i, (w, kb) in enumerate(visits):
            e = s_ref[i]
            a = e[:tile] * c0[w] - e[tile:] * c1[w]
            o_ref[i] = _dot(a.astype(BF16), v_ref[rows(kb), :])

        for o, out in zip(per_tile(o_ref, jnp.add), (olo_ref, ohi_ref)):
            out[...] = (o * lax.rsqrt(jnp.mean(o * o, axis=-1, keepdims=True) + HEAD_EPS) * out_scale).astype(BF16)

    for k in range(nq // 2):
        pl.when(pl.program_id(2) == k)(functools.partial(body, k))


def attn_a(proj, lam, tb, cfg):
    tile = cfg.attn_tile
    s = cfg.seq
    nq = s // tile
    assert nq % 2 == 0
    npair = nq // 2
    dv = 2 * cfg.da_qk
    assert dv == LANES
    kcol = cfg.ha
    vcol = 2 * cfg.ha
    kern = functools.partial(_attn_a_kernel, tile=tile, half=cfg.da_qk, nq=nq, out_scale=1.0 - cfg.lam_init)
    half_shape = jax.ShapeDtypeStruct((cfg.batch * npair * tile, cfg.ha * dv), BF16)
    y_lo, y_hi = pl.pallas_call(
        kern,
        grid=(cfg.ha, cfg.batch, npair),
        in_specs=[pl.BlockSpec(memory_space=pltpu.SMEM),
                  pl.BlockSpec((tile, LANES), lambda h, b, p: (b * nq + p, h)),
                  pl.BlockSpec((tile, LANES), lambda h, b, p: (b * nq + nq - 1 - p, h)),
                  pl.BlockSpec((s, LANES), lambda h, b, p: (b, kcol + h)),
                  pl.BlockSpec((s, LANES), lambda h, b, p: (b, vcol + h)),
                  pl.BlockSpec((1, 3, tile, tile), lambda h, b, p: (h, 0, 0, 0))],
        out_specs=[pl.BlockSpec((tile, LANES), lambda h, b, p: (b * npair + p, h)),
                   pl.BlockSpec((tile, LANES), lambda h, b, p: (b * npair + npair - 1 - p, h))],
        out_shape=[half_shape, half_shape],
        scratch_shapes=[pltpu.VMEM((nq + 1, 2 * tile, tile), F32),
                        pltpu.VMEM((nq + 1, 2 * tile, LANES), F32),
                        pltpu.VMEM((nq + 1, 2 * tile, LANES), F32),
                        pltpu.VMEM((nq + 1, tile, LANES), F32)],
        compiler_params=_params(3),
        name="attn_a",
    )(lam, proj, proj, proj, proj, tb)
    w = cfg.ha * dv
    rows = npair * tile
    return jnp.concatenate([y_lo.reshape(cfg.batch, rows, w), y_hi.reshape(cfg.batch, rows, w)],
                           axis=1).reshape(cfg.batch * s, w)


ATTN_B_TILES = 2


def _attn_b_kernel(q_ref, k_ref, v_ref, tb_ref, o_ref, *, tile, nwin, nstep):
    def body(step):
        for u in range(ATTN_B_TILES):
            qi = step * ATTN_B_TILES + u
            q = q_ref[u * tile:(u + 1) * tile, :]
            ss, vs = [], []
            for j in range(nwin):
                kb = qi - (nwin - 1) + j
                if kb < 0:
                    continue
                ss.append(_dot_t(q, k_ref[kb * tile:(kb + 1) * tile, :]) + tb_ref[0, j])
                vs.append(v_ref[kb * tile:(kb + 1) * tile, :])
            m = jnp.max(ss[0], axis=-1, keepdims=True)
            for s in ss[1:]:
                m = jnp.maximum(m, jnp.max(s, axis=-1, keepdims=True))
            ps = [jnp.exp2(s - m) for s in ss]
            l = jnp.sum(ps[0], axis=-1, keepdims=True)
            for p in ps[1:]:
                l = l + jnp.sum(p, axis=-1, keepdims=True)
            acc = _dot(ps[0].astype(BF16), vs[0])
            for p, v in zip(ps[1:], vs[1:]):
                acc = acc + _dot(p.astype(BF16), v)
            o_ref[u * tile:(u + 1) * tile, :] = (acc * (1.0 / l)).astype(BF16)

    for step in range(nstep):
        pl.when(pl.program_id(2) == step)(functools.partial(body, step))


def attn_b(proj, tb, cfg):
    t = proj.shape[0]
    tile = cfg.attn_tile
    s = cfg.seq
    nq = s // tile
    assert cfg.db == LANES and nq % ATTN_B_TILES == 0
    nstep = nq // ATTN_B_TILES
    rows = ATTN_B_TILES * tile
    nwin = (cfg.left_chunks * cfg.chunk) // tile + 1
    base = (2 * cfg.ha * 2 * cfg.da_qk + cfg.ha * 2 * cfg.da_qk) // LANES
    kern = functools.partial(_attn_b_kernel, tile=tile, nwin=nwin, nstep=nstep)
    return pl.pallas_call(
        kern,
        grid=(cfg.hb, cfg.batch, nstep),
        in_specs=[pl.BlockSpec((rows, LANES), lambda h, b, i: (b * nstep + i, base + h)),
                  pl.BlockSpec((s, LANES), lambda h, b, i: (b, base + cfg.hb + h)),
                  pl.BlockSpec((s, LANES), lambda h, b, i: (b, base + 2 * cfg.hb + h)),
                  pl.BlockSpec((1, nwin, tile, tile), lambda h, b, i: (h, 0, 0, 0))],
        out_specs=pl.BlockSpec((rows, LANES), lambda h, b, i: (b * nstep + i, h)),
        out_shape=jax.ShapeDtypeStruct((t, cfg.hb * cfg.db), BF16),
        compiler_params=_params(3),
        name="attn_b",
    )(proj, proj, proj, tb)


def _merge_kernel(ya_ref, yb_ref, ga_ref, gb_ref, x_ref, gta_ref, scm_ref, shm_ref, gmoe_ref,
                  wua_ref, wub_ref, wo_ref, wrh_ref, wrl_ref, h1_ref, v_ref, lg_ref, *, sub):
    ns = v_ref.shape[0] // h1_ref.shape[0]
    for r0 in range(0, h1_ref.shape[0], sub):
        rows = slice(r0, r0 + sub)
        a = _dot(ya_ref[rows, :], wua_ref[...])
        b = _dot(yb_ref[rows, :], wub_ref[...])
        merged = _sigmoid(ga_ref[rows, :].astype(F32)) * a + _sigmoid(gb_ref[rows, :].astype(F32)) * b
        att = _dot(merged.astype(BF16), wo_ref[...])
        h1 = x_ref[rows, :] + gta_ref[0] * att
        h1_ref[rows, :] = h1
        y = h1 * lax.rsqrt(jnp.mean(h1 * h1, axis=-1, keepdims=True) + EPS) * gmoe_ref[...]
        v = y * (1.0 + scm_ref[0]) + shm_ref[0]
        _store_slabs(v_ref.at[pl.ds(r0 * ns, sub * ns), :], _pack_pairs(v), ns)
        v_hi, v_lo = _split_bf16(v)
        wrh = wrh_ref[...]
        lg_ref[:, rows] = _dot_t(wrh, v_hi) + _dot_t(wrh, v_lo) + _dot_t(wrl_ref[...], v_hi)


def merge(ya, yb, proj, x2, gta, scm, shm, gmoe, wua, wub, wo, wrh, wrl, cfg):
    t, d = x2.shape
    tm = min(cfg.merge_tm, cfg.seq)
    per_b = cfg.seq // tm
    wa = ya.shape[1]
    wb = yb.shape[1]
    gate0 = (proj.shape[1] - 2 * d) // d
    ne = wrh.shape[0]
    ns = d // (2 * LANES)
    const = lambda i: (0, 0)
    row3 = lambda i: (i // per_b, 0, 0)
    resident = lambda shape: pl.BlockSpec(shape, const, pipeline_mode=pl.Buffered(1))
    return pl.pallas_call(
        functools.partial(_merge_kernel, sub=min(cfg.merge_sub, tm)),
        grid=(t // tm,),
        in_specs=[pl.BlockSpec((tm, wa), lambda i: (i, 0)),
                  pl.BlockSpec((tm, wb), lambda i: (i, 0)),
                  pl.BlockSpec((tm, d), lambda i: (i, gate0)),
                  pl.BlockSpec((tm, d), lambda i: (i, gate0 + 1)),
                  pl.BlockSpec((tm, d), lambda i: (i, 0)),
                  pl.BlockSpec((1, 1, d), row3),
                  pl.BlockSpec((1, 1, d), row3),
                  pl.BlockSpec((1, 1, d), row3),
                  pl.BlockSpec((1, d), const),
                  resident((wa, d)),
                  resident((wb, d)),
                  resident((d, d)),
                  resident((ne, d)),
                  resident((ne, d))],
        out_specs=[pl.BlockSpec((tm, d), lambda i: (i, 0)),
                   pl.BlockSpec((tm * ns, LANES), lambda i: (i, 0)),
                   pl.BlockSpec((ne, tm), lambda i: (0, i))],
        out_shape=[jax.ShapeDtypeStruct((t, d), F32),
                   jax.ShapeDtypeStruct((t * ns, LANES), U32),
                   jax.ShapeDtypeStruct((ne, t), F32)],
        compiler_params=_params(1),
        name="merge",
    )(ya, yb, proj, proj, x2, gta, scm, shm, gmoe, wua, wub, wo, wrh, wrl)


def _route_kernel(lg_ref, rb_ref, tri_ref, w_ref, lin_ref, kin_ref, gend_ref, carry_ref,
                  *, n_exp, n_groups, topk_groups, top_k, scale):
    gsz = n_exp // n_groups
    tc = lg_ref.shape[1]
    ninf = -jnp.inf
    step = pl.program_id(0)

    @pl.when(step == 0)
    def _():
        carry_ref[...] = jnp.zeros(carry_ref.shape, F32)

    row = lax.broadcasted_iota(jnp.int32, (gsz, tc), 0).astype(F32)
    sc, bi = [], []
    for g in range(n_groups):
        s = _sigmoid(lg_ref[g * gsz:(g + 1) * gsz, :])
        sc.append(s)
        bi.append(s + rb_ref[g * gsz:(g + 1) * gsz, :])
    gs = []
    for x in bi:
        m1 = jnp.max(x, axis=0, keepdims=True)
        eq = x == m1
        cnt = jnp.sum(jnp.where(eq, 1.0, 0.0), axis=0, keepdims=True)
        second = jnp.max(jnp.where(eq, ninf, x), axis=0, keepdims=True)
        gs.append(m1 + jnp.where(cnt >= 2.0, m1, second))
    gmat = jnp.concatenate(gs, axis=0)
    grow = lax.broadcasted_iota(jnp.int32, gmat.shape, 0).astype(F32)
    gsel = jnp.zeros(gmat.shape, F32)
    for _ in range(topk_groups):
        m = jnp.max(gmat, axis=0, keepdims=True)
        idx = jnp.min(jnp.where(gmat == m, grow, float(n_groups)), axis=0, keepdims=True)
        hit = grow == idx
        gsel = jnp.where(hit, 1.0, gsel)
        gmat = jnp.where(hit, ninf, gmat)
    cur = [jnp.where(gsel[g:g + 1, :] > 0.5, bi[g], ninf) for g in range(n_groups)]
    ids = [row + float(g * gsz) for g in range(n_groups)]
    w_rows = []
    chosen = [jnp.zeros((gsz, tc), F32) for _ in range(n_groups)]
    kplus = [jnp.zeros((gsz, tc), F32) for _ in range(n_groups)]
    for k in range(top_k):
        mg = cur[0]
        for c in cur[1:]:
            mg = jnp.maximum(mg, c)
        m = jnp.max(mg, axis=0, keepdims=True)
        cand = jnp.where(cur[0] == m, ids[0], float(n_exp))
        for c, i in zip(cur[1:], ids[1:]):
            cand = jnp.minimum(cand, jnp.where(c == m, i, float(n_exp)))
        idx = jnp.min(cand, axis=0, keepdims=True)
        wsum = jnp.zeros((gsz, tc), F32)
        for g in range(n_groups):
            hit = ids[g] == idx
            wsum = wsum + jnp.where(hit, sc[g], 0.0)
            cur[g] = jnp.where(hit, ninf, cur[g])
            chosen[g] = jnp.where(hit, 1.0, chosen[g])
            kplus[g] = jnp.where(hit, float(k + 1), kplus[g])
        w_rows.append(jnp.sum(wsum, axis=0, keepdims=True))
    w = jnp.concatenate(w_rows, axis=0)
    w_ref[...] = w / jnp.sum(w, axis=0, keepdims=True) * scale

    onehot = jnp.concatenate(chosen, axis=0)
    incl = _dot(onehot.astype(BF16), tri_ref[...])
    lin_ref[...] = incl.astype(BF16)
    kin_ref[...] = jnp.concatenate(kplus, axis=0).astype(BF16)
    gps = tc // LANES
    lane = lax.broadcasted_iota(jnp.int32, (n_exp, LANES), 1)
    run = carry_ref[...]
    ends = jnp.zeros((n_exp, LANES), F32)
    for g in range(gps):
        ends = jnp.where(lane == gps + g, run, ends)
        run = run + incl[:, (g + 1) * LANES - 1:(g + 1) * LANES]
        ends = jnp.where(lane == g, run, ends)
    carry_ref[...] = run
    gend_ref[0] = ends


def route(lg, rb, cfg):
    ne, t = lg.shape
    nx = cfg.n_experts
    tc = min(cfg.route_tc, t)
    nsteps = t // tc
    tri = jnp.asarray(np.kron(np.eye(tc // LANES), np.triu(np.ones((LANES, LANES)))), BF16)
    kern = functools.partial(_route_kernel, n_exp=cfg.n_experts, n_groups=cfg.n_groups,
                             topk_groups=cfg.topk_groups, top_k=cfg.top_k, scale=cfg.routed_scale)
    return pl.pallas_call(
        kern,
        grid=(nsteps,),
        in_specs=[pl.BlockSpec((ne, tc), lambda i: (0, i)),
                  pl.BlockSpec((ne, tc), lambda i: (0, 0)),
                  pl.BlockSpec((tc, tc), lambda i: (0, 0))],
        out_specs=[pl.BlockSpec((cfg.top_k, tc), lambda i: (0, i)),
                   pl.BlockSpec((nx, tc), lambda i: (0, i)),
                   pl.BlockSpec((nx, tc), lambda i: (0, i)),
                   pl.BlockSpec((1, nx, LANES), lambda i: (i, 0, 0))],
        out_shape=[jax.ShapeDtypeStruct((cfg.top_k, t), F32),
                   jax.ShapeDtypeStruct((nx, t), BF16),
                   jax.ShapeDtypeStruct((nx, t), BF16),
                   jax.ShapeDtypeStruct((nsteps, nx, LANES), F32)],
        scratch_shapes=[pltpu.VMEM((cfg.n_experts, LANES), F32)],
        compiler_params=_params(1),
        name="route",
    )(lg, rb, tri)


SLOT_BLOCKS = 4


def _slots_kernel(be_ref, r0_ref, cnt_ref, gstart_ref, gend_ref, lin_ref, kin_ref, tok_ref, dst_ref,
                  *, bm, top_k, n_tok, n_assign):
    step = pl.program_id(0)
    rows = lax.broadcasted_iota(jnp.int32, (bm, LANES), 0)
    lane = lax.broadcasted_iota(jnp.int32, (bm, LANES), 1).astype(F32)
    ones = jnp.ones((16, LANES), BF16)
    gids = lax.broadcasted_iota(jnp.int32, (16, LANES), 1).astype(F32).astype(BF16)
    lane_row = lax.broadcasted_iota(jnp.int32, (1, bm), 1)
    for u in range(SLOT_BLOCKS):
        b = step * SLOT_BLOCKS + u
        e = be_ref[b]
        r0 = r0_ref[b]
        cnt = cnt_ref[b]
        r = (rows + r0).astype(F32)
        gstart = gstart_ref[pl.ds(e, 1), :]
        in_group = jnp.logical_and(gstart <= r, r < gend_ref[pl.ds(e, 1), :])
        group = jnp.where(in_group, 1.0, 0.0).astype(BF16)
        want = r - jnp.sum(jnp.where(in_group, gstart, 0.0), axis=1, keepdims=True) + 1.0
        kin = _dot(group, kin_ref[e])
        hit = jnp.logical_and(_dot(group, lin_ref[e]) == want, kin > 0.5)
        t_loc = _dot_t(ones, jnp.where(hit, lane, 0.0).astype(BF16))[0:1]
        kp1 = _dot_t(ones, jnp.where(hit, kin, 0.0).astype(BF16))[0:1]
        gid = _dot_t(gids, group)[0:1]
        tok = (gid * float(LANES) + t_loc).astype(jnp.int32)
        kk = kp1.astype(jnp.int32) - 1
        valid = (lane_row + r0) < cnt
        dummy = n_assign + lax.rem(b, 4) * bm + lane_row
        tok_ref[u] = jnp.where(valid, tok, lax.rem(b * bm + lane_row, n_tok))
        dst_ref[u] = jnp.where(valid, kk * n_tok + tok, dummy)


def slots(block_e, r0, cntb, gstart, gend, lin, kin, cfg, n_tok):
    nb = block_e.shape[0]
    bm = cfg.moe_rows
    assert nb % SLOT_BLOCKS == 0 and gstart.shape[1] == LANES and lin.shape[1:] == (LANES, LANES)
    kern = functools.partial(_slots_kernel, bm=bm, top_k=cfg.top_k, n_tok=n_tok, n_assign=n_tok * cfg.top_k)
    whole = lambda a: pl.BlockSpec(a.shape, lambda i, be, r0, cn: (0,) * a.ndim)
    out_blk = pl.BlockSpec((SLOT_BLOCKS, 1, bm), lambda i, be, r0, cn: (i, 0, 0))
    grid_spec = pltpu.PrefetchScalarGridSpec(
        num_scalar_prefetch=3,
        grid=(nb // SLOT_BLOCKS,),
        in_specs=[whole(gstart), whole(gend), whole(lin), whole(kin)],
        out_specs=[out_blk, out_blk],
    )
    return pl.pallas_call(
        kern,
        grid_spec=grid_spec,
        out_shape=[jax.ShapeDtypeStruct((nb, 1, bm), jnp.int32), jax.ShapeDtypeStruct((nb, 1, bm), jnp.int32)],
        compiler_params=_params(1),
        name="slots",
    )(block_e, r0, cntb, gstart, gend, lin, kin)


MOE_RING = 4
MOE_DUMMY = 4


def _moe_kernel(be_ref, tok0_ref, tok1_ref, tok2_ref, tokn_ref, dstp_ref, dstc_ref, v_hbm, wg_hbm, wu_hbm, wd_hbm,
                yk_hbm, xbuf, ybuf, xbf, wst_g, wst_u, wst_d, wgb, wub, wdb, gsem, ssem, wsem, *, bm, n_assign):
    b = pl.program_id(0)
    nb = pl.num_programs(0)
    slot = lax.rem(b, MOE_RING)
    ns = v_hbm.shape[1]

    def weight_copies(e, ws):
        return [pltpu.make_async_copy(src.at[e], dst.at[ws], wsem.at[ws])
                for src, dst in ((wg_hbm, wst_g), (wu_hbm, wst_u), (wd_hbm, wst_d))]

    def unpack_block(s):
        lo, hi = _unpack_pairs(_load_slabs(xbuf.at[s], bm, ns))
        xbf[s % 2] = jnp.concatenate([lo, hi], axis=1).astype(BF16)

    def gather_copy(t, r, s):
        return pltpu.make_async_copy(v_hbm.at[t], xbuf.at[s, pl.ds(r * ns, ns), :], gsem.at[s])

    def scatter_copy(r, d, s):
        return pltpu.make_async_copy(ybuf.at[s, pl.ds(r * ns, ns), :], yk_hbm.at[d], ssem.at[s])

    def wait_gather(s):
        for r in range(bm):
            gather_copy(0, r, s).wait()

    def wait_scatter(s):
        for r in range(bm):
            scatter_copy(r, 0, s).wait()

    @pl.when(b == 0)
    def _():
        ybuf[...] = jnp.zeros(ybuf.shape, U32)
        for s in range(MOE_RING - 1):
            for r in range(bm):
                scatter_copy(r, n_assign + s * bm + r, s).start(priority=r % 2)
        for s, tok_ref in enumerate((tok0_ref, tok1_ref, tok2_ref)):
            for r in range(bm):
                gather_copy(tok_ref[0, 0, r], r, s).start(priority=r % 2)
        wait_gather(0)
        unpack_block(0)
        for c in weight_copies(be_ref[0], 0):
            c.start()

    @pl.when(be_ref[nb + 1 + b] == 1)
    def _():
        ws = be_ref[2 * nb + 1 + b]
        nxt_e = be_ref[3 * nb + 1 + b]
        for c in weight_copies(be_ref[b], ws):
            c.wait()

        @pl.when(nxt_e >= 0)
        def _():
            for c in weight_copies(nxt_e, 1 - ws):
                c.start()

        wgb[...] = wst_g[ws].astype(BF16)
        wub[...] = wst_u[ws].astype(BF16)
        wdb[...] = wst_d[ws].astype(BF16)

    def block_step(s):
        nxt, prv = (s + 1) % MOE_RING, (s + 3) % MOE_RING
        wait_gather(nxt)
        wait_scatter(s)
        for r in range(bm):
            scatter_copy(r, dstp_ref[0, 0, r], prv).start(priority=r % 2)
        for r in range(bm):
            gather_copy(tokn_ref[0, 0, r], r, prv).start(priority=r % 2)
        x = xbf[s % 2]
        g = _dot(x, wgb[...])
        u = _dot(x, wub[...])
        h = (g * _sigmoid(g)) * u
        unpack_block(nxt)
        _store_slabs(ybuf.at[s], _pack_pairs(_dot(h.astype(BF16), wdb[...])), ns)

    last = be_ref[nb] - 1
    for s in range(MOE_RING):
        pl.when(jnp.logical_and(slot == s, b <= last))(functools.partial(block_step, s))

    @pl.when(b == last)
    def _():
        def drain(s):
            for r in range(bm):
                scatter_copy(r, dstc_ref[0, 0, r], s).start(priority=r % 2)
            for j in (2, 3):
                wait_gather((s + j) % MOE_RING)
            for j in (1, 2, 3, 0):
                wait_scatter((s + j) % MOE_RING)

        for s in range(MOE_RING):
            pl.when(slot == s)(functools.partial(drain, s))


def moe(vp, block_e, n_used, slot_tok, slot_dst, wg, wu, wd, cfg):
    d = wg.shape[1]
    ns = d // (2 * LANES)
    t = vp.shape[0] // ns
    bm = cfg.moe_rows
    v3 = vp.reshape(t, ns, LANES)
    nb = block_e.shape[0]
    assert nb >= 2
    de = wg.shape[2]
    n_assign = t * cfg.top_k
    first = (n_assign + 3 * bm + jnp.arange(bm, dtype=jnp.int32)).reshape(1, 1, bm)
    dst_ext = jnp.concatenate([first, slot_dst], axis=0)
    kern = functools.partial(_moe_kernel, bm=bm, n_assign=n_assign)
    smem_blk = lambda f: pl.BlockSpec((1, 1, bm), f, memory_space=pltpu.SMEM)
    grid_spec = pltpu.PrefetchScalarGridSpec(
        num_scalar_prefetch=1,
        grid=(nb,),
        in_specs=[smem_blk(lambda b, be: (0, 0, 0)),
                  smem_blk(lambda b, be: (1, 0, 0)),
                  smem_blk(lambda b, be: (min(2, nb - 1), 0, 0)),
                  smem_blk(lambda b, be: (jnp.minimum(b + 3, nb - 1), 0, 0)),
                  smem_blk(lambda b, be: (b, 0, 0)),
                  smem_blk(lambda b, be: (b + 1, 0, 0)),
                  pl.BlockSpec(memory_space=pl.ANY)] + [pl.BlockSpec(memory_space=pl.ANY)] * 3,
        out_specs=pl.BlockSpec(memory_space=pl.ANY),
        scratch_shapes=[pltpu.VMEM((MOE_RING, bm * ns, LANES), U32),
                        pltpu.VMEM((MOE_RING, bm * ns, LANES), U32),
                        pltpu.VMEM((2, bm, d), BF16),
                        pltpu.VMEM((2, d, de), F32), pltpu.VMEM((2, d, de), F32), pltpu.VMEM((2, de, d), F32),
                        pltpu.VMEM((d, de), BF16), pltpu.VMEM((d, de), BF16), pltpu.VMEM((de, d), BF16),
                        pltpu.SemaphoreType.DMA((MOE_RING,)), pltpu.SemaphoreType.DMA((MOE_RING,)),
                        pltpu.SemaphoreType.DMA((2,))],
    )
    idx = jnp.arange(nb, dtype=jnp.int32)
    is_first = jnp.logical_and(idx < n_used, block_e != jnp.concatenate([jnp.full((1,), -1, jnp.int32), block_e[:-1]]))
    stage = (jnp.cumsum(is_first.astype(jnp.int32)) - 1) % 2
    later_first = jnp.flip(lax.cummin(jnp.flip(jnp.where(is_first, idx, nb))))
    nxt_pos = jnp.concatenate([later_first[1:], jnp.full((1,), nb, jnp.int32)])
    nxt_e = jnp.sum(jnp.where(idx[None, :] == nxt_pos[:, None], block_e[None, :] + 1, 0), axis=1) - 1
    sched = jnp.concatenate([block_e, n_used.reshape(1), is_first.astype(jnp.int32), stage.astype(jnp.int32),
                             nxt_e.astype(jnp.int32)])
    return pl.pallas_call(
        kern,
        grid_spec=grid_spec,
        out_shape=jax.ShapeDtypeStruct((n_assign + MOE_DUMMY * bm, ns, LANES), U32),
        compiler_params=_params(1),
        name="moe",
    )(sched, slot_tok, slot_tok, slot_tok, slot_tok, dst_ext, dst_ext, v3, wg, wu, wd)


def _final_kernel(w_ref, v_ref, h1_ref, gtm_ref, gfin_ref, sg_ref, su_ref, sd_ref, *rest, top_k):
    yk_refs, o_ref, lo_ref, hi_ref = rest[:top_k], rest[top_k], rest[top_k + 1], rest[top_k + 2]
    tm = h1_ref.shape[0]
    ns = yk_refs[0].shape[1]

    def combine(t, carry):
        lo, hi = _unpack_pairs(yk_refs[0][t])
        acc_lo, acc_hi = w_ref[t, 0] * lo, w_ref[t, 0] * hi
        for k in range(1, top_k):
            lo, hi = _unpack_pairs(yk_refs[k][t])
            acc_lo, acc_hi = acc_lo + w_ref[t, k] * lo, acc_hi + w_ref[t, k] * hi
        rows = pl.ds(pl.multiple_of(t * ns, ns), ns)
        lo_ref[rows, :] = acc_lo
        hi_ref[rows, :] = acc_hi
        return carry

    lax.fori_loop(0, tm, combine, 0, unroll=8)
    routed = jnp.concatenate([_load_slabs(lo_ref, tm, ns), _load_slabs(hi_ref, tm, ns)], axis=1)

    vb = jnp.concatenate(_unpack_pairs(_load_slabs(v_ref, tm, ns)), axis=1).astype(BF16)
    g = _dot(vb, sg_ref[...])
    u = _dot(vb, su_ref[...])
    y = routed + _dot(((g * _sigmoid(g)) * u).astype(BF16), sd_ref[...])
    h2 = h1_ref[...] + gtm_ref[0] * y
    o_ref[...] = h2 * lax.rsqrt(jnp.mean(h2 * h2, axis=-1, keepdims=True) + EPS) * gfin_ref[...]


def final(vp, h1, yk3, w_tk, gtm, gfin, sg, su, sd, cfg):
    t, d = h1.shape
    tm = min(cfg.final_tm, cfg.seq)
    per_b = cfg.seq // tm
    nt = t // tm
    ds = sg.shape[1]
    nslab = d // (2 * LANES)
    const = lambda i: (0, 0)
    kern = functools.partial(_final_kernel, top_k=cfg.top_k)
    plane = lambda k: pl.BlockSpec((tm, nslab, LANES), lambda i: (k * nt + i, 0, 0))
    return pl.pallas_call(
        kern,
        grid=(nt,),
        in_specs=[pl.BlockSpec((tm, cfg.top_k), lambda i: (i, 0), memory_space=pltpu.SMEM),
                  pl.BlockSpec((tm * nslab, LANES), lambda i: (i, 0)),
                  pl.BlockSpec((tm, d), lambda i: (i, 0)),
                  pl.BlockSpec((1, 1, d), lambda i: (i // per_b, 0, 0)),
                  pl.BlockSpec((1, d), const),
                  pl.BlockSpec((d, ds), const),
                  pl.BlockSpec((d, ds), const),
                  pl.BlockSpec((ds, d), const)] + [plane(k) for k in range(cfg.top_k)],
        out_specs=pl.BlockSpec((tm, d), lambda i: (i, 0)),
        out_shape=jax.ShapeDtypeStruct((t, d), F32),
        scratch_shapes=[pltpu.VMEM((tm * nslab, LANES), F32), pltpu.VMEM((tm * nslab, LANES), F32)],
        compiler_params=_params(1),
        name="final",
    )(w_tk, vp, h1, gtm, gfin, sg, su, sd, *([yk3] * cfg.top_k))


def _t5_bucket(rel, cfg):
    nb = cfg.t5_buckets // 2
    ret = (rel > 0).astype(np.int32) * nb
    n = np.abs(rel)
    max_exact = nb // 2
    large = max_exact + (np.log(np.maximum(n, 1) / max_exact) / math.log(cfg.t5_max_dist / max_exact)
                         * (nb - max_exact)).astype(np.int32)
    large = np.minimum(large, nb - 1)
    return (ret + np.where(n < max_exact, n, large)).astype(np.int32)


def _toeplitz(vec, n):
    lead = vec.shape[:-1]
    flat = jnp.tile(vec, (1,) * len(lead) + (n,))[..., :n * (2 * n - 1)]
    return flat.reshape(lead + (n, 2 * n - 1))[..., :n]


def _rel_offsets(n):
    m = np.arange(2 * n)
    return np.where(m < n, m, m - 2 * n)


def _bias_tables_a(t5_bias, cfg):
    tile = cfg.attn_tile
    assert tile + 1 >= cfg.t5_max_dist and tile % cfg.chunk == 0
    ql = np.arange(tile)[:, None]
    kl = np.arange(tile)[None, :]
    allowed = (kl // cfg.chunk) <= (ql // cfg.chunk)
    r = _rel_offsets(tile)
    t5 = t5_bias.astype(F32)
    diag = _toeplitz(t5[_t5_bucket(r, cfg)].T, tile)
    prev = _toeplitz(t5[_t5_bucket(r - tile, cfg)].T, tile)
    far = jnp.broadcast_to(t5[_t5_bucket(np.array(-2 * tile), cfg)][:, None, None], diag.shape)
    return jnp.stack([jnp.where(allowed[None], diag, NEG), prev, far], axis=1)


def _bias_tables_b(rel_table, cfg):
    tile = cfg.attn_tile
    assert (cfg.left_chunks * cfg.chunk) % tile == 0 and tile % cfg.chunk == 0
    nwin = (cfg.left_chunks * cfg.chunk) // tile + 1
    ql = np.arange(tile)[:, None]
    kl = np.arange(tile)[None, :]
    r = _rel_offsets(tile)
    rt = rel_table.astype(F32)
    tabs = []
    for j in range(nwin):
        off = (j - (nwin - 1)) * tile
        idx = np.clip(r + off, -cfg.rel_clip, cfg.rel_clip) + cfg.rel_clip
        dchunk = ql // cfg.chunk - (kl + off) // cfg.chunk
        band = (dchunk >= 0) & (dchunk <= cfg.left_chunks)
        tabs.append(jnp.where(band[None], _toeplitz(rt[idx].T, tile), NEG))
    return jnp.stack(tabs, axis=1)


def _block_tables(counts, n_assign, cfg):
    e_ = cfg.n_experts
    bm = cfg.moe_rows
    nb = -(-(n_assign + e_ * (bm - 1)) // bm)
    pcounts = (counts + bm - 1) // bm * bm
    pends = jnp.cumsum(pcounts)
    pstarts = pends - pcounts
    start_b = jnp.arange(nb, dtype=jnp.int32) * bm
    block_e = jnp.minimum(jnp.sum((pends[None, :] <= start_b[:, None]).astype(jnp.int32), axis=1), e_ - 1)
    mine = jnp.arange(e_, dtype=jnp.int32)[None, :] == block_e[:, None]
    pstart_b = jnp.sum(jnp.where(mine, pstarts[None, :], 0), axis=1)
    count_b = jnp.sum(jnp.where(mine, counts[None, :], 0), axis=1)
    n_used = (pends[e_ - 1] // bm).astype(jnp.int32)
    return block_e.astype(jnp.int32), (start_b - pstart_b).astype(jnp.int32), count_b.astype(jnp.int32), n_used


def _forward(cfg, x, c, w_ada, b_ada, g_attn, w_in, lambda_qk, t5_bias, rel_bias_b, w_up_a, w_up_b, w_o,
             g_moe, w_router, router_bias, w_exp_gate, w_exp_up, w_exp_down,
             w_sh_gate, w_sh_up, w_sh_down, g_final):
    b_, s_, d_ = x.shape
    t_ = b_ * s_
    l = 0
    qk_a = cfg.ha * 2 * cfg.da_qk
    width_a = cfg.ha * 2 * cfg.da_qk
    width_b = cfg.hb * cfg.db
    in_cols = w_in.shape[2]

    rows = max(16, b_)
    c_pad = jnp.zeros((rows, d_), F32).at[:b_].set(c)
    mod = ada_mod(c_pad, w_ada[l], b_ada[l], cfg)[:b_]
    sh_a, sc_a, gt_a, sh_m, sc_m, gt_m = [m.reshape(b_, 1, d_) for m in jnp.split(mod, cfg.n_mod, axis=-1)]

    colscale = np.ones((1, in_cols), np.float32)
    colscale[0, :qk_a] = cfg.da_qk ** -0.5 * LOG2E
    qb0 = 2 * qk_a + width_a
    colscale[0, qb0:qb0 + width_b] = cfg.db ** -0.5 * LOG2E
    x2 = x.reshape(t_, d_)
    proj = in_proj(x2, g_attn[l].reshape(1, d_), sc_a, sh_a, w_in[l].astype(BF16), jnp.asarray(colscale), cfg)

    lq = lambda_qk[l].astype(F32)
    lam = jnp.exp(jnp.sum(lq[0] * lq[1])) - jnp.exp(jnp.sum(lq[2] * lq[3])) + cfg.lam_init
    ya = attn_a(proj, lam.reshape(1), _bias_tables_a(t5_bias * LOG2E, cfg), cfg)
    yb = attn_b(proj, _bias_tables_b(rel_bias_b[l] * LOG2E, cfg), cfg)

    ne_pad = max(LANES, cfg.n_experts)
    wrt = jnp.zeros((ne_pad, d_), F32).at[:cfg.n_experts].set(w_router[l].T)
    wrh, wrl = _split_bf16(wrt)
    h1, v, lg = merge(ya, yb, proj, x2, gt_a, sc_m, sh_m, g_moe[l].reshape(1, d_),
                      w_up_a[l].astype(BF16), w_up_b[l].astype(BF16), w_o[l].astype(BF16), wrh, wrl, cfg)

    tc = min(cfg.route_tc, t_)
    rb = jnp.zeros((ne_pad, tc), F32).at[:cfg.n_experts].set(
        jnp.broadcast_to(router_bias[l].astype(F32)[:, None], (cfg.n_experts, tc)))
    w_kt, lin, kin, gend_raw = route(lg, rb, cfg)
    gps, n_grp, nx = tc // LANES, t_ // LANES, cfg.n_experts
    assert n_grp <= LANES
    per_group = lambda a: jnp.pad(a.transpose(1, 0, 2).reshape(nx, n_grp), ((0, 0), (0, LANES - n_grp)),
                                  constant_values=1e9)
    gend, gstart = per_group(gend_raw[:, :, :gps]), per_group(gend_raw[:, :, gps:2 * gps])
    by_group = lambda a: jnp.pad(a.reshape(nx, n_grp, LANES), ((0, 0), (0, LANES - n_grp), (0, 0)))
    counts = gend[:, n_grp - 1].astype(jnp.int32)
    block_e, r0, count_b, n_used = _block_tables(counts, t_ * cfg.top_k, cfg)
    slot_tok, slot_dst = slots(block_e, r0, count_b, gstart, gend, by_group(lin), by_group(kin), cfg, t_)

    yk = moe(v, block_e, n_used, slot_tok, slot_dst, w_exp_gate[l], w_exp_up[l], w_exp_down[l], cfg)
    out = final(v, h1, yk, w_kt.T, gt_m, g_final.reshape(1, d_),
                w_sh_gate[l].astype(BF16), w_sh_up[l].astype(BF16), w_sh_down[l].astype(BF16), cfg)
    return out.reshape(b_, s_, d_)


def kernel(x, c, w_ada, b_ada, g_attn, w_in, lambda_qk, t5_bias, rel_bias_b, w_up_a, w_up_b, w_o, g_moe,
           w_router, router_bias, w_exp_gate, w_exp_up, w_exp_down, w_sh_gate, w_sh_up, w_sh_down, g_final):
    return _forward(Cfg(), x, c, w_ada, b_ada, g_attn, w_in, lambda_qk, t5_bias, rel_bias_b, w_up_a, w_up_b,
                    w_o, g_moe, w_router, router_bias, w_exp_gate, w_exp_up, w_exp_down,
                    w_sh_gate, w_sh_up, w_sh_down, g_final)
```
